```python
import math
import jax, jax.numpy as jnp
from jax import lax
import numpy as np

D_MODEL = 1024
BATCH = 2
SEQ = 8192
DEPTH = 2
DEC_BATCH = 128
DEC_SEQ = 4
PAST_LEN = 2048
PAGE_SIZE = 128

PLE_DIM = 256
W_A = D_MODEL // 4
W_B = D_MODEL // 4
W_C = D_MODEL // 4
W_D = D_MODEL - W_A - W_B - W_C
N_A_HEADS = 4
D_A_HEAD = W_A // (2 * N_A_HEADS)
N_B_HEADS = 4
D_B_HEAD = W_B // N_B_HEADS
N_IDX_HEADS = 4
D_IDX = 64
DSA_TOPK = 256
C_WIDTH = 3
D_WIDTH = 31
PEER_HEADS = 8
PEER_DK = 256
N_KEYS = 128
N_EXPERTS = N_KEYS * N_KEYS
PEER_TOPK = 16
Q_BLOCK = 128
LN_EPS = 1e-5
DN_ALPHA = (2 * DEPTH) ** 0.25
DN_BETA = (8 * DEPTH) ** -0.25
IN_SPLITS = (2 * N_A_HEADS * D_A_HEAD, 2 * N_A_HEADS * D_A_HEAD, 2 * N_A_HEADS * D_A_HEAD,
             N_B_HEADS * D_B_HEAD, N_B_HEADS * D_B_HEAD, N_B_HEADS * D_B_HEAD,
             N_IDX_HEADS * D_IDX, D_IDX, N_IDX_HEADS,
             W_C, W_C, W_C, W_D, W_D)
N_IN = sum(IN_SPLITS)

kernel_name = "hymba_diffattn_dsa_conv_peer_decoder_step"


def layer_norm(x, g, b):
    xf = x.astype(jnp.float32)
    mu = jnp.mean(xf, -1, keepdims=True)
    xc = xf - mu
    var = jnp.mean(xc * xc, -1, keepdims=True)
    return (xc * lax.rsqrt(var + LN_EPS)).astype(x.dtype) * g + b


def rms_norm(x, g):
    xf = x.astype(jnp.float32)
    return (xf * lax.rsqrt(jnp.mean(xf * xf, -1, keepdims=True) + LN_EPS)).astype(x.dtype) * g


def split_cols(z):
    parts, off = [], 0
    for n in IN_SPLITS:
        parts.append(z[..., off:off + n])
        off += n
    return parts


def over_blocks(fn, *arrays):
    T = arrays[0].shape[1]
    if T <= Q_BLOCK or T % Q_BLOCK:
        return fn(*arrays)
    nb = T // Q_BLOCK
    split = lambda a: jnp.moveaxis(a.reshape((a.shape[0], nb, Q_BLOCK) + a.shape[2:]), 1, 0)
    out = lax.map(lambda args: fn(*args), tuple(split(a) for a in arrays))
    return jnp.moveaxis(out, 0, 1).reshape((out.shape[1], T) + out.shape[3:])


def gather_pages(pool, page_table):
    g = pool[page_table]
    return g.reshape((g.shape[0], g.shape[1] * g.shape[2]) + g.shape[3:])


def causal_dwconv(u_ext, w):
    return lax.conv_general_dilated(u_ext, w[:, None, :], window_strides=(1,), padding='VALID',
                                    dimension_numbers=('NWC', 'WIO', 'NWC'),
                                    feature_group_count=w.shape[1])


def diff_attn_block(qp, q, k, v, lam):
    t = qp[0]
    s = jnp.einsum('bthcd,bshcd->bhcts', q, k).astype(jnp.float32) * (D_A_HEAD ** -0.5)
    mask = jnp.arange(k.shape[1])[None, :] <= t[:, None]
    pr = jax.nn.softmax(jnp.where(mask, s, -jnp.inf), axis=-1)
    a = pr[:, :, 0] - lam * pr[:, :, 1]
    return jnp.einsum('bhts,bshe->bthe', a.astype(v.dtype), v)


def dsa_block(qp, q, qi, wi, k, v, ki, topk):
    t = qp[0]
    isc = jnp.einsum('bthd,bsd->bths', qi, ki).astype(jnp.float32)
    isc = jnp.einsum('bths,bth->bts', jax.nn.relu(isc), wi.astype(jnp.float32))
    admissible = jnp.arange(k.shape[1])[None, None, :] <= t[None, :, None]
    _, idx = lax.top_k(jnp.where(admissible, isc, -jnp.inf), topk)
    bi = jnp.arange(k.shape[0])[:, None, None]
    k_sel = k[bi, idx]
    v_sel = v[bi, idx]
    s = jnp.einsum('bthd,btkhd->bhtk', q, k_sel).astype(jnp.float32) * (D_B_HEAD ** -0.5)
    valid = (idx <= t[None, :, None])[:, None]
    pr = jax.nn.softmax(jnp.where(valid, s, -jnp.inf), axis=-1)
    return jnp.einsum('bhtk,btkhd->bthd', pr.astype(v.dtype), v_sel)


def peer(x, wq, subkeys, u, v):
    q = (x @ wq).reshape(x.shape[:2] + (PEER_HEADS, 2, PEER_DK // 2))
    s = jnp.einsum('bthcd,ckd->bthck', q, subkeys).astype(jnp.float32)
    sv, si = lax.top_k(s, PEER_TOPK)
    cand = sv[..., 0, :, None] + sv[..., 1, None, :]
    cid = si[..., 0, :, None] * N_KEYS + si[..., 1, None, :]
    cand = cand.reshape(cand.shape[:-2] + (PEER_TOPK * PEER_TOPK,))
    cid = cid.reshape(cid.shape[:-2] + (PEER_TOPK * PEER_TOPK,))
    fv, fi = lax.top_k(cand, PEER_TOPK)
    eid = jnp.take_along_axis(cid, fi, axis=-1)
    g = jax.nn.softmax(fv, axis=-1)
    act = jax.nn.gelu(jnp.einsum('btd,bthkd->bthk', x, u[eid]).astype(jnp.float32), approximate=False)
    return jnp.einsum('bthk,bthkd->btd', (g * act).astype(x.dtype), v[eid])


def trunk_layer(layer_idx, x, p, past, lw):
    (a_k_past, a_v_past, b_k_past, b_v_past, b_ki_past, c_buf, d_buf) = past
    (w_in, a_lambda, a_subln_g, c_conv_w, d_conv_w, d_conv_b, d_ln_g, d_ln_b, d_pw_w, d_pw_b,
     w_out, ln1_g, ln1_b, peer_wq, peer_subkeys, peer_u, peer_v, ln2_g, ln2_b,
     ple_w, ple_gate_w, ple_gate_b) = lw
    nb, T, _ = x.shape
    P = a_k_past.shape[1]
    q_pos = (P + jnp.arange(T, dtype=jnp.int32))[None]
    (qa, ka, va, qb, kb, vb, qi, ki, wi, cb, cc, cx, da, dg) = split_cols(x @ w_in)

    qa = qa.reshape(nb, T, N_A_HEADS, 2, D_A_HEAD)
    ka = ka.reshape(nb, T, N_A_HEADS, 2, D_A_HEAD)
    va = va.reshape(nb, T, N_A_HEADS, 2 * D_A_HEAD)
    lam_init = 0.8 - 0.6 * math.exp(-0.3 * (layer_idx + 1))
    lv = a_lambda.astype(jnp.float32)
    lam = jnp.exp(jnp.sum(lv[0] * lv[1])) - jnp.exp(jnp.sum(lv[2] * lv[3])) + lam_init
    ka_all = jnp.concatenate([a_k_past, ka], axis=1)
    va_all = jnp.concatenate([a_v_past, va], axis=1)
    oa = over_blocks(lambda qp, qq: diff_attn_block(qp, qq, ka_all, va_all, lam), q_pos, qa)
    oa = (rms_norm(oa, a_subln_g) * (1.0 - lam_init)).reshape(nb, T, W_A)

    qb = qb.reshape(nb, T, N_B_HEADS, D_B_HEAD)
    kb = kb.reshape(nb, T, N_B_HEADS, D_B_HEAD)
    vb = vb.reshape(nb, T, N_B_HEADS, D_B_HEAD)
    qi = qi.reshape(nb, T, N_IDX_HEADS, D_IDX)
    kb_all = jnp.concatenate([b_k_past, kb], axis=1)
    vb_all = jnp.concatenate([b_v_past, vb], axis=1)
    ki_all = jnp.concatenate([b_ki_past, ki], axis=1)
    topk = min(DSA_TOPK, (P + T) // 4)
    ob = over_blocks(lambda qp, qq, q_i, w_i: dsa_block(qp, qq, q_i, w_i, kb_all, vb_all, ki_all, topk),
                     q_pos, qb, qi, wi).reshape(nb, T, W_B)

    uc_ext = jnp.concatenate([c_buf, cc * cx], axis=1)
    oc = cb * causal_dwconv(uc_ext, c_conv_w)
    c_new = uc_ext[:, -(C_WIDTH - 1):]

    ud_ext = jnp.concatenate([d_buf, da * jax.nn.sigmoid(dg)], axis=1)
    dconv = causal_dwconv(ud_ext, d_conv_w) + d_conv_b
    od = jax.nn.silu(layer_norm(dconv, d_ln_g, d_ln_b)) @ d_pw_w + d_pw_b
    d_new = ud_ext[:, -(D_WIDTH - 1):]

    mix = jnp.concatenate([oa, ob, oc, od], axis=-1) @ w_out
    h = layer_norm(DN_ALPHA * x + mix, ln1_g, ln1_b)
    f = over_blocks(lambda hh: peer(hh, peer_wq, peer_subkeys, peer_u, peer_v), h)
    h = layer_norm(DN_ALPHA * h + f, ln2_g, ln2_b)
    y = h + jax.nn.sigmoid(h @ ple_gate_w + ple_gate_b) * (p @ ple_w)
    return y, (ka, va, kb, vb, ki, c_new, d_new)


def setup_inputs(seed: int = 0) -> dict:
    key = jax.random.key(seed)
    ks = iter(jax.random.split(key, 48))
    nrm = lambda shape, scale: jax.random.normal(next(ks), shape, jnp.float32) * scale
    n_pages = PAST_LEN // PAGE_SIZE
    n_used = DEC_BATCH * n_pages
    n_pool = n_used + n_used // 4
    page_table = jax.random.permutation(next(ks), n_pool)[:n_used].reshape(DEC_BATCH, n_pages).astype(jnp.int32)
    return {
        "x_prompt": nrm((BATCH, SEQ, D_MODEL), 1.0),
        "x_sample": nrm((DEC_BATCH, DEC_SEQ, D_MODEL), 1.0),
        "cache_a_k": nrm((DEPTH, n_pool, PAGE_SIZE, N_A_HEADS, 2, D_A_HEAD), 1.0),
        "cache_a_v": nrm((DEPTH, n_pool, PAGE_SIZE, N_A_HEADS, 2 * D_A_HEAD), 1.0),
        "cache_b_k": nrm((DEPTH, n_pool, PAGE_SIZE, N_B_HEADS, D_B_HEAD), 1.0),
        "cache_b_v": nrm((DEPTH, n_pool, PAGE_SIZE, N_B_HEADS, D_B_HEAD), 1.0),
        "cache_b_kidx": nrm((DEPTH, n_pool, PAGE_SIZE, D_IDX), 1.0),
        "state_c_conv": nrm((DEPTH, DEC_BATCH, C_WIDTH - 1, W_C), 1.0),
        "state_d_conv": nrm((DEPTH, DEC_BATCH, D_WIDTH - 1, W_D), 0.5),
        "page_table": page_table,
        "p_prompt": nrm((DEPTH, BATCH, SEQ, PLE_DIM), 1.0),
        "p_sample": nrm((DEPTH, DEC_BATCH, DEC_SEQ, PLE_DIM), 1.0),
        "w_in": nrm((DEPTH, D_MODEL, N_IN), D_MODEL ** -0.5),
        "a_lambda": nrm((DEPTH, 4, D_A_HEAD), 0.1),
        "a_subln_g": 1.0 + nrm((DEPTH, 2 * D_A_HEAD), 0.02),
        "c_conv_w": nrm((DEPTH, C_WIDTH, W_C), C_WIDTH ** -0.5),
        "d_conv_w": nrm((DEPTH, D_WIDTH, W_D), D_WIDTH ** -0.5),
        "d_conv_b": nrm((DEPTH, W_D), 0.02),
        "d_ln_g": 1.0 + nrm((DEPTH, W_D), 0.02),
        "d_ln_b": nrm((DEPTH, W_D), 0.02),
        "d_pw_w": nrm((DEPTH, W_D, W_D), W_D ** -0.5),
        "d_pw_b": nrm((DEPTH, W_D), 0.02),
        "w_out": nrm((DEPTH, D_MODEL, D_MODEL), DN_BETA * D_MODEL ** -0.5),
        "ln1_g": 1.0 + nrm((DEPTH, D_MODEL), 0.02),
        "ln1_b": nrm((DEPTH, D_MODEL), 0.02),
        "peer_wq": nrm((DEPTH, D_MODEL, PEER_HEADS * PEER_DK), D_MODEL ** -0.5),
        "peer_subkeys": nrm((DEPTH, 2, N_KEYS, PEER_DK // 2), (PEER_DK // 2) ** -0.5),
        "peer_u": nrm((DEPTH, N_EXPERTS, D_MODEL), D_MODEL ** -0.5),
        "peer_v": nrm((DEPTH, N_EXPERTS, D_MODEL), DN_BETA * PEER_HEADS ** -0.5),
        "ln2_g": 1.0 + nrm((DEPTH, D_MODEL), 0.02),
        "ln2_b": nrm((DEPTH, D_MODEL), 0.02),
        "ple_w": nrm((DEPTH, PLE_DIM, D_MODEL), PLE_DIM ** -0.5),
        "ple_gate_w": nrm((DEPTH, D_MODEL, D_MODEL), D_MODEL ** -0.5),
        "ple_gate_b": nrm((DEPTH, D_MODEL), 0.02),
    }


def reference(x_prompt, x_sample, cache_a_k, cache_a_v, cache_b_k, cache_b_v, cache_b_kidx,
              state_c_conv, state_d_conv, page_table, p_prompt, p_sample,
              w_in, a_lambda, a_subln_g, c_conv_w, d_conv_w, d_conv_b, d_ln_g, d_ln_b, d_pw_w, d_pw_b,
              w_out, ln1_g, ln1_b, peer_wq, peer_subkeys, peer_u, peer_v, ln2_g, ln2_b,
              ple_w, ple_gate_w, ple_gate_b):
    dt = x_prompt.dtype
    nbp = x_prompt.shape[0]
    xp, xs = x_prompt, x_sample
    news_p, news_s = [], []
    for l in range(DEPTH):
        lw = (w_in[l], a_lambda[l], a_subln_g[l], c_conv_w[l], d_conv_w[l], d_conv_b[l], d_ln_g[l], d_ln_b[l],
              d_pw_w[l], d_pw_b[l], w_out[l], ln1_g[l], ln1_b[l], peer_wq[l], peer_subkeys[l], peer_u[l],
              peer_v[l], ln2_g[l], ln2_b[l], ple_w[l], ple_gate_w[l], ple_gate_b[l])
        past_p = (jnp.zeros((nbp, 0, N_A_HEADS, 2, D_A_HEAD), dt),
                  jnp.zeros((nbp, 0, N_A_HEADS, 2 * D_A_HEAD), dt),
                  jnp.zeros((nbp, 0, N_B_HEADS, D_B_HEAD), dt),
                  jnp.zeros((nbp, 0, N_B_HEADS, D_B_HEAD), dt),
                  jnp.zeros((nbp, 0, D_IDX), dt),
                  jnp.zeros((nbp, C_WIDTH - 1, W_C), dt),
                  jnp.zeros((nbp, D_WIDTH - 1, W_D), dt))
        past_s = (gather_pages(cache_a_k[l], page_table),
                  gather_pages(cache_a_v[l], page_table),
                  gather_pages(cache_b_k[l], page_table),
                  gather_pages(cache_b_v[l], page_table),
                  gather_pages(cache_b_kidx[l], page_table),
                  state_c_conv[l], state_d_conv[l])
        xp, new_p = trunk_layer(l, xp, p_prompt[l], past_p, lw)
        xs, new_s = trunk_layer(l, xs, p_sample[l], past_s, lw)
        news_p.append(new_p)
        news_s.append(new_s)
    st = lambda news, i: jnp.stack([n[i] for n in news], axis=0)
    return (xp, xs,
            st(news_p, 0), st(news_s, 0),
            st(news_p, 1), st(news_s, 1),
            st(news_p, 2), st(news_s, 2),
            st(news_p, 3), st(news_s, 3),
            st(news_p, 4), st(news_s, 4),
            st(news_p, 5), st(news_s, 5),
            st(news_p, 6), st(news_s, 6))
```

```python
import functools
import math

import jax
import jax.numpy as jnp
from jax import lax
from jax.experimental import pallas as pl
from jax.experimental.pallas import tpu as pltpu

F32 = jnp.float32
BF16 = jnp.bfloat16
I32 = jnp.int32

LN_EPS = 1e-5
N_HEADS = 4
HEAD_W = 64
GROUP_W = 256
D_IDX = 64
DSA_TOPK = 256
C_WIDTH = 3
D_WIDTH = 31
PEER_HEADS = 8
PEER_TOPK = 16
N_KEYS = 128
NEG_BIG = -1e30
INT_MIN = -2 ** 31
VMEM_LIMIT = 56 * 1024 * 1024

_SEG = dict(qa=0, ka=256, va=512, qb=768, kb=1024, vb=1280, qi=1536, kiwi=1792,
            cb=1920, cc=2176, cx=2432, da=2688, dg=2944)
_N_PACKED = 3200

_NT = (((1,), (1,)), ((), ()))


def _cparams(sem):
    return pltpu.CompilerParams(dimension_semantics=sem, vmem_limit_bytes=VMEM_LIMIT)


def _split_bf16(x):
    hi = x.astype(BF16)
    lo = (x - hi.astype(F32)).astype(BF16)
    return hi, lo


def _layer_norm(x, g, b):
    mu = jnp.mean(x, axis=-1, keepdims=True)
    xc = x - mu
    var = jnp.mean(xc * xc, axis=-1, keepdims=True)
    return xc * lax.rsqrt(var + LN_EPS) * g + b


def _sigmoid(x):
    return 1.0 / (1.0 + jnp.exp(-x))


def _proj_kernel(x_ref, w_ref, ka_ref, va_ref, kb_ref, vb_ref, kiwi_ref, qa_ref, qb_ref, qi_ref,
                 kab_ref, vab_ref, kbb_ref, vbb_ref, cb_ref, uc_ref, ud_ref, *, scale_a, scale_b):
    xb = x_ref[...].astype(BF16)

    def seg(name, width=GROUP_W):
        off = _SEG[name]
        return jnp.dot(xb, w_ref[:, off:off + width], preferred_element_type=F32)

    ka = seg("ka"); va = seg("va"); kb = seg("kb"); vb = seg("vb")
    ka_ref[...] = ka; va_ref[...] = va; kb_ref[...] = kb; vb_ref[...] = vb
    kab_ref[...] = ka.astype(BF16); vab_ref[...] = va.astype(BF16)
    kbb_ref[...] = kb.astype(BF16); vbb_ref[...] = vb.astype(BF16)
    kiwi_ref[...] = seg("kiwi", 128)
    qa_ref[...] = seg("qa") * scale_a
    qb_ref[...] = seg("qb") * scale_b
    qi_ref[...] = seg("qi")
    cb_ref[...] = seg("cb")
    uc_ref[...] = seg("cc") * seg("cx")
    ud_ref[...] = seg("da") * _sigmoid(seg("dg"))


def _project(x, w_packed, tm):
    n, d = x.shape
    f32o = lambda w: jax.ShapeDtypeStruct((n, w), F32)
    bfo = lambda w: jax.ShapeDtypeStruct((n, w), BF16)
    out_shape = [f32o(256)] * 4 + [f32o(128)] + [f32o(256)] * 3 + [bfo(256)] * 4 + [f32o(256)] * 3
    ospec = lambda w: pl.BlockSpec((tm, w), lambda i: (i, 0))
    out_specs = [ospec(256)] * 4 + [ospec(128)] + [ospec(256)] * 3 + [ospec(256)] * 4 + [ospec(256)] * 3
    kern = functools.partial(_proj_kernel, scale_a=float(32 ** -0.5), scale_b=float(HEAD_W ** -0.5))
    return pl.pallas_call(
        kern, grid=(n // tm,),
        in_specs=[pl.BlockSpec((tm, d), lambda i: (i, 0)),
                  pl.BlockSpec((d, _N_PACKED), lambda i: (0, 0))],
        out_specs=out_specs, out_shape=out_shape,
        compiler_params=_cparams(("parallel",)), name="proj",
    )(x, w_packed)


def _gather_kernel(pt_ref, ak, av, bk, bv, bki, nak, nav, nbk, nbv, nki,
                   o_ak, o_av, o_bk, o_bv, o_ki, *, n_pages, page, t_new):
    j = pl.program_id(1)

    @pl.when(j < n_pages)
    def _():
        o_ak[0] = ak[0].astype(BF16)
        o_av[0] = av[0].astype(BF16)
        o_bk[0] = bk[0].astype(BF16)
        o_bv[0] = bv[0].astype(BF16)
        o_ki[0] = bki[0]

    @pl.when(j == n_pages)
    def _():
        def tail(new):
            z = jnp.zeros((page - t_new, new.shape[-1]), F32)
            return jnp.concatenate([new[0], z], axis=0)
        o_ak[0] = tail(nak).astype(BF16)
        o_av[0] = tail(nav).astype(BF16)
        o_bk[0] = tail(nbk).astype(BF16)
        o_bv[0] = tail(nbv).astype(BF16)
        o_ki[0] = tail(nki)


def _gather_pages(page_table, caches, news):
    bsz, n_pages = page_table.shape
    page = caches[0].shape[1]
    t_new = news[0].shape[1]
    s_tot = (n_pages + 1) * page
    cache_spec = lambda c: pl.BlockSpec(
        (1, page, c), lambda b, j, pt: (pt[b, jnp.minimum(j, n_pages - 1)], 0, 0))
    new_spec = lambda c: pl.BlockSpec((1, t_new, c), lambda b, j, pt: (b, 0, 0))
    out_spec = lambda c: pl.BlockSpec((1, page, c), lambda b, j, pt: (b, j, 0))
    widths = [a.shape[-1] for a in caches]
    out_dt = [BF16, BF16, BF16, BF16, F32]
    kern = functools.partial(_gather_kernel, n_pages=n_pages, page=page, t_new=t_new)
    return pl.pallas_call(
        kern,
        grid_spec=pltpu.PrefetchScalarGridSpec(
            num_scalar_prefetch=1, grid=(bsz, n_pages + 1),
            in_specs=[cache_spec(c) for c in widths] + [new_spec(c) for c in widths],
            out_specs=[out_spec(c) for c in widths]),
        out_shape=[jax.ShapeDtypeStruct((bsz, s_tot, c), dt) for c, dt in zip(widths, out_dt)],
        compiler_params=_cparams(("parallel", "arbitrary")), name="gather_pages",
    )(page_table, *caches, *news)


def _head_bcast(cols, lane_head):
    return jnp.where(lane_head == 0, cols[0],
                     jnp.where(lane_head == 1, cols[1],
                               jnp.where(lane_head == 2, cols[2], cols[3])))


def _flash(q_stack, n_groups, k_ref, v_ref, n_kv, tq, tk, mask_fn, m_ref, l_ref, acc_ref, zero_masked):
    n_comp = n_groups // N_HEADS
    lane_head = lax.broadcasted_iota(I32, (1, GROUP_W), 1) // HEAD_W
    m_ref[...] = jnp.full(m_ref.shape, NEG_BIG, F32)
    l_ref[...] = jnp.zeros(l_ref.shape, F32)
    acc_ref[...] = jnp.zeros(acc_ref.shape, F32)

    def body(c, carry):
        k0 = pl.multiple_of(c * tk, tk)
        kblk = k_ref[0, pl.ds(k0, tk), :]
        vblk = v_ref[0, pl.ds(k0, tk), :]
        s = lax.dot_general(q_stack, kblk, _NT, preferred_element_type=F32)
        s = s.reshape(n_groups, tq, tk)
        mask = mask_fn(c, k0)[None]
        s = jnp.where(mask, s, NEG_BIG)
        m_old = m_ref[...]
        m_new = jnp.maximum(m_old, jnp.max(s, axis=2, keepdims=True))
        alpha = jnp.exp(m_old - m_new)
        p = jnp.exp(s - m_new)
        if zero_masked:
            p = jnp.where(mask, p, 0.0)
        l_ref[...] = alpha * l_ref[...] + jnp.sum(p, axis=2, keepdims=True)
        m_ref[...] = m_new
        vstack = jnp.concatenate(
            [jnp.where(lane_head == h, vblk, jnp.zeros_like(vblk)) for h in range(N_HEADS)], axis=0)
        for comp in range(n_comp):
            base = comp * N_HEADS
            pcat = jnp.concatenate([p[base + h] for h in range(N_HEADS)], axis=1).astype(BF16)
            pv = jnp.dot(pcat, vstack, preferred_element_type=F32)
            af = _head_bcast([alpha[base + h] for h in range(N_HEADS)], lane_head)
            acc_ref[comp] = acc_ref[comp] * af + pv
        return carry

    lax.fori_loop(0, n_kv, body, 0)
    return lane_head


def _num_kv_blocks(q0, tq, tk, past, s_tot):
    return jnp.minimum((past + q0 + tq + tk - 1) // tk, s_tot // tk)


def _attn_a_kernel(q_ref, k_ref, v_ref, lam_ref, g_ref, gm_ref, o_ref, m_ref, l_ref, acc_ref,
                   *, tq, tk, past, s_tot, lam_init):
    q0 = pl.program_id(1) * tq
    q = q_ref[0]
    lane_grp = lax.broadcasted_iota(I32, (1, GROUP_W), 1) // 32
    q_stack = jnp.concatenate(
        [jnp.where(lane_grp == (2 * h + c), q, 0.0) for c in range(2) for h in range(N_HEADS)],
        axis=0).astype(BF16)
    n_kv = _num_kv_blocks(q0, tq, tk, past, s_tot)
    row_pos = past + q0 + lax.broadcasted_iota(I32, (tq, 1), 0)

    def mask_fn(c, k0):
        return (k0 + lax.broadcasted_iota(I32, (1, tk), 1)) <= row_pos

    lane_head = _flash(q_stack, 2 * N_HEADS, k_ref, v_ref, n_kv, tq, tk, mask_fn,
                       m_ref, l_ref, acc_ref, zero_masked=False)
    rl = 1.0 / l_ref[...]
    o0 = acc_ref[0] * _head_bcast([rl[h] for h in range(N_HEADS)], lane_head)
    o1 = acc_ref[1] * _head_bcast([rl[N_HEADS + h] for h in range(N_HEADS)], lane_head)
    lv = lam_ref[...]
    lam = (jnp.exp(jnp.sum(lv[0:1] * lv[1:2], axis=1, keepdims=True))
           - jnp.exp(jnp.sum(lv[2:3] * lv[3:4], axis=1, keepdims=True)) + lam_init)
    o = o0 - lam * o1
    hi, lo = _split_bf16(o * o)
    ms = (jnp.dot(hi, gm_ref[...], preferred_element_type=F32)
          + jnp.dot(lo, gm_ref[...], preferred_element_type=F32))
    o_ref[0] = o * lax.rsqrt(ms + LN_EPS) * g_ref[...] * (1.0 - lam_init)


def _attention_a(q, k, v, a_lambda, g_full, gmean, *, tq, tk, past, lam_init):
    bsz, t, _ = q.shape
    s_tot = k.shape[1]
    kern = functools.partial(_attn_a_kernel, tq=tq, tk=tk, past=past, s_tot=s_tot, lam_init=lam_init)
    return pl.pallas_call(
        kern, grid=(bsz, t // tq),
        in_specs=[pl.BlockSpec((1, tq, GROUP_W), lambda b, i: (b, i, 0)),
                  pl.BlockSpec((1, s_tot, GROUP_W), lambda b, i: (b, 0, 0)),
                  pl.BlockSpec((1, s_tot, GROUP_W), lambda b, i: (b, 0, 0)),
                  pl.BlockSpec(a_lambda.shape, lambda b, i: (0, 0)),
                  pl.BlockSpec((1, GROUP_W), lambda b, i: (0, 0)),
                  pl.BlockSpec((GROUP_W, GROUP_W), lambda b, i: (0, 0))],
        out_specs=pl.BlockSpec((1, tq, GROUP_W), lambda b, i: (b, i, 0)),
        out_shape=jax.ShapeDtypeStruct((bsz, t, GROUP_W), F32),
        scratch_shapes=[pltpu.VMEM((2 * N_HEADS, tq, 1), F32), pltpu.VMEM((2 * N_HEADS, tq, 1), F32),
                        pltpu.VMEM((2, tq, GROUP_W), F32)],
        compiler_params=_cparams(("parallel", "arbitrary")), name="attn_a",
    )(q, k, v, a_lambda, g_full, gmean)


def _dsa_kernel(q_ref, k_ref, v_ref, qi_ref, ki_ref, wi_ref, o_ref, key_ref, y_ref, m_ref, l_ref, acc_ref,
                *, tq, tk, past, s_tot, topk, idx_bits):
    q0 = pl.program_id(1) * tq
    n_kv = _num_kv_blocks(q0, tq, tk, past, s_tot)
    row_pos = past + q0 + lax.broadcasted_iota(I32, (tq, 1), 0)
    col_iota = lax.broadcasted_iota(I32, (1, tk), 1)
    int_min = jnp.int32(INT_MIN)

    qi = qi_ref[0].reshape(N_HEADS * tq, 3 * D_IDX).astype(BF16)
    wi = wi_ref[0]
    w_h = [wi[:, h:h + 1] for h in range(N_HEADS)]

    def score_body(c, carry):
        k0 = pl.multiple_of(c * tk, tk)
        kib = ki_ref[0, pl.ds(k0, tk), :]
        sc = lax.dot_general(qi, kib, _NT, preferred_element_type=F32).reshape(N_HEADS, tq, tk)
        isc = jnp.maximum(sc[0], 0.0) * w_h[0]
        for h in range(1, N_HEADS):
            isc = isc + jnp.maximum(sc[h], 0.0) * w_h[h]
        isc = jnp.where(isc == 0.0, 0.0, isc)
        bits = pltpu.bitcast(isc, I32)
        key = bits ^ ((bits >> 31) & jnp.int32(0x7FFFFFFF))
        key_ref[c] = jnp.where((k0 + col_iota) <= row_pos, key, int_min)
        return carry

    lax.fori_loop(0, n_kv, score_body, 0)

    def count(pred):
        def body(c, acc):
            k0 = c * tk
            ones = jnp.where(pred(key_ref[c], k0), 1.0, 0.0)
            part = ones[:, 0:128]
            for i in range(1, tk // 128):
                part = part + ones[:, 128 * i:128 * (i + 1)]
            return acc + part
        acc = lax.fori_loop(0, n_kv, body, jnp.zeros((tq, 128), F32))
        return jnp.sum(acc, axis=1, keepdims=True)

    kk = jnp.float32(topk)

    def bit_body(r, prefix):
        cand_u = prefix | lax.shift_left(jnp.int32(1), 31 - r)
        cand_s = cand_u ^ int_min
        cnt = count(lambda blk, k0: blk >= cand_s)
        return jnp.where(cnt >= kk, cand_u, prefix)

    prefix = lax.fori_loop(0, 32, bit_body, jnp.zeros((tq, 1), I32))
    thr = prefix ^ int_min

    n_gt = count(lambda blk, k0: blk > thr)
    n_ge = count(lambda blk, k0: blk >= thr)
    r_need = kk - n_gt
    any_tie = jnp.max(jnp.where(n_ge > kk, 1.0, 0.0)) > 0.0
    y_ref[...] = jnp.full((tq, 1), s_tot, I32)

    @pl.when(any_tie)
    def _():
        def ybody(r, y):
            cand = y | lax.shift_left(jnp.int32(1), idx_bits - 1 - r)
            cnt = count(lambda blk, k0: (blk == thr) & ((k0 + col_iota) < cand))
            return jnp.where(cnt < r_need, cand, y)
        y_ref[...] = lax.fori_loop(0, idx_bits, ybody, jnp.zeros((tq, 1), I32))

    y_last = y_ref[...]

    def mask_fn(c, k0):
        blk = key_ref[c]
        col = k0 + col_iota
        sel = (blk > thr) | ((blk == thr) & (col <= y_last))
        return sel & (col <= row_pos)

    q = q_ref[0]
    lane_head0 = lax.broadcasted_iota(I32, (1, GROUP_W), 1) // HEAD_W
    q_stack = jnp.concatenate([jnp.where(lane_head0 == h, q, 0.0) for h in range(N_HEADS)],
                              axis=0).astype(BF16)
    lane_head = _flash(q_stack, N_HEADS, k_ref, v_ref, n_kv, tq, tk, mask_fn,
                       m_ref, l_ref, acc_ref, zero_masked=True)
    rl = 1.0 / l_ref[...]
    o_ref[0] = acc_ref[0] * _head_bcast([rl[h] for h in range(N_HEADS)], lane_head)


def _dsa(q, k, v, qi3, ki3, wi, *, tq, tk, past, topk):
    bsz, t, _ = q.shape
    s_tot = k.shape[1]
    idx_bits = int(math.ceil(math.log2(s_tot))) + 1
    kern = functools.partial(_dsa_kernel, tq=tq, tk=tk, past=past, s_tot=s_tot, topk=topk, idx_bits=idx_bits)
    return pl.pallas_call(
        kern, grid=(bsz, t // tq),
        in_specs=[pl.BlockSpec((1, tq, GROUP_W), lambda b, i: (b, i, 0)),
                  pl.BlockSpec((1, s_tot, GROUP_W), lambda b, i: (b, 0, 0)),
                  pl.BlockSpec((1, s_tot, GROUP_W), lambda b, i: (b, 0, 0)),
                  pl.BlockSpec((1, N_HEADS, tq, 3 * D_IDX), lambda b, i: (b, 0, i, 0)),
                  pl.BlockSpec((1, s_tot, 3 * D_IDX), lambda b, i: (b, 0, 0)),
                  pl.BlockSpec((1, tq, N_HEADS), lambda b, i: (b, i, 0))],
        out_specs=pl.BlockSpec((1, tq, GROUP_W), lambda b, i: (b, i, 0)),
        out_shape=jax.ShapeDtypeStruct((bsz, t, GROUP_W), F32),
        scratch_shapes=[pltpu.VMEM((s_tot // tk, tq, tk), I32), pltpu.VMEM((tq, 1), I32),
                        pltpu.VMEM((N_HEADS, tq, 1), F32), pltpu.VMEM((N_HEADS, tq, 1), F32),
                        pltpu.VMEM((1, tq, GROUP_W), F32)],
        compiler_params=_cparams(("parallel", "arbitrary")), name="dsa",
    )(q, k, v, qi3, ki3, wi)


_HALO_C = 8
_HALO_D = 32


def _conv_kernel(*refs, tm, single_tile):
    if single_tile:
        (cb_ref, uc_ref, ud_ref, hc_ref, hd_ref, cw_ref, dw_ref, db_ref, lg_ref, lb_ref,
         oc_ref, od_ref, ec_ref, ed_ref) = refs
        pc_ref = pd_ref = None
    else:
        (cb_ref, uc_ref, ud_ref, pc_ref, pd_ref, hc_ref, hd_ref, cw_ref, dw_ref, db_ref, lg_ref, lb_ref,
         oc_ref, od_ref, ec_ref, ed_ref) = refs
    i = pl.program_id(1)

    @pl.when(i == 0)
    def _():
        ec_ref[0:_HALO_C] = hc_ref[0]
        ed_ref[0:_HALO_D] = hd_ref[0]

    if not single_tile:
        @pl.when(i > 0)
        def _():
            ec_ref[0:_HALO_C] = pc_ref[0]
            ed_ref[0:_HALO_D] = pd_ref[0]

    ec_ref[_HALO_C:_HALO_C + tm] = uc_ref[0]
    ed_ref[_HALO_D:_HALO_D + tm] = ud_ref[0]

    cw = cw_ref[...]
    acc = ec_ref[pl.ds(_HALO_C - (C_WIDTH - 1), tm), :] * cw[0:1]
    for w in range(1, C_WIDTH):
        acc = acc + ec_ref[pl.ds(_HALO_C - (C_WIDTH - 1) + w, tm), :] * cw[w:w + 1]
    oc_ref[0] = cb_ref[0] * acc

    dw = dw_ref[...]
    acc = ed_ref[pl.ds(_HALO_D - (D_WIDTH - 1), tm), :] * dw[0:1]
    for w in range(1, D_WIDTH):
        acc = acc + ed_ref[pl.ds(_HALO_D - (D_WIDTH - 1) + w, tm), :] * dw[w:w + 1]
    z = _layer_norm(acc + db_ref[...], lg_ref[...], lb_ref[...])
    od_ref[0] = z * _sigmoid(z)


def _convs(cb, uc, ud, hist_c, hist_d, c_w, d_w, d_b, ln_g, ln_b, *, tm):
    bsz, t, w = cb.shape
    nt = t // tm
    single = nt == 1
    main = pl.BlockSpec((1, tm, w), lambda b, i: (b, i, 0))
    prev_c = pl.BlockSpec((1, _HALO_C, w), lambda b, i: (b, jnp.maximum(i * (tm // _HALO_C) - 1, 0), 0))
    prev_d = pl.BlockSpec((1, _HALO_D, w), lambda b, i: (b, jnp.maximum(i * (tm // _HALO_D) - 1, 0), 0))
    hist_cs = pl.BlockSpec((1, _HALO_C, w), lambda b, i: (b, 0, 0))
    hist_ds = pl.BlockSpec((1, _HALO_D, w), lambda b, i: (b, 0, 0))
    full2 = lambda a: pl.BlockSpec(a.shape, lambda b, i: (0, 0))
    in_specs = [main, main, main] + ([] if single else [prev_c, prev_d]) + [hist_cs, hist_ds] + \
        [full2(c_w), full2(d_w), full2(d_b), full2(ln_g), full2(ln_b)]
    args = [cb, uc, ud] + ([] if single else [uc, ud]) + [hist_c, hist_d, c_w, d_w, d_b, ln_g, ln_b]
    kern = functools.partial(_conv_kernel, tm=tm, single_tile=single)
    return pl.pallas_call(
        kern, grid=(bsz, nt), in_specs=in_specs,
        out_specs=[main, main],
        out_shape=[jax.ShapeDtypeStruct((bsz, t, w), F32)] * 2,
        scratch_shapes=[pltpu.VMEM((_HALO_C + tm, w), F32), pltpu.VMEM((_HALO_D + tm, w), F32)],
        compiler_params=_cparams(("parallel", "arbitrary")), name="convs",
    )(*args)


def _mix_kernel(x_ref, oa_ref, ob_ref, oc_ref, od_ref, pw_ref, pwb_ref, wo_ref, g_ref, b_ref, h_ref, *, alpha):
    od = jnp.dot(od_ref[...].astype(BF16), pw_ref[...], preferred_element_type=F32) + pwb_ref[...]
    cat = jnp.concatenate([oa_ref[...], ob_ref[...], oc_ref[...], od], axis=1).astype(BF16)
    mix = jnp.dot(cat, wo_ref[...], preferred_element_type=F32)
    h_ref[...] = _layer_norm(alpha * x_ref[...] + mix, g_ref[...], b_ref[...])


def _mix(x, oa, ob, oc, od, pw, pwb, wo, g, b, *, tm, alpha):
    n, d = x.shape
    row = lambda w: pl.BlockSpec((tm, w), lambda i: (i, 0))
    full = lambda a: pl.BlockSpec(a.shape, lambda i: (0, 0))
    return pl.pallas_call(
        functools.partial(_mix_kernel, alpha=alpha), grid=(n // tm,),
        in_specs=[row(d), row(GROUP_W), row(GROUP_W), row(GROUP_W), row(GROUP_W),
                  full(pw), full(pwb), full(wo), full(g), full(b)],
        out_specs=row(d), out_shape=jax.ShapeDtypeStruct((n, d), F32),
        compiler_params=_cparams(("parallel",)), name="mix",
    )(x, oa, ob, oc, od, pw, pwb, wo, g, b)


def _staircase():
    return [(a, b) for a in range(PEER_TOPK) for b in range(PEER_TOPK) if (a + 1) * (b + 1) <= PEER_TOPK]


def _extract_top(x, payload, n_out, n_rows):
    row = lax.broadcasted_iota(I32, x.shape, 0).astype(F32)
    vals, idxs, pays = [], [], []
    for _ in range(n_out):
        m = jnp.max(x, axis=0, keepdims=True)
        idx = jnp.min(jnp.where(x == m, row, float(n_rows)), axis=0, keepdims=True)
        hit = row == idx
        vals.append(m)
        idxs.append(idx)
        if payload is not None:
            pays.append(jnp.max(jnp.where(hit, payload, -1.0), axis=0, keepdims=True))
        x = jnp.where(hit, -jnp.inf, x)
    return vals, idxs, pays


def _peer_select_kernel(h_ref, wq_ref, sk_ref, eid_ref, g_ref, sv_ref, si_ref, *, tm):
    ht = h_ref[...].T.astype(BF16)
    n_groups = 2 * PEER_HEADS

    def group_body(g, carry):
        r0 = pl.multiple_of(g * N_KEYS, N_KEYS)
        qg = jnp.dot(wq_ref[pl.ds(r0, N_KEYS), :], ht, preferred_element_type=F32)
        st = jnp.dot(sk_ref[g % 2], qg.astype(BF16), preferred_element_type=F32)
        vals, idxs, _ = _extract_top(st, None, PEER_TOPK, N_KEYS)
        sv_ref[g] = jnp.concatenate(vals, axis=0)
        si_ref[g] = jnp.concatenate(idxs, axis=0)
        return carry

    lax.fori_loop(0, n_groups, group_body, 0)

    pairs = _staircase()
    n_cand = len(pairs)
    n_pad = (-n_cand) % 8

    def head_body(hd, carry):
        sv0 = sv_ref[2 * hd]; sv1 = sv_ref[2 * hd + 1]
        si0 = si_ref[2 * hd]; si1 = si_ref[2 * hd + 1]
        cv = [sv0[a:a + 1] + sv1[b:b + 1] for a, b in pairs]
        ce = [si0[a:a + 1] * float(N_KEYS) + si1[b:b + 1] for a, b in pairs]
        if n_pad:
            cv.append(jnp.full((n_pad, tm), -jnp.inf, F32))
            ce.append(jnp.full((n_pad, tm), -1.0, F32))
        cand = jnp.concatenate(cv, axis=0)
        cand_e = jnp.concatenate(ce, axis=0)
        vals, _, pays = _extract_top(cand, cand_e, PEER_TOPK, n_cand + n_pad)
        fv = jnp.concatenate(vals, axis=0)
        e = jnp.exp(fv - fv[0:1])
        gate = e / jnp.sum(e, axis=0, keepdims=True)
        r0 = pl.multiple_of(hd * PEER_TOPK, PEER_TOPK)
        eid_ref[pl.ds(r0, PEER_TOPK), :] = jnp.concatenate(pays, axis=0).astype(I32)
        g_ref[pl.ds(r0, PEER_TOPK), :] = gate
        return carry

    lax.fori_loop(0, PEER_HEADS, head_body, 0)


def _peer_select(h, wq_t, sk, *, tm):
    n, d = h.shape
    n_slots = PEER_HEADS * PEER_TOPK
    return pl.pallas_call(
        functools.partial(_peer_select_kernel, tm=tm), grid=(n // tm,),
        in_specs=[pl.BlockSpec((tm, d), lambda i: (i, 0)),
                  pl.BlockSpec(wq_t.shape, lambda i: (0, 0)),
                  pl.BlockSpec(sk.shape, lambda i: (0, 0, 0))],
        out_specs=[pl.BlockSpec((n_slots, tm), lambda i: (0, i)),
                   pl.BlockSpec((n_slots, tm), lambda i: (0, i))],
        out_shape=[jax.ShapeDtypeStruct((n_slots, n), I32), jax.ShapeDtypeStruct((n_slots, n), F32)],
        scratch_shapes=[pltpu.VMEM((2 * PEER_HEADS, PEER_TOPK, tm), F32),
                        pltpu.VMEM((2 * PEER_HEADS, PEER_TOPK, tm), F32)],
        compiler_params=_cparams(("parallel",)), name="peer_select",
    )(h, wq_t, sk)


def _peer_gates_kernel(eid_ref, g_ref, w_ref, *, tm):
    sub = lax.broadcasted_iota(I32, (N_KEYS, PEER_HEADS * PEER_TOPK), 0)

    def body(t, carry):
        e = eid_ref[pl.ds(t, 1), :]
        g = g_ref[pl.ds(t, 1), :]
        g_hi = g.astype(BF16).astype(F32)
        g_lo = g - g_hi
        oh_i = jnp.where((e >> 7) == sub, 1.0, 0.0).astype(BF16)
        hit_j = (e & (N_KEYS - 1)) == sub
        gj_hi = jnp.where(hit_j, g_hi, 0.0).astype(BF16)
        gj_lo = jnp.where(hit_j, g_lo, 0.0).astype(BF16)
        lhs = jnp.concatenate([oh_i, oh_i], axis=1)
        rhs = jnp.concatenate([gj_hi, gj_lo], axis=1)
        w = lax.dot_general(lhs, rhs, _NT, preferred_element_type=F32)
        w_ref[t] = w.astype(BF16)
        return carry

    lax.fori_loop(0, tm, body, 0)


def _peer_gates(eid, gate, *, tm):
    n, n_slots = eid.shape
    return pl.pallas_call(
        functools.partial(_peer_gates_kernel, tm=tm), grid=(n // tm,),
        in_specs=[pl.BlockSpec((tm, n_slots), lambda i: (i, 0)),
                  pl.BlockSpec((tm, n_slots), lambda i: (i, 0))],
        out_specs=pl.BlockSpec((tm, N_KEYS, N_KEYS), lambda i: (i, 0, 0)),
        out_shape=jax.ShapeDtypeStruct((n, N_KEYS, N_KEYS), BF16),
        compiler_params=_cparams(("parallel",)), name="peer_gates",
    )(eid, gate)


def _gelu(x):
    return 0.5 * x * (1.0 + lax.erf(x * (2.0 ** -0.5)))


def _peer_dense_kernel(h_ref, w_ref, u_ref, v_ref, f_ref):
    @pl.when(pl.program_id(1) == 0)
    def _():
        f_ref[...] = jnp.zeros(f_ref.shape, F32)

    act = lax.dot_general(h_ref[...], u_ref[...], _NT, preferred_element_type=F32)
    y = (w_ref[...].astype(F32) * _gelu(act)).astype(BF16)
    f_ref[...] += jnp.dot(y, v_ref[...], preferred_element_type=F32)


def _peer_dense(hb, wmat, u, v, *, tm, te):
    n, d = hb.shape
    n_exp = u.shape[0]
    return pl.pallas_call(
        _peer_dense_kernel, grid=(n // tm, n_exp // te),
        in_specs=[pl.BlockSpec((tm, d), lambda i, e: (i, 0)),
                  pl.BlockSpec((tm, te), lambda i, e: (i, e)),
                  pl.BlockSpec((te, d), lambda i, e: (e, 0)),
                  pl.BlockSpec((te, d), lambda i, e: (e, 0))],
        out_specs=pl.BlockSpec((tm, d), lambda i, e: (i, 0)),
        out_shape=jax.ShapeDtypeStruct((n, d), F32),
        compiler_params=_cparams(("parallel", "arbitrary")), name="peer_dense",
    )(hb, wmat, u, v)


def _final_kernel(h_ref, f_ref, p_ref, g_ref, b_ref, wg_ref, bg_ref, wp_ref, y_ref, *, alpha):
    h2 = _layer_norm(alpha * h_ref[...] + f_ref[...], g_ref[...], b_ref[...])
    gate = _sigmoid(jnp.dot(h2.astype(BF16), wg_ref[...], preferred_element_type=F32) + bg_ref[...])
    emb = jnp.dot(p_ref[...].astype(BF16), wp_ref[...], preferred_element_type=F32)
    y_ref[...] = h2 + gate * emb


def _final(h, f, p, g, b, wg, bg, wp, *, tm, alpha):
    n, d = h.shape
    row = lambda w: pl.BlockSpec((tm, w), lambda i: (i, 0))
    full = lambda a: pl.BlockSpec(a.shape, lambda i: (0, 0))
    return pl.pallas_call(
        functools.partial(_final_kernel, alpha=alpha), grid=(n // tm,),
        in_specs=[row(d), row(d), row(p.shape[1]), full(g), full(b), full(wg), full(bg), full(wp)],
        out_specs=row(d), out_shape=jax.ShapeDtypeStruct((n, d), F32),
        compiler_params=_cparams(("parallel",)), name="final",
    )(h, f, p, g, b, wg, bg, wp)


def _pack_w_in(w_in):
    cols = [w_in[:, 0:1792], w_in[:, 1792:1860], jnp.zeros((w_in.shape[0], 60), w_in.dtype), w_in[:, 1860:3140]]
    return jnp.concatenate(cols, axis=1).astype(BF16)


def _row(a):
    return a.reshape(1, -1)


def _tile_rows(n, pref):
    t = pref
    while n % t:
        t //= 2
    return t


def _prep_weights(lw, depth):
    (w_in, a_lambda, a_subln_g, c_conv_w, d_conv_w, d_conv_b, d_ln_g, d_ln_b, d_pw_w, d_pw_b,
     w_out, ln1_g, ln1_b, peer_wq, peer_subkeys, peer_u, peer_v, ln2_g, ln2_b,
     ple_w, ple_gate_w, ple_gate_b) = lw
    head_of_lane = jnp.arange(GROUP_W) // HEAD_W
    gmean = (head_of_lane[:, None] == head_of_lane[None, :]).astype(BF16) * (1.0 / HEAD_W)
    return dict(
        w_in=_pack_w_in(w_in), a_lambda=a_lambda, a_g=_row(jnp.tile(a_subln_g, N_HEADS)), gmean=gmean.astype(BF16),
        c_w=c_conv_w, d_w=d_conv_w, d_b=_row(d_conv_b), d_ln_g=_row(d_ln_g), d_ln_b=_row(d_ln_b),
        d_pw=d_pw_w.astype(BF16), d_pwb=_row(d_pw_b), w_out=w_out.astype(BF16),
        ln1_g=_row(ln1_g), ln1_b=_row(ln1_b),
        wq_t=peer_wq.T.astype(BF16), sk=peer_subkeys.astype(BF16),
        u=peer_u.astype(BF16), v=peer_v.astype(BF16),
        ln2_g=_row(ln2_g), ln2_b=_row(ln2_b),
        ple_w=ple_w.astype(BF16), wg=ple_gate_w.astype(BF16), bg=_row(ple_gate_b),
        alpha=float((2 * depth) ** 0.25),
    )


def _index_operands(qi, ki):
    bsz, t, _ = qi.shape
    q4 = qi.reshape(bsz, t, N_HEADS, D_IDX)
    q_hi = q4.astype(BF16).astype(F32)
    q_lo = (q4 - q_hi).astype(BF16).astype(F32)
    qi3 = jnp.concatenate([q_hi, q_lo, q_hi], axis=-1).transpose(0, 2, 1, 3)
    k_hi = ki.astype(BF16)
    k_lo = (ki - k_hi.astype(F32)).astype(BF16)
    ki3 = jnp.concatenate([k_hi, k_hi, k_lo], axis=-1)
    return qi3, ki3


def _mixers(w, layer_idx, bsz, t_pad, proj, kv, hist_c, hist_d, *, past, topk, tq, tk, conv_tm):
    (ka, va, kb, vb, kiwi, qa, qb, qi, kab, vab, kbb, vbb, cb, uc, ud) = proj
    r3 = lambda a: a.reshape(bsz, t_pad, a.shape[-1])
    k_a, v_a, k_b, v_b, ki_all = kv
    lam_init = 0.8 - 0.6 * math.exp(-0.3 * (layer_idx + 1))
    oa = _attention_a(r3(qa), k_a, v_a, w["a_lambda"], w["a_g"], w["gmean"],
                      tq=tq, tk=tk, past=past, lam_init=lam_init)
    qi3, ki3 = _index_operands(r3(qi), ki_all)
    wi = r3(kiwi)[:, :, D_IDX:D_IDX + N_HEADS]
    ob = _dsa(r3(qb), k_b, v_b, qi3, ki3, wi, tq=tq, tk=tk, past=past, topk=topk)
    oc, od = _convs(r3(cb), r3(uc), r3(ud), hist_c, hist_d, w["c_w"], w["d_w"], w["d_b"],
                    w["d_ln_g"], w["d_ln_b"], tm=conv_tm)
    flat = lambda a: a.reshape(bsz * t_pad, a.shape[-1])
    return flat(oa), flat(ob), flat(oc), flat(od)


def _token_tail(w, x, oa, ob, oc, od, valid_rows, p):
    n = x.shape[0]
    h = _mix(x, oa, ob, oc, od, w["d_pw"], w["d_pwb"], w["w_out"], w["ln1_g"], w["ln1_b"],
             tm=_tile_rows(n, 512), alpha=w["alpha"])
    if valid_rows is not None:
        bsz, t_pad, t = valid_rows
        h = h.reshape(bsz, t_pad, -1)[:, :t].reshape(bsz * t, -1)
    n = h.shape[0]
    eid_t, gate_t = _peer_select(h, w["wq_t"], w["sk"], tm=_tile_rows(n, 256))
    wmat = _peer_gates(eid_t.T, gate_t.T, tm=_tile_rows(n, 128))
    wmat = wmat.reshape(n, N_KEYS * N_KEYS)
    f = _peer_dense(h.astype(BF16), wmat, w["u"], w["v"], tm=_tile_rows(n, 1024), te=512)
    return _final(h, f, p, w["ln2_g"], w["ln2_b"], w["wg"], w["bg"], w["ple_w"],
                  tm=_tile_rows(n, 512), alpha=w["alpha"])


def _layer(layer_idx, depth, xp, xs, caches, state_c, state_d, page_table, pp, ps, lw):
    w = _prep_weights(lw, depth)
    bp, tp, d = xp.shape
    bs, ts, _ = xs.shape
    ts_pad = 8
    n_pages = page_table.shape[1]
    page = caches[0].shape[1]
    past = n_pages * page

    proj_p = _project(xp.reshape(bp * tp, d), w["w_in"], _tile_rows(bp * tp, 512))
    r3p = lambda a: a.reshape(bp, tp, a.shape[-1])
    kv_p = (r3p(proj_p[8]), r3p(proj_p[9]), r3p(proj_p[10]), r3p(proj_p[11]), r3p(proj_p[4])[:, :, :D_IDX])
    zc = jnp.zeros((bp, _HALO_C, GROUP_W), F32)
    zd = jnp.zeros((bp, _HALO_D, GROUP_W), F32)
    mixed_p = _mixers(w, layer_idx, bp, tp, proj_p, kv_p, zc, zd, past=0, topk=min(DSA_TOPK, tp // 4),
                      tq=_tile_rows(tp, 128), tk=_tile_rows(tp, 256), conv_tm=_tile_rows(tp, 512))
    yp = _token_tail(w, xp.reshape(bp * tp, d), *mixed_p, None, pp.reshape(bp * tp, -1)).reshape(bp, tp, d)
    uc_p, ud_p = r3p(proj_p[13]), r3p(proj_p[14])
    new_p = (r3p(proj_p[0]), r3p(proj_p[1]), r3p(proj_p[2]), r3p(proj_p[3]), r3p(proj_p[4])[:, :, :D_IDX],
             jnp.concatenate([zc, uc_p], axis=1)[:, -(C_WIDTH - 1):],
             jnp.concatenate([zd, ud_p], axis=1)[:, -(D_WIDTH - 1):])

    xs_pad = jnp.pad(xs, ((0, 0), (0, ts_pad - ts), (0, 0)))
    proj_s = _project(xs_pad.reshape(bs * ts_pad, d), w["w_in"], _tile_rows(bs * ts_pad, 512))
    r3s = lambda a: a.reshape(bs, ts_pad, a.shape[-1])
    news = [r3s(proj_s[0]), r3s(proj_s[1]), r3s(proj_s[2]), r3s(proj_s[3]), r3s(proj_s[4])[:, :, :D_IDX]]
    kv_s = _gather_pages(page_table, caches, news)
    hist_c = jnp.concatenate([jnp.zeros((bs, _HALO_C - (C_WIDTH - 1), GROUP_W), F32), state_c], axis=1)
    hist_d = jnp.concatenate([jnp.zeros((bs, _HALO_D - (D_WIDTH - 1), GROUP_W), F32), state_d], axis=1)
    mixed_s = _mixers(w, layer_idx, bs, ts_pad, proj_s, kv_s, hist_c, hist_d, past=past,
                      topk=min(DSA_TOPK, (past + ts) // 4), tq=ts_pad, tk=page, conv_tm=ts_pad)
    ys = _token_tail(w, xs_pad.reshape(bs * ts_pad, d), *mixed_s, (bs, ts_pad, ts),
                     ps.reshape(bs * ts, -1)).reshape(bs, ts, d)
    uc_s, ud_s = r3s(proj_s[13])[:, :ts], r3s(proj_s[14])[:, :ts]
    new_s = tuple(a[:, :ts] for a in news) + (
        jnp.concatenate([state_c, uc_s], axis=1)[:, -(C_WIDTH - 1):],
        jnp.concatenate([state_d, ud_s], axis=1)[:, -(D_WIDTH - 1):])
    return yp, ys, new_p, new_s


def kernel(x_prompt, x_sample, cache_a_k, cache_a_v, cache_b_k, cache_b_v, cache_b_kidx, state_c_conv, state_d_conv, page_table, p_prompt, p_sample, w_in, a_lambda, a_subln_g, c_conv_w, d_conv_w, d_conv_b, d_ln_g, d_ln_b, d_pw_w, d_pw_b, w_out, ln1_g, ln1_b, peer_wq, peer_subkeys, peer_u, peer_v, ln2_g, ln2_b, ple_w, ple_gate_w, ple_gate_b):
    depth = w_in.shape[0]
    n_pool, page = cache_a_k.shape[1], cache_a_k.shape[2]
    xp, xs = x_prompt, x_sample
    news_p, news_s = [], []
    for l in range(depth):
        lw = (w_in[l], a_lambda[l], a_subln_g[l], c_conv_w[l], d_conv_w[l], d_conv_b[l], d_ln_g[l], d_ln_b[l],
              d_pw_w[l], d_pw_b[l], w_out[l], ln1_g[l], ln1_b[l], peer_wq[l], peer_subkeys[l], peer_u[l],
              peer_v[l], ln2_g[l], ln2_b[l], ple_w[l], ple_gate_w[l], ple_gate_b[l])
        caches = [cache_a_k[l].reshape(n_pool, page, GROUP_W), cache_a_v[l].reshape(n_pool, page, GROUP_W),
                  cache_b_k[l].reshape(n_pool, page, GROUP_W), cache_b_v[l].reshape(n_pool, page, GROUP_W),
                  cache_b_kidx[l]]
        xp, xs, new_p, new_s = _layer(l, depth, xp, xs, caches, state_c_conv[l], state_d_conv[l],
                                      page_table, p_prompt[l], p_sample[l], lw)
        news_p.append(new_p)
        news_s.append(new_s)

    bp, tp, _ = x_prompt.shape
    bs, ts, _ = x_sample.shape
    shapes = [(N_HEADS, 2, 32), (N_HEADS, HEAD_W), (N_HEADS, HEAD_W), (N_HEADS, HEAD_W), (D_IDX,)]

    def stack(news, i, lead):
        a = jnp.stack([n[i] for n in news], axis=0)
        return a.reshape((depth,) + lead + shapes[i]) if i < 5 else a

    outs = [xp, xs]
    for i in range(7):
        outs.append(stack(news_p, i, (bp, tp)))
        outs.append(stack(news_s, i, (bs, ts)))
    return tuple(outs)
```

```python
import functools
import math

import jax
import jax.numpy as jnp
from jax import lax
from jax.experimental import pallas as pl
from jax.experimental.pallas import tpu as pltpu

F32 = jnp.float32
BF16 = jnp.bfloat16
I32 = jnp.int32

LN_EPS = 1e-5
N_HEADS = 4
HEAD_W = 64
GROUP_W = 256
D_IDX = 64
DSA_TOPK = 256
C_WIDTH = 3
D_WIDTH = 31
PEER_HEADS = 8
PEER_TOPK = 16
N_KEYS = 128
LANES = 128
NEG_BIG = -1e30
INT_MIN = -2 ** 31
VMEM_LIMIT = 56 * 1024 * 1024

_SEG = dict(qa=0, ka=256, va=512, qb=768, kb=1024, vb=1280, qi=1536, kiwi=1792,
            cb=1920, cc=2176, cx=2432, da=2688, dg=2944)
_N_PACKED = 3200
_SEG_T = dict(qa=0, qb=256, qi=512, kiwi=768, va=896, vb=1152)
_N_PACKED_T = 1408

_NT = (((1,), (1,)), ((), ()))


def _cparams(sem):
    return pltpu.CompilerParams(dimension_semantics=sem, vmem_limit_bytes=VMEM_LIMIT)


def _split_bf16(x):
    hi = x.astype(BF16)
    lo = (x - hi.astype(F32)).astype(BF16)
    return hi, lo


def _layer_norm(x, g, b):
    mu = jnp.mean(x, axis=-1, keepdims=True)
    xc = x - mu
    var = jnp.mean(xc * xc, axis=-1, keepdims=True)
    return xc * lax.rsqrt(var + LN_EPS) * g + b


def _sigmoid(x):
    return 1.0 / (1.0 + jnp.exp(-x))


def _proj_kernel(x_ref, w_ref, wt_ref, ka_ref, va_ref, kb_ref, vb_ref, kiwi_ref, kab_ref, kbb_ref,
                 cb_ref, uc_ref, ud_ref, qat_ref, qbt_ref, qit_ref, kiwit_ref, vat_ref, vbt_ref,
                 *, scale_a, scale_b):
    xb = x_ref[...].astype(BF16)

    def seg(name, width=GROUP_W):
        off = _SEG[name]
        return jnp.dot(xb, w_ref[:, off:off + width], preferred_element_type=F32)

    def seg_t(name, width=GROUP_W):
        off = _SEG_T[name]
        return lax.dot_general(wt_ref[off:off + width, :], xb, _NT, preferred_element_type=F32)

    ka = seg("ka"); kb = seg("kb")
    ka_ref[...] = ka; kb_ref[...] = kb
    kab_ref[...] = ka.astype(BF16); kbb_ref[...] = kb.astype(BF16)
    va_ref[...] = seg("va"); vb_ref[...] = seg("vb")
    kiwi_ref[...] = seg("kiwi", 128)
    cb_ref[...] = seg("cb")
    uc_ref[...] = seg("cc") * seg("cx")
    ud_ref[...] = seg("da") * _sigmoid(seg("dg"))
    qat_ref[...] = seg_t("qa") * scale_a
    qbt_ref[...] = seg_t("qb") * scale_b
    qit_ref[...] = seg_t("qi")
    kiwit_ref[...] = seg_t("kiwi", 128)
    vat_ref[...] = seg_t("va").astype(BF16)
    vbt_ref[...] = seg_t("vb").astype(BF16)


_PROJ_NAMES = ("ka", "va", "kb", "vb", "kiwi", "kab", "kbb", "cb", "uc", "ud",
               "qat", "qbt", "qit", "kiwit", "vat", "vbt")


def _project(x, w_packed, wt_packed, tm):
    n, d = x.shape
    nat = lambda w, dt=F32: (jax.ShapeDtypeStruct((n, w), dt), pl.BlockSpec((tm, w), lambda i: (i, 0)))
    tr = lambda w, dt=F32: (jax.ShapeDtypeStruct((w, n), dt), pl.BlockSpec((w, tm), lambda i: (0, i)))
    outs = [nat(256), nat(256), nat(256), nat(256), nat(128), nat(256, BF16), nat(256, BF16),
            nat(256), nat(256), nat(256),
            tr(256), tr(256), tr(256), tr(128), tr(256, BF16), tr(256, BF16)]
    kern = functools.partial(_proj_kernel, scale_a=float(32 ** -0.5), scale_b=float(HEAD_W ** -0.5))
    res = pl.pallas_call(
        kern, grid=(n // tm,),
        in_specs=[pl.BlockSpec((tm, d), lambda i: (i, 0)),
                  pl.BlockSpec((d, _N_PACKED), lambda i: (0, 0)),
                  pl.BlockSpec((_N_PACKED_T, d), lambda i: (0, 0))],
        out_specs=[o[1] for o in outs], out_shape=[o[0] for o in outs],
        compiler_params=_cparams(("parallel",)), name="proj",
    )(x, w_packed, wt_packed)
    return dict(zip(_PROJ_NAMES, res))


def _gather_kernel(pt_ref, ak, av, bk, bv, bki, nak, navt, nbk, nbvt, nki,
                   o_ak, o_avt, o_bk, o_bvt, o_ki, *, n_pages):
    j = pl.program_id(1)

    @pl.when(j < n_pages)
    def _():
        o_ak[...] = ak[0].astype(BF16)
        o_bk[...] = bk[0].astype(BF16)
        o_avt[...] = av[0].T.astype(BF16)
        o_bvt[...] = bv[0].T.astype(BF16)
        o_ki[...] = bki[0]

    @pl.when(j == n_pages)
    def _():
        o_ak[...] = nak[0].astype(BF16)
        o_bk[...] = nbk[0].astype(BF16)
        o_avt[...] = navt[0]
        o_bvt[...] = nbvt[0]
        o_ki[...] = nki[0]


def _gather_pages(page_table, caches, news):
    bsz, n_pages = page_table.shape
    page = caches[0].shape[1]
    nb = n_pages + 1
    s_all = bsz * nb * page
    cache_spec = lambda c: pl.BlockSpec(
        (1, page, c), lambda b, j, pt: (pt[b, jnp.minimum(j, n_pages - 1)], 0, 0))
    new_nat = lambda c: pl.BlockSpec((1, page, c), lambda b, j, pt: (b, 0, 0))
    new_tr = pl.BlockSpec((1, GROUP_W, page), lambda b, j, pt: (b, 0, 0))
    out_nat = lambda c: pl.BlockSpec((page, c), lambda b, j, pt: (b * nb + j, 0))
    out_tr = pl.BlockSpec((GROUP_W, page), lambda b, j, pt: (0, b * nb + j))
    kern = functools.partial(_gather_kernel, n_pages=n_pages)
    return pl.pallas_call(
        kern,
        grid_spec=pltpu.PrefetchScalarGridSpec(
            num_scalar_prefetch=1, grid=(bsz, nb),
            in_specs=[cache_spec(GROUP_W)] * 4 + [cache_spec(D_IDX)]
                     + [new_nat(GROUP_W), new_tr, new_nat(GROUP_W), new_tr, new_nat(D_IDX)],
            out_specs=[out_nat(GROUP_W), out_tr, out_nat(GROUP_W), out_tr, out_nat(D_IDX)]),
        out_shape=[jax.ShapeDtypeStruct((s_all, GROUP_W), BF16), jax.ShapeDtypeStruct((GROUP_W, s_all), BF16),
                   jax.ShapeDtypeStruct((s_all, GROUP_W), BF16), jax.ShapeDtypeStruct((GROUP_W, s_all), BF16),
                   jax.ShapeDtypeStruct((s_all, D_IDX), F32)],
        compiler_params=_cparams(("parallel", "arbitrary")), name="gather_pages",
    )(page_table, *caches, *news)


def _attn_core(qst, n_groups, k_ref, vt_ref, n_full, n_kv, tq, tk, mask_fn, m_ref, l_ref, acc_ref, zero_masked):
    n_comp = n_groups // N_HEADS
    row_head = lax.broadcasted_iota(I32, (GROUP_W, 1), 0) // HEAD_W
    m_ref[...] = jnp.full(m_ref.shape, NEG_BIG, F32)
    l_ref[...] = jnp.zeros(l_ref.shape, F32)
    acc_ref[...] = jnp.zeros(acc_ref.shape, F32)
    grp = lambda a, g: a[:, g * tq:(g + 1) * tq]

    def step(c, masked):
        k0 = pl.multiple_of(c * tk, tk)
        kblk = k_ref[pl.ds(k0, tk), :]
        vt = vt_ref[:, pl.ds(k0, tk)]
        s = jnp.dot(kblk, qst, preferred_element_type=F32)
        if masked:
            mask = mask_fn(c, k0)
            s = jnp.concatenate([jnp.where(mask, grp(s, g), NEG_BIG) for g in range(n_groups)], axis=1)
        m_old = m_ref[...]
        m_new = jnp.maximum(m_old, jnp.max(s, axis=0, keepdims=True))
        alpha = jnp.exp(m_old - m_new)
        p = jnp.exp(s - m_new)
        if masked and zero_masked:
            p = jnp.concatenate([jnp.where(mask, grp(p, g), 0.0) for g in range(n_groups)], axis=1)
        l_ref[...] = alpha * l_ref[...] + jnp.sum(p, axis=0, keepdims=True)
        m_ref[...] = m_new
        pb = p.astype(BF16)
        vstack = jnp.concatenate(
            [jnp.where(row_head == h, vt, jnp.zeros_like(vt)) for h in range(N_HEADS)], axis=1)
        for comp in range(n_comp):
            base = comp * N_HEADS
            pcat = jnp.concatenate([grp(pb, base + h) for h in range(N_HEADS)], axis=0)
            pv = jnp.dot(vstack, pcat, preferred_element_type=F32)
            af = jnp.concatenate(
                [jnp.broadcast_to(grp(alpha, base + h), (HEAD_W, tq)) for h in range(N_HEADS)], axis=0)
            acc_ref[comp] = acc_ref[comp] * af + pv

    def full_body(c, carry):
        step(c, False)
        return carry

    def masked_body(c, carry):
        step(c, True)
        return carry

    lax.fori_loop(0, n_full, full_body, 0)
    lax.fori_loop(n_full, n_kv, masked_body, 0)


def _normalised(acc, l_row, base, tq):
    rl = 1.0 / l_row
    rf = jnp.concatenate(
        [jnp.broadcast_to(rl[:, (base + h) * tq:(base + h + 1) * tq], (HEAD_W, tq)) for h in range(N_HEADS)],
        axis=0)
    return acc * rf


def _num_kv_blocks(q0, tq, tk, past, s_tot):
    return jnp.minimum((past + q0 + tq + tk - 1) // tk, s_tot // tk)


def _attn_a_kernel(qt_ref, k_ref, vt_ref, lam_ref, g_ref, gm_ref, o_ref, m_ref, l_ref, acc_ref,
                   *, tq, tk, past, s_tot, lam_init):
    q0 = pl.program_id(1) * tq
    qt = qt_ref[...]
    row_grp = lax.broadcasted_iota(I32, (GROUP_W, 1), 0) // 32
    qst = jnp.concatenate(
        [jnp.where(row_grp == (2 * h + c), qt, 0.0) for c in range(2) for h in range(N_HEADS)],
        axis=1).astype(BF16)
    n_kv = _num_kv_blocks(q0, tq, tk, past, s_tot)
    n_full = jnp.minimum((past + q0 + 1) // tk, n_kv)
    row_pos = past + q0 + lax.broadcasted_iota(I32, (1, tq), 1)
    key_iota = lax.broadcasted_iota(I32, (tk, 1), 0)

    def mask_fn(c, k0):
        return (k0 + key_iota) <= row_pos

    _attn_core(qst, 2 * N_HEADS, k_ref, vt_ref, n_full, n_kv, tq, tk, mask_fn,
               m_ref, l_ref, acc_ref, zero_masked=False)
    l_row = l_ref[...]
    o0 = _normalised(acc_ref[0], l_row, 0, tq)
    o1 = _normalised(acc_ref[1], l_row, N_HEADS, tq)
    lv = lam_ref[...]
    lam = (jnp.exp(jnp.sum(lv[0:1] * lv[1:2], axis=1, keepdims=True))
           - jnp.exp(jnp.sum(lv[2:3] * lv[3:4], axis=1, keepdims=True)) + lam_init)
    o = (o0 - lam * o1).T
    hi, lo = _split_bf16(o * o)
    ms = (jnp.dot(hi, gm_ref[...], preferred_element_type=F32)
          + jnp.dot(lo, gm_ref[...], preferred_element_type=F32))
    o_ref[...] = o * lax.rsqrt(ms + LN_EPS) * g_ref[...] * (1.0 - lam_init)


def _attention_a(qt, k, vt, a_lambda, g_full, gmean, *, bsz, tq, tk, past, lam_init):
    t = qt.shape[1] // bsz
    s_tot = k.shape[0] // bsz
    nq = t // tq
    kern = functools.partial(_attn_a_kernel, tq=tq, tk=tk, past=past, s_tot=s_tot, lam_init=lam_init)
    return pl.pallas_call(
        kern, grid=(bsz, nq),
        in_specs=[pl.BlockSpec((GROUP_W, tq), lambda b, i: (0, b * nq + i)),
                  pl.BlockSpec((s_tot, GROUP_W), lambda b, i: (b, 0)),
                  pl.BlockSpec((GROUP_W, s_tot), lambda b, i: (0, b)),
                  pl.BlockSpec(a_lambda.shape, lambda b, i: (0, 0)),
                  pl.BlockSpec((1, GROUP_W), lambda b, i: (0, 0)),
                  pl.BlockSpec((GROUP_W, GROUP_W), lambda b, i: (0, 0))],
        out_specs=pl.BlockSpec((tq, GROUP_W), lambda b, i: (b * nq + i, 0)),
        out_shape=jax.ShapeDtypeStruct((bsz * t, GROUP_W), F32),
        scratch_shapes=[pltpu.VMEM((1, 2 * N_HEADS * tq), F32), pltpu.VMEM((1, 2 * N_HEADS * tq), F32),
                        pltpu.VMEM((2, GROUP_W, tq), F32)],
        compiler_params=_cparams(("parallel", "arbitrary")), name="attn_a",
    )(qt, k, vt, a_lambda, g_full, gmean)


def _dsa_kernel(qt_ref, k_ref, vt_ref, qit_ref, ki_ref, kiwit_ref, o_ref, key_ref, y_ref, m_ref, l_ref, acc_ref,
                *, tq, tk, past, s_tot, topk, idx_bits, t_valid):
    q0 = pl.program_id(1) * tq
    n_kv = _num_kv_blocks(q0, tq, tk, past, s_tot)
    row_pos = past + q0 + lax.broadcasted_iota(I32, (1, tq), 1)
    key_iota = lax.broadcasted_iota(I32, (tk, 1), 0)
    int_min = jnp.int32(INT_MIN)

    qit = qit_ref[...]
    blocks = []
    for h in range(N_HEADS):
        qh = qit[h * D_IDX:(h + 1) * D_IDX, :]
        hi, lo = _split_bf16(qh)
        blocks.append(jnp.concatenate([hi, lo, hi], axis=0))
    qi = jnp.concatenate(blocks, axis=1)
    kiwit = kiwit_ref[...]
    w_h = [kiwit[D_IDX + h:D_IDX + h + 1, :] for h in range(N_HEADS)]

    def score_body(c, carry):
        k0 = pl.multiple_of(c * tk, tk)
        kib = ki_ref[pl.ds(k0, tk), :]
        sc = jnp.dot(kib, qi, preferred_element_type=F32)
        isc = jnp.maximum(sc[:, 0:tq], 0.0) * w_h[0]
        for h in range(1, N_HEADS):
            isc = isc + jnp.maximum(sc[:, h * tq:(h + 1) * tq], 0.0) * w_h[h]
        isc = jnp.where(isc == 0.0, 0.0, isc)
        bits = pltpu.bitcast(isc, I32)
        key = bits ^ ((bits >> 31) & jnp.int32(0x7FFFFFFF))
        key_ref[c] = jnp.where((k0 + key_iota) <= row_pos, key, int_min)
        return carry

    lax.fori_loop(0, n_kv, score_body, 0)

    def count(pred):
        def block_count(c):
            ones = jnp.where(pred(key_ref[c], c * tk), 1.0, 0.0)
            return jnp.sum(ones.reshape(tk // 8, 8, tq), axis=0)

        def body(i, acc):
            return acc + block_count(2 * i) + block_count(2 * i + 1)

        acc = lax.fori_loop(0, n_kv // 2, body, jnp.zeros((8, tq), F32))
        odd = (n_kv & 1).astype(F32)
        acc = acc + odd * block_count(n_kv - 1)
        return jnp.sum(acc, axis=0, keepdims=True)

    kk = jnp.float32(topk)

    def bit_body(r, prefix):
        cand_u = prefix | lax.shift_left(jnp.int32(1), 31 - r)
        cand_s = cand_u ^ int_min
        cnt = count(lambda blk, k0: blk >= cand_s)
        return jnp.where(cnt >= kk, cand_u, prefix)

    prefix = lax.fori_loop(0, 32, bit_body, jnp.zeros((1, tq), I32))
    thr = prefix ^ int_min

    n_gt = count(lambda blk, k0: blk > thr)
    n_ge = count(lambda blk, k0: blk >= thr)
    r_need = kk - n_gt
    live = lax.broadcasted_iota(I32, (1, tq), 1) < t_valid
    any_tie = jnp.max(jnp.where(live, n_ge - kk, 0.0)) > 0.0
    y_ref[...] = jnp.full((1, tq), s_tot, I32)

    @pl.when(any_tie)
    def _():
        def ybody(r, y):
            cand = y | lax.shift_left(jnp.int32(1), idx_bits - 1 - r)
            cnt = count(lambda blk, k0: (blk == thr) & ((k0 + key_iota) < cand))
            return jnp.where(cnt < r_need, cand, y)
        y_ref[...] = lax.fori_loop(0, idx_bits, ybody, jnp.zeros((1, tq), I32))

    y_last = y_ref[...]

    def mask_fn(c, k0):
        blk = key_ref[c]
        pos = k0 + key_iota
        sel = (blk > thr) | ((blk == thr) & (pos <= y_last))
        return sel & (pos <= row_pos)

    qt = qt_ref[...]
    row_head = lax.broadcasted_iota(I32, (GROUP_W, 1), 0) // HEAD_W
    qst = jnp.concatenate([jnp.where(row_head == h, qt, 0.0) for h in range(N_HEADS)], axis=1).astype(BF16)
    _attn_core(qst, N_HEADS, k_ref, vt_ref, 0, n_kv, tq, tk, mask_fn, m_ref, l_ref, acc_ref, zero_masked=True)
    o_ref[...] = _normalised(acc_ref[0], l_ref[...], 0, tq).T


def _dsa(qt, k, vt, qit, ki3, kiwit, *, bsz, tq, tk, past, topk, t_valid):
    t = qt.shape[1] // bsz
    s_tot = k.shape[0] // bsz
    nq = t // tq
    idx_bits = int(math.ceil(math.log2(s_tot))) + 1
    kern = functools.partial(_dsa_kernel, tq=tq, tk=tk, past=past, s_tot=s_tot, topk=topk,
                             idx_bits=idx_bits, t_valid=t_valid)
    qspec = lambda w: pl.BlockSpec((w, tq), lambda b, i: (0, b * nq + i))
    return pl.pallas_call(
        kern, grid=(bsz, nq),
        in_specs=[qspec(GROUP_W),
                  pl.BlockSpec((s_tot, GROUP_W), lambda b, i: (b, 0)),
                  pl.BlockSpec((GROUP_W, s_tot), lambda b, i: (0, b)),
                  qspec(GROUP_W),
                  pl.BlockSpec((s_tot, 3 * D_IDX), lambda b, i: (b, 0)),
                  qspec(LANES)],
        out_specs=pl.BlockSpec((tq, GROUP_W), lambda b, i: (b * nq + i, 0)),
        out_shape=jax.ShapeDtypeStruct((bsz * t, GROUP_W), F32),
        scratch_shapes=[pltpu.VMEM((s_tot // tk, tk, tq), I32), pltpu.VMEM((1, tq), I32),
                        pltpu.VMEM((1, N_HEADS * tq), F32), pltpu.VMEM((1, N_HEADS * tq), F32),
                        pltpu.VMEM((1, GROUP_W, tq), F32)],
        compiler_params=_cparams(("parallel", "arbitrary")), name="dsa",
    )(qt, k, vt, qit, ki3, kiwit)


_HALO_C = 8
_HALO_D = 32


def _conv_kernel(*refs, tm, single_tile):
    if single_tile:
        (cb_ref, uc_ref, ud_ref, hc_ref, hd_ref, cw_ref, dw_ref, db_ref, lg_ref, lb_ref,
         oc_ref, od_ref, ec_ref, ed_ref) = refs
        pc_ref = pd_ref = None
    else:
        (cb_ref, uc_ref, ud_ref, pc_ref, pd_ref, hc_ref, hd_ref, cw_ref, dw_ref, db_ref, lg_ref, lb_ref,
         oc_ref, od_ref, ec_ref, ed_ref) = refs
    i = pl.program_id(1)

    @pl.when(i == 0)
    def _():
        ec_ref[0:_HALO_C] = hc_ref[0]
        ed_ref[0:_HALO_D] = hd_ref[0]

    if not single_tile:
        @pl.when(i > 0)
        def _():
            ec_ref[0:_HALO_C] = pc_ref[0]
            ed_ref[0:_HALO_D] = pd_ref[0]

    ec_ref[_HALO_C:_HALO_C + tm] = uc_ref[0]
    ed_ref[_HALO_D:_HALO_D + tm] = ud_ref[0]

    cw = cw_ref[...]
    acc = ec_ref[pl.ds(_HALO_C - (C_WIDTH - 1), tm), :] * cw[0:1]
    for w in range(1, C_WIDTH):
        acc = acc + ec_ref[pl.ds(_HALO_C - (C_WIDTH - 1) + w, tm), :] * cw[w:w + 1]
    oc_ref[0] = cb_ref[0] * acc

    dw = dw_ref[...]
    acc = ed_ref[pl.ds(_HALO_D - (D_WIDTH - 1), tm), :] * dw[0:1]
    for w in range(1, D_WIDTH):
        acc = acc + ed_ref[pl.ds(_HALO_D - (D_WIDTH - 1) + w, tm), :] * dw[w:w + 1]
    z = _layer_norm(acc + db_ref[...], lg_ref[...], lb_ref[...])
    od_ref[0] = z * _sigmoid(z)


def _convs(cb, uc, ud, hist_c, hist_d, c_w, d_w, d_b, ln_g, ln_b, *, tm):
    bsz, t, w = cb.shape
    nt = t // tm
    single = nt == 1
    main = pl.BlockSpec((1, tm, w), lambda b, i: (b, i, 0))
    prev_c = pl.BlockSpec((1, _HALO_C, w), lambda b, i: (b, jnp.maximum(i * (tm // _HALO_C) - 1, 0), 0))
    prev_d = pl.BlockSpec((1, _HALO_D, w), lambda b, i: (b, jnp.maximum(i * (tm // _HALO_D) - 1, 0), 0))
    hist_cs = pl.BlockSpec((1, _HALO_C, w), lambda b, i: (b, 0, 0))
    hist_ds = pl.BlockSpec((1, _HALO_D, w), lambda b, i: (b, 0, 0))
    full2 = lambda a: pl.BlockSpec(a.shape, lambda b, i: (0, 0))
    in_specs = [main, main, main] + ([] if single else [prev_c, prev_d]) + [hist_cs, hist_ds] + \
        [full2(c_w), full2(d_w), full2(d_b), full2(ln_g), full2(ln_b)]
    args = [cb, uc, ud] + ([] if single else [uc, ud]) + [hist_c, hist_d, c_w, d_w, d_b, ln_g, ln_b]
    kern = functools.partial(_conv_kernel, tm=tm, single_tile=single)
    return pl.pallas_call(
        kern, grid=(bsz, nt), in_specs=in_specs,
        out_specs=[main, main],
        out_shape=[jax.ShapeDtypeStruct((bsz, t, w), F32)] * 2,
        scratch_shapes=[pltpu.VMEM((_HALO_C + tm, w), F32), pltpu.VMEM((_HALO_D + tm, w), F32)],
        compiler_params=_cparams(("parallel", "arbitrary")), name="convs",
    )(*args)


def _mix_kernel(x_ref, oa_ref, ob_ref, oc_ref, od_ref, pw_ref, pwb_ref, wo_ref, g_ref, b_ref, h_ref, *, alpha):
    od = jnp.dot(od_ref[...].astype(BF16), pw_ref[...], preferred_element_type=F32) + pwb_ref[...]
    cat = jnp.concatenate([oa_ref[...], ob_ref[...], oc_ref[...], od], axis=1).astype(BF16)
    mix = jnp.dot(cat, wo_ref[...], preferred_element_type=F32)
    h_ref[...] = _layer_norm(alpha * x_ref[...] + mix, g_ref[...], b_ref[...])


def _mix(x, oa, ob, oc, od, pw, pwb, wo, g, b, *, tm, alpha):
    n, d = x.shape
    row = lambda w: pl.BlockSpec((tm, w), lambda i: (i, 0))
    full = lambda a: pl.BlockSpec(a.shape, lambda i: (0, 0))
    return pl.pallas_call(
        functools.partial(_mix_kernel, alpha=alpha), grid=(n // tm,),
        in_specs=[row(d), row(GROUP_W), row(GROUP_W), row(GROUP_W), row(GROUP_W),
                  full(pw), full(pwb), full(wo), full(g), full(b)],
        out_specs=row(d), out_shape=jax.ShapeDtypeStruct((n, d), F32),
        compiler_params=_cparams(("parallel",)), name="mix",
    )(x, oa, ob, oc, od, pw, pwb, wo, g, b)


def _staircase():
    return [(a, b) for a in range(PEER_TOPK) for b in range(PEER_TOPK) if (a + 1) * (b + 1) <= PEER_TOPK]


def _extract_top(x, payload, n_out, n_rows):
    row = lax.broadcasted_iota(I32, x.shape, 0).astype(F32)
    vals, idxs, pays = [], [], []
    for _ in range(n_out):
        m = jnp.max(x, axis=0, keepdims=True)
        idx = jnp.min(jnp.where(x == m, row, float(n_rows)), axis=0, keepdims=True)
        hit = row == idx
        vals.append(m)
        idxs.append(idx)
        if payload is not None:
            pays.append(jnp.max(jnp.where(hit, payload, -1.0), axis=0, keepdims=True))
        x = jnp.where(hit, -jnp.inf, x)
    return vals, idxs, pays


def _peer_select_kernel(h_ref, wq_ref, sk_ref, eid_ref, g_ref, sv_ref, si_ref, *, tm):
    ht = h_ref[...].T.astype(BF16)
    n_groups = 2 * PEER_HEADS

    def group_body(g, carry):
        r0 = pl.multiple_of(g * N_KEYS, N_KEYS)
        qg = jnp.dot(wq_ref[pl.ds(r0, N_KEYS), :], ht, preferred_element_type=F32)
        st = jnp.dot(sk_ref[g % 2], qg.astype(BF16), preferred_element_type=F32)
        vals, idxs, _ = _extract_top(st, None, PEER_TOPK, N_KEYS)
        sv_ref[g] = jnp.concatenate(vals, axis=0)
        si_ref[g] = jnp.concatenate(idxs, axis=0)
        return carry

    lax.fori_loop(0, n_groups, group_body, 0)

    pairs = _staircase()
    n_cand = len(pairs)
    n_pad = (-n_cand) % 8

    def head_body(hd, carry):
        sv0 = sv_ref[2 * hd]; sv1 = sv_ref[2 * hd + 1]
        si0 = si_ref[2 * hd]; si1 = si_ref[2 * hd + 1]
        cv = [sv0[a:a + 1] + sv1[b:b + 1] for a, b in pairs]
        ce = [si0[a:a + 1] * float(N_KEYS) + si1[b:b + 1] for a, b in pairs]
        if n_pad:
            cv.append(jnp.full((n_pad, tm), -jnp.inf, F32))
            ce.append(jnp.full((n_pad, tm), -1.0, F32))
        cand = jnp.concatenate(cv, axis=0)
        cand_e = jnp.concatenate(ce, axis=0)
        vals, _, pays = _extract_top(cand, cand_e, PEER_TOPK, n_cand + n_pad)
        fv = jnp.concatenate(vals, axis=0)
        e = jnp.exp(fv - fv[0:1])
        gate = e / jnp.sum(e, axis=0, keepdims=True)
        r0 = pl.multiple_of(hd * PEER_TOPK, PEER_TOPK)
        eid_ref[pl.ds(r0, PEER_TOPK), :] = jnp.concatenate(pays, axis=0).astype(I32)
        g_ref[pl.ds(r0, PEER_TOPK), :] = gate
        return carry

    lax.fori_loop(0, PEER_HEADS, head_body, 0)


def _peer_select(h, wq_t, sk, *, tm):
    n, d = h.shape
    n_slots = PEER_HEADS * PEER_TOPK
    return pl.pallas_call(
        functools.partial(_peer_select_kernel, tm=tm), grid=(n // tm,),
        in_specs=[pl.BlockSpec((tm, d), lambda i: (i, 0)),
                  pl.BlockSpec(wq_t.shape, lambda i: (0, 0)),
                  pl.BlockSpec(sk.shape, lambda i: (0, 0, 0))],
        out_specs=[pl.BlockSpec((n_slots, tm), lambda i: (0, i)),
                   pl.BlockSpec((n_slots, tm), lambda i: (0, i))],
        out_shape=[jax.ShapeDtypeStruct((n_slots, n), I32), jax.ShapeDtypeStruct((n_slots, n), F32)],
        scratch_shapes=[pltpu.VMEM((2 * PEER_HEADS, PEER_TOPK, tm), F32),
                        pltpu.VMEM((2 * PEER_HEADS, PEER_TOPK, tm), F32)],
        compiler_params=_cparams(("parallel",)), name="peer_select",
    )(h, wq_t, sk)


_GATE_UNROLL = 4


def _peer_gates_kernel(eid_ref, g_ref, w_ref, *, tm):
    sub = lax.broadcasted_iota(I32, (N_KEYS, PEER_HEADS * PEER_TOPK), 0)

    def body(t, carry):
        e = eid_ref[pl.ds(t, 1), :]
        g = g_ref[pl.ds(t, 1), :]
        g_hi = g.astype(BF16).astype(F32)
        g_lo = g - g_hi
        oh_i = jnp.where((e >> 7) == sub, 1.0, 0.0).astype(BF16)
        hit_j = (e & (N_KEYS - 1)) == sub
        gj_hi = jnp.where(hit_j, g_hi, 0.0).astype(BF16)
        gj_lo = jnp.where(hit_j, g_lo, 0.0).astype(BF16)
        lhs = jnp.concatenate([oh_i, oh_i], axis=1)
        rhs = jnp.concatenate([gj_hi, gj_lo], axis=1)
        w = lax.dot_general(lhs, rhs, _NT, preferred_element_type=F32)
        w_ref[t] = w.astype(BF16)
        return carry

    lax.fori_loop(0, tm, body, 0, unroll=_GATE_UNROLL)


def _peer_gates(eid, gate, *, tm):
    n, n_slots = eid.shape
    return pl.pallas_call(
        functools.partial(_peer_gates_kernel, tm=tm), grid=(n // tm,),
        in_specs=[pl.BlockSpec((tm, n_slots), lambda i: (i, 0)),
                  pl.BlockSpec((tm, n_slots), lambda i: (i, 0))],
        out_specs=pl.BlockSpec((tm, N_KEYS, N_KEYS), lambda i: (i, 0, 0)),
        out_shape=jax.ShapeDtypeStruct((n, N_KEYS, N_KEYS), BF16),
        compiler_params=_cparams(("parallel",)), name="peer_gates",
    )(eid, gate)


def _gelu(x):
    return 0.5 * x * (1.0 + lax.erf(x * (2.0 ** -0.5)))


def _peer_dense_kernel(h_ref, w_ref, u_ref, v_ref, f_ref):
    @pl.when(pl.program_id(1) == 0)
    def _():
        f_ref[...] = jnp.zeros(f_ref.shape, F32)

    act = lax.dot_general(h_ref[...], u_ref[...], _NT, preferred_element_type=F32)
    y = (w_ref[...].astype(F32) * _gelu(act)).astype(BF16)
    f_ref[...] += jnp.dot(y, v_ref[...], preferred_element_type=F32)


def _peer_dense(hb, wmat, u, v, *, tm, te):
    n, d = hb.shape
    n_exp = u.shape[0]
    return pl.pallas_call(
        _peer_dense_kernel, grid=(n // tm, n_exp // te),
        in_specs=[pl.BlockSpec((tm, d), lambda i, e: (i, 0)),
                  pl.BlockSpec((tm, te), lambda i, e: (i, e)),
                  pl.BlockSpec((te, d), lambda i, e: (e, 0)),
                  pl.BlockSpec((te, d), lambda i, e: (e, 0))],
        out_specs=pl.BlockSpec((tm, d), lambda i, e: (i, 0)),
        out_shape=jax.ShapeDtypeStruct((n, d), F32),
        compiler_params=_cparams(("parallel", "arbitrary")), name="peer_dense",
    )(hb, wmat, u, v)


def _final_kernel(h_ref, f_ref, p_ref, g_ref, b_ref, wg_ref, bg_ref, wp_ref, y_ref, *, alpha):
    h2 = _layer_norm(alpha * h_ref[...] + f_ref[...], g_ref[...], b_ref[...])
    gate = _sigmoid(jnp.dot(h2.astype(BF16), wg_ref[...], preferred_element_type=F32) + bg_ref[...])
    emb = jnp.dot(p_ref[...].astype(BF16), wp_ref[...], preferred_element_type=F32)
    y_ref[...] = h2 + gate * emb


def _final(h, f, p, g, b, wg, bg, wp, *, tm, alpha):
    n, d = h.shape
    row = lambda w: pl.BlockSpec((tm, w), lambda i: (i, 0))
    full = lambda a: pl.BlockSpec(a.shape, lambda i: (0, 0))
    return pl.pallas_call(
        functools.partial(_final_kernel, alpha=alpha), grid=(n // tm,),
        in_specs=[row(d), row(d), row(p.shape[1]), full(g), full(b), full(wg), full(bg), full(wp)],
        out_specs=row(d), out_shape=jax.ShapeDtypeStruct((n, d), F32),
        compiler_params=_cparams(("parallel",)), name="final",
    )(h, f, p, g, b, wg, bg, wp)


def _pack_w_in(w_in):
    d = w_in.shape[0]
    cols = [w_in[:, 0:1792], w_in[:, 1792:1860], jnp.zeros((d, 60), w_in.dtype), w_in[:, 1860:3140]]
    nat = jnp.concatenate(cols, axis=1).astype(BF16)
    seg = lambda name, width=GROUP_W: nat[:, _SEG[name]:_SEG[name] + width]
    tr = jnp.concatenate([seg("qa"), seg("qb"), seg("qi"), seg("kiwi", 128), seg("va"), seg("vb")], axis=1).T
    return nat, tr


def _row(a):
    return a.reshape(1, -1)


def _tile_rows(n, pref):
    t = pref
    while n % t:
        t //= 2
    return t


def _prep_weights(lw, depth):
    (w_in, a_lambda, a_subln_g, c_conv_w, d_conv_w, d_conv_b, d_ln_g, d_ln_b, d_pw_w, d_pw_b,
     w_out, ln1_g, ln1_b, peer_wq, peer_subkeys, peer_u, peer_v, ln2_g, ln2_b,
     ple_w, ple_gate_w, ple_gate_b) = lw
    head_of_lane = jnp.arange(GROUP_W) // HEAD_W
    gmean = (head_of_lane[:, None] == head_of_lane[None, :]).astype(BF16) * (1.0 / HEAD_W)
    w_nat, w_tr = _pack_w_in(w_in)
    return dict(
        w_in=w_nat, w_in_t=w_tr, a_lambda=a_lambda, a_g=_row(jnp.tile(a_subln_g, N_HEADS)),
        gmean=gmean.astype(BF16),
        c_w=c_conv_w, d_w=d_conv_w, d_b=_row(d_conv_b), d_ln_g=_row(d_ln_g), d_ln_b=_row(d_ln_b),
        d_pw=d_pw_w.astype(BF16), d_pwb=_row(d_pw_b), w_out=w_out.astype(BF16),
        ln1_g=_row(ln1_g), ln1_b=_row(ln1_b),
        wq_t=peer_wq.T.astype(BF16), sk=peer_subkeys.astype(BF16),
        u=peer_u.astype(BF16), v=peer_v.astype(BF16),
        ln2_g=_row(ln2_g), ln2_b=_row(ln2_b),
        ple_w=ple_w.astype(BF16), wg=ple_gate_w.astype(BF16), bg=_row(ple_gate_b),
        alpha=float((2 * depth) ** 0.25),
    )


def _index_keys(ki):
    k_hi = ki.astype(BF16)
    k_lo = (ki - k_hi.astype(F32)).astype(BF16)
    return jnp.concatenate([k_hi, k_hi, k_lo], axis=-1)


def _pad_query_lanes(a, bsz, t, t_lanes):
    w = a.shape[0]
    a = a.reshape(w, bsz, t)
    return jnp.pad(a, ((0, 0), (0, 0), (0, t_lanes - t))).reshape(w, bsz * t_lanes)


def _token_tail(w, x, oa, ob, oc, od, valid_rows, p):
    n = x.shape[0]
    h = _mix(x, oa, ob, oc, od, w["d_pw"], w["d_pwb"], w["w_out"], w["ln1_g"], w["ln1_b"],
             tm=_tile_rows(n, 512), alpha=w["alpha"])
    if valid_rows is not None:
        bsz, t_pad, t = valid_rows
        h = h.reshape(bsz, t_pad, -1)[:, :t].reshape(bsz * t, -1)
    n = h.shape[0]
    eid_t, gate_t = _peer_select(h, w["wq_t"], w["sk"], tm=_tile_rows(n, 256))
    wmat = _peer_gates(eid_t.T, gate_t.T, tm=_tile_rows(n, 128))
    wmat = wmat.reshape(n, N_KEYS * N_KEYS)
    f = _peer_dense(h.astype(BF16), wmat, w["u"], w["v"], tm=_tile_rows(n, 1024), te=512)
    return _final(h, f, p, w["ln2_g"], w["ln2_b"], w["wg"], w["bg"], w["ple_w"],
                  tm=_tile_rows(n, 512), alpha=w["alpha"])


def _layer(layer_idx, depth, xp, xs, caches, state_c, state_d, page_table, pp, ps, lw):
    w = _prep_weights(lw, depth)
    bp, tp, d = xp.shape
    bs, ts, _ = xs.shape
    ts_pad = 8
    n_pages = page_table.shape[1]
    page = caches[0].shape[1]
    past = n_pages * page
    lam_init = 0.8 - 0.6 * math.exp(-0.3 * (layer_idx + 1))

    pr = _project(xp.reshape(bp * tp, d), w["w_in"], w["w_in_t"], _tile_rows(bp * tp, 512))
    tq = _tile_rows(tp, 256)
    tk = _tile_rows(tp, 256)
    oa = _attention_a(pr["qat"], pr["kab"], pr["vat"], w["a_lambda"], w["a_g"], w["gmean"],
                      bsz=bp, tq=tq, tk=tk, past=0, lam_init=lam_init)
    ob = _dsa(pr["qbt"], pr["kbb"], pr["vbt"], pr["qit"], _index_keys(pr["kiwi"][:, :D_IDX]), pr["kiwit"],
              bsz=bp, tq=tq, tk=tk, past=0, topk=min(DSA_TOPK, tp // 4), t_valid=tq)
    r3p = lambda a: a.reshape(bp, tp, a.shape[-1])
    zc = jnp.zeros((bp, _HALO_C, GROUP_W), F32)
    zd = jnp.zeros((bp, _HALO_D, GROUP_W), F32)
    oc, od = _convs(r3p(pr["cb"]), r3p(pr["uc"]), r3p(pr["ud"]), zc, zd, w["c_w"], w["d_w"], w["d_b"],
                    w["d_ln_g"], w["d_ln_b"], tm=_tile_rows(tp, 512))
    yp = _token_tail(w, xp.reshape(bp * tp, d), oa, ob, oc.reshape(bp * tp, -1), od.reshape(bp * tp, -1),
                     None, pp.reshape(bp * tp, -1)).reshape(bp, tp, d)
    new_p = (r3p(pr["ka"]), r3p(pr["va"]), r3p(pr["kb"]), r3p(pr["vb"]), r3p(pr["kiwi"])[:, :, :D_IDX],
             jnp.concatenate([zc, r3p(pr["uc"])], axis=1)[:, -(C_WIDTH - 1):],
             jnp.concatenate([zd, r3p(pr["ud"])], axis=1)[:, -(D_WIDTH - 1):])

    xs_pad = jnp.pad(xs, ((0, 0), (0, ts_pad - ts), (0, 0))).reshape(bs * ts_pad, d)
    sr = _project(xs_pad, w["w_in"], w["w_in_t"], _tile_rows(bs * ts_pad, 512))
    r3s = lambda a: a.reshape(bs, ts_pad, a.shape[-1])
    pad_rows = lambda a: jnp.pad(r3s(a), ((0, 0), (0, page - ts_pad), (0, 0)))
    new_vt = lambda a: jnp.pad(a.reshape(GROUP_W, bs, ts_pad).transpose(1, 0, 2),
                               ((0, 0), (0, 0), (0, page - ts_pad)))
    news = [pad_rows(sr["ka"]), new_vt(sr["vat"]), pad_rows(sr["kb"]), new_vt(sr["vbt"]),
            pad_rows(sr["kiwi"][:, :D_IDX])]
    k_a, vt_a, k_b, vt_b, ki_all = _gather_pages(page_table, caches, news)
    tq_s = LANES
    lanes = lambda a: _pad_query_lanes(a, bs, ts_pad, tq_s)
    oa_s = _attention_a(lanes(sr["qat"]), k_a, vt_a, w["a_lambda"], w["a_g"], w["gmean"],
                        bsz=bs, tq=tq_s, tk=page, past=past, lam_init=lam_init)
    ob_s = _dsa(lanes(sr["qbt"]), k_b, vt_b, lanes(sr["qit"]), _index_keys(ki_all), lanes(sr["kiwit"]),
                bsz=bs, tq=tq_s, tk=page, past=past, topk=min(DSA_TOPK, (past + ts) // 4), t_valid=ts_pad)
    unpad = lambda a: a.reshape(bs, tq_s, GROUP_W)[:, :ts_pad].reshape(bs * ts_pad, GROUP_W)
    hist_c = jnp.concatenate([jnp.zeros((bs, _HALO_C - (C_WIDTH - 1), GROUP_W), F32), state_c], axis=1)
    hist_d = jnp.concatenate([jnp.zeros((bs, _HALO_D - (D_WIDTH - 1), GROUP_W), F32), state_d], axis=1)
    oc_s, od_s = _convs(r3s(sr["cb"]), r3s(sr["uc"]), r3s(sr["ud"]), hist_c, hist_d, w["c_w"], w["d_w"],
                        w["d_b"], w["d_ln_g"], w["d_ln_b"], tm=ts_pad)
    ys = _token_tail(w, xs_pad, unpad(oa_s), unpad(ob_s), oc_s.reshape(bs * ts_pad, -1),
                     od_s.reshape(bs * ts_pad, -1), (bs, ts_pad, ts), ps.reshape(bs * ts, -1)).reshape(bs, ts, d)
    cut = lambda a: r3s(a)[:, :ts]
    new_s = (cut(sr["ka"]), cut(sr["va"]), cut(sr["kb"]), cut(sr["vb"]), cut(sr["kiwi"])[:, :, :D_IDX],
             jnp.concatenate([state_c, cut(sr["uc"])], axis=1)[:, -(C_WIDTH - 1):],
             jnp.concatenate([state_d, cut(sr["ud"])], axis=1)[:, -(D_WIDTH - 1):])
    return yp, ys, new_p, new_s


def kernel(x_prompt, x_sample, cache_a_k, cache_a_v, cache_b_k, cache_b_v, cache_b_kidx, state_c_conv, state_d_conv, page_table, p_prompt, p_sample, w_in, a_lambda, a_subln_g, c_conv_w, d_conv_w, d_conv_b, d_ln_g, d_ln_b, d_pw_w, d_pw_b, w_out, ln1_g, ln1_b, peer_wq, peer_subkeys, peer_u, peer_v, ln2_g, ln2_b, ple_w, ple_gate_w, ple_gate_b):
    depth = w_in.shape[0]
    n_pool, page = cache_a_k.shape[1], cache_a_k.shape[2]
    xp, xs = x_prompt, x_sample
    news_p, news_s = [], []
    for l in range(depth):
        lw = (w_in[l], a_lambda[l], a_subln_g[l], c_conv_w[l], d_conv_w[l], d_conv_b[l], d_ln_g[l], d_ln_b[l],
              d_pw_w[l], d_pw_b[l], w_out[l], ln1_g[l], ln1_b[l], peer_wq[l], peer_subkeys[l], peer_u[l],
              peer_v[l], ln2_g[l], ln2_b[l], ple_w[l], ple_gate_w[l], ple_gate_b[l])
        caches = [cache_a_k[l].reshape(n_pool, page, GROUP_W), cache_a_v[l].reshape(n_pool, page, GROUP_W),
                  cache_b_k[l].reshape(n_pool, page, GROUP_W), cache_b_v[l].reshape(n_pool, page, GROUP_W),
                  cache_b_kidx[l]]
        xp, xs, new_p, new_s = _layer(l, depth, xp, xs, caches, state_c_conv[l], state_d_conv[l],
                                      page_table, p_prompt[l], p_sample[l], lw)
        news_p.append(new_p)
        news_s.append(new_s)

    bp, tp, _ = x_prompt.shape
    bs, ts, _ = x_sample.shape
    shapes = [(N_HEADS, 2, 32), (N_HEADS, HEAD_W), (N_HEADS, HEAD_W), (N_HEADS, HEAD_W), (D_IDX,)]

    def stack(news, i, lead):
        a = jnp.stack([n[i] for n in news], axis=0)
        return a.reshape((depth,) + lead + shapes[i]) if i < 5 else a

    outs = [xp, xs]
    for i in range(7):
        outs.append(stack(news_p, i, (bp, tp)))
        outs.append(stack(news_s, i, (bs, ts)))
    return tuple(outs)
```

```python
import functools
import math

import jax
import jax.numpy as jnp
from jax import lax
from jax.experimental import pallas as pl
from jax.experimental.pallas import tpu as pltpu

F32 = jnp.float32
BF16 = jnp.bfloat16
I32 = jnp.int32

LN_EPS = 1e-5
N_HEADS = 4
HEAD_W = 64
GROUP_W = 256
D_IDX = 64
DSA_TOPK = 256
C_WIDTH = 3
D_WIDTH = 31
PEER_HEADS = 8
PEER_TOPK = 16
N_KEYS = 128
LANES = 128
NEG_BIG = -1e30
INT_MIN = -2 ** 31
VMEM_LIMIT = 56 * 1024 * 1024

_SEG = dict(qa=0, ka=256, va=512, qb=768, kb=1024, vb=1280, qi=1536, kiwi=1792,
            cb=1920, cc=2176, cx=2432, da=2688, dg=2944)
_N_PACKED = 3200
_SEG_T = dict(qa=0, qb=256, qi=512, kiwi=768, va=896, vb=1152)
_N_PACKED_T = 1408

_NT = (((1,), (1,)), ((), ()))


def _cparams(sem):
    return pltpu.CompilerParams(dimension_semantics=sem, vmem_limit_bytes=VMEM_LIMIT)


def _split_bf16(x):
    hi = x.astype(BF16)
    lo = (x - hi.astype(F32)).astype(BF16)
    return hi, lo


def _layer_norm(x, g, b):
    mu = jnp.mean(x, axis=-1, keepdims=True)
    xc = x - mu
    var = jnp.mean(xc * xc, axis=-1, keepdims=True)
    return xc * lax.rsqrt(var + LN_EPS) * g + b


def _sigmoid(x):
    return 1.0 / (1.0 + jnp.exp(-x))


_PROJ_COMMON = ("ka", "va", "kb", "vb", "kiwi", "cb", "uc", "ud")
_PROJ_FEATURE_MAJOR = ("kab", "kbb", "qat", "qbt", "qit", "kiwit", "vat", "vbt")
_PROJ_ROW_MAJOR = ("qa", "qb", "qi")


def _proj_kernel(x_ref, w_ref, wt_ref, *out_refs, names, scale_a, scale_b):
    o = dict(zip(names, out_refs))
    xb = x_ref[...].astype(BF16)

    def seg(name, width=GROUP_W):
        off = _SEG[name]
        return jnp.dot(xb, w_ref[:, off:off + width], preferred_element_type=F32)

    def seg_t(name, width=GROUP_W):
        off = _SEG_T[name]
        return lax.dot_general(wt_ref[off:off + width, :], xb, _NT, preferred_element_type=F32)

    ka = seg("ka"); kb = seg("kb")
    o["ka"][...] = ka; o["kb"][...] = kb
    o["va"][...] = seg("va"); o["vb"][...] = seg("vb")
    o["kiwi"][...] = seg("kiwi", LANES)
    o["cb"][...] = seg("cb")
    o["uc"][...] = seg("cc") * seg("cx")
    o["ud"][...] = seg("da") * _sigmoid(seg("dg"))
    if "qat" in o:
        o["kab"][...] = ka.astype(BF16); o["kbb"][...] = kb.astype(BF16)
        o["qat"][...] = seg_t("qa") * scale_a
        o["qbt"][...] = seg_t("qb") * scale_b
        o["qit"][...] = seg_t("qi")
        o["kiwit"][...] = seg_t("kiwi", LANES)
        o["vat"][...] = seg_t("va").astype(BF16)
        o["vbt"][...] = seg_t("vb").astype(BF16)
    else:
        o["qa"][...] = seg("qa") * scale_a
        o["qb"][...] = seg("qb") * scale_b
        o["qi"][...] = seg("qi")


def _project(x, w_packed, wt_packed, tm, feature_major):
    n, d = x.shape
    nat = lambda w, dt=F32: (jax.ShapeDtypeStruct((n, w), dt), pl.BlockSpec((tm, w), lambda i: (i, 0)))
    tr = lambda w, dt=F32: (jax.ShapeDtypeStruct((w, n), dt), pl.BlockSpec((w, tm), lambda i: (0, i)))
    outs = [nat(256), nat(256), nat(256), nat(256), nat(LANES), nat(256), nat(256), nat(256)]
    if feature_major:
        names = _PROJ_COMMON + _PROJ_FEATURE_MAJOR
        outs += [nat(256, BF16), nat(256, BF16), tr(256), tr(256), tr(256), tr(LANES), tr(256, BF16), tr(256, BF16)]
    else:
        names = _PROJ_COMMON + _PROJ_ROW_MAJOR
        outs += [nat(256), nat(256), nat(256)]
    kern = functools.partial(_proj_kernel, names=names, scale_a=float(32 ** -0.5), scale_b=float(HEAD_W ** -0.5))
    res = pl.pallas_call(
        kern, grid=(n // tm,),
        in_specs=[pl.BlockSpec((tm, d), lambda i: (i, 0)),
                  pl.BlockSpec((d, _N_PACKED), lambda i: (0, 0)),
                  pl.BlockSpec((_N_PACKED_T, d), lambda i: (0, 0))],
        out_specs=[o[1] for o in outs], out_shape=[o[0] for o in outs],
        compiler_params=_cparams(("parallel",)), name="proj",
    )(x, w_packed, wt_packed)
    return dict(zip(names, res))


def _lane_heads(o, l, base, rows, lane_head):
    out = None
    for h in range(N_HEADS):
        r0 = (base + h) * rows
        part = jnp.where(lane_head == h, o[r0:r0 + rows] / l[r0:r0 + rows], 0.0)
        out = part if out is None else out + part
    return out


def _sample_attn_kernel(pt_ref, *refs, n_pages, page, rows, past, topk, idx_bits, lam_init, t_valid):
    npg = n_pages
    ak, av, bk, bv, kic = (refs[i * npg:(i + 1) * npg] for i in range(5))
    (nak, nav, nbk, nbv, nki, qa_ref, qb_ref, qi_ref, kiwi_ref, lam_ref, g_ref, gm_ref,
     oa_ref, ob_ref, y_ref) = refs[5 * npg:]
    nblk = npg + 1
    s_tot = nblk * page
    lane_head = lax.broadcasted_iota(I32, (1, GROUP_W), 1) // HEAD_W
    col_last = past + lax.broadcasted_iota(I32, (1, page), 1)
    t_row = lax.broadcasted_iota(I32, (rows, 1), 0)
    int_min = jnp.int32(INT_MIN)

    def blocks(page_refs, new_ref, dtype):
        new = new_ref[0]
        pad = jnp.zeros((page - rows, new.shape[-1]), F32)
        return [r[0].astype(dtype) for r in page_refs] + [jnp.concatenate([new, pad], axis=0).astype(dtype)]

    def softmax_pv(q_stack, n_groups, k_blocks, v_blocks, masks):
        def masked(s, mk, fill):
            if mk is None:
                return s
            s3 = jnp.where(mk[None], s.reshape(n_groups, rows, page), fill)
            return s3.reshape(n_groups * rows, page)
        s = [masked(lax.dot_general(q_stack, kb, _NT, preferred_element_type=F32), mk, NEG_BIG)
             for kb, mk in zip(k_blocks, masks)]
        m = s[0].max(axis=1, keepdims=True)
        for sj in s[1:]:
            m = jnp.maximum(m, sj.max(axis=1, keepdims=True))
        l = None
        o = None
        for sj, vb, mk in zip(s, v_blocks, masks):
            p = masked(jnp.exp(sj - m), mk, 0.0)
            lj = p.sum(axis=1, keepdims=True)
            oj = jnp.dot(p.astype(BF16), vb, preferred_element_type=F32)
            l = lj if l is None else l + lj
            o = oj if o is None else o + oj
        return o, l

    causal_last = col_last <= (past + t_row)

    qa = qa_ref[0]
    lane_grp = lax.broadcasted_iota(I32, (1, GROUP_W), 1) // 32
    qa_stack = jnp.concatenate(
        [jnp.where(lane_grp == (2 * h + c), qa, 0.0) for c in range(2) for h in range(N_HEADS)],
        axis=0).astype(BF16)
    o, l = softmax_pv(qa_stack, 2 * N_HEADS, blocks(ak, nak, BF16), blocks(av, nav, BF16),
                      [None] * npg + [causal_last])
    o0 = _lane_heads(o, l, 0, rows, lane_head)
    o1 = _lane_heads(o, l, N_HEADS, rows, lane_head)
    lv = lam_ref[...]
    lam = (jnp.exp(jnp.sum(lv[0:1] * lv[1:2], axis=1, keepdims=True))
           - jnp.exp(jnp.sum(lv[2:3] * lv[3:4], axis=1, keepdims=True)) + lam_init)
    od = o0 - lam * o1
    hi, lo = _split_bf16(od * od)
    ms = (jnp.dot(hi, gm_ref[...], preferred_element_type=F32)
          + jnp.dot(lo, gm_ref[...], preferred_element_type=F32))
    oa_ref[0] = od * lax.rsqrt(ms + LN_EPS) * g_ref[...] * (1.0 - lam_init)

    qi = qi_ref[0]
    qi_rows = jnp.concatenate([qi[:, h * D_IDX:(h + 1) * D_IDX] for h in range(N_HEADS)], axis=0)
    q_hi, q_lo = _split_bf16(qi_rows)
    kiwi = kiwi_ref[0]
    w_h = [kiwi[:, D_IDX + h:D_IDX + h + 1] for h in range(N_HEADS)]
    key_blocks = []
    for j, kib in enumerate(blocks(kic, nki, F32)):
        k_hi, k_lo = _split_bf16(kib)
        sc = (lax.dot_general(q_hi, k_hi, _NT, preferred_element_type=F32)
              + lax.dot_general(q_lo, k_hi, _NT, preferred_element_type=F32)
              + lax.dot_general(q_hi, k_lo, _NT, preferred_element_type=F32))
        isc = jnp.maximum(sc[0:rows], 0.0) * w_h[0]
        for h in range(1, N_HEADS):
            isc = isc + jnp.maximum(sc[h * rows:(h + 1) * rows], 0.0) * w_h[h]
        isc = jnp.where(isc == 0.0, 0.0, isc)
        bits = pltpu.bitcast(isc, I32)
        key = bits ^ ((bits >> 31) & jnp.int32(0x7FFFFFFF))
        if j == npg:
            key = jnp.where(causal_last, key, int_min)
        key_blocks.append(key)
    keys = jnp.concatenate(key_blocks, axis=1)
    cols = lax.broadcasted_iota(I32, (1, s_tot), 1)
    kk = jnp.float32(topk)
    count = lambda pred: jnp.sum(jnp.where(pred, 1.0, 0.0), axis=1, keepdims=True)

    def bit_body(r, prefix):
        cand_u = prefix | lax.shift_left(jnp.int32(1), 31 - r)
        return jnp.where(count(keys >= (cand_u ^ int_min)) >= kk, cand_u, prefix)

    thr = lax.fori_loop(0, 32, bit_body, jnp.zeros((rows, 1), I32)) ^ int_min
    r_need = kk - count(keys > thr)
    any_tie = jnp.max(jnp.where(t_row < t_valid, count(keys >= thr) - kk, 0.0)) > 0.0
    y_ref[...] = jnp.full((rows, 1), s_tot, I32)

    @pl.when(any_tie)
    def _():
        def ybody(r, y):
            cand = y | lax.shift_left(jnp.int32(1), idx_bits - 1 - r)
            return jnp.where(count((keys == thr) & (cols < cand)) < r_need, cand, y)
        y_ref[...] = lax.fori_loop(0, idx_bits, ybody, jnp.zeros((rows, 1), I32))

    sel = ((keys > thr) | ((keys == thr) & (cols <= y_ref[...]))) & (cols <= (past + t_row))
    qb = qb_ref[0]
    qb_stack = jnp.concatenate([jnp.where(lane_head == h, qb, 0.0) for h in range(N_HEADS)], axis=0).astype(BF16)
    o, l = softmax_pv(qb_stack, N_HEADS, blocks(bk, nbk, BF16), blocks(bv, nbv, BF16),
                      [sel[:, j * page:(j + 1) * page] for j in range(nblk)])
    ob_ref[0] = _lane_heads(o, l, 0, rows, lane_head)


def _sample_attention(page_table, caches, news, qa, qb, qi, kiwi, a_lambda, g_full, gmean,
                      *, topk, lam_init, t_valid):
    bsz, n_pages = page_table.shape
    page = caches[0].shape[1]
    rows = qa.shape[1]
    past = n_pages * page
    idx_bits = int(math.ceil(math.log2((n_pages + 1) * page))) + 1
    page_spec = lambda c, j: pl.BlockSpec((1, page, c), lambda b, pt: (pt[b, j], 0, 0))
    seq_spec = lambda c: pl.BlockSpec((1, rows, c), lambda b, pt: (b, 0, 0))
    full2 = lambda a: pl.BlockSpec(a.shape, lambda b, pt: (0, 0))
    in_specs, args = [], []
    for cache in caches:
        for j in range(n_pages):
            in_specs.append(page_spec(cache.shape[-1], j))
            args.append(cache)
    for a in list(news) + [qa, qb, qi, kiwi]:
        in_specs.append(seq_spec(a.shape[-1]))
        args.append(a)
    for a in (a_lambda, g_full, gmean):
        in_specs.append(full2(a))
        args.append(a)
    kern = functools.partial(_sample_attn_kernel, n_pages=n_pages, page=page, rows=rows, past=past, topk=topk,
                             idx_bits=idx_bits, lam_init=lam_init, t_valid=t_valid)
    return pl.pallas_call(
        kern,
        grid_spec=pltpu.PrefetchScalarGridSpec(
            num_scalar_prefetch=1, grid=(bsz,), in_specs=in_specs,
            out_specs=[seq_spec(GROUP_W), seq_spec(GROUP_W)],
            scratch_shapes=[pltpu.VMEM((rows, 1), I32)]),
        out_shape=[jax.ShapeDtypeStruct((bsz, rows, GROUP_W), F32)] * 2,
        compiler_params=_cparams(("parallel",)), name="sample_attn",
    )(page_table, *args)


def _attn_core(qst, n_groups, k_ref, vt_ref, n_full, n_kv, tq, tk, mask_fn, m_ref, l_ref, acc_ref, zero_masked):
    n_comp = n_groups // N_HEADS
    row_head = lax.broadcasted_iota(I32, (GROUP_W, 1), 0) // HEAD_W
    m_ref[...] = jnp.full(m_ref.shape, NEG_BIG, F32)
    l_ref[...] = jnp.zeros(l_ref.shape, F32)
    acc_ref[...] = jnp.zeros(acc_ref.shape, F32)
    grp = lambda a, g: a[:, g * tq:(g + 1) * tq]

    def step(c, masked):
        k0 = pl.multiple_of(c * tk, tk)
        kblk = k_ref[pl.ds(k0, tk), :]
        vt = vt_ref[:, pl.ds(k0, tk)]
        s = jnp.dot(kblk, qst, preferred_element_type=F32)
        if masked:
            mask = mask_fn(c, k0)
            s = jnp.concatenate([jnp.where(mask, grp(s, g), NEG_BIG) for g in range(n_groups)], axis=1)
        m_old = m_ref[...]
        m_new = jnp.maximum(m_old, jnp.max(s, axis=0, keepdims=True))
        alpha = jnp.exp(m_old - m_new)
        p = jnp.exp(s - m_new)
        if masked and zero_masked:
            p = jnp.concatenate([jnp.where(mask, grp(p, g), 0.0) for g in range(n_groups)], axis=1)
        l_ref[...] = alpha * l_ref[...] + jnp.sum(p, axis=0, keepdims=True)
        m_ref[...] = m_new
        pb = p.astype(BF16)
        vstack = jnp.concatenate(
            [jnp.where(row_head == h, vt, jnp.zeros_like(vt)) for h in range(N_HEADS)], axis=1)
        for comp in range(n_comp):
            base = comp * N_HEADS
            pcat = jnp.concatenate([grp(pb, base + h) for h in range(N_HEADS)], axis=0)
            pv = jnp.dot(vstack, pcat, preferred_element_type=F32)
            af = jnp.concatenate(
                [jnp.broadcast_to(grp(alpha, base + h), (HEAD_W, tq)) for h in range(N_HEADS)], axis=0)
            acc_ref[comp] = acc_ref[comp] * af + pv

    def full_body(c, carry):
        step(c, False)
        return carry

    def masked_body(c, carry):
        step(c, True)
        return carry

    lax.fori_loop(0, n_full, full_body, 0)
    lax.fori_loop(n_full, n_kv, masked_body, 0)


def _normalised(acc, l_row, base, tq):
    rl = 1.0 / l_row
    rf = jnp.concatenate(
        [jnp.broadcast_to(rl[:, (base + h) * tq:(base + h + 1) * tq], (HEAD_W, tq)) for h in range(N_HEADS)],
        axis=0)
    return acc * rf


def _num_kv_blocks(q0, tq, tk, past, s_tot):
    return jnp.minimum((past + q0 + tq + tk - 1) // tk, s_tot // tk)


def _attn_a_kernel(qt_ref, k_ref, vt_ref, lam_ref, g_ref, gm_ref, o_ref, m_ref, l_ref, acc_ref,
                   *, tq, tk, past, s_tot, lam_init):
    q0 = pl.program_id(1) * tq
    qt = qt_ref[...]
    row_grp = lax.broadcasted_iota(I32, (GROUP_W, 1), 0) // 32
    qst = jnp.concatenate(
        [jnp.where(row_grp == (2 * h + c), qt, 0.0) for c in range(2) for h in range(N_HEADS)],
        axis=1).astype(BF16)
    n_kv = _num_kv_blocks(q0, tq, tk, past, s_tot)
    n_full = jnp.minimum((past + q0 + 1) // tk, n_kv)
    row_pos = past + q0 + lax.broadcasted_iota(I32, (1, tq), 1)
    key_iota = lax.broadcasted_iota(I32, (tk, 1), 0)

    def mask_fn(c, k0):
        return (k0 + key_iota) <= row_pos

    _attn_core(qst, 2 * N_HEADS, k_ref, vt_ref, n_full, n_kv, tq, tk, mask_fn,
               m_ref, l_ref, acc_ref, zero_masked=False)
    l_row = l_ref[...]
    o0 = _normalised(acc_ref[0], l_row, 0, tq)
    o1 = _normalised(acc_ref[1], l_row, N_HEADS, tq)
    lv = lam_ref[...]
    lam = (jnp.exp(jnp.sum(lv[0:1] * lv[1:2], axis=1, keepdims=True))
           - jnp.exp(jnp.sum(lv[2:3] * lv[3:4], axis=1, keepdims=True)) + lam_init)
    o = (o0 - lam * o1).T
    hi, lo = _split_bf16(o * o)
    ms = (jnp.dot(hi, gm_ref[...], preferred_element_type=F32)
          + jnp.dot(lo, gm_ref[...], preferred_element_type=F32))
    o_ref[...] = o * lax.rsqrt(ms + LN_EPS) * g_ref[...] * (1.0 - lam_init)


def _attention_a(qt, k, vt, a_lambda, g_full, gmean, *, bsz, tq, tk, past, lam_init):
    t = qt.shape[1] // bsz
    s_tot = k.shape[0] // bsz
    nq = t // tq
    kern = functools.partial(_attn_a_kernel, tq=tq, tk=tk, past=past, s_tot=s_tot, lam_init=lam_init)
    return pl.pallas_call(
        kern, grid=(bsz, nq),
        in_specs=[pl.BlockSpec((GROUP_W, tq), lambda b, i: (0, b * nq + i)),
                  pl.BlockSpec((s_tot, GROUP_W), lambda b, i: (b, 0)),
                  pl.BlockSpec((GROUP_W, s_tot), lambda b, i: (0, b)),
                  pl.BlockSpec(a_lambda.shape, lambda b, i: (0, 0)),
                  pl.BlockSpec((1, GROUP_W), lambda b, i: (0, 0)),
                  pl.BlockSpec((GROUP_W, GROUP_W), lambda b, i: (0, 0))],
        out_specs=pl.BlockSpec((tq, GROUP_W), lambda b, i: (b * nq + i, 0)),
        out_shape=jax.ShapeDtypeStruct((bsz * t, GROUP_W), F32),
        scratch_shapes=[pltpu.VMEM((1, 2 * N_HEADS * tq), F32), pltpu.VMEM((1, 2 * N_HEADS * tq), F32),
                        pltpu.VMEM((2, GROUP_W, tq), F32)],
        compiler_params=_cparams(("parallel", "arbitrary")), name="attn_a",
    )(qt, k, vt, a_lambda, g_full, gmean)


def _dsa_kernel(qt_ref, k_ref, vt_ref, qit_ref, ki_ref, kiwit_ref, o_ref, key_ref, y_ref, m_ref, l_ref, acc_ref,
                *, tq, tk, past, s_tot, topk, idx_bits, t_valid):
    q0 = pl.program_id(1) * tq
    n_kv = _num_kv_blocks(q0, tq, tk, past, s_tot)
    row_pos = past + q0 + lax.broadcasted_iota(I32, (1, tq), 1)
    key_iota = lax.broadcasted_iota(I32, (tk, 1), 0)
    int_min = jnp.int32(INT_MIN)

    qit = qit_ref[...]
    blocks = []
    for h in range(N_HEADS):
        qh = qit[h * D_IDX:(h + 1) * D_IDX, :]
        hi, lo = _split_bf16(qh)
        blocks.append(jnp.concatenate([hi, lo, hi], axis=0))
    qi = jnp.concatenate(blocks, axis=1)
    kiwit = kiwit_ref[...]
    w_h = [kiwit[D_IDX + h:D_IDX + h + 1, :] for h in range(N_HEADS)]

    def score_body(c, carry):
        k0 = pl.multiple_of(c * tk, tk)
        kib = ki_ref[pl.ds(k0, tk), :]
        sc = jnp.dot(kib, qi, preferred_element_type=F32)
        isc = jnp.maximum(sc[:, 0:tq], 0.0) * w_h[0]
        for h in range(1, N_HEADS):
            isc = isc + jnp.maximum(sc[:, h * tq:(h + 1) * tq], 0.0) * w_h[h]
        isc = jnp.where(isc == 0.0, 0.0, isc)
        bits = pltpu.bitcast(isc, I32)
        key = bits ^ ((bits >> 31) & jnp.int32(0x7FFFFFFF))
        key_ref[c] = jnp.where((k0 + key_iota) <= row_pos, key, int_min)
        return carry

    lax.fori_loop(0, n_kv, score_body, 0)

    def count(pred):
        def block_count(c):
            ones = jnp.where(pred(key_ref[c], c * tk), 1.0, 0.0)
            return jnp.sum(ones.reshape(tk // 8, 8, tq), axis=0)

        def body(i, acc):
            return acc + block_count(2 * i) + block_count(2 * i + 1)

        acc = lax.fori_loop(0, n_kv // 2, body, jnp.zeros((8, tq), F32))
        odd = (n_kv & 1).astype(F32)
        acc = acc + odd * block_count(n_kv - 1)
        return jnp.sum(acc, axis=0, keepdims=True)

    kk = jnp.float32(topk)

    def bit_body(r, prefix):
        cand_u = prefix | lax.shift_left(jnp.int32(1), 31 - r)
        cand_s = cand_u ^ int_min
        cnt = count(lambda blk, k0: blk >= cand_s)
        return jnp.where(cnt >= kk, cand_u, prefix)

    prefix = lax.fori_loop(0, 32, bit_body, jnp.zeros((1, tq), I32))
    thr = prefix ^ int_min

    n_gt = count(lambda blk, k0: blk > thr)
    n_ge = count(lambda blk, k0: blk >= thr)
    r_need = kk - n_gt
    live = lax.broadcasted_iota(I32, (1, tq), 1) < t_valid
    any_tie = jnp.max(jnp.where(live, n_ge - kk, 0.0)) > 0.0
    y_ref[...] = jnp.full((1, tq), s_tot, I32)

    @pl.when(any_tie)
    def _():
        def ybody(r, y):
            cand = y | lax.shift_left(jnp.int32(1), idx_bits - 1 - r)
            cnt = count(lambda blk, k0: (blk == thr) & ((k0 + key_iota) < cand))
            return jnp.where(cnt < r_need, cand, y)
        y_ref[...] = lax.fori_loop(0, idx_bits, ybody, jnp.zeros((1, tq), I32))

    y_last = y_ref[...]

    def mask_fn(c, k0):
        blk = key_ref[c]
        pos = k0 + key_iota
        sel = (blk > thr) | ((blk == thr) & (pos <= y_last))
        return sel & (pos <= row_pos)

    qt = qt_ref[...]
    row_head = lax.broadcasted_iota(I32, (GROUP_W, 1), 0) // HEAD_W
    qst = jnp.concatenate([jnp.where(row_head == h, qt, 0.0) for h in range(N_HEADS)], axis=1).astype(BF16)
    _attn_core(qst, N_HEADS, k_ref, vt_ref, 0, n_kv, tq, tk, mask_fn, m_ref, l_ref, acc_ref, zero_masked=True)
    o_ref[...] = _normalised(acc_ref[0], l_ref[...], 0, tq).T


def _dsa(qt, k, vt, qit, ki3, kiwit, *, bsz, tq, tk, past, topk, t_valid):
    t = qt.shape[1] // bsz
    s_tot = k.shape[0] // bsz
    nq = t // tq
    idx_bits = int(math.ceil(math.log2(s_tot))) + 1
    kern = functools.partial(_dsa_kernel, tq=tq, tk=tk, past=past, s_tot=s_tot, topk=topk,
                             idx_bits=idx_bits, t_valid=t_valid)
    qspec = lambda w: pl.BlockSpec((w, tq), lambda b, i: (0, b * nq + i))
    return pl.pallas_call(
        kern, grid=(bsz, nq),
        in_specs=[qspec(GROUP_W),
                  pl.BlockSpec((s_tot, GROUP_W), lambda b, i: (b, 0)),
                  pl.BlockSpec((GROUP_W, s_tot), lambda b, i: (0, b)),
                  qspec(GROUP_W),
                  pl.BlockSpec((s_tot, 3 * D_IDX), lambda b, i: (b, 0)),
                  qspec(LANES)],
        out_specs=pl.BlockSpec((tq, GROUP_W), lambda b, i: (b * nq + i, 0)),
        out_shape=jax.ShapeDtypeStruct((bsz * t, GROUP_W), F32),
        scratch_shapes=[pltpu.VMEM((s_tot // tk, tk, tq), I32), pltpu.VMEM((1, tq), I32),
                        pltpu.VMEM((1, N_HEADS * tq), F32), pltpu.VMEM((1, N_HEADS * tq), F32),
                        pltpu.VMEM((1, GROUP_W, tq), F32)],
        compiler_params=_cparams(("parallel", "arbitrary")), name="dsa",
    )(qt, k, vt, qit, ki3, kiwit)


_HALO_C = 8
_HALO_D = 32


def _conv_kernel(*refs, tm, single_tile):
    if single_tile:
        (cb_ref, uc_ref, ud_ref, hc_ref, hd_ref, cw_ref, dw_ref, db_ref, lg_ref, lb_ref,
         oc_ref, od_ref, ec_ref, ed_ref) = refs
        pc_ref = pd_ref = None
    else:
        (cb_ref, uc_ref, ud_ref, pc_ref, pd_ref, hc_ref, hd_ref, cw_ref, dw_ref, db_ref, lg_ref, lb_ref,
         oc_ref, od_ref, ec_ref, ed_ref) = refs
    i = pl.program_id(1)

    @pl.when(i == 0)
    def _():
        ec_ref[0:_HALO_C] = hc_ref[0]
        ed_ref[0:_HALO_D] = hd_ref[0]

    if not single_tile:
        @pl.when(i > 0)
        def _():
            ec_ref[0:_HALO_C] = pc_ref[0]
            ed_ref[0:_HALO_D] = pd_ref[0]

    ec_ref[_HALO_C:_HALO_C + tm] = uc_ref[0]
    ed_ref[_HALO_D:_HALO_D + tm] = ud_ref[0]

    cw = cw_ref[...]
    acc = ec_ref[pl.ds(_HALO_C - (C_WIDTH - 1), tm), :] * cw[0:1]
    for w in range(1, C_WIDTH):
        acc = acc + ec_ref[pl.ds(_HALO_C - (C_WIDTH - 1) + w, tm), :] * cw[w:w + 1]
    oc_ref[0] = cb_ref[0] * acc

    dw = dw_ref[...]
    acc = ed_ref[pl.ds(_HALO_D - (D_WIDTH - 1), tm), :] * dw[0:1]
    for w in range(1, D_WIDTH):
        acc = acc + ed_ref[pl.ds(_HALO_D - (D_WIDTH - 1) + w, tm), :] * dw[w:w + 1]
    z = _layer_norm(acc + db_ref[...], lg_ref[...], lb_ref[...])
    od_ref[0] = z * _sigmoid(z)


def _convs(cb, uc, ud, hist_c, hist_d, c_w, d_w, d_b, ln_g, ln_b, *, tm):
    bsz, t, w = cb.shape
    nt = t // tm
    single = nt == 1
    main = pl.BlockSpec((1, tm, w), lambda b, i: (b, i, 0))
    prev_c = pl.BlockSpec((1, _HALO_C, w), lambda b, i: (b, jnp.maximum(i * (tm // _HALO_C) - 1, 0), 0))
    prev_d = pl.BlockSpec((1, _HALO_D, w), lambda b, i: (b, jnp.maximum(i * (tm // _HALO_D) - 1, 0), 0))
    hist_cs = pl.BlockSpec((1, _HALO_C, w), lambda b, i: (b, 0, 0))
    hist_ds = pl.BlockSpec((1, _HALO_D, w), lambda b, i: (b, 0, 0))
    full2 = lambda a: pl.BlockSpec(a.shape, lambda b, i: (0, 0))
    in_specs = [main, main, main] + ([] if single else [prev_c, prev_d]) + [hist_cs, hist_ds] + \
        [full2(c_w), full2(d_w), full2(d_b), full2(ln_g), full2(ln_b)]
    args = [cb, uc, ud] + ([] if single else [uc, ud]) + [hist_c, hist_d, c_w, d_w, d_b, ln_g, ln_b]
    kern = functools.partial(_conv_kernel, tm=tm, single_tile=single)
    return pl.pallas_call(
        kern, grid=(bsz, nt), in_specs=in_specs,
        out_specs=[main, main],
        out_shape=[jax.ShapeDtypeStruct((bsz, t, w), F32)] * 2,
        scratch_shapes=[pltpu.VMEM((_HALO_C + tm, w), F32), pltpu.VMEM((_HALO_D + tm, w), F32)],
        compiler_params=_cparams(("parallel", "arbitrary")), name="convs",
    )(*args)


def _mix_kernel(x_ref, oa_ref, ob_ref, oc_ref, od_ref, pw_ref, pwb_ref, wo_ref, g_ref, b_ref, h_ref, *, alpha):
    od = jnp.dot(od_ref[...].astype(BF16), pw_ref[...], preferred_element_type=F32) + pwb_ref[...]
    cat = jnp.concatenate([oa_ref[...], ob_ref[...], oc_ref[...], od], axis=1).astype(BF16)
    mix = jnp.dot(cat, wo_ref[...], preferred_element_type=F32)
    h_ref[...] = _layer_norm(alpha * x_ref[...] + mix, g_ref[...], b_ref[...])


def _mix(x, oa, ob, oc, od, pw, pwb, wo, g, b, *, tm, alpha):
    n, d = x.shape
    row = lambda w: pl.BlockSpec((tm, w), lambda i: (i, 0))
    full = lambda a: pl.BlockSpec(a.shape, lambda i: (0, 0))
    return pl.pallas_call(
        functools.partial(_mix_kernel, alpha=alpha), grid=(n // tm,),
        in_specs=[row(d), row(GROUP_W), row(GROUP_W), row(GROUP_W), row(GROUP_W),
                  full(pw), full(pwb), full(wo), full(g), full(b)],
        out_specs=row(d), out_shape=jax.ShapeDtypeStruct((n, d), F32),
        compiler_params=_cparams(("parallel",)), name="mix",
    )(x, oa, ob, oc, od, pw, pwb, wo, g, b)


def _staircase():
    return [(a, b) for a in range(PEER_TOPK) for b in range(PEER_TOPK) if (a + 1) * (b + 1) <= PEER_TOPK]


def _extract_top(x, payload, n_out, n_rows):
    row = lax.broadcasted_iota(I32, x.shape, 0).astype(F32)
    vals, idxs, pays = [], [], []
    for _ in range(n_out):
        m = jnp.max(x, axis=0, keepdims=True)
        idx = jnp.min(jnp.where(x == m, row, float(n_rows)), axis=0, keepdims=True)
        hit = row == idx
        vals.append(m)
        idxs.append(idx)
        if payload is not None:
            pays.append(jnp.max(jnp.where(hit, payload, -1.0), axis=0, keepdims=True))
        x = jnp.where(hit, -jnp.inf, x)
    return vals, idxs, pays


def _peer_select_kernel(h_ref, wq_ref, sk_ref, eid_ref, g_ref, sv_ref, si_ref, *, tm):
    ht = h_ref[...].T.astype(BF16)
    n_groups = 2 * PEER_HEADS

    def group_body(g, carry):
        r0 = pl.multiple_of(g * N_KEYS, N_KEYS)
        qg = jnp.dot(wq_ref[pl.ds(r0, N_KEYS), :], ht, preferred_element_type=F32)
        st = jnp.dot(sk_ref[g % 2], qg.astype(BF16), preferred_element_type=F32)
        vals, idxs, _ = _extract_top(st, None, PEER_TOPK, N_KEYS)
        sv_ref[g] = jnp.concatenate(vals, axis=0)
        si_ref[g] = jnp.concatenate(idxs, axis=0)
        return carry

    lax.fori_loop(0, n_groups, group_body, 0)

    pairs = _staircase()
    n_cand = len(pairs)
    n_pad = (-n_cand) % 8

    def head_body(hd, carry):
        sv0 = sv_ref[2 * hd]; sv1 = sv_ref[2 * hd + 1]
        si0 = si_ref[2 * hd]; si1 = si_ref[2 * hd + 1]
        cv = [sv0[a:a + 1] + sv1[b:b + 1] for a, b in pairs]
        ce = [si0[a:a + 1] * float(N_KEYS) + si1[b:b + 1] for a, b in pairs]
        if n_pad:
            cv.append(jnp.full((n_pad, tm), -jnp.inf, F32))
            ce.append(jnp.full((n_pad, tm), -1.0, F32))
        cand = jnp.concatenate(cv, axis=0)
        cand_e = jnp.concatenate(ce, axis=0)
        vals, _, pays = _extract_top(cand, cand_e, PEER_TOPK, n_cand + n_pad)
        fv = jnp.concatenate(vals, axis=0)
        e = jnp.exp(fv - fv[0:1])
        gate = e / jnp.sum(e, axis=0, keepdims=True)
        r0 = pl.multiple_of(hd * PEER_TOPK, PEER_TOPK)
        eid_ref[pl.ds(r0, PEER_TOPK), :] = jnp.concatenate(pays, axis=0).astype(I32)
        g_ref[pl.ds(r0, PEER_TOPK), :] = gate
        return carry

    lax.fori_loop(0, PEER_HEADS, head_body, 0)


def _peer_select(h, wq_t, sk, *, tm):
    n, d = h.shape
    n_slots = PEER_HEADS * PEER_TOPK
    return pl.pallas_call(
        functools.partial(_peer_select_kernel, tm=tm), grid=(n // tm,),
        in_specs=[pl.BlockSpec((tm, d), lambda i: (i, 0)),
                  pl.BlockSpec(wq_t.shape, lambda i: (0, 0)),
                  pl.BlockSpec(sk.shape, lambda i: (0, 0, 0))],
        out_specs=[pl.BlockSpec((n_slots, tm), lambda i: (0, i)),
                   pl.BlockSpec((n_slots, tm), lambda i: (0, i))],
        out_shape=[jax.ShapeDtypeStruct((n_slots, n), I32), jax.ShapeDtypeStruct((n_slots, n), F32)],
        scratch_shapes=[pltpu.VMEM((2 * PEER_HEADS, PEER_TOPK, tm), F32),
                        pltpu.VMEM((2 * PEER_HEADS, PEER_TOPK, tm), F32)],
        compiler_params=_cparams(("parallel",)), name="peer_select",
    )(h, wq_t, sk)


_GATE_TOKENS = 8


def _peer_gates_kernel(eid_ref, g_ref, w_ref, *, tm):
    sub = lax.broadcasted_iota(I32, (N_KEYS, PEER_HEADS * PEER_TOPK), 0)

    def token_gates(t):
        e = eid_ref[pl.ds(t, 1), :]
        g = g_ref[pl.ds(t, 1), :]
        g_hi = g.astype(BF16).astype(F32)
        g_lo = g - g_hi
        oh_i = jnp.where((e >> 7) == sub, 1.0, 0.0).astype(BF16)
        hit_j = (e & (N_KEYS - 1)) == sub
        gj_hi = jnp.where(hit_j, g_hi, 0.0).astype(BF16)
        gj_lo = jnp.where(hit_j, g_lo, 0.0).astype(BF16)
        lhs = jnp.concatenate([oh_i, oh_i], axis=1)
        rhs = jnp.concatenate([gj_hi, gj_lo], axis=1)
        return lax.dot_general(lhs, rhs, _NT, preferred_element_type=F32)

    def body(b, carry):
        t0 = pl.multiple_of(b * _GATE_TOKENS, _GATE_TOKENS)
        w = jnp.stack([token_gates(t0 + k) for k in range(_GATE_TOKENS)], axis=0)
        w_ref[:, pl.ds(t0, _GATE_TOKENS), :] = pltpu.einshape("tij->itj", w)
        return carry

    lax.fori_loop(0, tm // _GATE_TOKENS, body, 0)


def _peer_gates(eid, gate, *, tm):
    n, n_slots = eid.shape
    return pl.pallas_call(
        functools.partial(_peer_gates_kernel, tm=tm), grid=(n // tm,),
        in_specs=[pl.BlockSpec((tm, n_slots), lambda i: (i, 0)),
                  pl.BlockSpec((tm, n_slots), lambda i: (i, 0))],
        out_specs=pl.BlockSpec((N_KEYS, tm, N_KEYS), lambda i: (0, i, 0)),
        out_shape=jax.ShapeDtypeStruct((N_KEYS, n, N_KEYS), F32),
        compiler_params=_cparams(("parallel",)), name="peer_gates",
    )(eid, gate)


def _gelu(x):
    return 0.5 * x * (1.0 + lax.erf(x * (2.0 ** -0.5)))


def _peer_dense_kernel(h_ref, w_ref, u_ref, v_ref, f_ref):
    @pl.when(pl.program_id(1) == 0)
    def _():
        f_ref[...] = jnp.zeros(f_ref.shape, F32)

    act = _gelu(lax.dot_general(h_ref[...], u_ref[...], _NT, preferred_element_type=F32))
    y = jnp.concatenate(
        [w_ref[i] * act[:, i * N_KEYS:(i + 1) * N_KEYS] for i in range(w_ref.shape[0])],
        axis=1).astype(BF16)
    f_ref[...] += jnp.dot(y, v_ref[...], preferred_element_type=F32)


def _peer_dense(hb, wmat, u, v, *, tm, te):
    n, d = hb.shape
    n_exp = u.shape[0]
    return pl.pallas_call(
        _peer_dense_kernel, grid=(n // tm, n_exp // te),
        in_specs=[pl.BlockSpec((tm, d), lambda i, e: (i, 0)),
                  pl.BlockSpec((te // N_KEYS, tm, N_KEYS), lambda i, e: (e, i, 0)),
                  pl.BlockSpec((te, d), lambda i, e: (e, 0)),
                  pl.BlockSpec((te, d), lambda i, e: (e, 0))],
        out_specs=pl.BlockSpec((tm, d), lambda i, e: (i, 0)),
        out_shape=jax.ShapeDtypeStruct((n, d), F32),
        compiler_params=_cparams(("parallel", "arbitrary")), name="peer_dense",
    )(hb, wmat, u, v)


def _final_kernel(h_ref, f_ref, p_ref, g_ref, b_ref, wg_ref, bg_ref, wp_ref, y_ref, *, alpha):
    h2 = _layer_norm(alpha * h_ref[...] + f_ref[...], g_ref[...], b_ref[...])
    gate = _sigmoid(jnp.dot(h2.astype(BF16), wg_ref[...], preferred_element_type=F32) + bg_ref[...])
    emb = jnp.dot(p_ref[...].astype(BF16), wp_ref[...], preferred_element_type=F32)
    y_ref[...] = h2 + gate * emb


def _final(h, f, p, g, b, wg, bg, wp, *, tm, alpha):
    n, d = h.shape
    row = lambda w: pl.BlockSpec((tm, w), lambda i: (i, 0))
    full = lambda a: pl.BlockSpec(a.shape, lambda i: (0, 0))
    return pl.pallas_call(
        functools.partial(_final_kernel, alpha=alpha), grid=(n // tm,),
        in_specs=[row(d), row(d), row(p.shape[1]), full(g), full(b), full(wg), full(bg), full(wp)],
        out_specs=row(d), out_shape=jax.ShapeDtypeStruct((n, d), F32),
        compiler_params=_cparams(("parallel",)), name="final",
    )(h, f, p, g, b, wg, bg, wp)


def _pack_w_in(w_in):
    d = w_in.shape[0]
    cols = [w_in[:, 0:1792], w_in[:, 1792:1860], jnp.zeros((d, 60), w_in.dtype), w_in[:, 1860:3140]]
    nat = jnp.concatenate(cols, axis=1).astype(BF16)
    seg = lambda name, width=GROUP_W: nat[:, _SEG[name]:_SEG[name] + width]
    tr = jnp.concatenate([seg("qa"), seg("qb"), seg("qi"), seg("kiwi", 128), seg("va"), seg("vb")], axis=1).T
    return nat, tr


def _row(a):
    return a.reshape(1, -1)


def _tile_rows(n, pref):
    t = pref
    while n % t:
        t //= 2
    return t


def _prep_weights(lw, depth):
    (w_in, a_lambda, a_subln_g, c_conv_w, d_conv_w, d_conv_b, d_ln_g, d_ln_b, d_pw_w, d_pw_b,
     w_out, ln1_g, ln1_b, peer_wq, peer_subkeys, peer_u, peer_v, ln2_g, ln2_b,
     ple_w, ple_gate_w, ple_gate_b) = lw
    head_of_lane = jnp.arange(GROUP_W) // HEAD_W
    gmean = (head_of_lane[:, None] == head_of_lane[None, :]).astype(BF16) * (1.0 / HEAD_W)
    w_nat, w_tr = _pack_w_in(w_in)
    return dict(
        w_in=w_nat, w_in_t=w_tr, a_lambda=a_lambda, a_g=_row(jnp.tile(a_subln_g, N_HEADS)),
        gmean=gmean.astype(BF16),
        c_w=c_conv_w, d_w=d_conv_w, d_b=_row(d_conv_b), d_ln_g=_row(d_ln_g), d_ln_b=_row(d_ln_b),
        d_pw=d_pw_w.astype(BF16), d_pwb=_row(d_pw_b), w_out=w_out.astype(BF16),
        ln1_g=_row(ln1_g), ln1_b=_row(ln1_b),
        wq_t=peer_wq.T.astype(BF16), sk=peer_subkeys.astype(BF16),
        u=peer_u.astype(BF16), v=peer_v.astype(BF16),
        ln2_g=_row(ln2_g), ln2_b=_row(ln2_b),
        ple_w=ple_w.astype(BF16), wg=ple_gate_w.astype(BF16), bg=_row(ple_gate_b),
        alpha=float((2 * depth) ** 0.25),
    )


def _index_keys(ki):
    k_hi = ki.astype(BF16)
    k_lo = (ki - k_hi.astype(F32)).astype(BF16)
    return jnp.concatenate([k_hi, k_hi, k_lo], axis=-1)


def _token_tail(w, x, oa, ob, oc, od, valid_rows, p):
    n = x.shape[0]
    h = _mix(x, oa, ob, oc, od, w["d_pw"], w["d_pwb"], w["w_out"], w["ln1_g"], w["ln1_b"],
             tm=_tile_rows(n, 512), alpha=w["alpha"])
    if valid_rows is not None:
        bsz, t_pad, t = valid_rows
        h = h.reshape(bsz, t_pad, -1)[:, :t].reshape(bsz * t, -1)
    n = h.shape[0]
    eid_t, gate_t = _peer_select(h, w["wq_t"], w["sk"], tm=_tile_rows(n, 256))
    wmat = _peer_gates(eid_t.T, gate_t.T, tm=_tile_rows(n, 128))
    f = _peer_dense(h.astype(BF16), wmat, w["u"], w["v"], tm=_tile_rows(n, 1024), te=8 * N_KEYS)
    return _final(h, f, p, w["ln2_g"], w["ln2_b"], w["wg"], w["bg"], w["ple_w"],
                  tm=_tile_rows(n, 512), alpha=w["alpha"])


def _layer(layer_idx, depth, xp, xs, caches, state_c, state_d, page_table, pp, ps, lw):
    w = _prep_weights(lw, depth)
    bp, tp, d = xp.shape
    bs, ts, _ = xs.shape
    ts_pad = 8
    n_pages = page_table.shape[1]
    page = caches[0].shape[1]
    past = n_pages * page
    lam_init = 0.8 - 0.6 * math.exp(-0.3 * (layer_idx + 1))

    pr = _project(xp.reshape(bp * tp, d), w["w_in"], w["w_in_t"], _tile_rows(bp * tp, 512), feature_major=True)
    tq = _tile_rows(tp, 256)
    tk = _tile_rows(tp, 256)
    oa = _attention_a(pr["qat"], pr["kab"], pr["vat"], w["a_lambda"], w["a_g"], w["gmean"],
                      bsz=bp, tq=tq, tk=tk, past=0, lam_init=lam_init)
    ob = _dsa(pr["qbt"], pr["kbb"], pr["vbt"], pr["qit"], _index_keys(pr["kiwi"][:, :D_IDX]), pr["kiwit"],
              bsz=bp, tq=tq, tk=tk, past=0, topk=min(DSA_TOPK, tp // 4), t_valid=tq)
    r3p = lambda a: a.reshape(bp, tp, a.shape[-1])
    zc = jnp.zeros((bp, _HALO_C, GROUP_W), F32)
    zd = jnp.zeros((bp, _HALO_D, GROUP_W), F32)
    oc, od = _convs(r3p(pr["cb"]), r3p(pr["uc"]), r3p(pr["ud"]), zc, zd, w["c_w"], w["d_w"], w["d_b"],
                    w["d_ln_g"], w["d_ln_b"], tm=_tile_rows(tp, 512))
    yp = _token_tail(w, xp.reshape(bp * tp, d), oa, ob, oc.reshape(bp * tp, -1), od.reshape(bp * tp, -1),
                     None, pp.reshape(bp * tp, -1)).reshape(bp, tp, d)
    new_p = (r3p(pr["ka"]), r3p(pr["va"]), r3p(pr["kb"]), r3p(pr["vb"]), r3p(pr["kiwi"])[:, :, :D_IDX],
             jnp.concatenate([zc, r3p(pr["uc"])], axis=1)[:, -(C_WIDTH - 1):],
             jnp.concatenate([zd, r3p(pr["ud"])], axis=1)[:, -(D_WIDTH - 1):])

    xs_pad = jnp.pad(xs, ((0, 0), (0, ts_pad - ts), (0, 0))).reshape(bs * ts_pad, d)
    sr = _project(xs_pad, w["w_in"], w["w_in_t"], _tile_rows(bs * ts_pad, 512), feature_major=False)
    r3s = lambda a: a.reshape(bs, ts_pad, a.shape[-1])
    news = [r3s(sr["ka"]), r3s(sr["va"]), r3s(sr["kb"]), r3s(sr["vb"]), r3s(sr["kiwi"])[:, :, :D_IDX]]
    oa_s, ob_s = _sample_attention(page_table, caches, news, r3s(sr["qa"]), r3s(sr["qb"]), r3s(sr["qi"]),
                                   r3s(sr["kiwi"]), w["a_lambda"], w["a_g"], w["gmean"],
                                   topk=min(DSA_TOPK, (past + ts) // 4), lam_init=lam_init, t_valid=ts)
    unpad = lambda a: a.reshape(bs * ts_pad, GROUP_W)
    hist_c = jnp.concatenate([jnp.zeros((bs, _HALO_C - (C_WIDTH - 1), GROUP_W), F32), state_c], axis=1)
    hist_d = jnp.concatenate([jnp.zeros((bs, _HALO_D - (D_WIDTH - 1), GROUP_W), F32), state_d], axis=1)
    oc_s, od_s = _convs(r3s(sr["cb"]), r3s(sr["uc"]), r3s(sr["ud"]), hist_c, hist_d, w["c_w"], w["d_w"],
                        w["d_b"], w["d_ln_g"], w["d_ln_b"], tm=ts_pad)
    ys = _token_tail(w, xs_pad, unpad(oa_s), unpad(ob_s), oc_s.reshape(bs * ts_pad, -1),
                     od_s.reshape(bs * ts_pad, -1), (bs, ts_pad, ts), ps.reshape(bs * ts, -1)).reshape(bs, ts, d)
    cut = lambda a: r3s(a)[:, :ts]
    new_s = (cut(sr["ka"]), cut(sr["va"]), cut(sr["kb"]), cut(sr["vb"]), cut(sr["kiwi"])[:, :, :D_IDX],
             jnp.concatenate([state_c, cut(sr["uc"])], axis=1)[:, -(C_WIDTH - 1):],
             jnp.concatenate([state_d, cut(sr["ud"])], axis=1)[:, -(D_WIDTH - 1):])
    return yp, ys, new_p, new_s


def kernel(x_prompt, x_sample, cache_a_k, cache_a_v, cache_b_k, cache_b_v, cache_b_kidx, state_c_conv, state_d_conv, page_table, p_prompt, p_sample, w_in, a_lambda, a_subln_g, c_conv_w, d_conv_w, d_conv_b, d_ln_g, d_ln_b, d_pw_w, d_pw_b, w_out, ln1_g, ln1_b, peer_wq, peer_subkeys, peer_u, peer_v, ln2_g, ln2_b, ple_w, ple_gate_w, ple_gate_b):
    depth = w_in.shape[0]
    n_pool, page = cache_a_k.shape[1], cache_a_k.shape[2]
    xp, xs = x_prompt, x_sample
    news_p, news_s = [], []
    for l in range(depth):
        lw = (w_in[l], a_lambda[l], a_subln_g[l], c_conv_w[l], d_conv_w[l], d_conv_b[l], d_ln_g[l], d_ln_b[l],
              d_pw_w[l], d_pw_b[l], w_out[l], ln1_g[l], ln1_b[l], peer_wq[l], peer_subkeys[l], peer_u[l],
              peer_v[l], ln2_g[l], ln2_b[l], ple_w[l], ple_gate_w[l], ple_gate_b[l])
        caches = [cache_a_k[l].reshape(n_pool, page, GROUP_W), cache_a_v[l].reshape(n_pool, page, GROUP_W),
                  cache_b_k[l].reshape(n_pool, page, GROUP_W), cache_b_v[l].reshape(n_pool, page, GROUP_W),
                  cache_b_kidx[l]]
        xp, xs, new_p, new_s = _layer(l, depth, xp, xs, caches, state_c_conv[l], state_d_conv[l],
                                      page_table, p_prompt[l], p_sample[l], lw)
        news_p.append(new_p)
        news_s.append(new_s)

    bp, tp, _ = x_prompt.shape
    bs, ts, _ = x_sample.shape
    shapes = [(N_HEADS, 2, 32), (N_HEADS, HEAD_W), (N_HEADS, HEAD_W), (N_HEADS, HEAD_W), (D_IDX,)]

    def stack(news, i, lead):
        a = jnp.stack([n[i] for n in news], axis=0)
        return a.reshape((depth,) + lead + shapes[i]) if i < 5 else a

    outs = [xp, xs]
    for i in range(7):
        outs.append(stack(news_p, i, (bp, tp)))
        outs.append(stack(news_s, i, (bs, ts)))
    return tuple(outs)
```

```python
import functools
import math

import jax
import jax.numpy as jnp
from jax import lax
from jax.experimental import pallas as pl
from jax.experimental.pallas import tpu as pltpu

F32 = jnp.float32
BF16 = jnp.bfloat16
I32 = jnp.int32

LN_EPS = 1e-5
N_HEADS = 4
HEAD_W = 64
GROUP_W = 256
D_IDX = 64
DSA_TOPK = 256
C_WIDTH = 3
D_WIDTH = 31
PEER_HEADS = 8
PEER_TOPK = 16
N_KEYS = 128
LANES = 128
NEG_BIG = -1e30
INT_MIN = -2 ** 31
VMEM_LIMIT = 56 * 1024 * 1024

_SEG = dict(qa=0, ka=256, va=512, qb=768, kb=1024, vb=1280, qi=1536, kiwi=1792,
            cb=1920, cc=2176, cx=2432, da=2688, dg=2944)
_N_PACKED = 3200
_SEG_T = dict(qa=0, qb=256, qi=512, kiwi=768, va=896, vb=1152)
_N_PACKED_T = 1408

_NT = (((1,), (1,)), ((), ()))


def _cparams(sem):
    return pltpu.CompilerParams(dimension_semantics=sem, vmem_limit_bytes=VMEM_LIMIT)


def _split_bf16(x):
    hi = x.astype(BF16)
    lo = (x - hi.astype(F32)).astype(BF16)
    return hi, lo


def _layer_norm(x, g, b):
    mu = jnp.mean(x, axis=-1, keepdims=True)
    xc = x - mu
    var = jnp.mean(xc * xc, axis=-1, keepdims=True)
    return xc * lax.rsqrt(var + LN_EPS) * g + b


def _sigmoid(x):
    return 1.0 / (1.0 + jnp.exp(-x))


_PROJ_COMMON = ("ka", "va", "kb", "vb", "kiwi", "cb", "uc", "ud")
_PROJ_FEATURE_MAJOR = ("kab", "kbb", "qat", "qbt", "qit", "kiwit", "vat", "vbt")
_PROJ_ROW_MAJOR = ("qa", "qb", "qi")


def _proj_kernel(x_ref, w_ref, wt_ref, *out_refs, names, scale_a, scale_b):
    o = dict(zip(names, out_refs))
    xb = x_ref[...].astype(BF16)

    def seg(name, width=GROUP_W):
        off = _SEG[name]
        return jnp.dot(xb, w_ref[:, off:off + width], preferred_element_type=F32)

    def seg_t(name, width=GROUP_W):
        off = _SEG_T[name]
        return lax.dot_general(wt_ref[off:off + width, :], xb, _NT, preferred_element_type=F32)

    ka = seg("ka"); kb = seg("kb")
    o["ka"][...] = ka; o["kb"][...] = kb
    o["va"][...] = seg("va"); o["vb"][...] = seg("vb")
    o["kiwi"][...] = seg("kiwi", LANES)
    o["cb"][...] = seg("cb")
    o["uc"][...] = seg("cc") * seg("cx")
    o["ud"][...] = seg("da") * _sigmoid(seg("dg"))
    if "qat" in o:
        o["kab"][...] = ka.astype(BF16); o["kbb"][...] = kb.astype(BF16)
        o["qat"][...] = seg_t("qa") * scale_a
        o["qbt"][...] = seg_t("qb") * scale_b
        o["qit"][...] = seg_t("qi")
        o["kiwit"][...] = seg_t("kiwi", LANES)
        o["vat"][...] = seg_t("va").astype(BF16)
        o["vbt"][...] = seg_t("vb").astype(BF16)
    else:
        o["qa"][...] = seg("qa") * scale_a
        o["qb"][...] = seg("qb") * scale_b
        o["qi"][...] = seg("qi")


def _project(x, w_packed, wt_packed, tm, feature_major):
    n, d = x.shape
    nat = lambda w, dt=F32: (jax.ShapeDtypeStruct((n, w), dt), pl.BlockSpec((tm, w), lambda i: (i, 0)))
    tr = lambda w, dt=F32: (jax.ShapeDtypeStruct((w, n), dt), pl.BlockSpec((w, tm), lambda i: (0, i)))
    outs = [nat(256), nat(256), nat(256), nat(256), nat(LANES), nat(256), nat(256), nat(256)]
    if feature_major:
        names = _PROJ_COMMON + _PROJ_FEATURE_MAJOR
        outs += [nat(256, BF16), nat(256, BF16), tr(256), tr(256), tr(256), tr(LANES), tr(256, BF16), tr(256, BF16)]
    else:
        names = _PROJ_COMMON + _PROJ_ROW_MAJOR
        outs += [nat(256), nat(256), nat(256)]
    kern = functools.partial(_proj_kernel, names=names, scale_a=float(32 ** -0.5), scale_b=float(HEAD_W ** -0.5))
    res = pl.pallas_call(
        kern, grid=(n // tm,),
        in_specs=[pl.BlockSpec((tm, d), lambda i: (i, 0)),
                  pl.BlockSpec((d, _N_PACKED), lambda i: (0, 0)),
                  pl.BlockSpec((_N_PACKED_T, d), lambda i: (0, 0))],
        out_specs=[o[1] for o in outs], out_shape=[o[0] for o in outs],
        compiler_params=_cparams(("parallel",)), name="proj",
    )(x, w_packed, wt_packed)
    return dict(zip(names, res))


def _lane_heads(o, l, base, rows, lane_head):
    out = None
    for h in range(N_HEADS):
        r0 = (base + h) * rows
        part = jnp.where(lane_head == h, o[r0:r0 + rows] / l[r0:r0 + rows], 0.0)
        out = part if out is None else out + part
    return out


def _sample_attn_kernel(pt_ref, *refs, n_pages, page, rows, past, topk, idx_bits, lam_init, t_valid):
    npg = n_pages
    ak, av, bk, bv, kic = (refs[i * npg:(i + 1) * npg] for i in range(5))
    (nak, nav, nbk, nbv, nki, qa_ref, qb_ref, qi_ref, kiwi_ref, lam_ref, g_ref, gm_ref,
     oa_ref, ob_ref, y_ref) = refs[5 * npg:]
    nblk = npg + 1
    s_tot = nblk * page
    lane_head = lax.broadcasted_iota(I32, (1, GROUP_W), 1) // HEAD_W
    col_last = past + lax.broadcasted_iota(I32, (1, page), 1)
    t_row = lax.broadcasted_iota(I32, (rows, 1), 0)
    int_min = jnp.int32(INT_MIN)

    def blocks(page_refs, new_ref, dtype):
        new = new_ref[0]
        pad = jnp.zeros((page - rows, new.shape[-1]), F32)
        return [r[0].astype(dtype) for r in page_refs] + [jnp.concatenate([new, pad], axis=0).astype(dtype)]

    def softmax_pv(q_stack, n_groups, k_blocks, v_blocks, masks):
        def masked(s, mk, fill):
            if mk is None:
                return s
            s3 = jnp.where(mk[None], s.reshape(n_groups, rows, page), fill)
            return s3.reshape(n_groups * rows, page)
        s = [masked(lax.dot_general(q_stack, kb, _NT, preferred_element_type=F32), mk, NEG_BIG)
             for kb, mk in zip(k_blocks, masks)]
        m = s[0].max(axis=1, keepdims=True)
        for sj in s[1:]:
            m = jnp.maximum(m, sj.max(axis=1, keepdims=True))
        l = None
        o = None
        for sj, vb, mk in zip(s, v_blocks, masks):
            p = masked(jnp.exp(sj - m), mk, 0.0)
            lj = p.sum(axis=1, keepdims=True)
            oj = jnp.dot(p.astype(BF16), vb, preferred_element_type=F32)
            l = lj if l is None else l + lj
            o = oj if o is None else o + oj
        return o, l

    causal_last = col_last <= (past + t_row)

    qa = qa_ref[0]
    lane_grp = lax.broadcasted_iota(I32, (1, GROUP_W), 1) // 32
    qa_stack = jnp.concatenate(
        [jnp.where(lane_grp == (2 * h + c), qa, 0.0) for c in range(2) for h in range(N_HEADS)],
        axis=0).astype(BF16)
    o, l = softmax_pv(qa_stack, 2 * N_HEADS, blocks(ak, nak, BF16), blocks(av, nav, BF16),
                      [None] * npg + [causal_last])
    o0 = _lane_heads(o, l, 0, rows, lane_head)
    o1 = _lane_heads(o, l, N_HEADS, rows, lane_head)
    lv = lam_ref[...]
    lam = (jnp.exp(jnp.sum(lv[0:1] * lv[1:2], axis=1, keepdims=True))
           - jnp.exp(jnp.sum(lv[2:3] * lv[3:4], axis=1, keepdims=True)) + lam_init)
    od = o0 - lam * o1
    hi, lo = _split_bf16(od * od)
    ms = (jnp.dot(hi, gm_ref[...], preferred_element_type=F32)
          + jnp.dot(lo, gm_ref[...], preferred_element_type=F32))
    oa_ref[0] = od * lax.rsqrt(ms + LN_EPS) * g_ref[...] * (1.0 - lam_init)

    qi = qi_ref[0]
    qi_rows = jnp.concatenate([qi[:, h * D_IDX:(h + 1) * D_IDX] for h in range(N_HEADS)], axis=0)
    q_hi, q_lo = _split_bf16(qi_rows)
    kiwi = kiwi_ref[0]
    w_h = [kiwi[:, D_IDX + h:D_IDX + h + 1] for h in range(N_HEADS)]
    key_blocks = []
    for j, kib in enumerate(blocks(kic, nki, F32)):
        k_hi, k_lo = _split_bf16(kib)
        sc = (lax.dot_general(q_hi, k_hi, _NT, preferred_element_type=F32)
              + lax.dot_general(q_lo, k_hi, _NT, preferred_element_type=F32)
              + lax.dot_general(q_hi, k_lo, _NT, preferred_element_type=F32))
        isc = jnp.maximum(sc[0:rows], 0.0) * w_h[0]
        for h in range(1, N_HEADS):
            isc = isc + jnp.maximum(sc[h * rows:(h + 1) * rows], 0.0) * w_h[h]
        isc = jnp.where(isc == 0.0, 0.0, isc)
        bits = pltpu.bitcast(isc, I32)
        key = bits ^ ((bits >> 31) & jnp.int32(0x7FFFFFFF))
        if j == npg:
            key = jnp.where(causal_last, key, int_min)
        key_blocks.append(key)
    keys = jnp.concatenate(key_blocks, axis=1)
    cols = lax.broadcasted_iota(I32, (1, s_tot), 1)
    kk = jnp.float32(topk)
    count = lambda pred: jnp.sum(jnp.where(pred, 1.0, 0.0), axis=1, keepdims=True)

    def bit_body(r, prefix):
        cand_u = prefix | lax.shift_left(jnp.int32(1), 31 - r)
        return jnp.where(count(keys >= (cand_u ^ int_min)) >= kk, cand_u, prefix)

    thr = lax.fori_loop(0, 32, bit_body, jnp.zeros((rows, 1), I32)) ^ int_min
    r_need = kk - count(keys > thr)
    any_tie = jnp.max(jnp.where(t_row < t_valid, count(keys >= thr) - kk, 0.0)) > 0.0
    y_ref[...] = jnp.full((rows, 1), s_tot, I32)

    @pl.when(any_tie)
    def _():
        def ybody(r, y):
            cand = y | lax.shift_left(jnp.int32(1), idx_bits - 1 - r)
            return jnp.where(count((keys == thr) & (cols < cand)) < r_need, cand, y)
        y_ref[...] = lax.fori_loop(0, idx_bits, ybody, jnp.zeros((rows, 1), I32))

    sel = ((keys > thr) | ((keys == thr) & (cols <= y_ref[...]))) & (cols <= (past + t_row))
    qb = qb_ref[0]
    qb_stack = jnp.concatenate([jnp.where(lane_head == h, qb, 0.0) for h in range(N_HEADS)], axis=0).astype(BF16)
    o, l = softmax_pv(qb_stack, N_HEADS, blocks(bk, nbk, BF16), blocks(bv, nbv, BF16),
                      [sel[:, j * page:(j + 1) * page] for j in range(nblk)])
    ob_ref[0] = _lane_heads(o, l, 0, rows, lane_head)


def _sample_attention(page_table, caches, news, qa, qb, qi, kiwi, a_lambda, g_full, gmean,
                      *, topk, lam_init, t_valid):
    bsz, n_pages = page_table.shape
    page = caches[0].shape[1]
    rows = qa.shape[1]
    past = n_pages * page
    idx_bits = int(math.ceil(math.log2((n_pages + 1) * page))) + 1
    page_spec = lambda c, j: pl.BlockSpec((1, page, c), lambda b, pt: (pt[b, j], 0, 0))
    seq_spec = lambda c: pl.BlockSpec((1, rows, c), lambda b, pt: (b, 0, 0))
    full2 = lambda a: pl.BlockSpec(a.shape, lambda b, pt: (0, 0))
    in_specs, args = [], []
    for cache in caches:
        for j in range(n_pages):
            in_specs.append(page_spec(cache.shape[-1], j))
            args.append(cache)
    for a in list(news) + [qa, qb, qi, kiwi]:
        in_specs.append(seq_spec(a.shape[-1]))
        args.append(a)
    for a in (a_lambda, g_full, gmean):
        in_specs.append(full2(a))
        args.append(a)
    kern = functools.partial(_sample_attn_kernel, n_pages=n_pages, page=page, rows=rows, past=past, topk=topk,
                             idx_bits=idx_bits, lam_init=lam_init, t_valid=t_valid)
    return pl.pallas_call(
        kern,
        grid_spec=pltpu.PrefetchScalarGridSpec(
            num_scalar_prefetch=1, grid=(bsz,), in_specs=in_specs,
            out_specs=[seq_spec(GROUP_W), seq_spec(GROUP_W)],
            scratch_shapes=[pltpu.VMEM((rows, 1), I32)]),
        out_shape=[jax.ShapeDtypeStruct((bsz, rows, GROUP_W), F32)] * 2,
        compiler_params=_cparams(("parallel",)), name="sample_attn",
    )(page_table, *args)


def _attn_core(qst, n_groups, k_ref, vt_ref, n_full, n_kv, tq, tk, mask_fn, m_ref, l_ref, acc_ref, zero_masked):
    n_comp = n_groups // N_HEADS
    row_head = lax.broadcasted_iota(I32, (GROUP_W, 1), 0) // HEAD_W
    m_ref[...] = jnp.full(m_ref.shape, NEG_BIG, F32)
    l_ref[...] = jnp.zeros(l_ref.shape, F32)
    acc_ref[...] = jnp.zeros(acc_ref.shape, F32)
    grp = lambda a, g: a[:, g * tq:(g + 1) * tq]

    def step(c, n_blk, masked):
        width = n_blk * tk
        k0 = pl.multiple_of(c * tk, tk)
        kblk = k_ref[pl.ds(k0, width), :]
        vt = vt_ref[:, pl.ds(k0, width)]
        s = jnp.dot(kblk, qst, preferred_element_type=F32)

        def apply_masks(a, fill):
            parts = []
            for i in range(n_blk):
                mask = masks[i]
                sub = a[i * tk:(i + 1) * tk]
                parts.append(jnp.concatenate(
                    [jnp.where(mask, grp(sub, g), fill) for g in range(n_groups)], axis=1))
            return parts[0] if n_blk == 1 else jnp.concatenate(parts, axis=0)

        if masked:
            masks = [mask_fn(c + i, k0 + i * tk) for i in range(n_blk)]
            s = apply_masks(s, NEG_BIG)
        m_old = m_ref[...]
        m_new = jnp.maximum(m_old, jnp.max(s, axis=0, keepdims=True))
        alpha = jnp.exp(m_old - m_new)
        p = jnp.exp(s - m_new)
        if masked and zero_masked:
            p = apply_masks(p, 0.0)
        l_ref[...] = alpha * l_ref[...] + jnp.sum(p, axis=0, keepdims=True)
        m_ref[...] = m_new
        pb = p.astype(BF16)
        vstack = jnp.concatenate(
            [jnp.where(row_head == h, vt, jnp.zeros_like(vt)) for h in range(N_HEADS)], axis=1)
        for comp in range(n_comp):
            base = comp * N_HEADS
            pcat = jnp.concatenate([grp(pb, base + h) for h in range(N_HEADS)], axis=0)
            pv = jnp.dot(vstack, pcat, preferred_element_type=F32)
            af = jnp.concatenate(
                [jnp.broadcast_to(grp(alpha, base + h), (HEAD_W, tq)) for h in range(N_HEADS)], axis=0)
            acc_ref[comp] = acc_ref[comp] * af + pv

    def run(lo, hi, masked):
        def pair_body(i, carry):
            step(lo + 2 * i, 2, masked)
            return carry
        n = hi - lo
        if isinstance(n, int) and n == 0:
            return
        lax.fori_loop(0, n // 2, pair_body, 0)

        @pl.when(n % 2 == 1)
        def _():
            step(hi - 1, 1, masked)

    run(0, n_full, False)
    run(n_full, n_kv, True)


def _normalised(acc, l_row, base, tq):
    rl = 1.0 / l_row
    rf = jnp.concatenate(
        [jnp.broadcast_to(rl[:, (base + h) * tq:(base + h + 1) * tq], (HEAD_W, tq)) for h in range(N_HEADS)],
        axis=0)
    return acc * rf


def _num_kv_blocks(q0, tq, tk, past, s_tot):
    return jnp.minimum((past + q0 + tq + tk - 1) // tk, s_tot // tk)


def _attn_a_kernel(qt_ref, k_ref, vt_ref, lam_ref, g_ref, gm_ref, o_ref, m_ref, l_ref, acc_ref,
                   *, tq, tk, past, s_tot, lam_init):
    q0 = pl.program_id(1) * tq
    qt = qt_ref[...]
    row_grp = lax.broadcasted_iota(I32, (GROUP_W, 1), 0) // 32
    qst = jnp.concatenate(
        [jnp.where(row_grp == (2 * h + c), qt, 0.0) for c in range(2) for h in range(N_HEADS)],
        axis=1).astype(BF16)
    n_kv = _num_kv_blocks(q0, tq, tk, past, s_tot)
    n_full = jnp.minimum((past + q0 + 1) // tk, n_kv)
    row_pos = past + q0 + lax.broadcasted_iota(I32, (1, tq), 1)
    key_iota = lax.broadcasted_iota(I32, (tk, 1), 0)

    def mask_fn(c, k0):
        return (k0 + key_iota) <= row_pos

    _attn_core(qst, 2 * N_HEADS, k_ref, vt_ref, n_full, n_kv, tq, tk, mask_fn,
               m_ref, l_ref, acc_ref, zero_masked=False)
    l_row = l_ref[...]
    o0 = _normalised(acc_ref[0], l_row, 0, tq)
    o1 = _normalised(acc_ref[1], l_row, N_HEADS, tq)
    lv = lam_ref[...]
    lam = (jnp.exp(jnp.sum(lv[0:1] * lv[1:2], axis=1, keepdims=True))
           - jnp.exp(jnp.sum(lv[2:3] * lv[3:4], axis=1, keepdims=True)) + lam_init)
    o = (o0 - lam * o1).T
    hi, lo = _split_bf16(o * o)
    ms = (jnp.dot(hi, gm_ref[...], preferred_element_type=F32)
          + jnp.dot(lo, gm_ref[...], preferred_element_type=F32))
    o_ref[...] = o * lax.rsqrt(ms + LN_EPS) * g_ref[...] * (1.0 - lam_init)


def _attention_a(qt, k, vt, a_lambda, g_full, gmean, *, bsz, tq, tk, past, lam_init):
    t = qt.shape[1] // bsz
    s_tot = k.shape[0] // bsz
    nq = t // tq
    kern = functools.partial(_attn_a_kernel, tq=tq, tk=tk, past=past, s_tot=s_tot, lam_init=lam_init)
    return pl.pallas_call(
        kern, grid=(bsz, nq),
        in_specs=[pl.BlockSpec((GROUP_W, tq), lambda b, i: (0, b * nq + i)),
                  pl.BlockSpec((s_tot, GROUP_W), lambda b, i: (b, 0)),
                  pl.BlockSpec((GROUP_W, s_tot), lambda b, i: (0, b)),
                  pl.BlockSpec(a_lambda.shape, lambda b, i: (0, 0)),
                  pl.BlockSpec((1, GROUP_W), lambda b, i: (0, 0)),
                  pl.BlockSpec((GROUP_W, GROUP_W), lambda b, i: (0, 0))],
        out_specs=pl.BlockSpec((tq, GROUP_W), lambda b, i: (b * nq + i, 0)),
        out_shape=jax.ShapeDtypeStruct((bsz * t, GROUP_W), F32),
        scratch_shapes=[pltpu.VMEM((1, 2 * N_HEADS * tq), F32), pltpu.VMEM((1, 2 * N_HEADS * tq), F32),
                        pltpu.VMEM((2, GROUP_W, tq), F32)],
        compiler_params=_cparams(("parallel", "arbitrary")), name="attn_a",
    )(qt, k, vt, a_lambda, g_full, gmean)


def _dsa_kernel(qt_ref, k_ref, vt_ref, qit_ref, ki_ref, kiwit_ref, o_ref, key_ref, y_ref, m_ref, l_ref, acc_ref,
                *, tq, tk, past, s_tot, topk, idx_bits, t_valid):
    q0 = pl.program_id(1) * tq
    n_kv = _num_kv_blocks(q0, tq, tk, past, s_tot)
    row_pos = past + q0 + lax.broadcasted_iota(I32, (1, tq), 1)
    key_iota = lax.broadcasted_iota(I32, (tk, 1), 0)
    int_min = jnp.int32(INT_MIN)

    qit = qit_ref[...]
    blocks = []
    for h in range(N_HEADS):
        qh = qit[h * D_IDX:(h + 1) * D_IDX, :]
        hi, lo = _split_bf16(qh)
        blocks.append(jnp.concatenate([hi, lo, hi], axis=0))
    qi = jnp.concatenate(blocks, axis=1)
    kiwit = kiwit_ref[...]
    w_h = [kiwit[D_IDX + h:D_IDX + h + 1, :] for h in range(N_HEADS)]

    def score_body(c, carry):
        k0 = pl.multiple_of(c * tk, tk)
        kib = ki_ref[pl.ds(k0, tk), :]
        sc = jnp.dot(kib, qi, preferred_element_type=F32)
        isc = jnp.maximum(sc[:, 0:tq], 0.0) * w_h[0]
        for h in range(1, N_HEADS):
            isc = isc + jnp.maximum(sc[:, h * tq:(h + 1) * tq], 0.0) * w_h[h]
        isc = jnp.where(isc == 0.0, 0.0, isc)
        bits = pltpu.bitcast(isc, I32)
        key = bits ^ ((bits >> 31) & jnp.int32(0x7FFFFFFF))
        key_ref[c] = jnp.where((k0 + key_iota) <= row_pos, key, int_min)
        return carry

    lax.fori_loop(0, n_kv, score_body, 0)

    def count(pred):
        def block_count(c):
            ones = jnp.where(pred(key_ref[c], c * tk), 1.0, 0.0)
            return jnp.sum(ones.reshape(tk // 8, 8, tq), axis=0)

        def body(i, acc):
            return acc + block_count(2 * i) + block_count(2 * i + 1)

        acc = lax.fori_loop(0, n_kv // 2, body, jnp.zeros((8, tq), F32))
        odd = (n_kv & 1).astype(F32)
        acc = acc + odd * block_count(n_kv - 1)
        return jnp.sum(acc, axis=0, keepdims=True)

    kk = jnp.float32(topk)

    def bit_body(r, prefix):
        cand_u = prefix | lax.shift_left(jnp.int32(1), 31 - r)
        cand_s = cand_u ^ int_min
        cnt = count(lambda blk, k0: blk >= cand_s)
        return jnp.where(cnt >= kk, cand_u, prefix)

    prefix = lax.fori_loop(0, 32, bit_body, jnp.zeros((1, tq), I32))
    thr = prefix ^ int_min

    n_gt = count(lambda blk, k0: blk > thr)
    n_ge = count(lambda blk, k0: blk >= thr)
    r_need = kk - n_gt
    live = lax.broadcasted_iota(I32, (1, tq), 1) < t_valid
    any_tie = jnp.max(jnp.where(live, n_ge - kk, 0.0)) > 0.0
    y_ref[...] = jnp.full((1, tq), s_tot, I32)

    @pl.when(any_tie)
    def _():
        def ybody(r, y):
            cand = y | lax.shift_left(jnp.int32(1), idx_bits - 1 - r)
            cnt = count(lambda blk, k0: (blk == thr) & ((k0 + key_iota) < cand))
            return jnp.where(cnt < r_need, cand, y)
        y_ref[...] = lax.fori_loop(0, idx_bits, ybody, jnp.zeros((1, tq), I32))

    y_last = y_ref[...]

    def mask_fn(c, k0):
        blk = key_ref[c]
        pos = k0 + key_iota
        sel = (blk > thr) | ((blk == thr) & (pos <= y_last))
        return sel & (pos <= row_pos)

    qt = qt_ref[...]
    row_head = lax.broadcasted_iota(I32, (GROUP_W, 1), 0) // HEAD_W
    qst = jnp.concatenate([jnp.where(row_head == h, qt, 0.0) for h in range(N_HEADS)], axis=1).astype(BF16)
    _attn_core(qst, N_HEADS, k_ref, vt_ref, 0, n_kv, tq, tk, mask_fn, m_ref, l_ref, acc_ref, zero_masked=True)
    o_ref[...] = _normalised(acc_ref[0], l_ref[...], 0, tq).T


def _dsa(qt, k, vt, qit, ki3, kiwit, *, bsz, tq, tk, past, topk, t_valid):
    t = qt.shape[1] // bsz
    s_tot = k.shape[0] // bsz
    nq = t // tq
    idx_bits = int(math.ceil(math.log2(s_tot))) + 1
    kern = functools.partial(_dsa_kernel, tq=tq, tk=tk, past=past, s_tot=s_tot, topk=topk,
                             idx_bits=idx_bits, t_valid=t_valid)
    qspec = lambda w: pl.BlockSpec((w, tq), lambda b, i: (0, b * nq + i))
    return pl.pallas_call(
        kern, grid=(bsz, nq),
        in_specs=[qspec(GROUP_W),
                  pl.BlockSpec((s_tot, GROUP_W), lambda b, i: (b, 0)),
                  pl.BlockSpec((GROUP_W, s_tot), lambda b, i: (0, b)),
                  qspec(GROUP_W),
                  pl.BlockSpec((s_tot, 3 * D_IDX), lambda b, i: (b, 0)),
                  qspec(LANES)],
        out_specs=pl.BlockSpec((tq, GROUP_W), lambda b, i: (b * nq + i, 0)),
        out_shape=jax.ShapeDtypeStruct((bsz * t, GROUP_W), F32),
        scratch_shapes=[pltpu.VMEM((s_tot // tk, tk, tq), I32), pltpu.VMEM((1, tq), I32),
                        pltpu.VMEM((1, N_HEADS * tq), F32), pltpu.VMEM((1, N_HEADS * tq), F32),
                        pltpu.VMEM((1, GROUP_W, tq), F32)],
        compiler_params=_cparams(("parallel", "arbitrary")), name="dsa",
    )(qt, k, vt, qit, ki3, kiwit)


_HALO_C = 8
_HALO_D = 32


def _conv_kernel(*refs, tm, single_tile):
    if single_tile:
        (cb_ref, uc_ref, ud_ref, hc_ref, hd_ref, cw_ref, dw_ref, db_ref, lg_ref, lb_ref,
         oc_ref, od_ref, ec_ref, ed_ref) = refs
        pc_ref = pd_ref = None
    else:
        (cb_ref, uc_ref, ud_ref, pc_ref, pd_ref, hc_ref, hd_ref, cw_ref, dw_ref, db_ref, lg_ref, lb_ref,
         oc_ref, od_ref, ec_ref, ed_ref) = refs
    i = pl.program_id(1)

    @pl.when(i == 0)
    def _():
        ec_ref[0:_HALO_C] = hc_ref[0]
        ed_ref[0:_HALO_D] = hd_ref[0]

    if not single_tile:
        @pl.when(i > 0)
        def _():
            ec_ref[0:_HALO_C] = pc_ref[0]
            ed_ref[0:_HALO_D] = pd_ref[0]

    ec_ref[_HALO_C:_HALO_C + tm] = uc_ref[0]
    ed_ref[_HALO_D:_HALO_D + tm] = ud_ref[0]

    cw = cw_ref[...]
    acc = ec_ref[pl.ds(_HALO_C - (C_WIDTH - 1), tm), :] * cw[0:1]
    for w in range(1, C_WIDTH):
        acc = acc + ec_ref[pl.ds(_HALO_C - (C_WIDTH - 1) + w, tm), :] * cw[w:w + 1]
    oc_ref[0] = cb_ref[0] * acc

    dw = dw_ref[...]
    acc = ed_ref[pl.ds(_HALO_D - (D_WIDTH - 1), tm), :] * dw[0:1]
    for w in range(1, D_WIDTH):
        acc = acc + ed_ref[pl.ds(_HALO_D - (D_WIDTH - 1) + w, tm), :] * dw[w:w + 1]
    z = _layer_norm(acc + db_ref[...], lg_ref[...], lb_ref[...])
    od_ref[0] = z * _sigmoid(z)


def _convs(cb, uc, ud, hist_c, hist_d, c_w, d_w, d_b, ln_g, ln_b, *, tm):
    bsz, t, w = cb.shape
    nt = t // tm
    single = nt == 1
    main = pl.BlockSpec((1, tm, w), lambda b, i: (b, i, 0))
    prev_c = pl.BlockSpec((1, _HALO_C, w), lambda b, i: (b, jnp.maximum(i * (tm // _HALO_C) - 1, 0), 0))
    prev_d = pl.BlockSpec((1, _HALO_D, w), lambda b, i: (b, jnp.maximum(i * (tm // _HALO_D) - 1, 0), 0))
    hist_cs = pl.BlockSpec((1, _HALO_C, w), lambda b, i: (b, 0, 0))
    hist_ds = pl.BlockSpec((1, _HALO_D, w), lambda b, i: (b, 0, 0))
    full2 = lambda a: pl.BlockSpec(a.shape, lambda b, i: (0, 0))
    in_specs = [main, main, main] + ([] if single else [prev_c, prev_d]) + [hist_cs, hist_ds] + \
        [full2(c_w), full2(d_w), full2(d_b), full2(ln_g), full2(ln_b)]
    args = [cb, uc, ud] + ([] if single else [uc, ud]) + [hist_c, hist_d, c_w, d_w, d_b, ln_g, ln_b]
    kern = functools.partial(_conv_kernel, tm=tm, single_tile=single)
    return pl.pallas_call(
        kern, grid=(bsz, nt), in_specs=in_specs,
        out_specs=[main, main],
        out_shape=[jax.ShapeDtypeStruct((bsz, t, w), F32)] * 2,
        scratch_shapes=[pltpu.VMEM((_HALO_C + tm, w), F32), pltpu.VMEM((_HALO_D + tm, w), F32)],
        compiler_params=_cparams(("parallel", "arbitrary")), name="convs",
    )(*args)


def _mix_kernel(x_ref, oa_ref, ob_ref, oc_ref, od_ref, pw_ref, pwb_ref, wo_ref, g_ref, b_ref, h_ref, *, alpha):
    od = jnp.dot(od_ref[...].astype(BF16), pw_ref[...], preferred_element_type=F32) + pwb_ref[...]
    cat = jnp.concatenate([oa_ref[...], ob_ref[...], oc_ref[...], od], axis=1).astype(BF16)
    mix = jnp.dot(cat, wo_ref[...], preferred_element_type=F32)
    h_ref[...] = _layer_norm(alpha * x_ref[...] + mix, g_ref[...], b_ref[...])


def _mix(x, oa, ob, oc, od, pw, pwb, wo, g, b, *, tm, alpha):
    n, d = x.shape
    row = lambda w: pl.BlockSpec((tm, w), lambda i: (i, 0))
    full = lambda a: pl.BlockSpec(a.shape, lambda i: (0, 0))
    return pl.pallas_call(
        functools.partial(_mix_kernel, alpha=alpha), grid=(n // tm,),
        in_specs=[row(d), row(GROUP_W), row(GROUP_W), row(GROUP_W), row(GROUP_W),
                  full(pw), full(pwb), full(wo), full(g), full(b)],
        out_specs=row(d), out_shape=jax.ShapeDtypeStruct((n, d), F32),
        compiler_params=_cparams(("parallel",)), name="mix",
    )(x, oa, ob, oc, od, pw, pwb, wo, g, b)


def _staircase():
    return [(a, b) for a in range(PEER_TOPK) for b in range(PEER_TOPK) if (a + 1) * (b + 1) <= PEER_TOPK]


def _tree_rows(x, op):
    while x.shape[0] > 1:
        half = x.shape[0] // 2
        y = op(x[:half], x[half:2 * half])
        x = y if x.shape[0] % 2 == 0 else jnp.concatenate([y, x[2 * half:]], axis=0)
    return x


def _extract_top(x, payload, n_out, n_rows):
    row = lax.broadcasted_iota(I32, x.shape, 0).astype(F32)
    vals, idxs, pays = [], [], []
    for _ in range(n_out):
        m = _tree_rows(x, jnp.maximum)
        idx = _tree_rows(jnp.where(x == m, row, float(n_rows)), jnp.minimum)
        hit = row == idx
        vals.append(m)
        idxs.append(idx)
        if payload is not None:
            pays.append(_tree_rows(jnp.where(hit, payload, -1.0), jnp.maximum))
        x = jnp.where(hit, -jnp.inf, x)
    return vals, idxs, pays


def _peer_select_kernel(h_ref, wq_ref, sk_ref, eid_ref, g_ref, sv_ref, si_ref, *, tm):
    ht = h_ref[...].T.astype(BF16)
    n_groups = 2 * PEER_HEADS
    nt = tm // LANES

    def to_tiles(x):
        return pltpu.einshape("trl->rtl", jnp.stack([x[:, LANES * t:LANES * (t + 1)] for t in range(nt)], axis=0))

    def from_tiles(y):
        z = pltpu.einshape("rtl->trl", y)
        return jnp.concatenate([z[t] for t in range(nt)], axis=1)

    def group_body(g, carry):
        r0 = pl.multiple_of(g * N_KEYS, N_KEYS)
        qg = jnp.dot(wq_ref[pl.ds(r0, N_KEYS), :], ht, preferred_element_type=F32)
        st = jnp.dot(sk_ref[g % 2], qg.astype(BF16), preferred_element_type=F32)
        vals, idxs, _ = _extract_top(to_tiles(st), None, PEER_TOPK, N_KEYS)
        sv_ref[g] = jnp.concatenate(vals, axis=0)
        si_ref[g] = jnp.concatenate(idxs, axis=0)
        return carry

    lax.fori_loop(0, n_groups, group_body, 0)

    pairs = _staircase()

    def head_body(hd, carry):
        sv0 = sv_ref[2 * hd]; sv1 = sv_ref[2 * hd + 1]
        si0 = si_ref[2 * hd]; si1 = si_ref[2 * hd + 1]
        cand = jnp.concatenate([sv0[a:a + 1] + sv1[b:b + 1] for a, b in pairs], axis=0)
        cand_e = jnp.concatenate([si0[a:a + 1] * float(N_KEYS) + si1[b:b + 1] for a, b in pairs], axis=0)
        vals, _, pays = _extract_top(cand, cand_e, PEER_TOPK, len(pairs))
        fv = jnp.concatenate(vals, axis=0)
        e = jnp.exp(fv - fv[0:1])
        gate = e / jnp.sum(e, axis=0, keepdims=True)
        r0 = pl.multiple_of(hd * PEER_TOPK, PEER_TOPK)
        eid_ref[pl.ds(r0, PEER_TOPK), :] = from_tiles(jnp.concatenate(pays, axis=0)).astype(I32)
        g_ref[pl.ds(r0, PEER_TOPK), :] = from_tiles(gate)
        return carry

    lax.fori_loop(0, PEER_HEADS, head_body, 0)


_SELECT_TOKENS = 8 * LANES


def _peer_select(h, wq_t, sk, *, tm):
    n, d = h.shape
    n_slots = PEER_HEADS * PEER_TOPK
    return pl.pallas_call(
        functools.partial(_peer_select_kernel, tm=tm), grid=(n // tm,),
        in_specs=[pl.BlockSpec((tm, d), lambda i: (i, 0)),
                  pl.BlockSpec(wq_t.shape, lambda i: (0, 0)),
                  pl.BlockSpec(sk.shape, lambda i: (0, 0, 0))],
        out_specs=[pl.BlockSpec((n_slots, tm), lambda i: (0, i)),
                   pl.BlockSpec((n_slots, tm), lambda i: (0, i))],
        out_shape=[jax.ShapeDtypeStruct((n_slots, n), I32), jax.ShapeDtypeStruct((n_slots, n), F32)],
        scratch_shapes=[pltpu.VMEM((2 * PEER_HEADS, PEER_TOPK, tm // LANES, LANES), F32),
                        pltpu.VMEM((2 * PEER_HEADS, PEER_TOPK, tm // LANES, LANES), F32)],
        compiler_params=_cparams(("parallel",)), name="peer_select",
    )(h, wq_t, sk)


_GATE_TOKENS = 8


def _peer_gates_kernel(eid_ref, g_ref, w_ref, *, tm):
    sub = lax.broadcasted_iota(I32, (N_KEYS, PEER_HEADS * PEER_TOPK), 0)

    def token_gates(t):
        e = eid_ref[pl.ds(t, 1), :]
        g = g_ref[pl.ds(t, 1), :]
        g_hi = g.astype(BF16).astype(F32)
        g_lo = g - g_hi
        oh_i = jnp.where((e >> 7) == sub, 1.0, 0.0).astype(BF16)
        hit_j = (e & (N_KEYS - 1)) == sub
        gj_hi = jnp.where(hit_j, g_hi, 0.0).astype(BF16)
        gj_lo = jnp.where(hit_j, g_lo, 0.0).astype(BF16)
        lhs = jnp.concatenate([oh_i, oh_i], axis=1)
        rhs = jnp.concatenate([gj_hi, gj_lo], axis=1)
        return lax.dot_general(lhs, rhs, _NT, preferred_element_type=F32)

    def body(b, carry):
        t0 = pl.multiple_of(b * _GATE_TOKENS, _GATE_TOKENS)
        w = jnp.stack([token_gates(t0 + k) for k in range(_GATE_TOKENS)], axis=0)
        w_ref[:, pl.ds(t0, _GATE_TOKENS), :] = pltpu.einshape("tij->itj", w)
        return carry

    lax.fori_loop(0, tm // _GATE_TOKENS, body, 0)


def _peer_gates(eid, gate, *, tm):
    n, n_slots = eid.shape
    return pl.pallas_call(
        functools.partial(_peer_gates_kernel, tm=tm), grid=(n // tm,),
        in_specs=[pl.BlockSpec((tm, n_slots), lambda i: (i, 0)),
                  pl.BlockSpec((tm, n_slots), lambda i: (i, 0))],
        out_specs=pl.BlockSpec((N_KEYS, tm, N_KEYS), lambda i: (0, i, 0)),
        out_shape=jax.ShapeDtypeStruct((N_KEYS, n, N_KEYS), F32),
        compiler_params=_cparams(("parallel",)), name="peer_gates",
    )(eid, gate)


def _gelu(x):
    return 0.5 * x * (1.0 + lax.erf(x * (2.0 ** -0.5)))


def _peer_dense_kernel(h_ref, w_ref, u_ref, v_ref, f_ref):
    @pl.when(pl.program_id(1) == 0)
    def _():
        f_ref[...] = jnp.zeros(f_ref.shape, F32)

    act = _gelu(lax.dot_general(h_ref[...], u_ref[...], _NT, preferred_element_type=F32))
    y = jnp.concatenate(
        [w_ref[i] * act[:, i * N_KEYS:(i + 1) * N_KEYS] for i in range(w_ref.shape[0])],
        axis=1).astype(BF16)
    f_ref[...] += jnp.dot(y, v_ref[...], preferred_element_type=F32)


def _peer_dense(hb, wmat, u, v, *, tm, te):
    n, d = hb.shape
    n_exp = u.shape[0]
    return pl.pallas_call(
        _peer_dense_kernel, grid=(n // tm, n_exp // te),
        in_specs=[pl.BlockSpec((tm, d), lambda i, e: (i, 0)),
                  pl.BlockSpec((te // N_KEYS, tm, N_KEYS), lambda i, e: (e, i, 0)),
                  pl.BlockSpec((te, d), lambda i, e: (e, 0)),
                  pl.BlockSpec((te, d), lambda i, e: (e, 0))],
        out_specs=pl.BlockSpec((tm, d), lambda i, e: (i, 0)),
        out_shape=jax.ShapeDtypeStruct((n, d), F32),
        compiler_params=_cparams(("parallel", "arbitrary")), name="peer_dense",
    )(hb, wmat, u, v)


def _final_kernel(h_ref, f_ref, p_ref, g_ref, b_ref, wg_ref, bg_ref, wp_ref, y_ref, *, alpha):
    h2 = _layer_norm(alpha * h_ref[...] + f_ref[...], g_ref[...], b_ref[...])
    gate = _sigmoid(jnp.dot(h2.astype(BF16), wg_ref[...], preferred_element_type=F32) + bg_ref[...])
    emb = jnp.dot(p_ref[...].astype(BF16), wp_ref[...], preferred_element_type=F32)
    y_ref[...] = h2 + gate * emb


def _final(h, f, p, g, b, wg, bg, wp, *, tm, alpha):
    n, d = h.shape
    row = lambda w: pl.BlockSpec((tm, w), lambda i: (i, 0))
    full = lambda a: pl.BlockSpec(a.shape, lambda i: (0, 0))
    return pl.pallas_call(
        functools.partial(_final_kernel, alpha=alpha), grid=(n // tm,),
        in_specs=[row(d), row(d), row(p.shape[1]), full(g), full(b), full(wg), full(bg), full(wp)],
        out_specs=row(d), out_shape=jax.ShapeDtypeStruct((n, d), F32),
        compiler_params=_cparams(("parallel",)), name="final",
    )(h, f, p, g, b, wg, bg, wp)


def _pack_w_in(w_in):
    d = w_in.shape[0]
    cols = [w_in[:, 0:1792], w_in[:, 1792:1860], jnp.zeros((d, 60), w_in.dtype), w_in[:, 1860:3140]]
    nat = jnp.concatenate(cols, axis=1).astype(BF16)
    seg = lambda name, width=GROUP_W: nat[:, _SEG[name]:_SEG[name] + width]
    tr = jnp.concatenate([seg("qa"), seg("qb"), seg("qi"), seg("kiwi", 128), seg("va"), seg("vb")], axis=1).T
    return nat, tr


def _row(a):
    return a.reshape(1, -1)


def _tile_rows(n, pref):
    t = pref
    while n % t:
        t //= 2
    return t


def _prep_weights(lw, depth):
    (w_in, a_lambda, a_subln_g, c_conv_w, d_conv_w, d_conv_b, d_ln_g, d_ln_b, d_pw_w, d_pw_b,
     w_out, ln1_g, ln1_b, peer_wq, peer_subkeys, peer_u, peer_v, ln2_g, ln2_b,
     ple_w, ple_gate_w, ple_gate_b) = lw
    head_of_lane = jnp.arange(GROUP_W) // HEAD_W
    gmean = (head_of_lane[:, None] == head_of_lane[None, :]).astype(BF16) * (1.0 / HEAD_W)
    w_nat, w_tr = _pack_w_in(w_in)
    return dict(
        w_in=w_nat, w_in_t=w_tr, a_lambda=a_lambda, a_g=_row(jnp.tile(a_subln_g, N_HEADS)),
        gmean=gmean.astype(BF16),
        c_w=c_conv_w, d_w=d_conv_w, d_b=_row(d_conv_b), d_ln_g=_row(d_ln_g), d_ln_b=_row(d_ln_b),
        d_pw=d_pw_w.astype(BF16), d_pwb=_row(d_pw_b), w_out=w_out.astype(BF16),
        ln1_g=_row(ln1_g), ln1_b=_row(ln1_b),
        wq_t=peer_wq.T.astype(BF16), sk=peer_subkeys.astype(BF16),
        u=peer_u.astype(BF16), v=peer_v.astype(BF16),
        ln2_g=_row(ln2_g), ln2_b=_row(ln2_b),
        ple_w=ple_w.astype(BF16), wg=ple_gate_w.astype(BF16), bg=_row(ple_gate_b),
        alpha=float((2 * depth) ** 0.25),
    )


def _index_keys(ki):
    k_hi = ki.astype(BF16)
    k_lo = (ki - k_hi.astype(F32)).astype(BF16)
    return jnp.concatenate([k_hi, k_hi, k_lo], axis=-1)


def _token_tail(w, x, oa, ob, oc, od, valid_rows, p):
    n = x.shape[0]
    h = _mix(x, oa, ob, oc, od, w["d_pw"], w["d_pwb"], w["w_out"], w["ln1_g"], w["ln1_b"],
             tm=_tile_rows(n, 512), alpha=w["alpha"])
    if valid_rows is not None:
        bsz, t_pad, t = valid_rows
        h = h.reshape(bsz, t_pad, -1)[:, :t].reshape(bsz * t, -1)
    n = h.shape[0]
    n_sel = -(-n // _SELECT_TOKENS) * _SELECT_TOKENS
    h_sel = h if n_sel == n else jnp.pad(h, ((0, n_sel - n), (0, 0)))
    eid_t, gate_t = _peer_select(h_sel, w["wq_t"], w["sk"], tm=_SELECT_TOKENS)
    wmat = _peer_gates(eid_t[:, :n].T, gate_t[:, :n].T, tm=_tile_rows(n, 128))
    f = _peer_dense(h.astype(BF16), wmat, w["u"], w["v"], tm=_tile_rows(n, 1024), te=8 * N_KEYS)
    return _final(h, f, p, w["ln2_g"], w["ln2_b"], w["wg"], w["bg"], w["ple_w"],
                  tm=_tile_rows(n, 512), alpha=w["alpha"])


def _layer(layer_idx, depth, xp, xs, caches, state_c, state_d, page_table, pp, ps, lw):
    w = _prep_weights(lw, depth)
    bp, tp, d = xp.shape
    bs, ts, _ = xs.shape
    ts_pad = 8
    n_pages = page_table.shape[1]
    page = caches[0].shape[1]
    past = n_pages * page
    lam_init = 0.8 - 0.6 * math.exp(-0.3 * (layer_idx + 1))

    pr = _project(xp.reshape(bp * tp, d), w["w_in"], w["w_in_t"], _tile_rows(bp * tp, 512), feature_major=True)
    tq = _tile_rows(tp, 256)
    tk = _tile_rows(tp, 256)
    oa = _attention_a(pr["qat"], pr["kab"], pr["vat"], w["a_lambda"], w["a_g"], w["gmean"],
                      bsz=bp, tq=tq, tk=tk, past=0, lam_init=lam_init)
    ob = _dsa(pr["qbt"], pr["kbb"], pr["vbt"], pr["qit"], _index_keys(pr["kiwi"][:, :D_IDX]), pr["kiwit"],
              bsz=bp, tq=tq, tk=tk, past=0, topk=min(DSA_TOPK, tp // 4), t_valid=tq)
    r3p = lambda a: a.reshape(bp, tp, a.shape[-1])
    zc = jnp.zeros((bp, _HALO_C, GROUP_W), F32)
    zd = jnp.zeros((bp, _HALO_D, GROUP_W), F32)
    oc, od = _convs(r3p(pr["cb"]), r3p(pr["uc"]), r3p(pr["ud"]), zc, zd, w["c_w"], w["d_w"], w["d_b"],
                    w["d_ln_g"], w["d_ln_b"], tm=_tile_rows(tp, 512))
    yp = _token_tail(w, xp.reshape(bp * tp, d), oa, ob, oc.reshape(bp * tp, -1), od.reshape(bp * tp, -1),
                     None, pp.reshape(bp * tp, -1)).reshape(bp, tp, d)
    new_p = (r3p(pr["ka"]), r3p(pr["va"]), r3p(pr["kb"]), r3p(pr["vb"]), r3p(pr["kiwi"])[:, :, :D_IDX],
             jnp.concatenate([zc, r3p(pr["uc"])], axis=1)[:, -(C_WIDTH - 1):],
             jnp.concatenate([zd, r3p(pr["ud"])], axis=1)[:, -(D_WIDTH - 1):])

    xs_pad = jnp.pad(xs, ((0, 0), (0, ts_pad - ts), (0, 0))).reshape(bs * ts_pad, d)
    sr = _project(xs_pad, w["w_in"], w["w_in_t"], _tile_rows(bs * ts_pad, 512), feature_major=False)
    r3s = lambda a: a.reshape(bs, ts_pad, a.shape[-1])
    news = [r3s(sr["ka"]), r3s(sr["va"]), r3s(sr["kb"]), r3s(sr["vb"]), r3s(sr["kiwi"])[:, :, :D_IDX]]
    oa_s, ob_s = _sample_attention(page_table, caches, news, r3s(sr["qa"]), r3s(sr["qb"]), r3s(sr["qi"]),
                                   r3s(sr["kiwi"]), w["a_lambda"], w["a_g"], w["gmean"],
                                   topk=min(DSA_TOPK, (past + ts) // 4), lam_init=lam_init, t_valid=ts)
    unpad = lambda a: a.reshape(bs * ts_pad, GROUP_W)
    hist_c = jnp.concatenate([jnp.zeros((bs, _HALO_C - (C_WIDTH - 1), GROUP_W), F32), state_c], axis=1)
    hist_d = jnp.concatenate([jnp.zeros((bs, _HALO_D - (D_WIDTH - 1), GROUP_W), F32), state_d], axis=1)
    oc_s, od_s = _convs(r3s(sr["cb"]), r3s(sr["uc"]), r3s(sr["ud"]), hist_c, hist_d, w["c_w"], w["d_w"],
                        w["d_b"], w["d_ln_g"], w["d_ln_b"], tm=ts_pad)
    ys = _token_tail(w, xs_pad, unpad(oa_s), unpad(ob_s), oc_s.reshape(bs * ts_pad, -1),
                     od_s.reshape(bs * ts_pad, -1), (bs, ts_pad, ts), ps.reshape(bs * ts, -1)).reshape(bs, ts, d)
    cut = lambda a: r3s(a)[:, :ts]
    new_s = (cut(sr["ka"]), cut(sr["va"]), cut(sr["kb"]), cut(sr["vb"]), cut(sr["kiwi"])[:, :, :D_IDX],
             jnp.concatenate([state_c, cut(sr["uc"])], axis=1)[:, -(C_WIDTH - 1):],
             jnp.concatenate([state_d, cut(sr["ud"])], axis=1)[:, -(D_WIDTH - 1):])
    return yp, ys, new_p, new_s


def kernel(x_prompt, x_sample, cache_a_k, cache_a_v, cache_b_k, cache_b_v, cache_b_kidx, state_c_conv, state_d_conv, page_table, p_prompt, p_sample, w_in, a_lambda, a_subln_g, c_conv_w, d_conv_w, d_conv_b, d_ln_g, d_ln_b, d_pw_w, d_pw_b, w_out, ln1_g, ln1_b, peer_wq, peer_subkeys, peer_u, peer_v, ln2_g, ln2_b, ple_w, ple_gate_w, ple_gate_b):
    depth = w_in.shape[0]
    n_pool, page = cache_a_k.shape[1], cache_a_k.shape[2]
    xp, xs = x_prompt, x_sample
    news_p, news_s = [], []
    for l in range(depth):
        lw = (w_in[l], a_lambda[l], a_subln_g[l], c_conv_w[l], d_conv_w[l], d_conv_b[l], d_ln_g[l], d_ln_b[l],
              d_pw_w[l], d_pw_b[l], w_out[l], ln1_g[l], ln1_b[l], peer_wq[l], peer_subkeys[l], peer_u[l],
              peer_v[l], ln2_g[l], ln2_b[l], ple_w[l], ple_gate_w[l], ple_gate_b[l])
        caches = [cache_a_k[l].reshape(n_pool, page, GROUP_W), cache_a_v[l].reshape(n_pool, page, GROUP_W),
                  cache_b_k[l].reshape(n_pool, page, GROUP_W), cache_b_v[l].reshape(n_pool, page, GROUP_W),
                  cache_b_kidx[l]]
        xp, xs, new_p, new_s = _layer(l, depth, xp, xs, caches, state_c_conv[l], state_d_conv[l],
                                      page_table, p_prompt[l], p_sample[l], lw)
        news_p.append(new_p)
        news_s.append(new_s)

    bp, tp, _ = x_prompt.shape
    bs, ts, _ = x_sample.shape
    shapes = [(N_HEADS, 2, 32), (N_HEADS, HEAD_W), (N_HEADS, HEAD_W), (N_HEADS, HEAD_W), (D_IDX,)]

    def stack(news, i, lead):
        a = jnp.stack([n[i] for n in news], axis=0)
        return a.reshape((depth,) + lead + shapes[i]) if i < 5 else a

    outs = [xp, xs]
    for i in range(7):
        outs.append(stack(news_p, i, (bp, tp)))
        outs.append(stack(news_s, i, (bs, ts)))
    return tuple(outs)
```

```python
import functools
import math

import jax
import jax.numpy as jnp
from jax import lax
from jax.experimental import pallas as pl
from jax.experimental.pallas import tpu as pltpu

F32 = jnp.float32
BF16 = jnp.bfloat16
I32 = jnp.int32

LN_EPS = 1e-5
N_HEADS = 4
HEAD_W = 64
GROUP_W = 256
D_IDX = 64
DSA_TOPK = 256
C_WIDTH = 3
D_WIDTH = 31
PEER_HEADS = 8
PEER_TOPK = 16
N_KEYS = 128
LANES = 128
NEG_BIG = -1e30
INT_MIN = -2 ** 31
VMEM_LIMIT = 56 * 1024 * 1024

_SEG = dict(qa=0, ka=256, va=512, qb=768, kb=1024, vb=1280, qi=1536, kiwi=1792,
            cb=1920, cc=2176, cx=2432, da=2688, dg=2944)
_N_PACKED = 3200
_SEG_T = dict(qa=0, qb=256, qi=512, kiwi=768, va=896, vb=1152)
_N_PACKED_T = 1408

_NT = (((1,), (1,)), ((), ()))


def _cparams(sem):
    return pltpu.CompilerParams(dimension_semantics=sem, vmem_limit_bytes=VMEM_LIMIT)


def _split_bf16(x):
    hi = x.astype(BF16)
    lo = (x - hi.astype(F32)).astype(BF16)
    return hi, lo


def _layer_norm(x, g, b):
    mu = jnp.mean(x, axis=-1, keepdims=True)
    xc = x - mu
    var = jnp.mean(xc * xc, axis=-1, keepdims=True)
    return xc * lax.rsqrt(var + LN_EPS) * g + b


def _sigmoid(x):
    return 1.0 / (1.0 + jnp.exp(-x))


_PROJ_COMMON = ("ka", "va", "kb", "vb", "kiwi", "cb", "uc", "ud")
_PROJ_FEATURE_MAJOR = ("kab", "kbb", "qat", "qbt", "qit", "kiwit", "vat", "vbt")
_PROJ_ROW_MAJOR = ("qa", "qb", "qi")


def _proj_kernel(x_ref, w_ref, wt_ref, *out_refs, names, scale_a, scale_b):
    o = dict(zip(names, out_refs))
    xb = x_ref[...].astype(BF16)

    def seg(name, width=GROUP_W):
        off = _SEG[name]
        return jnp.dot(xb, w_ref[:, off:off + width], preferred_element_type=F32)

    def seg_t(name, width=GROUP_W):
        off = _SEG_T[name]
        return lax.dot_general(wt_ref[off:off + width, :], xb, _NT, preferred_element_type=F32)

    ka = seg("ka"); kb = seg("kb")
    o["ka"][...] = ka; o["kb"][...] = kb
    o["va"][...] = seg("va"); o["vb"][...] = seg("vb")
    o["kiwi"][...] = seg("kiwi", LANES)
    o["cb"][...] = seg("cb")
    o["uc"][...] = seg("cc") * seg("cx")
    o["ud"][...] = seg("da") * _sigmoid(seg("dg"))
    if "qat" in o:
        o["kab"][...] = ka.astype(BF16); o["kbb"][...] = kb.astype(BF16)
        o["qat"][...] = seg_t("qa") * scale_a
        o["qbt"][...] = seg_t("qb") * scale_b
        o["qit"][...] = seg_t("qi")
        o["kiwit"][...] = seg_t("kiwi", LANES)
        o["vat"][...] = seg_t("va").astype(BF16)
        o["vbt"][...] = seg_t("vb").astype(BF16)
    else:
        o["qa"][...] = seg("qa") * scale_a
        o["qb"][...] = seg("qb") * scale_b
        o["qi"][...] = seg("qi")


def _project(x, w_packed, wt_packed, tm, feature_major):
    n, d = x.shape
    nat = lambda w, dt=F32: (jax.ShapeDtypeStruct((n, w), dt), pl.BlockSpec((tm, w), lambda i: (i, 0)))
    tr = lambda w, dt=F32: (jax.ShapeDtypeStruct((w, n), dt), pl.BlockSpec((w, tm), lambda i: (0, i)))
    outs = [nat(256), nat(256), nat(256), nat(256), nat(LANES), nat(256), nat(256), nat(256)]
    if feature_major:
        names = _PROJ_COMMON + _PROJ_FEATURE_MAJOR
        outs += [nat(256, BF16), nat(256, BF16), tr(256), tr(256), tr(256), tr(LANES), tr(256, BF16), tr(256, BF16)]
    else:
        names = _PROJ_COMMON + _PROJ_ROW_MAJOR
        outs += [nat(256), nat(256), nat(256)]
    kern = functools.partial(_proj_kernel, names=names, scale_a=float(32 ** -0.5), scale_b=float(HEAD_W ** -0.5))
    res = pl.pallas_call(
        kern, grid=(n // tm,),
        in_specs=[pl.BlockSpec((tm, d), lambda i: (i, 0)),
                  pl.BlockSpec((d, _N_PACKED), lambda i: (0, 0)),
                  pl.BlockSpec((_N_PACKED_T, d), lambda i: (0, 0))],
        out_specs=[o[1] for o in outs], out_shape=[o[0] for o in outs],
        compiler_params=_cparams(("parallel",)), name="proj",
    )(x, w_packed, wt_packed)
    return dict(zip(names, res))


def _lane_heads(o, l, base, rows, lane_head):
    out = None
    for h in range(N_HEADS):
        r0 = (base + h) * rows
        part = jnp.where(lane_head == h, o[r0:r0 + rows] / l[r0:r0 + rows], 0.0)
        out = part if out is None else out + part
    return out


def _sample_attn_kernel(pt_ref, *refs, n_pages, page, rows, past, topk, idx_bits, lam_init, t_valid):
    npg = n_pages
    ak, av, bk, bv, kic = (refs[i * npg:(i + 1) * npg] for i in range(5))
    (nak, nav, nbk, nbv, nki, qa_ref, qb_ref, qi_ref, kiwi_ref, lam_ref, g_ref, gm_ref,
     oa_ref, ob_ref, y_ref) = refs[5 * npg:]
    nblk = npg + 1
    s_tot = nblk * page
    lane_head = lax.broadcasted_iota(I32, (1, GROUP_W), 1) // HEAD_W
    col_last = past + lax.broadcasted_iota(I32, (1, page), 1)
    t_row = lax.broadcasted_iota(I32, (rows, 1), 0)
    int_min = jnp.int32(INT_MIN)

    def blocks(page_refs, new_ref, dtype):
        new = new_ref[0]
        pad = jnp.zeros((page - rows, new.shape[-1]), F32)
        return [r[0, 0].astype(dtype) for r in page_refs] + [jnp.concatenate([new, pad], axis=0).T.astype(dtype)]

    def softmax_pv(q_stack, n_groups, k_blocks, v_blocks, masks):
        def masked(s, mk, fill):
            if mk is None:
                return s
            s3 = jnp.where(mk[None], s.reshape(n_groups, rows, page), fill)
            return s3.reshape(n_groups * rows, page)
        s = [masked(jnp.dot(q_stack, kb, preferred_element_type=F32), mk, NEG_BIG)
             for kb, mk in zip(k_blocks, masks)]
        m = s[0].max(axis=1, keepdims=True)
        for sj in s[1:]:
            m = jnp.maximum(m, sj.max(axis=1, keepdims=True))
        l = None
        o = None
        for sj, vb, mk in zip(s, v_blocks, masks):
            p = masked(jnp.exp(sj - m), mk, 0.0)
            lj = p.sum(axis=1, keepdims=True)
            oj = lax.dot_general(p.astype(BF16), vb, _NT, preferred_element_type=F32)
            l = lj if l is None else l + lj
            o = oj if o is None else o + oj
        return o, l

    causal_last = col_last <= (past + t_row)

    qa = qa_ref[0]
    lane_grp = lax.broadcasted_iota(I32, (1, GROUP_W), 1) // 32
    qa_stack = jnp.concatenate(
        [jnp.where(lane_grp == (2 * h + c), qa, 0.0) for c in range(2) for h in range(N_HEADS)],
        axis=0).astype(BF16)
    o, l = softmax_pv(qa_stack, 2 * N_HEADS, blocks(ak, nak, BF16), blocks(av, nav, BF16),
                      [None] * npg + [causal_last])
    o0 = _lane_heads(o, l, 0, rows, lane_head)
    o1 = _lane_heads(o, l, N_HEADS, rows, lane_head)
    lv = lam_ref[...]
    lam = (jnp.exp(jnp.sum(lv[0:1] * lv[1:2], axis=1, keepdims=True))
           - jnp.exp(jnp.sum(lv[2:3] * lv[3:4], axis=1, keepdims=True)) + lam_init)
    od = o0 - lam * o1
    hi, lo = _split_bf16(od * od)
    ms = (jnp.dot(hi, gm_ref[...], preferred_element_type=F32)
          + jnp.dot(lo, gm_ref[...], preferred_element_type=F32))
    oa_ref[0] = od * lax.rsqrt(ms + LN_EPS) * g_ref[...] * (1.0 - lam_init)

    qi = qi_ref[0]
    qi_rows = jnp.concatenate([qi[:, h * D_IDX:(h + 1) * D_IDX] for h in range(N_HEADS)], axis=0)
    q_hi, q_lo = _split_bf16(qi_rows)
    kiwi = kiwi_ref[0]
    w_h = [kiwi[:, D_IDX + h:D_IDX + h + 1] for h in range(N_HEADS)]
    key_blocks = []
    for j, kib in enumerate(blocks(kic, nki, F32)):
        k_hi, k_lo = _split_bf16(kib)
        sc = (jnp.dot(q_hi, k_hi, preferred_element_type=F32)
              + jnp.dot(q_lo, k_hi, preferred_element_type=F32)
              + jnp.dot(q_hi, k_lo, preferred_element_type=F32))
        isc = jnp.maximum(sc[0:rows], 0.0) * w_h[0]
        for h in range(1, N_HEADS):
            isc = isc + jnp.maximum(sc[h * rows:(h + 1) * rows], 0.0) * w_h[h]
        isc = jnp.where(isc == 0.0, 0.0, isc)
        bits = pltpu.bitcast(isc, I32)
        key = bits ^ ((bits >> 31) & jnp.int32(0x7FFFFFFF))
        if j == npg:
            key = jnp.where(causal_last, key, int_min)
        key_blocks.append(key)
    keys = jnp.concatenate(key_blocks, axis=1)
    cols = lax.broadcasted_iota(I32, (1, s_tot), 1)
    kk = jnp.float32(topk)
    count = lambda pred: jnp.sum(jnp.where(pred, 1.0, 0.0), axis=1, keepdims=True)

    def bit_body(r, prefix):
        cand_u = prefix | lax.shift_left(jnp.int32(1), 31 - r)
        return jnp.where(count(keys >= (cand_u ^ int_min)) >= kk, cand_u, prefix)

    thr = lax.fori_loop(0, 32, bit_body, jnp.zeros((rows, 1), I32)) ^ int_min
    r_need = kk - count(keys > thr)
    any_tie = jnp.max(jnp.where(t_row < t_valid, count(keys >= thr) - kk, 0.0)) > 0.0
    y_ref[...] = jnp.full((rows, 1), s_tot, I32)

    @pl.when(any_tie)
    def _():
        def ybody(r, y):
            cand = y | lax.shift_left(jnp.int32(1), idx_bits - 1 - r)
            return jnp.where(count((keys == thr) & (cols < cand)) < r_need, cand, y)
        y_ref[...] = lax.fori_loop(0, idx_bits, ybody, jnp.zeros((rows, 1), I32))

    sel = ((keys > thr) | ((keys == thr) & (cols <= y_ref[...]))) & (cols <= (past + t_row))
    qb = qb_ref[0]
    qb_stack = jnp.concatenate([jnp.where(lane_head == h, qb, 0.0) for h in range(N_HEADS)], axis=0).astype(BF16)
    o, l = softmax_pv(qb_stack, N_HEADS, blocks(bk, nbk, BF16), blocks(bv, nbv, BF16),
                      [sel[:, j * page:(j + 1) * page] for j in range(nblk)])
    ob_ref[0] = _lane_heads(o, l, 0, rows, lane_head)


def _sample_attention(page_table, caches, layer, news, qa, qb, qi, kiwi, a_lambda, g_full, gmean,
                      *, topk, lam_init, t_valid):
    bsz, n_pages = page_table.shape
    page = caches[0].shape[3]
    rows = qa.shape[1]
    past = n_pages * page
    idx_bits = int(math.ceil(math.log2((n_pages + 1) * page))) + 1
    page_spec = lambda c, j: pl.BlockSpec((1, 1, c, page), lambda b, pt: (layer, pt[b, j], 0, 0))
    seq_spec = lambda c: pl.BlockSpec((1, rows, c), lambda b, pt: (b, 0, 0))
    full2 = lambda a: pl.BlockSpec(a.shape, lambda b, pt: (0, 0))
    in_specs, args = [], []
    for cache in caches:
        for j in range(n_pages):
            in_specs.append(page_spec(cache.shape[2], j))
            args.append(cache)
    for a in list(news) + [qa, qb, qi, kiwi]:
        in_specs.append(seq_spec(a.shape[-1]))
        args.append(a)
    for a in (a_lambda, g_full, gmean):
        in_specs.append(full2(a))
        args.append(a)
    kern = functools.partial(_sample_attn_kernel, n_pages=n_pages, page=page, rows=rows, past=past, topk=topk,
                             idx_bits=idx_bits, lam_init=lam_init, t_valid=t_valid)
    return pl.pallas_call(
        kern,
        grid_spec=pltpu.PrefetchScalarGridSpec(
            num_scalar_prefetch=1, grid=(bsz,), in_specs=in_specs,
            out_specs=[seq_spec(GROUP_W), seq_spec(GROUP_W)],
            scratch_shapes=[pltpu.VMEM((rows, 1), I32)]),
        out_shape=[jax.ShapeDtypeStruct((bsz, rows, GROUP_W), F32)] * 2,
        compiler_params=_cparams(("parallel",)), name="sample_attn",
    )(page_table, *args)


def _attn_core(qst, n_groups, k_ref, vt_ref, n_full, n_kv, tq, tk, mask_fn, m_ref, l_ref, acc_ref, zero_masked):
    n_comp = n_groups // N_HEADS
    row_head = lax.broadcasted_iota(I32, (GROUP_W, 1), 0) // HEAD_W
    m_ref[...] = jnp.full(m_ref.shape, NEG_BIG, F32)
    l_ref[...] = jnp.zeros(l_ref.shape, F32)
    acc_ref[...] = jnp.zeros(acc_ref.shape, F32)
    grp = lambda a, g: a[:, g * tq:(g + 1) * tq]

    def step(c, n_blk, masked):
        width = n_blk * tk
        k0 = pl.multiple_of(c * tk, tk)
        kblk = k_ref[pl.ds(k0, width), :]
        vt = vt_ref[:, pl.ds(k0, width)]
        s = jnp.dot(kblk, qst, preferred_element_type=F32)

        def apply_masks(a, fill):
            parts = []
            for i in range(n_blk):
                mask = masks[i]
                sub = a[i * tk:(i + 1) * tk]
                parts.append(jnp.concatenate(
                    [jnp.where(mask, grp(sub, g), fill) for g in range(n_groups)], axis=1))
            return parts[0] if n_blk == 1 else jnp.concatenate(parts, axis=0)

        if masked:
            masks = [mask_fn(c + i, k0 + i * tk) for i in range(n_blk)]
            s = apply_masks(s, NEG_BIG)
        m_old = m_ref[...]
        m_new = jnp.maximum(m_old, jnp.max(s, axis=0, keepdims=True))
        alpha = jnp.exp(m_old - m_new)
        p = jnp.exp(s - m_new)
        if masked and zero_masked:
            p = apply_masks(p, 0.0)
        l_ref[...] = alpha * l_ref[...] + jnp.sum(p, axis=0, keepdims=True)
        m_ref[...] = m_new
        pb = p.astype(BF16)
        vstack = jnp.concatenate(
            [jnp.where(row_head == h, vt, jnp.zeros_like(vt)) for h in range(N_HEADS)], axis=1)
        for comp in range(n_comp):
            base = comp * N_HEADS
            pcat = jnp.concatenate([grp(pb, base + h) for h in range(N_HEADS)], axis=0)
            pv = jnp.dot(vstack, pcat, preferred_element_type=F32)
            af = jnp.concatenate(
                [jnp.broadcast_to(grp(alpha, base + h), (HEAD_W, tq)) for h in range(N_HEADS)], axis=0)
            acc_ref[comp] = acc_ref[comp] * af + pv

    def run(lo, hi, masked):
        def pair_body(i, carry):
            step(lo + 2 * i, 2, masked)
            return carry
        n = hi - lo
        if isinstance(n, int) and n == 0:
            return
        lax.fori_loop(0, n // 2, pair_body, 0)

        @pl.when(n % 2 == 1)
        def _():
            step(hi - 1, 1, masked)

    run(0, n_full, False)
    run(n_full, n_kv, True)


def _normalised(acc, l_row, base, tq):
    rl = 1.0 / l_row
    rf = jnp.concatenate(
        [jnp.broadcast_to(rl[:, (base + h) * tq:(base + h + 1) * tq], (HEAD_W, tq)) for h in range(N_HEADS)],
        axis=0)
    return acc * rf


def _num_kv_blocks(q0, tq, tk, past, s_tot):
    return jnp.minimum((past + q0 + tq + tk - 1) // tk, s_tot // tk)


def _attn_a_kernel(qt_ref, k_ref, vt_ref, lam_ref, g_ref, gm_ref, o_ref, m_ref, l_ref, acc_ref,
                   *, tq, tk, past, s_tot, lam_init):
    q0 = pl.program_id(1) * tq
    qt = qt_ref[...]
    row_grp = lax.broadcasted_iota(I32, (GROUP_W, 1), 0) // 32
    qst = jnp.concatenate(
        [jnp.where(row_grp == (2 * h + c), qt, 0.0) for c in range(2) for h in range(N_HEADS)],
        axis=1).astype(BF16)
    n_kv = _num_kv_blocks(q0, tq, tk, past, s_tot)
    n_full = jnp.minimum((past + q0 + 1) // tk, n_kv)
    row_pos = past + q0 + lax.broadcasted_iota(I32, (1, tq), 1)
    key_iota = lax.broadcasted_iota(I32, (tk, 1), 0)

    def mask_fn(c, k0):
        return (k0 + key_iota) <= row_pos

    _attn_core(qst, 2 * N_HEADS, k_ref, vt_ref, n_full, n_kv, tq, tk, mask_fn,
               m_ref, l_ref, acc_ref, zero_masked=False)
    l_row = l_ref[...]
    o0 = _normalised(acc_ref[0], l_row, 0, tq)
    o1 = _normalised(acc_ref[1], l_row, N_HEADS, tq)
    lv = lam_ref[...]
    lam = (jnp.exp(jnp.sum(lv[0:1] * lv[1:2], axis=1, keepdims=True))
           - jnp.exp(jnp.sum(lv[2:3] * lv[3:4], axis=1, keepdims=True)) + lam_init)
    o = (o0 - lam * o1).T
    hi, lo = _split_bf16(o * o)
    ms = (jnp.dot(hi, gm_ref[...], preferred_element_type=F32)
          + jnp.dot(lo, gm_ref[...], preferred_element_type=F32))
    o_ref[...] = o * lax.rsqrt(ms + LN_EPS) * g_ref[...] * (1.0 - lam_init)


def _attention_a(qt, k, vt, a_lambda, g_full, gmean, *, bsz, tq, tk, past, lam_init):
    t = qt.shape[1] // bsz
    s_tot = k.shape[0] // bsz
    nq = t // tq
    kern = functools.partial(_attn_a_kernel, tq=tq, tk=tk, past=past, s_tot=s_tot, lam_init=lam_init)
    return pl.pallas_call(
        kern, grid=(bsz, nq),
        in_specs=[pl.BlockSpec((GROUP_W, tq), lambda b, i: (0, b * nq + i)),
                  pl.BlockSpec((s_tot, GROUP_W), lambda b, i: (b, 0)),
                  pl.BlockSpec((GROUP_W, s_tot), lambda b, i: (0, b)),
                  pl.BlockSpec(a_lambda.shape, lambda b, i: (0, 0)),
                  pl.BlockSpec((1, GROUP_W), lambda b, i: (0, 0)),
                  pl.BlockSpec((GROUP_W, GROUP_W), lambda b, i: (0, 0))],
        out_specs=pl.BlockSpec((tq, GROUP_W), lambda b, i: (b * nq + i, 0)),
        out_shape=jax.ShapeDtypeStruct((bsz * t, GROUP_W), F32),
        scratch_shapes=[pltpu.VMEM((1, 2 * N_HEADS * tq), F32), pltpu.VMEM((1, 2 * N_HEADS * tq), F32),
                        pltpu.VMEM((2, GROUP_W, tq), F32)],
        compiler_params=_cparams(("parallel", "arbitrary")), name="attn_a",
    )(qt, k, vt, a_lambda, g_full, gmean)


def _dsa_kernel(qt_ref, k_ref, vt_ref, qit_ref, ki_ref, kiwit_ref, o_ref, key_ref, y_ref, m_ref, l_ref, acc_ref,
                *, tq, tk, past, s_tot, topk, idx_bits, t_valid):
    q0 = pl.program_id(1) * tq
    n_kv = _num_kv_blocks(q0, tq, tk, past, s_tot)
    row_pos = past + q0 + lax.broadcasted_iota(I32, (1, tq), 1)
    key_iota = lax.broadcasted_iota(I32, (tk, 1), 0)
    int_min = jnp.int32(INT_MIN)

    qit = qit_ref[...]
    blocks = []
    for h in range(N_HEADS):
        qh = qit[h * D_IDX:(h + 1) * D_IDX, :]
        hi, lo = _split_bf16(qh)
        blocks.append(jnp.concatenate([hi, lo, hi], axis=0))
    qi = jnp.concatenate(blocks, axis=1)
    kiwit = kiwit_ref[...]
    w_h = [kiwit[D_IDX + h:D_IDX + h + 1, :] for h in range(N_HEADS)]

    def score_body(c, carry):
        k0 = pl.multiple_of(c * tk, tk)
        kib = ki_ref[pl.ds(k0, tk), :]
        sc = jnp.dot(kib, qi, preferred_element_type=F32)
        isc = jnp.maximum(sc[:, 0:tq], 0.0) * w_h[0]
        for h in range(1, N_HEADS):
            isc = isc + jnp.maximum(sc[:, h * tq:(h + 1) * tq], 0.0) * w_h[h]
        isc = jnp.where(isc == 0.0, 0.0, isc)
        bits = pltpu.bitcast(isc, I32)
        key = bits ^ ((bits >> 31) & jnp.int32(0x7FFFFFFF))
        key_ref[c] = jnp.where((k0 + key_iota) <= row_pos, key, int_min)
        return carry

    lax.fori_loop(0, n_kv, score_body, 0)

    def count(pred):
        def block_count(c):
            ones = jnp.where(pred(key_ref[c], c * tk), 1.0, 0.0)
            return jnp.sum(ones.reshape(tk // 8, 8, tq), axis=0)

        def body(i, acc):
            return acc + block_count(2 * i) + block_count(2 * i + 1)

        acc = lax.fori_loop(0, n_kv // 2, body, jnp.zeros((8, tq), F32))
        odd = (n_kv & 1).astype(F32)
        acc = acc + odd * block_count(n_kv - 1)
        return jnp.sum(acc, axis=0, keepdims=True)

    kk = jnp.float32(topk)

    def bit_body(r, prefix):
        cand_u = prefix | lax.shift_left(jnp.int32(1), 31 - r)
        cand_s = cand_u ^ int_min
        cnt = count(lambda blk, k0: blk >= cand_s)
        return jnp.where(cnt >= kk, cand_u, prefix)

    prefix = lax.fori_loop(0, 32, bit_body, jnp.zeros((1, tq), I32))
    thr = prefix ^ int_min

    n_gt = count(lambda blk, k0: blk > thr)
    n_ge = count(lambda blk, k0: blk >= thr)
    r_need = kk - n_gt
    live = lax.broadcasted_iota(I32, (1, tq), 1) < t_valid
    any_tie = jnp.max(jnp.where(live, n_ge - kk, 0.0)) > 0.0
    y_ref[...] = jnp.full((1, tq), s_tot, I32)

    @pl.when(any_tie)
    def _():
        def ybody(r, y):
            cand = y | lax.shift_left(jnp.int32(1), idx_bits - 1 - r)
            cnt = count(lambda blk, k0: (blk == thr) & ((k0 + key_iota) < cand))
            return jnp.where(cnt < r_need, cand, y)
        y_ref[...] = lax.fori_loop(0, idx_bits, ybody, jnp.zeros((1, tq), I32))

    y_last = y_ref[...]

    def mask_fn(c, k0):
        blk = key_ref[c]
        pos = k0 + key_iota
        sel = (blk > thr) | ((blk == thr) & (pos <= y_last))
        return sel & (pos <= row_pos)

    qt = qt_ref[...]
    row_head = lax.broadcasted_iota(I32, (GROUP_W, 1), 0) // HEAD_W
    qst = jnp.concatenate([jnp.where(row_head == h, qt, 0.0) for h in range(N_HEADS)], axis=1).astype(BF16)
    _attn_core(qst, N_HEADS, k_ref, vt_ref, 0, n_kv, tq, tk, mask_fn, m_ref, l_ref, acc_ref, zero_masked=True)
    o_ref[...] = _normalised(acc_ref[0], l_ref[...], 0, tq).T


def _dsa(qt, k, vt, qit, ki3, kiwit, *, bsz, tq, tk, past, topk, t_valid):
    t = qt.shape[1] // bsz
    s_tot = k.shape[0] // bsz
    nq = t // tq
    idx_bits = int(math.ceil(math.log2(s_tot))) + 1
    kern = functools.partial(_dsa_kernel, tq=tq, tk=tk, past=past, s_tot=s_tot, topk=topk,
                             idx_bits=idx_bits, t_valid=t_valid)
    qspec = lambda w: pl.BlockSpec((w, tq), lambda b, i: (0, b * nq + i))
    return pl.pallas_call(
        kern, grid=(bsz, nq),
        in_specs=[qspec(GROUP_W),
                  pl.BlockSpec((s_tot, GROUP_W), lambda b, i: (b, 0)),
                  pl.BlockSpec((GROUP_W, s_tot), lambda b, i: (0, b)),
                  qspec(GROUP_W),
                  pl.BlockSpec((s_tot, 3 * D_IDX), lambda b, i: (b, 0)),
                  qspec(LANES)],
        out_specs=pl.BlockSpec((tq, GROUP_W), lambda b, i: (b * nq + i, 0)),
        out_shape=jax.ShapeDtypeStruct((bsz * t, GROUP_W), F32),
        scratch_shapes=[pltpu.VMEM((s_tot // tk, tk, tq), I32), pltpu.VMEM((1, tq), I32),
                        pltpu.VMEM((1, N_HEADS * tq), F32), pltpu.VMEM((1, N_HEADS * tq), F32),
                        pltpu.VMEM((1, GROUP_W, tq), F32)],
        compiler_params=_cparams(("parallel", "arbitrary")), name="dsa",
    )(qt, k, vt, qit, ki3, kiwit)


_HALO_C = 8
_HALO_D = 32


def _conv_kernel(*refs, tm, single_tile):
    if single_tile:
        (cb_ref, uc_ref, ud_ref, hc_ref, hd_ref, cw_ref, dw_ref, db_ref, lg_ref, lb_ref,
         oc_ref, od_ref, ec_ref, ed_ref) = refs
        pc_ref = pd_ref = None
    else:
        (cb_ref, uc_ref, ud_ref, pc_ref, pd_ref, hc_ref, hd_ref, cw_ref, dw_ref, db_ref, lg_ref, lb_ref,
         oc_ref, od_ref, ec_ref, ed_ref) = refs
    i = pl.program_id(1)

    @pl.when(i == 0)
    def _():
        ec_ref[0:_HALO_C] = hc_ref[0]
        ed_ref[0:_HALO_D] = hd_ref[0]

    if not single_tile:
        @pl.when(i > 0)
        def _():
            ec_ref[0:_HALO_C] = pc_ref[0]
            ed_ref[0:_HALO_D] = pd_ref[0]

    ec_ref[_HALO_C:_HALO_C + tm] = uc_ref[0]
    ed_ref[_HALO_D:_HALO_D + tm] = ud_ref[0]

    cw = cw_ref[...]
    acc = ec_ref[pl.ds(_HALO_C - (C_WIDTH - 1), tm), :] * cw[0:1]
    for w in range(1, C_WIDTH):
        acc = acc + ec_ref[pl.ds(_HALO_C - (C_WIDTH - 1) + w, tm), :] * cw[w:w + 1]
    oc_ref[0] = cb_ref[0] * acc

    dw = dw_ref[...]
    acc = ed_ref[pl.ds(_HALO_D - (D_WIDTH - 1), tm), :] * dw[0:1]
    for w in range(1, D_WIDTH):
        acc = acc + ed_ref[pl.ds(_HALO_D - (D_WIDTH - 1) + w, tm), :] * dw[w:w + 1]
    z = _layer_norm(acc + db_ref[...], lg_ref[...], lb_ref[...])
    od_ref[0] = z * _sigmoid(z)


def _convs(cb, uc, ud, hist_c, hist_d, c_w, d_w, d_b, ln_g, ln_b, *, tm):
    bsz, t, w = cb.shape
    nt = t // tm
    single = nt == 1
    main = pl.BlockSpec((1, tm, w), lambda b, i: (b, i, 0))
    prev_c = pl.BlockSpec((1, _HALO_C, w), lambda b, i: (b, jnp.maximum(i * (tm // _HALO_C) - 1, 0), 0))
    prev_d = pl.BlockSpec((1, _HALO_D, w), lambda b, i: (b, jnp.maximum(i * (tm // _HALO_D) - 1, 0), 0))
    hist_cs = pl.BlockSpec((1, _HALO_C, w), lambda b, i: (b, 0, 0))
    hist_ds = pl.BlockSpec((1, _HALO_D, w), lambda b, i: (b, 0, 0))
    full2 = lambda a: pl.BlockSpec(a.shape, lambda b, i: (0, 0))
    in_specs = [main, main, main] + ([] if single else [prev_c, prev_d]) + [hist_cs, hist_ds] + \
        [full2(c_w), full2(d_w), full2(d_b), full2(ln_g), full2(ln_b)]
    args = [cb, uc, ud] + ([] if single else [uc, ud]) + [hist_c, hist_d, c_w, d_w, d_b, ln_g, ln_b]
    kern = functools.partial(_conv_kernel, tm=tm, single_tile=single)
    return pl.pallas_call(
        kern, grid=(bsz, nt), in_specs=in_specs,
        out_specs=[main, main],
        out_shape=[jax.ShapeDtypeStruct((bsz, t, w), F32)] * 2,
        scratch_shapes=[pltpu.VMEM((_HALO_C + tm, w), F32), pltpu.VMEM((_HALO_D + tm, w), F32)],
        compiler_params=_cparams(("parallel", "arbitrary")), name="convs",
    )(*args)


def _mix_kernel(x_ref, oa_ref, ob_ref, oc_ref, od_ref, pw_ref, pwb_ref, wo_ref, g_ref, b_ref, h_ref, *, alpha):
    od = jnp.dot(od_ref[...].astype(BF16), pw_ref[...], preferred_element_type=F32) + pwb_ref[...]
    cat = jnp.concatenate([oa_ref[...], ob_ref[...], oc_ref[...], od], axis=1).astype(BF16)
    mix = jnp.dot(cat, wo_ref[...], preferred_element_type=F32)
    h_ref[...] = _layer_norm(alpha * x_ref[...] + mix, g_ref[...], b_ref[...])


def _mix(x, oa, ob, oc, od, pw, pwb, wo, g, b, *, tm, alpha):
    n, d = x.shape
    row = lambda w: pl.BlockSpec((tm, w), lambda i: (i, 0))
    full = lambda a: pl.BlockSpec(a.shape, lambda i: (0, 0))
    return pl.pallas_call(
        functools.partial(_mix_kernel, alpha=alpha), grid=(n // tm,),
        in_specs=[row(d), row(GROUP_W), row(GROUP_W), row(GROUP_W), row(GROUP_W),
                  full(pw), full(pwb), full(wo), full(g), full(b)],
        out_specs=row(d), out_shape=jax.ShapeDtypeStruct((n, d), F32),
        compiler_params=_cparams(("parallel",)), name="mix",
    )(x, oa, ob, oc, od, pw, pwb, wo, g, b)


def _staircase():
    return [(a, b) for a in range(PEER_TOPK) for b in range(PEER_TOPK) if (a + 1) * (b + 1) <= PEER_TOPK]


def _tree_rows(x, op):
    while x.shape[0] > 1:
        half = x.shape[0] // 2
        y = op(x[:half], x[half:2 * half])
        x = y if x.shape[0] % 2 == 0 else jnp.concatenate([y, x[2 * half:]], axis=0)
    return x


def _extract_top(x, payload, n_out, n_rows):
    row = lax.broadcasted_iota(I32, x.shape, 0).astype(F32)
    vals, idxs, pays = [], [], []
    for _ in range(n_out):
        m = _tree_rows(x, jnp.maximum)
        idx = _tree_rows(jnp.where(x == m, row, float(n_rows)), jnp.minimum)
        hit = row == idx
        vals.append(m)
        idxs.append(idx)
        if payload is not None:
            pays.append(_tree_rows(jnp.where(hit, payload, -1.0), jnp.maximum))
        x = jnp.where(hit, -jnp.inf, x)
    return vals, idxs, pays


def _peer_select_kernel(h_ref, wq_ref, sk_ref, eid_ref, g_ref, sv_ref, si_ref, *, tm):
    ht = h_ref[...].T.astype(BF16)
    n_groups = 2 * PEER_HEADS
    nt = tm // LANES

    def to_tiles(x):
        return pltpu.einshape("trl->rtl", jnp.stack([x[:, LANES * t:LANES * (t + 1)] for t in range(nt)], axis=0))

    def from_tiles(y):
        z = pltpu.einshape("rtl->trl", y)
        return jnp.concatenate([z[t] for t in range(nt)], axis=1)

    def group_body(g, carry):
        r0 = pl.multiple_of(g * N_KEYS, N_KEYS)
        qg = jnp.dot(wq_ref[pl.ds(r0, N_KEYS), :], ht, preferred_element_type=F32)
        st = jnp.dot(sk_ref[g % 2], qg.astype(BF16), preferred_element_type=F32)
        vals, idxs, _ = _extract_top(to_tiles(st), None, PEER_TOPK, N_KEYS)
        sv_ref[g] = jnp.concatenate(vals, axis=0)
        si_ref[g] = jnp.concatenate(idxs, axis=0)
        return carry

    lax.fori_loop(0, n_groups, group_body, 0)

    pairs = _staircase()

    def head_body(hd, carry):
        sv0 = sv_ref[2 * hd]; sv1 = sv_ref[2 * hd + 1]
        si0 = si_ref[2 * hd]; si1 = si_ref[2 * hd + 1]
        cand = jnp.concatenate([sv0[a:a + 1] + sv1[b:b + 1] for a, b in pairs], axis=0)
        cand_e = jnp.concatenate([si0[a:a + 1] * float(N_KEYS) + si1[b:b + 1] for a, b in pairs], axis=0)
        vals, _, pays = _extract_top(cand, cand_e, PEER_TOPK, len(pairs))
        fv = jnp.concatenate(vals, axis=0)
        e = jnp.exp(fv - fv[0:1])
        gate = e / jnp.sum(e, axis=0, keepdims=True)
        r0 = pl.multiple_of(hd * PEER_TOPK, PEER_TOPK)
        eid_ref[pl.ds(r0, PEER_TOPK), :] = from_tiles(jnp.concatenate(pays, axis=0)).astype(I32)
        g_ref[pl.ds(r0, PEER_TOPK), :] = from_tiles(gate)
        return carry

    lax.fori_loop(0, PEER_HEADS, head_body, 0)


_SELECT_TOKENS = 8 * LANES


def _peer_select(h, wq_t, sk, *, tm):
    n, d = h.shape
    n_slots = PEER_HEADS * PEER_TOPK
    return pl.pallas_call(
        functools.partial(_peer_select_kernel, tm=tm), grid=(n // tm,),
        in_specs=[pl.BlockSpec((tm, d), lambda i: (i, 0)),
                  pl.BlockSpec(wq_t.shape, lambda i: (0, 0)),
                  pl.BlockSpec(sk.shape, lambda i: (0, 0, 0))],
        out_specs=[pl.BlockSpec((n_slots, tm), lambda i: (0, i)),
                   pl.BlockSpec((n_slots, tm), lambda i: (0, i))],
        out_shape=[jax.ShapeDtypeStruct((n_slots, n), I32), jax.ShapeDtypeStruct((n_slots, n), F32)],
        scratch_shapes=[pltpu.VMEM((2 * PEER_HEADS, PEER_TOPK, tm // LANES, LANES), F32),
                        pltpu.VMEM((2 * PEER_HEADS, PEER_TOPK, tm // LANES, LANES), F32)],
        compiler_params=_cparams(("parallel",)), name="peer_select",
    )(h, wq_t, sk)


_GATE_TOKENS = 8


def _peer_gates_kernel(eid_ref, g_ref, w_ref, *, tm):
    sub = lax.broadcasted_iota(I32, (N_KEYS, PEER_HEADS * PEER_TOPK), 0)

    def token_gates(t):
        e = eid_ref[pl.ds(t, 1), :]
        g = g_ref[pl.ds(t, 1), :]
        g_hi = g.astype(BF16).astype(F32)
        g_lo = g - g_hi
        oh_i = jnp.where((e >> 7) == sub, 1.0, 0.0).astype(BF16)
        hit_j = (e & (N_KEYS - 1)) == sub
        gj_hi = jnp.where(hit_j, g_hi, 0.0).astype(BF16)
        gj_lo = jnp.where(hit_j, g_lo, 0.0).astype(BF16)
        lhs = jnp.concatenate([oh_i, oh_i], axis=1)
        rhs = jnp.concatenate([gj_hi, gj_lo], axis=1)
        return lax.dot_general(lhs, rhs, _NT, preferred_element_type=F32)

    def body(b, carry):
        t0 = pl.multiple_of(b * _GATE_TOKENS, _GATE_TOKENS)
        w = jnp.stack([token_gates(t0 + k) for k in range(_GATE_TOKENS)], axis=0)
        w_ref[:, pl.ds(t0, _GATE_TOKENS), :] = pltpu.einshape("tij->itj", w)
        return carry

    lax.fori_loop(0, tm // _GATE_TOKENS, body, 0)


def _peer_gates(eid, gate, *, tm):
    n, n_slots = eid.shape
    return pl.pallas_call(
        functools.partial(_peer_gates_kernel, tm=tm), grid=(n // tm,),
        in_specs=[pl.BlockSpec((tm, n_slots), lambda i: (i, 0)),
                  pl.BlockSpec((tm, n_slots), lambda i: (i, 0))],
        out_specs=pl.BlockSpec((N_KEYS, tm, N_KEYS), lambda i: (0, i, 0)),
        out_shape=jax.ShapeDtypeStruct((N_KEYS, n, N_KEYS), F32),
        compiler_params=_cparams(("parallel",)), name="peer_gates",
    )(eid, gate)


def _gelu(x):
    return 0.5 * x * (1.0 + lax.erf(x * (2.0 ** -0.5)))


def _peer_dense_kernel(h_ref, w_ref, u_ref, v_ref, f_ref):
    @pl.when(pl.program_id(1) == 0)
    def _():
        f_ref[...] = jnp.zeros(f_ref.shape, F32)

    act = _gelu(lax.dot_general(h_ref[...], u_ref[...], _NT, preferred_element_type=F32))
    y = jnp.concatenate(
        [w_ref[i] * act[:, i * N_KEYS:(i + 1) * N_KEYS] for i in range(w_ref.shape[0])],
        axis=1).astype(BF16)
    f_ref[...] += jnp.dot(y, v_ref[...], preferred_element_type=F32)


def _peer_dense(hb, wmat, u, v, *, tm, te):
    n, d = hb.shape
    n_exp = u.shape[0]
    return pl.pallas_call(
        _peer_dense_kernel, grid=(n // tm, n_exp // te),
        in_specs=[pl.BlockSpec((tm, d), lambda i, e: (i, 0)),
                  pl.BlockSpec((te // N_KEYS, tm, N_KEYS), lambda i, e: (e, i, 0)),
                  pl.BlockSpec((te, d), lambda i, e: (e, 0)),
                  pl.BlockSpec((te, d), lambda i, e: (e, 0))],
        out_specs=pl.BlockSpec((tm, d), lambda i, e: (i, 0)),
        out_shape=jax.ShapeDtypeStruct((n, d), F32),
        compiler_params=_cparams(("parallel", "arbitrary")), name="peer_dense",
    )(hb, wmat, u, v)


def _final_kernel(h_ref, f_ref, p_ref, g_ref, b_ref, wg_ref, bg_ref, wp_ref, y_ref, *, alpha):
    h2 = _layer_norm(alpha * h_ref[...] + f_ref[...], g_ref[...], b_ref[...])
    gate = _sigmoid(jnp.dot(h2.astype(BF16), wg_ref[...], preferred_element_type=F32) + bg_ref[...])
    emb = jnp.dot(p_ref[...].astype(BF16), wp_ref[...], preferred_element_type=F32)
    y_ref[...] = h2 + gate * emb


def _final(h, f, p, g, b, wg, bg, wp, *, tm, alpha):
    n, d = h.shape
    row = lambda w: pl.BlockSpec((tm, w), lambda i: (i, 0))
    full = lambda a: pl.BlockSpec(a.shape, lambda i: (0, 0))
    return pl.pallas_call(
        functools.partial(_final_kernel, alpha=alpha), grid=(n // tm,),
        in_specs=[row(d), row(d), row(p.shape[1]), full(g), full(b), full(wg), full(bg), full(wp)],
        out_specs=row(d), out_shape=jax.ShapeDtypeStruct((n, d), F32),
        compiler_params=_cparams(("parallel",)), name="final",
    )(h, f, p, g, b, wg, bg, wp)


def _pack_w_in(w_in):
    d = w_in.shape[0]
    cols = [w_in[:, 0:1792], w_in[:, 1792:1860], jnp.zeros((d, 60), w_in.dtype), w_in[:, 1860:3140]]
    nat = jnp.concatenate(cols, axis=1).astype(BF16)
    seg = lambda name, width=GROUP_W: nat[:, _SEG[name]:_SEG[name] + width]
    tr = jnp.concatenate([seg("qa"), seg("qb"), seg("qi"), seg("kiwi", 128), seg("va"), seg("vb")], axis=1).T
    return nat, tr


def _row(a):
    return a.reshape(1, -1)


def _tile_rows(n, pref):
    t = pref
    while n % t:
        t //= 2
    return t


def _prep_weights(lw, depth):
    (w_in, a_lambda, a_subln_g, c_conv_w, d_conv_w, d_conv_b, d_ln_g, d_ln_b, d_pw_w, d_pw_b,
     w_out, ln1_g, ln1_b, peer_wq, peer_subkeys, peer_u, peer_v, ln2_g, ln2_b,
     ple_w, ple_gate_w, ple_gate_b) = lw
    head_of_lane = jnp.arange(GROUP_W) // HEAD_W
    gmean = (head_of_lane[:, None] == head_of_lane[None, :]).astype(BF16) * (1.0 / HEAD_W)
    w_nat, w_tr = _pack_w_in(w_in)
    return dict(
        w_in=w_nat, w_in_t=w_tr, a_lambda=a_lambda, a_g=_row(jnp.tile(a_subln_g, N_HEADS)),
        gmean=gmean.astype(BF16),
        c_w=c_conv_w, d_w=d_conv_w, d_b=_row(d_conv_b), d_ln_g=_row(d_ln_g), d_ln_b=_row(d_ln_b),
        d_pw=d_pw_w.astype(BF16), d_pwb=_row(d_pw_b), w_out=w_out.astype(BF16),
        ln1_g=_row(ln1_g), ln1_b=_row(ln1_b),
        wq_t=peer_wq.T.astype(BF16), sk=peer_subkeys.astype(BF16),
        u=peer_u.astype(BF16), v=peer_v.astype(BF16),
        ln2_g=_row(ln2_g), ln2_b=_row(ln2_b),
        ple_w=ple_w.astype(BF16), wg=ple_gate_w.astype(BF16), bg=_row(ple_gate_b),
        alpha=float((2 * depth) ** 0.25),
    )


def _index_keys(ki):
    k_hi = ki.astype(BF16)
    k_lo = (ki - k_hi.astype(F32)).astype(BF16)
    return jnp.concatenate([k_hi, k_hi, k_lo], axis=-1)


def _token_tail(w, x, oa, ob, oc, od, valid_rows, p):
    n = x.shape[0]
    h = _mix(x, oa, ob, oc, od, w["d_pw"], w["d_pwb"], w["w_out"], w["ln1_g"], w["ln1_b"],
             tm=_tile_rows(n, 512), alpha=w["alpha"])
    if valid_rows is not None:
        bsz, t_pad, t = valid_rows
        h = h.reshape(bsz, t_pad, -1)[:, :t].reshape(bsz * t, -1)
    n = h.shape[0]
    n_sel = -(-n // _SELECT_TOKENS) * _SELECT_TOKENS
    h_sel = h if n_sel == n else jnp.pad(h, ((0, n_sel - n), (0, 0)))
    eid_t, gate_t = _peer_select(h_sel, w["wq_t"], w["sk"], tm=_SELECT_TOKENS)
    wmat = _peer_gates(eid_t[:, :n].T, gate_t[:, :n].T, tm=_tile_rows(n, 128))
    f = _peer_dense(h.astype(BF16), wmat, w["u"], w["v"], tm=_tile_rows(n, 1024), te=8 * N_KEYS)
    return _final(h, f, p, w["ln2_g"], w["ln2_b"], w["wg"], w["bg"], w["ple_w"],
                  tm=_tile_rows(n, 512), alpha=w["alpha"])


def _layer(layer_idx, depth, xp, xs, caches, state_c, state_d, page_table, pp, ps, lw):
    w = _prep_weights(lw, depth)
    bp, tp, d = xp.shape
    bs, ts, _ = xs.shape
    ts_pad = 8
    n_pages = page_table.shape[1]
    page = caches[0].shape[3]
    past = n_pages * page
    lam_init = 0.8 - 0.6 * math.exp(-0.3 * (layer_idx + 1))

    pr = _project(xp.reshape(bp * tp, d), w["w_in"], w["w_in_t"], _tile_rows(bp * tp, 512), feature_major=True)
    tq = _tile_rows(tp, 256)
    tk = _tile_rows(tp, 256)
    oa = _attention_a(pr["qat"], pr["kab"], pr["vat"], w["a_lambda"], w["a_g"], w["gmean"],
                      bsz=bp, tq=tq, tk=tk, past=0, lam_init=lam_init)
    ob = _dsa(pr["qbt"], pr["kbb"], pr["vbt"], pr["qit"], _index_keys(pr["kiwi"][:, :D_IDX]), pr["kiwit"],
              bsz=bp, tq=tq, tk=tk, past=0, topk=min(DSA_TOPK, tp // 4), t_valid=tq)
    r3p = lambda a: a.reshape(bp, tp, a.shape[-1])
    zc = jnp.zeros((bp, _HALO_C, GROUP_W), F32)
    zd = jnp.zeros((bp, _HALO_D, GROUP_W), F32)
    oc, od = _convs(r3p(pr["cb"]), r3p(pr["uc"]), r3p(pr["ud"]), zc, zd, w["c_w"], w["d_w"], w["d_b"],
                    w["d_ln_g"], w["d_ln_b"], tm=_tile_rows(tp, 512))
    yp = _token_tail(w, xp.reshape(bp * tp, d), oa, ob, oc.reshape(bp * tp, -1), od.reshape(bp * tp, -1),
                     None, pp.reshape(bp * tp, -1)).reshape(bp, tp, d)
    new_p = (r3p(pr["ka"]), r3p(pr["va"]), r3p(pr["kb"]), r3p(pr["vb"]), r3p(pr["kiwi"])[:, :, :D_IDX],
             jnp.concatenate([zc, r3p(pr["uc"])], axis=1)[:, -(C_WIDTH - 1):],
             jnp.concatenate([zd, r3p(pr["ud"])], axis=1)[:, -(D_WIDTH - 1):])

    xs_pad = jnp.pad(xs, ((0, 0), (0, ts_pad - ts), (0, 0))).reshape(bs * ts_pad, d)
    sr = _project(xs_pad, w["w_in"], w["w_in_t"], _tile_rows(bs * ts_pad, 512), feature_major=False)
    r3s = lambda a: a.reshape(bs, ts_pad, a.shape[-1])
    news = [r3s(sr["ka"]), r3s(sr["va"]), r3s(sr["kb"]), r3s(sr["vb"]), r3s(sr["kiwi"])[:, :, :D_IDX]]
    oa_s, ob_s = _sample_attention(page_table, caches, layer_idx, news, r3s(sr["qa"]), r3s(sr["qb"]), r3s(sr["qi"]),
                                   r3s(sr["kiwi"]), w["a_lambda"], w["a_g"], w["gmean"],
                                   topk=min(DSA_TOPK, (past + ts) // 4), lam_init=lam_init, t_valid=ts)
    unpad = lambda a: a.reshape(bs * ts_pad, GROUP_W)
    hist_c = jnp.concatenate([jnp.zeros((bs, _HALO_C - (C_WIDTH - 1), GROUP_W), F32), state_c], axis=1)
    hist_d = jnp.concatenate([jnp.zeros((bs, _HALO_D - (D_WIDTH - 1), GROUP_W), F32), state_d], axis=1)
    oc_s, od_s = _convs(r3s(sr["cb"]), r3s(sr["uc"]), r3s(sr["ud"]), hist_c, hist_d, w["c_w"], w["d_w"],
                        w["d_b"], w["d_ln_g"], w["d_ln_b"], tm=ts_pad)
    ys = _token_tail(w, xs_pad, unpad(oa_s), unpad(ob_s), oc_s.reshape(bs * ts_pad, -1),
                     od_s.reshape(bs * ts_pad, -1), (bs, ts_pad, ts), ps.reshape(bs * ts, -1)).reshape(bs, ts, d)
    cut = lambda a: r3s(a)[:, :ts]
    new_s = (cut(sr["ka"]), cut(sr["va"]), cut(sr["kb"]), cut(sr["vb"]), cut(sr["kiwi"])[:, :, :D_IDX],
             jnp.concatenate([state_c, cut(sr["uc"])], axis=1)[:, -(C_WIDTH - 1):],
             jnp.concatenate([state_d, cut(sr["ud"])], axis=1)[:, -(D_WIDTH - 1):])
    return yp, ys, new_p, new_s


def kernel(x_prompt, x_sample, cache_a_k, cache_a_v, cache_b_k, cache_b_v, cache_b_kidx, state_c_conv, state_d_conv, page_table, p_prompt, p_sample, w_in, a_lambda, a_subln_g, c_conv_w, d_conv_w, d_conv_b, d_ln_g, d_ln_b, d_pw_w, d_pw_b, w_out, ln1_g, ln1_b, peer_wq, peer_subkeys, peer_u, peer_v, ln2_g, ln2_b, ple_w, ple_gate_w, ple_gate_b):
    depth = w_in.shape[0]
    n_pool, page = cache_a_k.shape[1], cache_a_k.shape[2]
    feature_major = lambda c: jnp.moveaxis(c, 2, -1).reshape(depth, n_pool, -1, page)
    caches = [feature_major(c) for c in (cache_a_k, cache_a_v, cache_b_k, cache_b_v, cache_b_kidx)]
    xp, xs = x_prompt, x_sample
    news_p, news_s = [], []
    for l in range(depth):
        lw = (w_in[l], a_lambda[l], a_subln_g[l], c_conv_w[l], d_conv_w[l], d_conv_b[l], d_ln_g[l], d_ln_b[l],
              d_pw_w[l], d_pw_b[l], w_out[l], ln1_g[l], ln1_b[l], peer_wq[l], peer_subkeys[l], peer_u[l],
              peer_v[l], ln2_g[l], ln2_b[l], ple_w[l], ple_gate_w[l], ple_gate_b[l])
        xp, xs, new_p, new_s = _layer(l, depth, xp, xs, caches, state_c_conv[l], state_d_conv[l],
                                      page_table, p_prompt[l], p_sample[l], lw)
        news_p.append(new_p)
        news_s.append(new_s)

    bp, tp, _ = x_prompt.shape
    bs, ts, _ = x_sample.shape
    shapes = [(N_HEADS, 2, 32), (N_HEADS, HEAD_W), (N_HEADS, HEAD_W), (N_HEADS, HEAD_W), (D_IDX,)]

    def stack(news, i, lead):
        a = jnp.stack([n[i] for n in news], axis=0)
        return a.reshape((depth,) + lead + shapes[i]) if i < 5 else a

    outs = [xp, xs]
    for i in range(7):
        outs.append(stack(news_p, i, (bp, tp)))
        outs.append(stack(news_s, i, (bs, ts)))
    return tuple(outs)
```

```python
import functools
import math

import jax
import jax.numpy as jnp
from jax import lax
from jax.experimental import pallas as pl
from jax.experimental.pallas import tpu as pltpu

F32 = jnp.float32
BF16 = jnp.bfloat16
I32 = jnp.int32

LN_EPS = 1e-5
N_HEADS = 4
HEAD_W = 64
GROUP_W = 256
D_IDX = 64
DSA_TOPK = 256
C_WIDTH = 3
D_WIDTH = 31
PEER_HEADS = 8
PEER_TOPK = 16
N_KEYS = 128
LANES = 128
NEG_BIG = -1e30
INT_MIN = -2 ** 31
VMEM_LIMIT = 56 * 1024 * 1024

_SEG = dict(qa=0, ka=256, va=512, qb=768, kb=1024, vb=1280, qi=1536, kiwi=1792,
            cb=1920, cc=2176, cx=2432, da=2688, dg=2944)
_N_PACKED = 3200
_SEG_T = dict(qa=0, qb=256, qi=512, kiwi=768, va=896, vb=1152)
_N_PACKED_T = 1408

_NT = (((1,), (1,)), ((), ()))


def _cparams(sem):
    return pltpu.CompilerParams(dimension_semantics=sem, vmem_limit_bytes=VMEM_LIMIT)


def _split_bf16(x):
    hi = x.astype(BF16)
    lo = (x - hi.astype(F32)).astype(BF16)
    return hi, lo


def _layer_norm(x, g, b):
    mu = jnp.mean(x, axis=-1, keepdims=True)
    xc = x - mu
    var = jnp.mean(xc * xc, axis=-1, keepdims=True)
    return xc * lax.rsqrt(var + LN_EPS) * g + b


def _sigmoid(x):
    return 1.0 / (1.0 + jnp.exp(-x))


_PROJ_COMMON = ("ka", "va", "kb", "vb", "kiwi", "cb", "uc", "ud")
_PROJ_FEATURE_MAJOR = ("kab", "kbb", "qat", "qbt", "qit", "kiwit", "vat", "vbt")
_PROJ_ROW_MAJOR = ("qa", "qb", "qi")


def _proj_kernel(x_ref, w_ref, wt_ref, *out_refs, names, scale_a, scale_b):
    o = dict(zip(names, out_refs))
    xb = x_ref[...].astype(BF16)

    def seg(name, width=GROUP_W):
        off = _SEG[name]
        return jnp.dot(xb, w_ref[:, off:off + width], preferred_element_type=F32)

    def seg_t(name, width=GROUP_W):
        off = _SEG_T[name]
        return lax.dot_general(wt_ref[off:off + width, :], xb, _NT, preferred_element_type=F32)

    ka = seg("ka"); kb = seg("kb")
    o["ka"][...] = ka; o["kb"][...] = kb
    o["va"][...] = seg("va"); o["vb"][...] = seg("vb")
    o["kiwi"][...] = seg("kiwi", LANES)
    o["cb"][...] = seg("cb")
    o["uc"][...] = seg("cc") * seg("cx")
    o["ud"][...] = seg("da") * _sigmoid(seg("dg"))
    if "qat" in o:
        o["kab"][...] = ka.astype(BF16); o["kbb"][...] = kb.astype(BF16)
        o["qat"][...] = seg_t("qa") * scale_a
        o["qbt"][...] = seg_t("qb") * scale_b
        o["qit"][...] = seg_t("qi")
        o["kiwit"][...] = seg_t("kiwi", LANES)
        o["vat"][...] = seg_t("va").astype(BF16)
        o["vbt"][...] = seg_t("vb").astype(BF16)
    else:
        o["qa"][...] = seg("qa") * scale_a
        o["qb"][...] = seg("qb") * scale_b
        o["qi"][...] = seg("qi")


def _project(x, w_packed, wt_packed, tm, feature_major):
    n, d = x.shape
    nat = lambda w, dt=F32: (jax.ShapeDtypeStruct((n, w), dt), pl.BlockSpec((tm, w), lambda i: (i, 0)))
    tr = lambda w, dt=F32: (jax.ShapeDtypeStruct((w, n), dt), pl.BlockSpec((w, tm), lambda i: (0, i)))
    outs = [nat(256), nat(256), nat(256), nat(256), nat(LANES), nat(256), nat(256), nat(256)]
    if feature_major:
        names = _PROJ_COMMON + _PROJ_FEATURE_MAJOR
        outs += [nat(256, BF16), nat(256, BF16), tr(256), tr(256), tr(256), tr(LANES), tr(256, BF16), tr(256, BF16)]
    else:
        names = _PROJ_COMMON + _PROJ_ROW_MAJOR
        outs += [nat(256), nat(256), nat(256)]
    kern = functools.partial(_proj_kernel, names=names, scale_a=float(32 ** -0.5), scale_b=float(HEAD_W ** -0.5))
    res = pl.pallas_call(
        kern, grid=(n // tm,),
        in_specs=[pl.BlockSpec((tm, d), lambda i: (i, 0)),
                  pl.BlockSpec((d, _N_PACKED), lambda i: (0, 0)),
                  pl.BlockSpec((_N_PACKED_T, d), lambda i: (0, 0))],
        out_specs=[o[1] for o in outs], out_shape=[o[0] for o in outs],
        compiler_params=_cparams(("parallel",)), name="proj",
    )(x, w_packed, wt_packed)
    return dict(zip(names, res))


def _lane_heads(o, l, base, rows, lane_head):
    out = None
    for h in range(N_HEADS):
        r0 = (base + h) * rows
        part = jnp.where(lane_head == h, o[r0:r0 + rows] / l[r0:r0 + rows], 0.0)
        out = part if out is None else out + part
    return out


def _sample_attn_kernel(pt_ref, *refs, n_pages, page, rows, past, topk, idx_bits, lam_init, t_valid):
    npg = n_pages
    ak, av, bk, bv, kic = (refs[i * npg:(i + 1) * npg] for i in range(5))
    (nak, nav, nbk, nbv, nki, qa_ref, qb_ref, qi_ref, kiwi_ref, lam_ref, g_ref, gm_ref,
     oa_ref, ob_ref, y_ref) = refs[5 * npg:]
    nblk = npg + 1
    s_tot = nblk * page
    lane_head = lax.broadcasted_iota(I32, (1, GROUP_W), 1) // HEAD_W
    col_last = past + lax.broadcasted_iota(I32, (1, page), 1)
    t_row = lax.broadcasted_iota(I32, (rows, 1), 0)
    int_min = jnp.int32(INT_MIN)

    def blocks(page_refs, new_ref, dtype):
        new = new_ref[0]
        pad = jnp.zeros((page - rows, new.shape[-1]), F32)
        return [r[0, 0].astype(dtype) for r in page_refs] + [jnp.concatenate([new, pad], axis=0).T.astype(dtype)]

    def softmax_pv(q_stack, n_groups, k_blocks, v_blocks, masks):
        def masked(s, mk, fill):
            if mk is None:
                return s
            s3 = jnp.where(mk[None], s.reshape(n_groups, rows, page), fill)
            return s3.reshape(n_groups * rows, page)
        s = [masked(jnp.dot(q_stack, kb, preferred_element_type=F32), mk, NEG_BIG)
             for kb, mk in zip(k_blocks, masks)]
        m = s[0].max(axis=1, keepdims=True)
        for sj in s[1:]:
            m = jnp.maximum(m, sj.max(axis=1, keepdims=True))
        l = None
        o = None
        for sj, vb, mk in zip(s, v_blocks, masks):
            p = masked(jnp.exp(sj - m), mk, 0.0)
            lj = p.sum(axis=1, keepdims=True)
            oj = lax.dot_general(p.astype(BF16), vb, _NT, preferred_element_type=F32)
            l = lj if l is None else l + lj
            o = oj if o is None else o + oj
        return o, l

    causal_last = col_last <= (past + t_row)

    qa = qa_ref[0]
    lane_grp = lax.broadcasted_iota(I32, (1, GROUP_W), 1) // 32
    qa_stack = jnp.concatenate(
        [jnp.where(lane_grp == (2 * h + c), qa, 0.0) for c in range(2) for h in range(N_HEADS)],
        axis=0).astype(BF16)
    o, l = softmax_pv(qa_stack, 2 * N_HEADS, blocks(ak, nak, BF16), blocks(av, nav, BF16),
                      [None] * npg + [causal_last])
    o0 = _lane_heads(o, l, 0, rows, lane_head)
    o1 = _lane_heads(o, l, N_HEADS, rows, lane_head)
    lv = lam_ref[...]
    lam = (jnp.exp(jnp.sum(lv[0:1] * lv[1:2], axis=1, keepdims=True))
           - jnp.exp(jnp.sum(lv[2:3] * lv[3:4], axis=1, keepdims=True)) + lam_init)
    od = o0 - lam * o1
    hi, lo = _split_bf16(od * od)
    ms = (jnp.dot(hi, gm_ref[...], preferred_element_type=F32)
          + jnp.dot(lo, gm_ref[...], preferred_element_type=F32))
    oa_ref[0] = od * lax.rsqrt(ms + LN_EPS) * g_ref[...] * (1.0 - lam_init)

    qi = qi_ref[0]
    qi_rows = jnp.concatenate([qi[:, h * D_IDX:(h + 1) * D_IDX] for h in range(N_HEADS)], axis=0)
    q_hi, q_lo = _split_bf16(qi_rows)
    kiwi = kiwi_ref[0]
    w_h = [kiwi[:, D_IDX + h:D_IDX + h + 1] for h in range(N_HEADS)]
    key_blocks = []
    for j, kib in enumerate(blocks(kic, nki, F32)):
        k_hi, k_lo = _split_bf16(kib)
        sc = (jnp.dot(q_hi, k_hi, preferred_element_type=F32)
              + jnp.dot(q_lo, k_hi, preferred_element_type=F32)
              + jnp.dot(q_hi, k_lo, preferred_element_type=F32))
        isc = jnp.maximum(sc[0:rows], 0.0) * w_h[0]
        for h in range(1, N_HEADS):
            isc = isc + jnp.maximum(sc[h * rows:(h + 1) * rows], 0.0) * w_h[h]
        isc = jnp.where(isc == 0.0, 0.0, isc)
        bits = pltpu.bitcast(isc, I32)
        key = bits ^ ((bits >> 31) & jnp.int32(0x7FFFFFFF))
        if j == npg:
            key = jnp.where(causal_last, key, int_min)
        key_blocks.append(key)
    keys = jnp.concatenate(key_blocks, axis=1)
    cols = lax.broadcasted_iota(I32, (1, s_tot), 1)
    kk = jnp.float32(topk)
    count = lambda pred: jnp.sum(jnp.where(pred, 1.0, 0.0), axis=1, keepdims=True)

    def bit_body(r, prefix):
        b1 = lax.shift_left(jnp.int32(1), 31 - 2 * r)
        b0 = lax.shift_left(jnp.int32(1), 30 - 2 * r)
        enough = lambda cand_u: count(keys >= (cand_u ^ int_min)) >= kk
        c10 = prefix | b1
        c11 = c10 | b0
        c01 = prefix | b0
        return jnp.where(enough(c10), jnp.where(enough(c11), c11, c10), jnp.where(enough(c01), c01, prefix))

    thr = lax.fori_loop(0, 16, bit_body, jnp.zeros((rows, 1), I32)) ^ int_min
    r_need = kk - count(keys > thr)
    any_tie = jnp.max(jnp.where(t_row < t_valid, count(keys >= thr) - kk, 0.0)) > 0.0
    y_ref[...] = jnp.full((rows, 1), s_tot, I32)

    @pl.when(any_tie)
    def _():
        def ybody(r, y):
            cand = y | lax.shift_left(jnp.int32(1), idx_bits - 1 - r)
            return jnp.where(count((keys == thr) & (cols < cand)) < r_need, cand, y)
        y_ref[...] = lax.fori_loop(0, idx_bits, ybody, jnp.zeros((rows, 1), I32))

    sel = ((keys > thr) | ((keys == thr) & (cols <= y_ref[...]))) & (cols <= (past + t_row))
    qb = qb_ref[0]
    qb_stack = jnp.concatenate([jnp.where(lane_head == h, qb, 0.0) for h in range(N_HEADS)], axis=0).astype(BF16)
    o, l = softmax_pv(qb_stack, N_HEADS, blocks(bk, nbk, BF16), blocks(bv, nbv, BF16),
                      [sel[:, j * page:(j + 1) * page] for j in range(nblk)])
    ob_ref[0] = _lane_heads(o, l, 0, rows, lane_head)


def _sample_attention(page_table, caches, layer, news, qa, qb, qi, kiwi, a_lambda, g_full, gmean,
                      *, topk, lam_init, t_valid):
    bsz, n_pages = page_table.shape
    page = caches[0].shape[3]
    rows = qa.shape[1]
    past = n_pages * page
    idx_bits = int(math.ceil(math.log2((n_pages + 1) * page))) + 1
    page_spec = lambda c, j: pl.BlockSpec((1, 1, c, page), lambda b, pt: (layer, pt[b, j], 0, 0))
    seq_spec = lambda c: pl.BlockSpec((1, rows, c), lambda b, pt: (b, 0, 0))
    full2 = lambda a: pl.BlockSpec(a.shape, lambda b, pt: (0, 0))
    in_specs, args = [], []
    for cache in caches:
        for j in range(n_pages):
            in_specs.append(page_spec(cache.shape[2], j))
            args.append(cache)
    for a in list(news) + [qa, qb, qi, kiwi]:
        in_specs.append(seq_spec(a.shape[-1]))
        args.append(a)
    for a in (a_lambda, g_full, gmean):
        in_specs.append(full2(a))
        args.append(a)
    kern = functools.partial(_sample_attn_kernel, n_pages=n_pages, page=page, rows=rows, past=past, topk=topk,
                             idx_bits=idx_bits, lam_init=lam_init, t_valid=t_valid)
    return pl.pallas_call(
        kern,
        grid_spec=pltpu.PrefetchScalarGridSpec(
            num_scalar_prefetch=1, grid=(bsz,), in_specs=in_specs,
            out_specs=[seq_spec(GROUP_W), seq_spec(GROUP_W)],
            scratch_shapes=[pltpu.VMEM((rows, 1), I32)]),
        out_shape=[jax.ShapeDtypeStruct((bsz, rows, GROUP_W), F32)] * 2,
        compiler_params=_cparams(("parallel",)), name="sample_attn",
    )(page_table, *args)


def _attn_core(qst, n_groups, k_ref, vt_ref, n_full, n_kv, tq, tk, mask_fn, m_ref, l_ref, acc_ref, zero_masked):
    n_comp = n_groups // N_HEADS
    row_head = lax.broadcasted_iota(I32, (GROUP_W, 1), 0) // HEAD_W
    m_ref[...] = jnp.full(m_ref.shape, NEG_BIG, F32)
    l_ref[...] = jnp.zeros(l_ref.shape, F32)
    acc_ref[...] = jnp.zeros(acc_ref.shape, F32)
    grp = lambda a, g: a[:, g * tq:(g + 1) * tq]

    def step(c, n_blk, masked):
        width = n_blk * tk
        k0 = pl.multiple_of(c * tk, tk)
        kblk = k_ref[pl.ds(k0, width), :]
        vt = vt_ref[:, pl.ds(k0, width)]
        masks = [mask_fn(c + i, k0 + i * tk) for i in range(n_blk)] if masked else None

        def apply_masks(a, fill):
            parts = [jnp.where(masks[i], a[i * tk:(i + 1) * tk], fill) for i in range(n_blk)]
            return parts[0] if n_blk == 1 else jnp.concatenate(parts, axis=0)

        m_old = m_ref[...]
        l_old = l_ref[...]
        m_new, l_new, alpha, pb = [], [], [], []
        for g in range(n_groups):
            s = jnp.dot(kblk, grp(qst, g), preferred_element_type=F32)
            if masked:
                s = apply_masks(s, NEG_BIG)
            m_g = jnp.maximum(grp(m_old, g), jnp.max(s, axis=0, keepdims=True))
            a_g = jnp.exp(grp(m_old, g) - m_g)
            p = jnp.exp(s - m_g)
            if masked and zero_masked:
                p = apply_masks(p, 0.0)
            m_new.append(m_g)
            alpha.append(a_g)
            l_new.append(a_g * grp(l_old, g) + jnp.sum(p, axis=0, keepdims=True))
            pb.append(p.astype(BF16))
        m_ref[...] = jnp.concatenate(m_new, axis=1)
        l_ref[...] = jnp.concatenate(l_new, axis=1)
        vstack = jnp.concatenate(
            [jnp.where(row_head == h, vt, jnp.zeros_like(vt)) for h in range(N_HEADS)], axis=1)
        for comp in range(n_comp):
            base = comp * N_HEADS
            pcat = jnp.concatenate(pb[base:base + N_HEADS], axis=0)
            pv = jnp.dot(vstack, pcat, preferred_element_type=F32)
            af = jnp.concatenate(
                [jnp.broadcast_to(alpha[base + h], (HEAD_W, tq)) for h in range(N_HEADS)], axis=0)
            acc_ref[comp] = acc_ref[comp] * af + pv

    def run(lo, hi, masked):
        def pair_body(i, carry):
            step(lo + 2 * i, 2, masked)
            return carry
        n = hi - lo
        if isinstance(n, int) and n == 0:
            return
        lax.fori_loop(0, n // 2, pair_body, 0)

        @pl.when(n % 2 == 1)
        def _():
            step(hi - 1, 1, masked)

    run(0, n_full, False)
    run(n_full, n_kv, True)


def _normalised(acc, l_row, base, tq):
    rl = 1.0 / l_row
    rf = jnp.concatenate(
        [jnp.broadcast_to(rl[:, (base + h) * tq:(base + h + 1) * tq], (HEAD_W, tq)) for h in range(N_HEADS)],
        axis=0)
    return acc * rf


def _num_kv_blocks(q0, tq, tk, past, s_tot):
    return jnp.minimum((past + q0 + tq + tk - 1) // tk, s_tot // tk)


def _attn_a_kernel(qt_ref, k_ref, vt_ref, lam_ref, g_ref, gm_ref, o_ref, m_ref, l_ref, acc_ref,
                   *, tq, tk, past, s_tot, lam_init):
    q0 = pl.program_id(1) * tq
    qt = qt_ref[...]
    row_grp = lax.broadcasted_iota(I32, (GROUP_W, 1), 0) // 32
    qst = jnp.concatenate(
        [jnp.where(row_grp == (2 * h + c), qt, 0.0) for c in range(2) for h in range(N_HEADS)],
        axis=1).astype(BF16)
    n_kv = _num_kv_blocks(q0, tq, tk, past, s_tot)
    n_full = jnp.minimum((past + q0 + 1) // tk, n_kv)
    row_pos = past + q0 + lax.broadcasted_iota(I32, (1, tq), 1)
    key_iota = lax.broadcasted_iota(I32, (tk, 1), 0)

    def mask_fn(c, k0):
        return (k0 + key_iota) <= row_pos

    _attn_core(qst, 2 * N_HEADS, k_ref, vt_ref, n_full, n_kv, tq, tk, mask_fn,
               m_ref, l_ref, acc_ref, zero_masked=False)
    l_row = l_ref[...]
    o0 = _normalised(acc_ref[0], l_row, 0, tq)
    o1 = _normalised(acc_ref[1], l_row, N_HEADS, tq)
    lv = lam_ref[...]
    lam = (jnp.exp(jnp.sum(lv[0:1] * lv[1:2], axis=1, keepdims=True))
           - jnp.exp(jnp.sum(lv[2:3] * lv[3:4], axis=1, keepdims=True)) + lam_init)
    o = (o0 - lam * o1).T
    hi, lo = _split_bf16(o * o)
    ms = (jnp.dot(hi, gm_ref[...], preferred_element_type=F32)
          + jnp.dot(lo, gm_ref[...], preferred_element_type=F32))
    o_ref[...] = o * lax.rsqrt(ms + LN_EPS) * g_ref[...] * (1.0 - lam_init)


def _attention_a(qt, k, vt, a_lambda, g_full, gmean, *, bsz, tq, tk, past, lam_init):
    t = qt.shape[1] // bsz
    s_tot = k.shape[0] // bsz
    nq = t // tq
    kern = functools.partial(_attn_a_kernel, tq=tq, tk=tk, past=past, s_tot=s_tot, lam_init=lam_init)
    return pl.pallas_call(
        kern, grid=(bsz, nq),
        in_specs=[pl.BlockSpec((GROUP_W, tq), lambda b, i: (0, b * nq + i)),
                  pl.BlockSpec((s_tot, GROUP_W), lambda b, i: (b, 0)),
                  pl.BlockSpec((GROUP_W, s_tot), lambda b, i: (0, b)),
                  pl.BlockSpec(a_lambda.shape, lambda b, i: (0, 0)),
                  pl.BlockSpec((1, GROUP_W), lambda b, i: (0, 0)),
                  pl.BlockSpec((GROUP_W, GROUP_W), lambda b, i: (0, 0))],
        out_specs=pl.BlockSpec((tq, GROUP_W), lambda b, i: (b * nq + i, 0)),
        out_shape=jax.ShapeDtypeStruct((bsz * t, GROUP_W), F32),
        scratch_shapes=[pltpu.VMEM((1, 2 * N_HEADS * tq), F32), pltpu.VMEM((1, 2 * N_HEADS * tq), F32),
                        pltpu.VMEM((2, GROUP_W, tq), F32)],
        compiler_params=_cparams(("parallel", "arbitrary")), name="attn_a",
    )(qt, k, vt, a_lambda, g_full, gmean)


def _dsa_kernel(qt_ref, k_ref, vt_ref, qit_ref, ki_ref, kiwit_ref, o_ref, key_ref, y_ref, m_ref, l_ref, acc_ref,
                *, tq, tk, past, s_tot, topk, idx_bits, t_valid):
    q0 = pl.program_id(1) * tq
    n_kv = _num_kv_blocks(q0, tq, tk, past, s_tot)
    row_pos = past + q0 + lax.broadcasted_iota(I32, (1, tq), 1)
    key_iota = lax.broadcasted_iota(I32, (tk, 1), 0)
    int_min = jnp.int32(INT_MIN)

    qit = qit_ref[...]
    blocks = []
    for h in range(N_HEADS):
        qh = qit[h * D_IDX:(h + 1) * D_IDX, :]
        hi, lo = _split_bf16(qh)
        blocks.append(jnp.concatenate([hi, lo, hi], axis=0))
    qi = jnp.concatenate(blocks, axis=1)
    kiwit = kiwit_ref[...]
    w_h = [kiwit[D_IDX + h:D_IDX + h + 1, :] for h in range(N_HEADS)]

    def score_body(c, carry):
        k0 = pl.multiple_of(c * tk, tk)
        kib = ki_ref[pl.ds(k0, tk), :]
        sc = jnp.dot(kib, qi, preferred_element_type=F32)
        isc = jnp.maximum(sc[:, 0:tq], 0.0) * w_h[0]
        for h in range(1, N_HEADS):
            isc = isc + jnp.maximum(sc[:, h * tq:(h + 1) * tq], 0.0) * w_h[h]
        isc = jnp.where(isc == 0.0, 0.0, isc)
        bits = pltpu.bitcast(isc, I32)
        key = bits ^ ((bits >> 31) & jnp.int32(0x7FFFFFFF))
        key_ref[c] = jnp.where((k0 + key_iota) <= row_pos, key, int_min)
        return carry

    lax.fori_loop(0, n_kv, score_body, 0)

    def count(pred):
        def block_count(c):
            ones = jnp.where(pred(key_ref[c], c * tk), 1.0, 0.0)
            return jnp.sum(ones.reshape(tk // 8, 8, tq), axis=0)

        def body(i, acc):
            return acc + block_count(2 * i) + block_count(2 * i + 1)

        acc = lax.fori_loop(0, n_kv // 2, body, jnp.zeros((8, tq), F32))
        odd = (n_kv & 1).astype(F32)
        acc = acc + odd * block_count(n_kv - 1)
        return jnp.sum(acc, axis=0, keepdims=True)

    kk = jnp.float32(topk)

    def bit_body(r, prefix):
        cand_u = prefix | lax.shift_left(jnp.int32(1), 31 - r)
        cand_s = cand_u ^ int_min
        cnt = count(lambda blk, k0: blk >= cand_s)
        return jnp.where(cnt >= kk, cand_u, prefix)

    prefix = lax.fori_loop(0, 32, bit_body, jnp.zeros((1, tq), I32))
    thr = prefix ^ int_min

    n_gt = count(lambda blk, k0: blk > thr)
    n_ge = count(lambda blk, k0: blk >= thr)
    r_need = kk - n_gt
    live = lax.broadcasted_iota(I32, (1, tq), 1) < t_valid
    any_tie = jnp.max(jnp.where(live, n_ge - kk, 0.0)) > 0.0
    y_ref[...] = jnp.full((1, tq), s_tot, I32)

    @pl.when(any_tie)
    def _():
        def ybody(r, y):
            cand = y | lax.shift_left(jnp.int32(1), idx_bits - 1 - r)
            cnt = count(lambda blk, k0: (blk == thr) & ((k0 + key_iota) < cand))
            return jnp.where(cnt < r_need, cand, y)
        y_ref[...] = lax.fori_loop(0, idx_bits, ybody, jnp.zeros((1, tq), I32))

    y_last = y_ref[...]

    def mask_fn(c, k0):
        blk = key_ref[c]
        pos = k0 + key_iota
        sel = (blk > thr) | ((blk == thr) & (pos <= y_last))
        return sel & (pos <= row_pos)

    qt = qt_ref[...]
    row_head = lax.broadcasted_iota(I32, (GROUP_W, 1), 0) // HEAD_W
    qst = jnp.concatenate([jnp.where(row_head == h, qt, 0.0) for h in range(N_HEADS)], axis=1).astype(BF16)
    _attn_core(qst, N_HEADS, k_ref, vt_ref, 0, n_kv, tq, tk, mask_fn, m_ref, l_ref, acc_ref, zero_masked=True)
    o_ref[...] = _normalised(acc_ref[0], l_ref[...], 0, tq).T


def _dsa(qt, k, vt, qit, ki3, kiwit, *, bsz, tq, tk, past, topk, t_valid):
    t = qt.shape[1] // bsz
    s_tot = k.shape[0] // bsz
    nq = t // tq
    idx_bits = int(math.ceil(math.log2(s_tot))) + 1
    kern = functools.partial(_dsa_kernel, tq=tq, tk=tk, past=past, s_tot=s_tot, topk=topk,
                             idx_bits=idx_bits, t_valid=t_valid)
    qspec = lambda w: pl.BlockSpec((w, tq), lambda b, i: (0, b * nq + i))
    return pl.pallas_call(
        kern, grid=(bsz, nq),
        in_specs=[qspec(GROUP_W),
                  pl.BlockSpec((s_tot, GROUP_W), lambda b, i: (b, 0)),
                  pl.BlockSpec((GROUP_W, s_tot), lambda b, i: (0, b)),
                  qspec(GROUP_W),
                  pl.BlockSpec((s_tot, 3 * D_IDX), lambda b, i: (b, 0)),
                  qspec(LANES)],
        out_specs=pl.BlockSpec((tq, GROUP_W), lambda b, i: (b * nq + i, 0)),
        out_shape=jax.ShapeDtypeStruct((bsz * t, GROUP_W), F32),
        scratch_shapes=[pltpu.VMEM((s_tot // tk, tk, tq), I32), pltpu.VMEM((1, tq), I32),
                        pltpu.VMEM((1, N_HEADS * tq), F32), pltpu.VMEM((1, N_HEADS * tq), F32),
                        pltpu.VMEM((1, GROUP_W, tq), F32)],
        compiler_params=_cparams(("parallel", "arbitrary")), name="dsa",
    )(qt, k, vt, qit, ki3, kiwit)


_HALO_C = 8
_HALO_D = 32


def _conv_kernel(*refs, tm, single_tile):
    if single_tile:
        (cb_ref, uc_ref, ud_ref, hc_ref, hd_ref, cw_ref, dw_ref, db_ref, lg_ref, lb_ref,
         oc_ref, od_ref, ec_ref, ed_ref) = refs
        pc_ref = pd_ref = None
    else:
        (cb_ref, uc_ref, ud_ref, pc_ref, pd_ref, hc_ref, hd_ref, cw_ref, dw_ref, db_ref, lg_ref, lb_ref,
         oc_ref, od_ref, ec_ref, ed_ref) = refs
    i = pl.program_id(1)

    @pl.when(i == 0)
    def _():
        ec_ref[0:_HALO_C] = hc_ref[0]
        ed_ref[0:_HALO_D] = hd_ref[0]

    if not single_tile:
        @pl.when(i > 0)
        def _():
            ec_ref[0:_HALO_C] = pc_ref[0]
            ed_ref[0:_HALO_D] = pd_ref[0]

    ec_ref[_HALO_C:_HALO_C + tm] = uc_ref[0]
    ed_ref[_HALO_D:_HALO_D + tm] = ud_ref[0]

    cw = cw_ref[...]
    acc = ec_ref[pl.ds(_HALO_C - (C_WIDTH - 1), tm), :] * cw[0:1]
    for w in range(1, C_WIDTH):
        acc = acc + ec_ref[pl.ds(_HALO_C - (C_WIDTH - 1) + w, tm), :] * cw[w:w + 1]
    oc_ref[0] = cb_ref[0] * acc

    dw = dw_ref[...]
    acc = ed_ref[pl.ds(_HALO_D - (D_WIDTH - 1), tm), :] * dw[0:1]
    for w in range(1, D_WIDTH):
        acc = acc + ed_ref[pl.ds(_HALO_D - (D_WIDTH - 1) + w, tm), :] * dw[w:w + 1]
    z = _layer_norm(acc + db_ref[...], lg_ref[...], lb_ref[...])
    od_ref[0] = z * _sigmoid(z)


def _convs(cb, uc, ud, hist_c, hist_d, c_w, d_w, d_b, ln_g, ln_b, *, tm):
    bsz, t, w = cb.shape
    nt = t // tm
    single = nt == 1
    main = pl.BlockSpec((1, tm, w), lambda b, i: (b, i, 0))
    prev_c = pl.BlockSpec((1, _HALO_C, w), lambda b, i: (b, jnp.maximum(i * (tm // _HALO_C) - 1, 0), 0))
    prev_d = pl.BlockSpec((1, _HALO_D, w), lambda b, i: (b, jnp.maximum(i * (tm // _HALO_D) - 1, 0), 0))
    hist_cs = pl.BlockSpec((1, _HALO_C, w), lambda b, i: (b, 0, 0))
    hist_ds = pl.BlockSpec((1, _HALO_D, w), lambda b, i: (b, 0, 0))
    full2 = lambda a: pl.BlockSpec(a.shape, lambda b, i: (0, 0))
    in_specs = [main, main, main] + ([] if single else [prev_c, prev_d]) + [hist_cs, hist_ds] + \
        [full2(c_w), full2(d_w), full2(d_b), full2(ln_g), full2(ln_b)]
    args = [cb, uc, ud] + ([] if single else [uc, ud]) + [hist_c, hist_d, c_w, d_w, d_b, ln_g, ln_b]
    kern = functools.partial(_conv_kernel, tm=tm, single_tile=single)
    return pl.pallas_call(
        kern, grid=(bsz, nt), in_specs=in_specs,
        out_specs=[main, main],
        out_shape=[jax.ShapeDtypeStruct((bsz, t, w), F32)] * 2,
        scratch_shapes=[pltpu.VMEM((_HALO_C + tm, w), F32), pltpu.VMEM((_HALO_D + tm, w), F32)],
        compiler_params=_cparams(("parallel", "arbitrary")), name="convs",
    )(*args)


def _mix_kernel(x_ref, oa_ref, ob_ref, oc_ref, od_ref, pw_ref, pwb_ref, wo_ref, g_ref, b_ref, h_ref, *, alpha):
    od = jnp.dot(od_ref[...].astype(BF16), pw_ref[...], preferred_element_type=F32) + pwb_ref[...]
    cat = jnp.concatenate([oa_ref[...], ob_ref[...], oc_ref[...], od], axis=1).astype(BF16)
    mix = jnp.dot(cat, wo_ref[...], preferred_element_type=F32)
    h_ref[...] = _layer_norm(alpha * x_ref[...] + mix, g_ref[...], b_ref[...])


def _mix(x, oa, ob, oc, od, pw, pwb, wo, g, b, *, tm, alpha):
    n, d = x.shape
    row = lambda w: pl.BlockSpec((tm, w), lambda i: (i, 0))
    full = lambda a: pl.BlockSpec(a.shape, lambda i: (0, 0))
    return pl.pallas_call(
        functools.partial(_mix_kernel, alpha=alpha), grid=(n // tm,),
        in_specs=[row(d), row(GROUP_W), row(GROUP_W), row(GROUP_W), row(GROUP_W),
                  full(pw), full(pwb), full(wo), full(g), full(b)],
        out_specs=row(d), out_shape=jax.ShapeDtypeStruct((n, d), F32),
        compiler_params=_cparams(("parallel",)), name="mix",
    )(x, oa, ob, oc, od, pw, pwb, wo, g, b)


def _staircase():
    return [(a, b) for a in range(PEER_TOPK) for b in range(PEER_TOPK) if (a + 1) * (b + 1) <= PEER_TOPK]


def _tree_rows(x, op):
    while x.shape[0] > 1:
        half = x.shape[0] // 2
        y = op(x[:half], x[half:2 * half])
        x = y if x.shape[0] % 2 == 0 else jnp.concatenate([y, x[2 * half:]], axis=0)
    return x


def _argmax_rows(x, row):
    v, i = x, row
    while v.shape[0] > 1:
        n = v.shape[0]
        half = n // 2
        pv = v[:2 * half].reshape((half, 2) + v.shape[1:])
        pi = i[:2 * half].reshape((half, 2) + i.shape[1:])
        take_b = pv[:, 1] > pv[:, 0]
        nv = jnp.where(take_b, pv[:, 1], pv[:, 0])
        ni = jnp.where(take_b, pi[:, 1], pi[:, 0])
        if n % 2:
            nv = jnp.concatenate([nv, v[2 * half:]], axis=0)
            ni = jnp.concatenate([ni, i[2 * half:]], axis=0)
        v, i = nv, ni
    return v, i


def _extract_top(x, payload, n_out, n_rows):
    row = lax.broadcasted_iota(I32, x.shape, 0).astype(F32)
    vals, idxs, pays = [], [], []
    for _ in range(n_out):
        m, idx = _argmax_rows(x, row)
        hit = row == idx
        vals.append(m)
        idxs.append(idx)
        if payload is not None:
            pays.append(_tree_rows(jnp.where(hit, payload, -1.0), jnp.maximum))
        x = jnp.where(hit, -jnp.inf, x)
    return vals, idxs, pays


def _peer_select_kernel(h_ref, wq_ref, sk_ref, eid_ref, g_ref, sv_ref, si_ref, *, tm):
    ht = h_ref[...].T.astype(BF16)
    n_groups = 2 * PEER_HEADS
    nt = tm // LANES

    def to_tiles(x):
        return pltpu.einshape("trl->rtl", jnp.stack([x[:, LANES * t:LANES * (t + 1)] for t in range(nt)], axis=0))

    def from_tiles(y):
        z = pltpu.einshape("rtl->trl", y)
        return jnp.concatenate([z[t] for t in range(nt)], axis=1)

    def group_body(g, carry):
        r0 = pl.multiple_of(g * N_KEYS, N_KEYS)
        qg = jnp.dot(wq_ref[pl.ds(r0, N_KEYS), :], ht, preferred_element_type=F32)
        st = jnp.dot(sk_ref[g % 2], qg.astype(BF16), preferred_element_type=F32)
        vals, idxs, _ = _extract_top(to_tiles(st), None, PEER_TOPK, N_KEYS)
        sv_ref[g] = jnp.concatenate(vals, axis=0)
        si_ref[g] = jnp.concatenate(idxs, axis=0)
        return carry

    lax.fori_loop(0, n_groups, group_body, 0)

    pairs = _staircase()

    def head_body(hd, carry):
        sv0 = sv_ref[2 * hd]; sv1 = sv_ref[2 * hd + 1]
        si0 = si_ref[2 * hd]; si1 = si_ref[2 * hd + 1]
        cand = jnp.concatenate([sv0[a:a + 1] + sv1[b:b + 1] for a, b in pairs], axis=0)
        cand_e = jnp.concatenate([si0[a:a + 1] * float(N_KEYS) + si1[b:b + 1] for a, b in pairs], axis=0)
        vals, _, pays = _extract_top(cand, cand_e, PEER_TOPK, len(pairs))
        fv = jnp.concatenate(vals, axis=0)
        e = jnp.exp(fv - fv[0:1])
        gate = e / jnp.sum(e, axis=0, keepdims=True)
        r0 = pl.multiple_of(hd * PEER_TOPK, PEER_TOPK)
        eid_ref[pl.ds(r0, PEER_TOPK), :] = from_tiles(jnp.concatenate(pays, axis=0)).astype(I32)
        g_ref[pl.ds(r0, PEER_TOPK), :] = from_tiles(gate)
        return carry

    lax.fori_loop(0, PEER_HEADS, head_body, 0)


_SELECT_TOKENS = 8 * LANES


def _peer_select(h, wq_t, sk, *, tm):
    n, d = h.shape
    n_slots = PEER_HEADS * PEER_TOPK
    return pl.pallas_call(
        functools.partial(_peer_select_kernel, tm=tm), grid=(n // tm,),
        in_specs=[pl.BlockSpec((tm, d), lambda i: (i, 0)),
                  pl.BlockSpec(wq_t.shape, lambda i: (0, 0)),
                  pl.BlockSpec(sk.shape, lambda i: (0, 0, 0))],
        out_specs=[pl.BlockSpec((n_slots, tm), lambda i: (0, i)),
                   pl.BlockSpec((n_slots, tm), lambda i: (0, i))],
        out_shape=[jax.ShapeDtypeStruct((n_slots, n), I32), jax.ShapeDtypeStruct((n_slots, n), F32)],
        scratch_shapes=[pltpu.VMEM((2 * PEER_HEADS, PEER_TOPK, tm // LANES, LANES), F32),
                        pltpu.VMEM((2 * PEER_HEADS, PEER_TOPK, tm // LANES, LANES), F32)],
        compiler_params=_cparams(("parallel",)), name="peer_select",
    )(h, wq_t, sk)


_GATE_TOKENS = 16


def _peer_gates_kernel(eid_ref, g_ref, w_ref, *, tm):
    sub = lax.broadcasted_iota(I32, (N_KEYS, PEER_HEADS * PEER_TOPK), 0)

    def token_gates(t):
        e = eid_ref[pl.ds(t, 1), :]
        g = g_ref[pl.ds(t, 1), :]
        oh_i = jnp.where((e >> 7) == sub, 1.0, 0.0).astype(BF16)
        g_j = jnp.where((e & (N_KEYS - 1)) == sub, g, 0.0).astype(BF16)
        return lax.dot_general(oh_i, g_j, _NT, preferred_element_type=F32)

    def body(b, carry):
        t0 = pl.multiple_of(b * _GATE_TOKENS, _GATE_TOKENS)
        w = jnp.stack([token_gates(t0 + k) for k in range(_GATE_TOKENS)], axis=0)
        w_ref[:, pl.ds(t0, _GATE_TOKENS), :] = pltpu.einshape("tij->itj", w).astype(BF16)
        return carry

    lax.fori_loop(0, tm // _GATE_TOKENS, body, 0)


def _peer_gates(eid, gate, *, tm):
    n, n_slots = eid.shape
    return pl.pallas_call(
        functools.partial(_peer_gates_kernel, tm=tm), grid=(n // tm,),
        in_specs=[pl.BlockSpec((tm, n_slots), lambda i: (i, 0)),
                  pl.BlockSpec((tm, n_slots), lambda i: (i, 0))],
        out_specs=pl.BlockSpec((N_KEYS, tm, N_KEYS), lambda i: (0, i, 0)),
        out_shape=jax.ShapeDtypeStruct((N_KEYS, n, N_KEYS), BF16),
        compiler_params=_cparams(("parallel",)), name="peer_gates",
    )(eid, gate)


def _gelu(x):
    return 0.5 * x * (1.0 + lax.erf(x * (2.0 ** -0.5)))


def _peer_dense_kernel(h_ref, w_ref, u_ref, v_ref, f_ref):
    @pl.when(pl.program_id(1) == 0)
    def _():
        f_ref[...] = jnp.zeros(f_ref.shape, F32)

    act = _gelu(lax.dot_general(h_ref[...], u_ref[...], _NT, preferred_element_type=F32))
    y = jnp.concatenate(
        [w_ref[i].astype(F32) * act[:, i * N_KEYS:(i + 1) * N_KEYS] for i in range(w_ref.shape[0])],
        axis=1).astype(BF16)
    f_ref[...] += jnp.dot(y, v_ref[...], preferred_element_type=F32)


def _peer_dense(hb, wmat, u, v, *, tm, te):
    n, d = hb.shape
    n_exp = u.shape[0]
    return pl.pallas_call(
        _peer_dense_kernel, grid=(n // tm, n_exp // te),
        in_specs=[pl.BlockSpec((tm, d), lambda i, e: (i, 0)),
                  pl.BlockSpec((te // N_KEYS, tm, N_KEYS), lambda i, e: (e, i, 0)),
                  pl.BlockSpec((te, d), lambda i, e: (e, 0)),
                  pl.BlockSpec((te, d), lambda i, e: (e, 0))],
        out_specs=pl.BlockSpec((tm, d), lambda i, e: (i, 0)),
        out_shape=jax.ShapeDtypeStruct((n, d), F32),
        compiler_params=_cparams(("parallel", "arbitrary")), name="peer_dense",
    )(hb, wmat, u, v)


def _final_kernel(h_ref, f_ref, p_ref, g_ref, b_ref, wg_ref, bg_ref, wp_ref, y_ref, *, alpha):
    h2 = _layer_norm(alpha * h_ref[...] + f_ref[...], g_ref[...], b_ref[...])
    gate = _sigmoid(jnp.dot(h2.astype(BF16), wg_ref[...], preferred_element_type=F32) + bg_ref[...])
    emb = jnp.dot(p_ref[...].astype(BF16), wp_ref[...], preferred_element_type=F32)
    y_ref[...] = h2 + gate * emb


def _final(h, f, p, g, b, wg, bg, wp, *, tm, alpha):
    n, d = h.shape
    row = lambda w: pl.BlockSpec((tm, w), lambda i: (i, 0))
    full = lambda a: pl.BlockSpec(a.shape, lambda i: (0, 0))
    return pl.pallas_call(
        functools.partial(_final_kernel, alpha=alpha), grid=(n // tm,),
        in_specs=[row(d), row(d), row(p.shape[1]), full(g), full(b), full(wg), full(bg), full(wp)],
        out_specs=row(d), out_shape=jax.ShapeDtypeStruct((n, d), F32),
        compiler_params=_cparams(("parallel",)), name="final",
    )(h, f, p, g, b, wg, bg, wp)


def _pack_w_in(w_in):
    d = w_in.shape[0]
    cols = [w_in[:, 0:1792], w_in[:, 1792:1860], jnp.zeros((d, 60), w_in.dtype), w_in[:, 1860:3140]]
    nat = jnp.concatenate(cols, axis=1).astype(BF16)
    seg = lambda name, width=GROUP_W: nat[:, _SEG[name]:_SEG[name] + width]
    tr = jnp.concatenate([seg("qa"), seg("qb"), seg("qi"), seg("kiwi", 128), seg("va"), seg("vb")], axis=1).T
    return nat, tr


def _row(a):
    return a.reshape(1, -1)


def _tile_rows(n, pref):
    t = pref
    while n % t:
        t //= 2
    return t


def _prep_weights(lw, depth):
    (w_in, a_lambda, a_subln_g, c_conv_w, d_conv_w, d_conv_b, d_ln_g, d_ln_b, d_pw_w, d_pw_b,
     w_out, ln1_g, ln1_b, peer_wq, peer_subkeys, peer_u, peer_v, ln2_g, ln2_b,
     ple_w, ple_gate_w, ple_gate_b) = lw
    head_of_lane = jnp.arange(GROUP_W) // HEAD_W
    gmean = (head_of_lane[:, None] == head_of_lane[None, :]).astype(BF16) * (1.0 / HEAD_W)
    w_nat, w_tr = _pack_w_in(w_in)
    return dict(
        w_in=w_nat, w_in_t=w_tr, a_lambda=a_lambda, a_g=_row(jnp.tile(a_subln_g, N_HEADS)),
        gmean=gmean.astype(BF16),
        c_w=c_conv_w, d_w=d_conv_w, d_b=_row(d_conv_b), d_ln_g=_row(d_ln_g), d_ln_b=_row(d_ln_b),
        d_pw=d_pw_w.astype(BF16), d_pwb=_row(d_pw_b), w_out=w_out.astype(BF16),
        ln1_g=_row(ln1_g), ln1_b=_row(ln1_b),
        wq_t=peer_wq.T.astype(BF16), sk=peer_subkeys.astype(BF16),
        u=peer_u.astype(BF16), v=peer_v.astype(BF16),
        ln2_g=_row(ln2_g), ln2_b=_row(ln2_b),
        ple_w=ple_w.astype(BF16), wg=ple_gate_w.astype(BF16), bg=_row(ple_gate_b),
        alpha=float((2 * depth) ** 0.25),
    )


def _index_keys(ki):
    k_hi = ki.astype(BF16)
    k_lo = (ki - k_hi.astype(F32)).astype(BF16)
    return jnp.concatenate([k_hi, k_hi, k_lo], axis=-1)


def _token_tail(w, x, oa, ob, oc, od, valid_rows, p):
    n = x.shape[0]
    h = _mix(x, oa, ob, oc, od, w["d_pw"], w["d_pwb"], w["w_out"], w["ln1_g"], w["ln1_b"],
             tm=_tile_rows(n, 512), alpha=w["alpha"])
    if valid_rows is not None:
        bsz, t_pad, t = valid_rows
        h = h.reshape(bsz, t_pad, -1)[:, :t].reshape(bsz * t, -1)
    n = h.shape[0]
    n_sel = -(-n // _SELECT_TOKENS) * _SELECT_TOKENS
    h_sel = h if n_sel == n else jnp.pad(h, ((0, n_sel - n), (0, 0)))
    eid_t, gate_t = _peer_select(h_sel, w["wq_t"], w["sk"], tm=_SELECT_TOKENS)
    n_gate = -(-n // _GATE_TOKENS) * _GATE_TOKENS
    wmat = _peer_gates(eid_t[:, :n_gate].T, gate_t[:, :n_gate].T, tm=_tile_rows(n_gate, 128))
    f = _peer_dense(h.astype(BF16), wmat, w["u"], w["v"], tm=_tile_rows(n, 1024), te=8 * N_KEYS)
    return _final(h, f, p, w["ln2_g"], w["ln2_b"], w["wg"], w["bg"], w["ple_w"],
                  tm=_tile_rows(n, 512), alpha=w["alpha"])


def _layer(layer_idx, depth, xp, xs, caches, state_c, state_d, page_table, pp, ps, lw):
    w = _prep_weights(lw, depth)
    bp, tp, d = xp.shape
    bs, ts, _ = xs.shape
    ts_pad = 8
    n_pages = page_table.shape[1]
    page = caches[0].shape[3]
    past = n_pages * page
    lam_init = 0.8 - 0.6 * math.exp(-0.3 * (layer_idx + 1))

    pr = _project(xp.reshape(bp * tp, d), w["w_in"], w["w_in_t"], _tile_rows(bp * tp, 512), feature_major=True)
    tq = _tile_rows(tp, 256)
    tk = _tile_rows(tp, 256)
    oa = _attention_a(pr["qat"], pr["kab"], pr["vat"], w["a_lambda"], w["a_g"], w["gmean"],
                      bsz=bp, tq=tq, tk=tk, past=0, lam_init=lam_init)
    ob = _dsa(pr["qbt"], pr["kbb"], pr["vbt"], pr["qit"], _index_keys(pr["kiwi"][:, :D_IDX]), pr["kiwit"],
              bsz=bp, tq=tq, tk=tk, past=0, topk=min(DSA_TOPK, tp // 4), t_valid=tq)
    r3p = lambda a: a.reshape(bp, tp, a.shape[-1])
    zc = jnp.zeros((bp, _HALO_C, GROUP_W), F32)
    zd = jnp.zeros((bp, _HALO_D, GROUP_W), F32)
    oc, od = _convs(r3p(pr["cb"]), r3p(pr["uc"]), r3p(pr["ud"]), zc, zd, w["c_w"], w["d_w"], w["d_b"],
                    w["d_ln_g"], w["d_ln_b"], tm=_tile_rows(tp, 512))
    yp = _token_tail(w, xp.reshape(bp * tp, d), oa, ob, oc.reshape(bp * tp, -1), od.reshape(bp * tp, -1),
                     None, pp.reshape(bp * tp, -1)).reshape(bp, tp, d)
    new_p = (r3p(pr["ka"]), r3p(pr["va"]), r3p(pr["kb"]), r3p(pr["vb"]), r3p(pr["kiwi"])[:, :, :D_IDX],
             jnp.concatenate([zc, r3p(pr["uc"])], axis=1)[:, -(C_WIDTH - 1):],
             jnp.concatenate([zd, r3p(pr["ud"])], axis=1)[:, -(D_WIDTH - 1):])

    xs_pad = jnp.pad(xs, ((0, 0), (0, ts_pad - ts), (0, 0))).reshape(bs * ts_pad, d)
    sr = _project(xs_pad, w["w_in"], w["w_in_t"], _tile_rows(bs * ts_pad, 512), feature_major=False)
    r3s = lambda a: a.reshape(bs, ts_pad, a.shape[-1])
    news = [r3s(sr["ka"]), r3s(sr["va"]), r3s(sr["kb"]), r3s(sr["vb"]), r3s(sr["kiwi"])[:, :, :D_IDX]]
    oa_s, ob_s = _sample_attention(page_table, caches, layer_idx, news, r3s(sr["qa"]), r3s(sr["qb"]), r3s(sr["qi"]),
                                   r3s(sr["kiwi"]), w["a_lambda"], w["a_g"], w["gmean"],
                                   topk=min(DSA_TOPK, (past + ts) // 4), lam_init=lam_init, t_valid=ts)
    unpad = lambda a: a.reshape(bs * ts_pad, GROUP_W)
    hist_c = jnp.concatenate([jnp.zeros((bs, _HALO_C - (C_WIDTH - 1), GROUP_W), F32), state_c], axis=1)
    hist_d = jnp.concatenate([jnp.zeros((bs, _HALO_D - (D_WIDTH - 1), GROUP_W), F32), state_d], axis=1)
    oc_s, od_s = _convs(r3s(sr["cb"]), r3s(sr["uc"]), r3s(sr["ud"]), hist_c, hist_d, w["c_w"], w["d_w"],
                        w["d_b"], w["d_ln_g"], w["d_ln_b"], tm=ts_pad)
    ys = _token_tail(w, xs_pad, unpad(oa_s), unpad(ob_s), oc_s.reshape(bs * ts_pad, -1),
                     od_s.reshape(bs * ts_pad, -1), (bs, ts_pad, ts), ps.reshape(bs * ts, -1)).reshape(bs, ts, d)
    cut = lambda a: r3s(a)[:, :ts]
    new_s = (cut(sr["ka"]), cut(sr["va"]), cut(sr["kb"]), cut(sr["vb"]), cut(sr["kiwi"])[:, :, :D_IDX],
             jnp.concatenate([state_c, cut(sr["uc"])], axis=1)[:, -(C_WIDTH - 1):],
             jnp.concatenate([state_d, cut(sr["ud"])], axis=1)[:, -(D_WIDTH - 1):])
    return yp, ys, new_p, new_s


def kernel(x_prompt, x_sample, cache_a_k, cache_a_v, cache_b_k, cache_b_v, cache_b_kidx, state_c_conv, state_d_conv, page_table, p_prompt, p_sample, w_in, a_lambda, a_subln_g, c_conv_w, d_conv_w, d_conv_b, d_ln_g, d_ln_b, d_pw_w, d_pw_b, w_out, ln1_g, ln1_b, peer_wq, peer_subkeys, peer_u, peer_v, ln2_g, ln2_b, ple_w, ple_gate_w, ple_gate_b):
    depth = w_in.shape[0]
    n_pool, page = cache_a_k.shape[1], cache_a_k.shape[2]
    feature_major = lambda c: jnp.moveaxis(c, 2, -1).reshape(depth, n_pool, -1, page)
    caches = [feature_major(c) for c in (cache_a_k, cache_a_v, cache_b_k, cache_b_v, cache_b_kidx)]
    xp, xs = x_prompt, x_sample
    news_p, news_s = [], []
    for l in range(depth):
        lw = (w_in[l], a_lambda[l], a_subln_g[l], c_conv_w[l], d_conv_w[l], d_conv_b[l], d_ln_g[l], d_ln_b[l],
              d_pw_w[l], d_pw_b[l], w_out[l], ln1_g[l], ln1_b[l], peer_wq[l], peer_subkeys[l], peer_u[l],
              peer_v[l], ln2_g[l], ln2_b[l], ple_w[l], ple_gate_w[l], ple_gate_b[l])
        xp, xs, new_p, new_s = _layer(l, depth, xp, xs, caches, state_c_conv[l], state_d_conv[l],
                                      page_table, p_prompt[l], p_sample[l], lw)
        news_p.append(new_p)
        news_s.append(new_s)

    bp, tp, _ = x_prompt.shape
    bs, ts, _ = x_sample.shape
    shapes = [(N_HEADS, 2, 32), (N_HEADS, HEAD_W), (N_HEADS, HEAD_W), (N_HEADS, HEAD_W), (D_IDX,)]

    def stack(news, i, lead):
        a = jnp.stack([n[i] for n in news], axis=0)
        return a.reshape((depth,) + lead + shapes[i]) if i < 5 else a

    outs = [xp, xs]
    for i in range(7):
        outs.append(stack(news_p, i, (bp, tp)))
        outs.append(stack(news_s, i, (bs, ts)))
    return tuple(outs)
```

```python
import functools
import math

import jax
import jax.numpy as jnp
from jax import lax
from jax.experimental import pallas as pl
from jax.experimental.pallas import tpu as pltpu

F32 = jnp.float32
BF16 = jnp.bfloat16
I32 = jnp.int32

LN_EPS = 1e-5
N_HEADS = 4
HEAD_W = 64
GROUP_W = 256
D_IDX = 64
DSA_TOPK = 256
C_WIDTH = 3
D_WIDTH = 31
PEER_HEADS = 8
PEER_TOPK = 16
N_KEYS = 128
LANES = 128
NEG_BIG = -1e30
INT_MIN = -2 ** 31
VMEM_LIMIT = 56 * 1024 * 1024

_SEG = dict(qa=0, ka=256, va=512, qb=768, kb=1024, vb=1280, qi=1536, kiwi=1792,
            cb=1920, cc=2176, cx=2432, da=2688, dg=2944)
_N_PACKED = 3200
_SEG_T = dict(qa=0, qb=256, qi=512, kiwi=768, va=896, vb=1152)
_N_PACKED_T = 1408

_NT = (((1,), (1,)), ((), ()))


def _cparams(sem):
    return pltpu.CompilerParams(dimension_semantics=sem, vmem_limit_bytes=VMEM_LIMIT)


def _split_bf16(x):
    hi = x.astype(BF16)
    lo = (x - hi.astype(F32)).astype(BF16)
    return hi, lo


def _layer_norm(x, g, b):
    mu = jnp.mean(x, axis=-1, keepdims=True)
    xc = x - mu
    var = jnp.mean(xc * xc, axis=-1, keepdims=True)
    return xc * lax.rsqrt(var + LN_EPS) * g + b


def _sigmoid(x):
    return 1.0 / (1.0 + jnp.exp(-x))


_PROJ_COMMON = ("ka", "va", "kb", "vb", "kiwi", "cb", "uc", "ud")
_PROJ_FEATURE_MAJOR = ("kab", "kbb", "qat", "qbt", "qit", "kiwit", "vat", "vbt")
_PROJ_ROW_MAJOR = ("qa", "qb", "qi")


def _proj_kernel(x_ref, w_ref, wt_ref, *out_refs, names, scale_a, scale_b):
    o = dict(zip(names, out_refs))
    xb = x_ref[...].astype(BF16)

    def seg(name, width=GROUP_W):
        off = _SEG[name]
        return jnp.dot(xb, w_ref[:, off:off + width], preferred_element_type=F32)

    def seg_t(name, width=GROUP_W):
        off = _SEG_T[name]
        return lax.dot_general(wt_ref[off:off + width, :], xb, _NT, preferred_element_type=F32)

    ka = seg("ka"); kb = seg("kb")
    o["ka"][...] = ka; o["kb"][...] = kb
    o["va"][...] = seg("va"); o["vb"][...] = seg("vb")
    o["kiwi"][...] = seg("kiwi", LANES)
    o["cb"][...] = seg("cb")
    o["uc"][...] = seg("cc") * seg("cx")
    o["ud"][...] = seg("da") * _sigmoid(seg("dg"))
    if "qat" in o:
        o["kab"][...] = ka.astype(BF16); o["kbb"][...] = kb.astype(BF16)
        o["qat"][...] = seg_t("qa") * scale_a
        o["qbt"][...] = seg_t("qb") * scale_b
        o["qit"][...] = seg_t("qi")
        o["kiwit"][...] = seg_t("kiwi", LANES)
        o["vat"][...] = seg_t("va").astype(BF16)
        o["vbt"][...] = seg_t("vb").astype(BF16)
    else:
        o["qa"][...] = seg("qa") * scale_a
        o["qb"][...] = seg("qb") * scale_b
        o["qi"][...] = seg("qi")


def _project(x, w_packed, wt_packed, tm, feature_major):
    n, d = x.shape
    nat = lambda w, dt=F32: (jax.ShapeDtypeStruct((n, w), dt), pl.BlockSpec((tm, w), lambda i: (i, 0)))
    tr = lambda w, dt=F32: (jax.ShapeDtypeStruct((w, n), dt), pl.BlockSpec((w, tm), lambda i: (0, i)))
    outs = [nat(256), nat(256), nat(256), nat(256), nat(LANES), nat(256), nat(256), nat(256)]
    if feature_major:
        names = _PROJ_COMMON + _PROJ_FEATURE_MAJOR
        outs += [nat(256, BF16), nat(256, BF16), tr(256), tr(256), tr(256), tr(LANES), tr(256, BF16), tr(256, BF16)]
    else:
        names = _PROJ_COMMON + _PROJ_ROW_MAJOR
        outs += [nat(256), nat(256), nat(256)]
    kern = functools.partial(_proj_kernel, names=names, scale_a=float(32 ** -0.5), scale_b=float(HEAD_W ** -0.5))
    res = pl.pallas_call(
        kern, grid=(n // tm,),
        in_specs=[pl.BlockSpec((tm, d), lambda i: (i, 0)),
                  pl.BlockSpec((d, _N_PACKED), lambda i: (0, 0)),
                  pl.BlockSpec((_N_PACKED_T, d), lambda i: (0, 0))],
        out_specs=[o[1] for o in outs], out_shape=[o[0] for o in outs],
        compiler_params=_cparams(("parallel",)), name="proj",
    )(x, w_packed, wt_packed)
    return dict(zip(names, res))


def _lane_heads(o, l, base, rows, lane_head):
    out = None
    for h in range(N_HEADS):
        r0 = (base + h) * rows
        part = jnp.where(lane_head == h, o[r0:r0 + rows] / l[r0:r0 + rows], 0.0)
        out = part if out is None else out + part
    return out


def _sample_attn_kernel(pt_ref, *refs, n_pages, page, rows, past, topk, idx_bits, lam_init, t_valid):
    npg = n_pages
    ak, av, bk, bv, kic = (refs[i * npg:(i + 1) * npg] for i in range(5))
    (nak, nav, nbk, nbv, nki, qa_ref, qb_ref, qi_ref, kiwi_ref, lam_ref, g_ref, gm_ref,
     oa_ref, ob_ref, y_ref) = refs[5 * npg:]
    nblk = npg + 1
    s_tot = nblk * page
    lane_head = lax.broadcasted_iota(I32, (1, GROUP_W), 1) // HEAD_W
    col_last = past + lax.broadcasted_iota(I32, (1, page), 1)
    t_row = lax.broadcasted_iota(I32, (rows, 1), 0)
    int_min = jnp.int32(INT_MIN)

    def blocks(page_refs, new_ref, dtype):
        new = new_ref[0]
        pad = jnp.zeros((page - rows, new.shape[-1]), F32)
        return [r[0, 0].astype(dtype) for r in page_refs] + [jnp.concatenate([new, pad], axis=0).T.astype(dtype)]

    def softmax_pv(q_stack, n_groups, k_blocks, v_blocks, masks):
        def masked(s, mk, fill):
            if mk is None:
                return s
            s3 = jnp.where(mk[None], s.reshape(n_groups, rows, page), fill)
            return s3.reshape(n_groups * rows, page)
        s = [masked(jnp.dot(q_stack, kb, preferred_element_type=F32), mk, NEG_BIG)
             for kb, mk in zip(k_blocks, masks)]
        m = s[0].max(axis=1, keepdims=True)
        for sj in s[1:]:
            m = jnp.maximum(m, sj.max(axis=1, keepdims=True))
        l = None
        o = None
        for sj, vb, mk in zip(s, v_blocks, masks):
            p = masked(jnp.exp(sj - m), mk, 0.0)
            lj = p.sum(axis=1, keepdims=True)
            oj = lax.dot_general(p.astype(BF16), vb, _NT, preferred_element_type=F32)
            l = lj if l is None else l + lj
            o = oj if o is None else o + oj
        return o, l

    causal_last = col_last <= (past + t_row)

    qa = qa_ref[0]
    lane_grp = lax.broadcasted_iota(I32, (1, GROUP_W), 1) // 32
    qa_stack = jnp.concatenate(
        [jnp.where(lane_grp == (2 * h + c), qa, 0.0) for c in range(2) for h in range(N_HEADS)],
        axis=0).astype(BF16)
    o, l = softmax_pv(qa_stack, 2 * N_HEADS, blocks(ak, nak, BF16), blocks(av, nav, BF16),
                      [None] * npg + [causal_last])
    o0 = _lane_heads(o, l, 0, rows, lane_head)
    o1 = _lane_heads(o, l, N_HEADS, rows, lane_head)
    lv = lam_ref[...]
    lam = (jnp.exp(jnp.sum(lv[0:1] * lv[1:2], axis=1, keepdims=True))
           - jnp.exp(jnp.sum(lv[2:3] * lv[3:4], axis=1, keepdims=True)) + lam_init)
    od = o0 - lam * o1
    hi, lo = _split_bf16(od * od)
    ms = (jnp.dot(hi, gm_ref[...], preferred_element_type=F32)
          + jnp.dot(lo, gm_ref[...], preferred_element_type=F32))
    oa_ref[0] = od * lax.rsqrt(ms + LN_EPS) * g_ref[...] * (1.0 - lam_init)

    qi = qi_ref[0]
    qi_rows = jnp.concatenate([qi[:, h * D_IDX:(h + 1) * D_IDX] for h in range(N_HEADS)], axis=0)
    q_hi, q_lo = _split_bf16(qi_rows)
    kiwi = kiwi_ref[0]
    w_h = [kiwi[:, D_IDX + h:D_IDX + h + 1] for h in range(N_HEADS)]
    key_blocks = []
    for j, kib in enumerate(blocks(kic, nki, F32)):
        k_hi, k_lo = _split_bf16(kib)
        sc = (jnp.dot(q_hi, k_hi, preferred_element_type=F32)
              + jnp.dot(q_lo, k_hi, preferred_element_type=F32)
              + jnp.dot(q_hi, k_lo, preferred_element_type=F32))
        isc = jnp.maximum(sc[0:rows], 0.0) * w_h[0]
        for h in range(1, N_HEADS):
            isc = isc + jnp.maximum(sc[h * rows:(h + 1) * rows], 0.0) * w_h[h]
        isc = jnp.where(isc == 0.0, 0.0, isc)
        bits = pltpu.bitcast(isc, I32)
        key = bits ^ ((bits >> 31) & jnp.int32(0x7FFFFFFF))
        if j == npg:
            key = jnp.where(causal_last, key, int_min)
        key_blocks.append(key)
    keys = jnp.concatenate(key_blocks, axis=1)
    cols = lax.broadcasted_iota(I32, (1, s_tot), 1)
    kk = jnp.float32(topk)
    count = lambda pred: jnp.sum(jnp.where(pred, 1.0, 0.0), axis=1, keepdims=True)

    def bit_body(r, prefix):
        b1 = lax.shift_left(jnp.int32(1), 31 - 2 * r)
        b0 = lax.shift_left(jnp.int32(1), 30 - 2 * r)
        enough = lambda cand_u: count(keys >= (cand_u ^ int_min)) >= kk
        c10 = prefix | b1
        c11 = c10 | b0
        c01 = prefix | b0
        return jnp.where(enough(c10), jnp.where(enough(c11), c11, c10), jnp.where(enough(c01), c01, prefix))

    thr = lax.fori_loop(0, 16, bit_body, jnp.zeros((rows, 1), I32)) ^ int_min
    r_need = kk - count(keys > thr)
    any_tie = jnp.max(jnp.where(t_row < t_valid, count(keys >= thr) - kk, 0.0)) > 0.0
    y_ref[...] = jnp.full((rows, 1), s_tot, I32)

    @pl.when(any_tie)
    def _():
        def ybody(r, y):
            cand = y | lax.shift_left(jnp.int32(1), idx_bits - 1 - r)
            return jnp.where(count((keys == thr) & (cols < cand)) < r_need, cand, y)
        y_ref[...] = lax.fori_loop(0, idx_bits, ybody, jnp.zeros((rows, 1), I32))

    sel = ((keys > thr) | ((keys == thr) & (cols <= y_ref[...]))) & (cols <= (past + t_row))
    qb = qb_ref[0]
    qb_stack = jnp.concatenate([jnp.where(lane_head == h, qb, 0.0) for h in range(N_HEADS)], axis=0).astype(BF16)
    o, l = softmax_pv(qb_stack, N_HEADS, blocks(bk, nbk, BF16), blocks(bv, nbv, BF16),
                      [sel[:, j * page:(j + 1) * page] for j in range(nblk)])
    ob_ref[0] = _lane_heads(o, l, 0, rows, lane_head)


def _sample_attention(page_table, caches, layer, news, qa, qb, qi, kiwi, a_lambda, g_full, gmean,
                      *, topk, lam_init, t_valid):
    bsz, n_pages = page_table.shape
    page = caches[0].shape[3]
    rows = qa.shape[1]
    past = n_pages * page
    idx_bits = int(math.ceil(math.log2((n_pages + 1) * page))) + 1
    page_spec = lambda c, j: pl.BlockSpec((1, 1, c, page), lambda b, pt: (layer, pt[b, j], 0, 0))
    seq_spec = lambda c: pl.BlockSpec((1, rows, c), lambda b, pt: (b, 0, 0))
    full2 = lambda a: pl.BlockSpec(a.shape, lambda b, pt: (0, 0))
    in_specs, args = [], []
    for cache in caches:
        for j in range(n_pages):
            in_specs.append(page_spec(cache.shape[2], j))
            args.append(cache)
    for a in list(news) + [qa, qb, qi, kiwi]:
        in_specs.append(seq_spec(a.shape[-1]))
        args.append(a)
    for a in (a_lambda, g_full, gmean):
        in_specs.append(full2(a))
        args.append(a)
    kern = functools.partial(_sample_attn_kernel, n_pages=n_pages, page=page, rows=rows, past=past, topk=topk,
                             idx_bits=idx_bits, lam_init=lam_init, t_valid=t_valid)
    return pl.pallas_call(
        kern,
        grid_spec=pltpu.PrefetchScalarGridSpec(
            num_scalar_prefetch=1, grid=(bsz,), in_specs=in_specs,
            out_specs=[seq_spec(GROUP_W), seq_spec(GROUP_W)],
            scratch_shapes=[pltpu.VMEM((rows, 1), I32)]),
        out_shape=[jax.ShapeDtypeStruct((bsz, rows, GROUP_W), F32)] * 2,
        compiler_params=_cparams(("parallel",)), name="sample_attn",
    )(page_table, *args)


def _attn_core(qst, n_groups, k_ref, vt_ref, n_full, n_kv, tq, tk, mask_fn, m_ref, l_ref, acc_ref, zero_masked):
    n_comp = n_groups // N_HEADS
    row_head = lax.broadcasted_iota(I32, (GROUP_W, 1), 0) // HEAD_W
    m_ref[...] = jnp.full(m_ref.shape, NEG_BIG, F32)
    l_ref[...] = jnp.zeros(l_ref.shape, F32)
    acc_ref[...] = jnp.zeros(acc_ref.shape, F32)
    grp = lambda a, g: a[:, g * tq:(g + 1) * tq]

    def step(c, n_blk, masked):
        width = n_blk * tk
        k0 = pl.multiple_of(c * tk, tk)
        kblk = k_ref[pl.ds(k0, width), :]
        vt = vt_ref[:, pl.ds(k0, width)]
        masks = [mask_fn(c + i, k0 + i * tk) for i in range(n_blk)] if masked else None

        def apply_masks(a, fill):
            parts = [jnp.where(masks[i], a[i * tk:(i + 1) * tk], fill) for i in range(n_blk)]
            return parts[0] if n_blk == 1 else jnp.concatenate(parts, axis=0)

        m_old = m_ref[...]
        l_old = l_ref[...]
        m_new, l_new, alpha, pb = [], [], [], []
        for g in range(n_groups):
            s = jnp.dot(kblk, grp(qst, g), preferred_element_type=F32)
            if masked:
                s = apply_masks(s, NEG_BIG)
            m_g = jnp.maximum(grp(m_old, g), jnp.max(s, axis=0, keepdims=True))
            a_g = jnp.exp(grp(m_old, g) - m_g)
            p = jnp.exp(s - m_g)
            if masked and zero_masked:
                p = apply_masks(p, 0.0)
            m_new.append(m_g)
            alpha.append(a_g)
            l_new.append(a_g * grp(l_old, g) + jnp.sum(p, axis=0, keepdims=True))
            pb.append(p.astype(BF16))
        m_ref[...] = jnp.concatenate(m_new, axis=1)
        l_ref[...] = jnp.concatenate(l_new, axis=1)
        for comp in range(n_comp):
            for h in range(N_HEADS):
                g = comp * N_HEADS + h
                rows = slice(h * HEAD_W, (h + 1) * HEAD_W)
                pv = jnp.dot(vt[rows], pb[g], preferred_element_type=F32)
                acc_ref[comp, rows, :] = acc_ref[comp, rows, :] * alpha[g] + pv

    def run(lo, hi, masked):
        def pair_body(i, carry):
            step(lo + 2 * i, 2, masked)
            return carry
        n = hi - lo
        if isinstance(n, int) and n == 0:
            return
        lax.fori_loop(0, n // 2, pair_body, 0)

        @pl.when(n % 2 == 1)
        def _():
            step(hi - 1, 1, masked)

    run(0, n_full, False)
    run(n_full, n_kv, True)


def _normalised(acc, l_row, base, tq):
    rl = 1.0 / l_row
    rf = jnp.concatenate(
        [jnp.broadcast_to(rl[:, (base + h) * tq:(base + h + 1) * tq], (HEAD_W, tq)) for h in range(N_HEADS)],
        axis=0)
    return acc * rf


def _num_kv_blocks(q0, tq, tk, past, s_tot):
    return jnp.minimum((past + q0 + tq + tk - 1) // tk, s_tot // tk)


def _attn_a_kernel(qt_ref, k_ref, vt_ref, lam_ref, g_ref, gm_ref, o_ref, m_ref, l_ref, acc_ref,
                   *, tq, tk, past, s_tot, lam_init):
    q0 = pl.program_id(1) * tq
    qt = qt_ref[...]
    row_grp = lax.broadcasted_iota(I32, (GROUP_W, 1), 0) // 32
    qst = jnp.concatenate(
        [jnp.where(row_grp == (2 * h + c), qt, 0.0) for c in range(2) for h in range(N_HEADS)],
        axis=1).astype(BF16)
    n_kv = _num_kv_blocks(q0, tq, tk, past, s_tot)
    n_full = jnp.minimum((past + q0 + 1) // tk, n_kv)
    row_pos = past + q0 + lax.broadcasted_iota(I32, (1, tq), 1)
    key_iota = lax.broadcasted_iota(I32, (tk, 1), 0)

    def mask_fn(c, k0):
        return (k0 + key_iota) <= row_pos

    _attn_core(qst, 2 * N_HEADS, k_ref, vt_ref, n_full, n_kv, tq, tk, mask_fn,
               m_ref, l_ref, acc_ref, zero_masked=False)
    l_row = l_ref[...]
    o0 = _normalised(acc_ref[0], l_row, 0, tq)
    o1 = _normalised(acc_ref[1], l_row, N_HEADS, tq)
    lv = lam_ref[...]
    lam = (jnp.exp(jnp.sum(lv[0:1] * lv[1:2], axis=1, keepdims=True))
           - jnp.exp(jnp.sum(lv[2:3] * lv[3:4], axis=1, keepdims=True)) + lam_init)
    o = (o0 - lam * o1).T
    hi, lo = _split_bf16(o * o)
    ms = (jnp.dot(hi, gm_ref[...], preferred_element_type=F32)
          + jnp.dot(lo, gm_ref[...], preferred_element_type=F32))
    o_ref[...] = o * lax.rsqrt(ms + LN_EPS) * g_ref[...] * (1.0 - lam_init)


def _attention_a(qt, k, vt, a_lambda, g_full, gmean, *, bsz, tq, tk, past, lam_init):
    t = qt.shape[1] // bsz
    s_tot = k.shape[0] // bsz
    nq = t // tq
    kern = functools.partial(_attn_a_kernel, tq=tq, tk=tk, past=past, s_tot=s_tot, lam_init=lam_init)
    return pl.pallas_call(
        kern, grid=(bsz, nq),
        in_specs=[pl.BlockSpec((GROUP_W, tq), lambda b, i: (0, b * nq + i)),
                  pl.BlockSpec((s_tot, GROUP_W), lambda b, i: (b, 0)),
                  pl.BlockSpec((GROUP_W, s_tot), lambda b, i: (0, b)),
                  pl.BlockSpec(a_lambda.shape, lambda b, i: (0, 0)),
                  pl.BlockSpec((1, GROUP_W), lambda b, i: (0, 0)),
                  pl.BlockSpec((GROUP_W, GROUP_W), lambda b, i: (0, 0))],
        out_specs=pl.BlockSpec((tq, GROUP_W), lambda b, i: (b * nq + i, 0)),
        out_shape=jax.ShapeDtypeStruct((bsz * t, GROUP_W), F32),
        scratch_shapes=[pltpu.VMEM((1, 2 * N_HEADS * tq), F32), pltpu.VMEM((1, 2 * N_HEADS * tq), F32),
                        pltpu.VMEM((2, GROUP_W, tq), F32)],
        compiler_params=_cparams(("parallel", "arbitrary")), name="attn_a",
    )(qt, k, vt, a_lambda, g_full, gmean)


def _dsa_kernel(qt_ref, k_ref, vt_ref, qit_ref, ki_ref, kiwit_ref, o_ref, key_ref, y_ref, m_ref, l_ref, acc_ref,
                *, tq, tk, past, s_tot, topk, idx_bits, t_valid):
    q0 = pl.program_id(1) * tq
    n_kv = _num_kv_blocks(q0, tq, tk, past, s_tot)
    row_pos = past + q0 + lax.broadcasted_iota(I32, (1, tq), 1)
    key_iota = lax.broadcasted_iota(I32, (tk, 1), 0)
    int_min = jnp.int32(INT_MIN)

    qit = qit_ref[...]
    blocks = []
    for h in range(N_HEADS):
        qh = qit[h * D_IDX:(h + 1) * D_IDX, :]
        hi, lo = _split_bf16(qh)
        blocks.append(jnp.concatenate([hi, lo, hi], axis=0))
    qi = jnp.concatenate(blocks, axis=1)
    kiwit = kiwit_ref[...]
    w_h = [kiwit[D_IDX + h:D_IDX + h + 1, :] for h in range(N_HEADS)]

    def score_body(c, carry):
        k0 = pl.multiple_of(c * tk, tk)
        kib = ki_ref[pl.ds(k0, tk), :]
        sc = jnp.dot(kib, qi, preferred_element_type=F32)
        isc = jnp.maximum(sc[:, 0:tq], 0.0) * w_h[0]
        for h in range(1, N_HEADS):
            isc = isc + jnp.maximum(sc[:, h * tq:(h + 1) * tq], 0.0) * w_h[h]
        isc = jnp.where(isc == 0.0, 0.0, isc)
        bits = pltpu.bitcast(isc, I32)
        key = bits ^ ((bits >> 31) & jnp.int32(0x7FFFFFFF))
        key_ref[c] = jnp.where((k0 + key_iota) <= row_pos, key, int_min)
        return carry

    lax.fori_loop(0, n_kv, score_body, 0)

    def count(pred):
        def block_count(c):
            ones = jnp.where(pred(key_ref[c], c * tk), 1.0, 0.0)
            return jnp.sum(ones.reshape(tk // 8, 8, tq), axis=0)

        def body(i, acc):
            return acc + block_count(2 * i) + block_count(2 * i + 1)

        acc = lax.fori_loop(0, n_kv // 2, body, jnp.zeros((8, tq), F32))
        odd = (n_kv & 1).astype(F32)
        acc = acc + odd * block_count(n_kv - 1)
        return jnp.sum(acc, axis=0, keepdims=True)

    kk = jnp.float32(topk)

    def bit_body(r, prefix):
        cand_u = prefix | lax.shift_left(jnp.int32(1), 31 - r)
        cand_s = cand_u ^ int_min
        cnt = count(lambda blk, k0: blk >= cand_s)
        return jnp.where(cnt >= kk, cand_u, prefix)

    prefix = lax.fori_loop(0, 32, bit_body, jnp.zeros((1, tq), I32))
    thr = prefix ^ int_min

    n_gt = count(lambda blk, k0: blk > thr)
    n_ge = count(lambda blk, k0: blk >= thr)
    r_need = kk - n_gt
    live = lax.broadcasted_iota(I32, (1, tq), 1) < t_valid
    any_tie = jnp.max(jnp.where(live, n_ge - kk, 0.0)) > 0.0
    y_ref[...] = jnp.full((1, tq), s_tot, I32)

    @pl.when(any_tie)
    def _():
        def ybody(r, y):
            cand = y | lax.shift_left(jnp.int32(1), idx_bits - 1 - r)
            cnt = count(lambda blk, k0: (blk == thr) & ((k0 + key_iota) < cand))
            return jnp.where(cnt < r_need, cand, y)
        y_ref[...] = lax.fori_loop(0, idx_bits, ybody, jnp.zeros((1, tq), I32))

    y_last = y_ref[...]

    def mask_fn(c, k0):
        blk = key_ref[c]
        pos = k0 + key_iota
        sel = (blk > thr) | ((blk == thr) & (pos <= y_last))
        return sel & (pos <= row_pos)

    qt = qt_ref[...]
    row_head = lax.broadcasted_iota(I32, (GROUP_W, 1), 0) // HEAD_W
    qst = jnp.concatenate([jnp.where(row_head == h, qt, 0.0) for h in range(N_HEADS)], axis=1).astype(BF16)
    _attn_core(qst, N_HEADS, k_ref, vt_ref, 0, n_kv, tq, tk, mask_fn, m_ref, l_ref, acc_ref, zero_masked=True)
    o_ref[...] = _normalised(acc_ref[0], l_ref[...], 0, tq).T


def _dsa(qt, k, vt, qit, ki3, kiwit, *, bsz, tq, tk, past, topk, t_valid):
    t = qt.shape[1] // bsz
    s_tot = k.shape[0] // bsz
    nq = t // tq
    idx_bits = int(math.ceil(math.log2(s_tot))) + 1
    kern = functools.partial(_dsa_kernel, tq=tq, tk=tk, past=past, s_tot=s_tot, topk=topk,
                             idx_bits=idx_bits, t_valid=t_valid)
    qspec = lambda w: pl.BlockSpec((w, tq), lambda b, i: (0, b * nq + i))
    return pl.pallas_call(
        kern, grid=(bsz, nq),
        in_specs=[qspec(GROUP_W),
                  pl.BlockSpec((s_tot, GROUP_W), lambda b, i: (b, 0)),
                  pl.BlockSpec((GROUP_W, s_tot), lambda b, i: (0, b)),
                  qspec(GROUP_W),
                  pl.BlockSpec((s_tot, 3 * D_IDX), lambda b, i: (b, 0)),
                  qspec(LANES)],
        out_specs=pl.BlockSpec((tq, GROUP_W), lambda b, i: (b * nq + i, 0)),
        out_shape=jax.ShapeDtypeStruct((bsz * t, GROUP_W), F32),
        scratch_shapes=[pltpu.VMEM((s_tot // tk, tk, tq), I32), pltpu.VMEM((1, tq), I32),
                        pltpu.VMEM((1, N_HEADS * tq), F32), pltpu.VMEM((1, N_HEADS * tq), F32),
                        pltpu.VMEM((1, GROUP_W, tq), F32)],
        compiler_params=_cparams(("parallel", "arbitrary")), name="dsa",
    )(qt, k, vt, qit, ki3, kiwit)


_HALO_C = 8
_HALO_D = 32


def _conv_kernel(*refs, tm, single_tile):
    if single_tile:
        (cb_ref, uc_ref, ud_ref, hc_ref, hd_ref, cw_ref, dw_ref, db_ref, lg_ref, lb_ref,
         oc_ref, od_ref, ec_ref, ed_ref) = refs
        pc_ref = pd_ref = None
    else:
        (cb_ref, uc_ref, ud_ref, pc_ref, pd_ref, hc_ref, hd_ref, cw_ref, dw_ref, db_ref, lg_ref, lb_ref,
         oc_ref, od_ref, ec_ref, ed_ref) = refs
    i = pl.program_id(1)

    @pl.when(i == 0)
    def _():
        ec_ref[0:_HALO_C] = hc_ref[0]
        ed_ref[0:_HALO_D] = hd_ref[0]

    if not single_tile:
        @pl.when(i > 0)
        def _():
            ec_ref[0:_HALO_C] = pc_ref[0]
            ed_ref[0:_HALO_D] = pd_ref[0]

    ec_ref[_HALO_C:_HALO_C + tm] = uc_ref[0]
    ed_ref[_HALO_D:_HALO_D + tm] = ud_ref[0]

    cw = cw_ref[...]
    acc = ec_ref[pl.ds(_HALO_C - (C_WIDTH - 1), tm), :] * cw[0:1]
    for w in range(1, C_WIDTH):
        acc = acc + ec_ref[pl.ds(_HALO_C - (C_WIDTH - 1) + w, tm), :] * cw[w:w + 1]
    oc_ref[0] = cb_ref[0] * acc

    dw = dw_ref[...]
    acc = ed_ref[pl.ds(_HALO_D - (D_WIDTH - 1), tm), :] * dw[0:1]
    for w in range(1, D_WIDTH):
        acc = acc + ed_ref[pl.ds(_HALO_D - (D_WIDTH - 1) + w, tm), :] * dw[w:w + 1]
    z = _layer_norm(acc + db_ref[...], lg_ref[...], lb_ref[...])
    od_ref[0] = z * _sigmoid(z)


def _convs(cb, uc, ud, hist_c, hist_d, c_w, d_w, d_b, ln_g, ln_b, *, tm):
    bsz, t, w = cb.shape
    nt = t // tm
    single = nt == 1
    main = pl.BlockSpec((1, tm, w), lambda b, i: (b, i, 0))
    prev_c = pl.BlockSpec((1, _HALO_C, w), lambda b, i: (b, jnp.maximum(i * (tm // _HALO_C) - 1, 0), 0))
    prev_d = pl.BlockSpec((1, _HALO_D, w), lambda b, i: (b, jnp.maximum(i * (tm // _HALO_D) - 1, 0), 0))
    hist_cs = pl.BlockSpec((1, _HALO_C, w), lambda b, i: (b, 0, 0))
    hist_ds = pl.BlockSpec((1, _HALO_D, w), lambda b, i: (b, 0, 0))
    full2 = lambda a: pl.BlockSpec(a.shape, lambda b, i: (0, 0))
    in_specs = [main, main, main] + ([] if single else [prev_c, prev_d]) + [hist_cs, hist_ds] + \
        [full2(c_w), full2(d_w), full2(d_b), full2(ln_g), full2(ln_b)]
    args = [cb, uc, ud] + ([] if single else [uc, ud]) + [hist_c, hist_d, c_w, d_w, d_b, ln_g, ln_b]
    kern = functools.partial(_conv_kernel, tm=tm, single_tile=single)
    return pl.pallas_call(
        kern, grid=(bsz, nt), in_specs=in_specs,
        out_specs=[main, main],
        out_shape=[jax.ShapeDtypeStruct((bsz, t, w), F32)] * 2,
        scratch_shapes=[pltpu.VMEM((_HALO_C + tm, w), F32), pltpu.VMEM((_HALO_D + tm, w), F32)],
        compiler_params=_cparams(("parallel", "arbitrary")), name="convs",
    )(*args)


def _mix_kernel(x_ref, oa_ref, ob_ref, oc_ref, od_ref, pw_ref, pwb_ref, wo_ref, g_ref, b_ref, h_ref, *, alpha):
    od = jnp.dot(od_ref[...].astype(BF16), pw_ref[...], preferred_element_type=F32) + pwb_ref[...]
    cat = jnp.concatenate([oa_ref[...], ob_ref[...], oc_ref[...], od], axis=1).astype(BF16)
    mix = jnp.dot(cat, wo_ref[...], preferred_element_type=F32)
    h_ref[...] = _layer_norm(alpha * x_ref[...] + mix, g_ref[...], b_ref[...])


def _mix(x, oa, ob, oc, od, pw, pwb, wo, g, b, *, tm, alpha):
    n, d = x.shape
    row = lambda w: pl.BlockSpec((tm, w), lambda i: (i, 0))
    full = lambda a: pl.BlockSpec(a.shape, lambda i: (0, 0))
    return pl.pallas_call(
        functools.partial(_mix_kernel, alpha=alpha), grid=(n // tm,),
        in_specs=[row(d), row(GROUP_W), row(GROUP_W), row(GROUP_W), row(GROUP_W),
                  full(pw), full(pwb), full(wo), full(g), full(b)],
        out_specs=row(d), out_shape=jax.ShapeDtypeStruct((n, d), F32),
        compiler_params=_cparams(("parallel",)), name="mix",
    )(x, oa, ob, oc, od, pw, pwb, wo, g, b)


def _staircase():
    return [(a, b) for a in range(PEER_TOPK) for b in range(PEER_TOPK) if (a + 1) * (b + 1) <= PEER_TOPK]


def _tree_rows(x, op):
    while x.shape[0] > 1:
        half = x.shape[0] // 2
        y = op(x[:half], x[half:2 * half])
        x = y if x.shape[0] % 2 == 0 else jnp.concatenate([y, x[2 * half:]], axis=0)
    return x


def _argmax_rows(x, row):
    v, i = x, row
    while v.shape[0] > 1:
        n = v.shape[0]
        half = n // 2
        pv = v[:2 * half].reshape((half, 2) + v.shape[1:])
        pi = i[:2 * half].reshape((half, 2) + i.shape[1:])
        take_b = pv[:, 1] > pv[:, 0]
        nv = jnp.where(take_b, pv[:, 1], pv[:, 0])
        ni = jnp.where(take_b, pi[:, 1], pi[:, 0])
        if n % 2:
            nv = jnp.concatenate([nv, v[2 * half:]], axis=0)
            ni = jnp.concatenate([ni, i[2 * half:]], axis=0)
        v, i = nv, ni
    return v, i


def _extract_top(x, payload, n_out, n_rows):
    row = lax.broadcasted_iota(I32, x.shape, 0).astype(F32)
    vals, idxs, pays = [], [], []
    for _ in range(n_out):
        m, idx = _argmax_rows(x, row)
        hit = row == idx
        vals.append(m)
        idxs.append(idx)
        if payload is not None:
            pays.append(_tree_rows(jnp.where(hit, payload, -1.0), jnp.maximum))
        x = jnp.where(hit, -jnp.inf, x)
    return vals, idxs, pays


def _peer_select_kernel(h_ref, wq_ref, sk_ref, eid_ref, g_ref, sv_ref, si_ref, *, tm):
    ht = h_ref[...].T.astype(BF16)
    n_groups = 2 * PEER_HEADS
    nt = tm // LANES

    def to_tiles(x):
        return pltpu.einshape("trl->rtl", jnp.stack([x[:, LANES * t:LANES * (t + 1)] for t in range(nt)], axis=0))

    def from_tiles(y):
        z = pltpu.einshape("rtl->trl", y)
        return jnp.concatenate([z[t] for t in range(nt)], axis=1)

    def group_body(g, carry):
        r0 = pl.multiple_of(g * N_KEYS, N_KEYS)
        qg = jnp.dot(wq_ref[pl.ds(r0, N_KEYS), :], ht, preferred_element_type=F32)
        st = jnp.dot(sk_ref[g % 2], qg.astype(BF16), preferred_element_type=F32)
        vals, idxs, _ = _extract_top(to_tiles(st), None, PEER_TOPK, N_KEYS)
        sv_ref[g] = jnp.concatenate(vals, axis=0)
        si_ref[g] = jnp.concatenate(idxs, axis=0)
        return carry

    lax.fori_loop(0, n_groups, group_body, 0)

    pairs = _staircase()

    def head_body(hd, carry):
        sv0 = sv_ref[2 * hd]; sv1 = sv_ref[2 * hd + 1]
        si0 = si_ref[2 * hd]; si1 = si_ref[2 * hd + 1]
        cand = jnp.concatenate([sv0[a:a + 1] + sv1[b:b + 1] for a, b in pairs], axis=0)
        cand_e = jnp.concatenate([si0[a:a + 1] * float(N_KEYS) + si1[b:b + 1] for a, b in pairs], axis=0)
        vals, _, pays = _extract_top(cand, cand_e, PEER_TOPK, len(pairs))
        fv = jnp.concatenate(vals, axis=0)
        e = jnp.exp(fv - fv[0:1])
        gate = e / jnp.sum(e, axis=0, keepdims=True)
        r0 = pl.multiple_of(hd * PEER_TOPK, PEER_TOPK)
        eid_ref[pl.ds(r0, PEER_TOPK), :] = from_tiles(jnp.concatenate(pays, axis=0)).astype(I32)
        g_ref[pl.ds(r0, PEER_TOPK), :] = from_tiles(gate)
        return carry

    lax.fori_loop(0, PEER_HEADS, head_body, 0)


_SELECT_TOKENS = 8 * LANES


def _peer_select(h, wq_t, sk, *, tm):
    n, d = h.shape
    n_slots = PEER_HEADS * PEER_TOPK
    return pl.pallas_call(
        functools.partial(_peer_select_kernel, tm=tm), grid=(n // tm,),
        in_specs=[pl.BlockSpec((tm, d), lambda i: (i, 0)),
                  pl.BlockSpec(wq_t.shape, lambda i: (0, 0)),
                  pl.BlockSpec(sk.shape, lambda i: (0, 0, 0))],
        out_specs=[pl.BlockSpec((n_slots, tm), lambda i: (0, i)),
                   pl.BlockSpec((n_slots, tm), lambda i: (0, i))],
        out_shape=[jax.ShapeDtypeStruct((n_slots, n), I32), jax.ShapeDtypeStruct((n_slots, n), F32)],
        scratch_shapes=[pltpu.VMEM((2 * PEER_HEADS, PEER_TOPK, tm // LANES, LANES), F32),
                        pltpu.VMEM((2 * PEER_HEADS, PEER_TOPK, tm // LANES, LANES), F32)],
        compiler_params=_cparams(("parallel",)), name="peer_select",
    )(h, wq_t, sk)


_GATE_TOKENS = 16


def _peer_gates_kernel(eid_ref, g_ref, w_ref, *, tm):
    sub = lax.broadcasted_iota(I32, (N_KEYS, PEER_HEADS * PEER_TOPK), 0)

    def token_gates(t):
        e = eid_ref[pl.ds(t, 1), :]
        g = g_ref[pl.ds(t, 1), :]
        oh_i = jnp.where((e >> 7) == sub, 1.0, 0.0).astype(BF16)
        g_j = jnp.where((e & (N_KEYS - 1)) == sub, g, 0.0).astype(BF16)
        return lax.dot_general(oh_i, g_j, _NT, preferred_element_type=F32)

    def body(b, carry):
        t0 = pl.multiple_of(b * _GATE_TOKENS, _GATE_TOKENS)
        w = jnp.stack([token_gates(t0 + k) for k in range(_GATE_TOKENS)], axis=0)
        w_ref[:, pl.ds(t0, _GATE_TOKENS), :] = pltpu.einshape("tij->itj", w).astype(BF16)
        return carry

    lax.fori_loop(0, tm // _GATE_TOKENS, body, 0)


def _peer_gates(eid, gate, *, tm):
    n, n_slots = eid.shape
    return pl.pallas_call(
        functools.partial(_peer_gates_kernel, tm=tm), grid=(n // tm,),
        in_specs=[pl.BlockSpec((tm, n_slots), lambda i: (i, 0)),
                  pl.BlockSpec((tm, n_slots), lambda i: (i, 0))],
        out_specs=pl.BlockSpec((N_KEYS, tm, N_KEYS), lambda i: (0, i, 0)),
        out_shape=jax.ShapeDtypeStruct((N_KEYS, n, N_KEYS), BF16),
        compiler_params=_cparams(("parallel",)), name="peer_gates",
    )(eid, gate)


def _gelu(x):
    return 0.5 * x * (1.0 + lax.erf(x * (2.0 ** -0.5)))


def _peer_dense_kernel(h_ref, w_ref, u_ref, v_ref, f_ref):
    @pl.when(pl.program_id(1) == 0)
    def _():
        f_ref[...] = jnp.zeros(f_ref.shape, F32)

    act = _gelu(lax.dot_general(h_ref[...], u_ref[...], _NT, preferred_element_type=F32))
    y = jnp.concatenate(
        [w_ref[i].astype(F32) * act[:, i * N_KEYS:(i + 1) * N_KEYS] for i in range(w_ref.shape[0])],
        axis=1).astype(BF16)
    f_ref[...] += jnp.dot(y, v_ref[...], preferred_element_type=F32)


def _peer_dense(hb, wmat, u, v, *, tm, te):
    n, d = hb.shape
    n_exp = u.shape[0]
    return pl.pallas_call(
        _peer_dense_kernel, grid=(n // tm, n_exp // te),
        in_specs=[pl.BlockSpec((tm, d), lambda i, e: (i, 0)),
                  pl.BlockSpec((te // N_KEYS, tm, N_KEYS), lambda i, e: (e, i, 0)),
                  pl.BlockSpec((te, d), lambda i, e: (e, 0)),
                  pl.BlockSpec((te, d), lambda i, e: (e, 0))],
        out_specs=pl.BlockSpec((tm, d), lambda i, e: (i, 0)),
        out_shape=jax.ShapeDtypeStruct((n, d), F32),
        compiler_params=_cparams(("parallel", "arbitrary")), name="peer_dense",
    )(hb, wmat, u, v)


def _final_kernel(h_ref, f_ref, p_ref, g_ref, b_ref, wg_ref, bg_ref, wp_ref, y_ref, *, alpha):
    h2 = _layer_norm(alpha * h_ref[...] + f_ref[...], g_ref[...], b_ref[...])
    gate = _sigmoid(jnp.dot(h2.astype(BF16), wg_ref[...], preferred_element_type=F32) + bg_ref[...])
    emb = jnp.dot(p_ref[...].astype(BF16), wp_ref[...], preferred_element_type=F32)
    y_ref[...] = h2 + gate * emb


def _final(h, f, p, g, b, wg, bg, wp, *, tm, alpha):
    n, d = h.shape
    row = lambda w: pl.BlockSpec((tm, w), lambda i: (i, 0))
    full = lambda a: pl.BlockSpec(a.shape, lambda i: (0, 0))
    return pl.pallas_call(
        functools.partial(_final_kernel, alpha=alpha), grid=(n // tm,),
        in_specs=[row(d), row(d), row(p.shape[1]), full(g), full(b), full(wg), full(bg), full(wp)],
        out_specs=row(d), out_shape=jax.ShapeDtypeStruct((n, d), F32),
        compiler_params=_cparams(("parallel",)), name="final",
    )(h, f, p, g, b, wg, bg, wp)


def _pack_w_in(w_in):
    d = w_in.shape[0]
    cols = [w_in[:, 0:1792], w_in[:, 1792:1860], jnp.zeros((d, 60), w_in.dtype), w_in[:, 1860:3140]]
    nat = jnp.concatenate(cols, axis=1).astype(BF16)
    seg = lambda name, width=GROUP_W: nat[:, _SEG[name]:_SEG[name] + width]
    tr = jnp.concatenate([seg("qa"), seg("qb"), seg("qi"), seg("kiwi", 128), seg("va"), seg("vb")], axis=1).T
    return nat, tr


def _row(a):
    return a.reshape(1, -1)


def _tile_rows(n, pref):
    t = pref
    while n % t:
        t //= 2
    return t


def _prep_weights(lw, depth):
    (w_in, a_lambda, a_subln_g, c_conv_w, d_conv_w, d_conv_b, d_ln_g, d_ln_b, d_pw_w, d_pw_b,
     w_out, ln1_g, ln1_b, peer_wq, peer_subkeys, peer_u, peer_v, ln2_g, ln2_b,
     ple_w, ple_gate_w, ple_gate_b) = lw
    head_of_lane = jnp.arange(GROUP_W) // HEAD_W
    gmean = (head_of_lane[:, None] == head_of_lane[None, :]).astype(BF16) * (1.0 / HEAD_W)
    w_nat, w_tr = _pack_w_in(w_in)
    return dict(
        w_in=w_nat, w_in_t=w_tr, a_lambda=a_lambda, a_g=_row(jnp.tile(a_subln_g, N_HEADS)),
        gmean=gmean.astype(BF16),
        c_w=c_conv_w, d_w=d_conv_w, d_b=_row(d_conv_b), d_ln_g=_row(d_ln_g), d_ln_b=_row(d_ln_b),
        d_pw=d_pw_w.astype(BF16), d_pwb=_row(d_pw_b), w_out=w_out.astype(BF16),
        ln1_g=_row(ln1_g), ln1_b=_row(ln1_b),
        wq_t=peer_wq.T.astype(BF16), sk=peer_subkeys.astype(BF16),
        u=peer_u.astype(BF16), v=peer_v.astype(BF16),
        ln2_g=_row(ln2_g), ln2_b=_row(ln2_b),
        ple_w=ple_w.astype(BF16), wg=ple_gate_w.astype(BF16), bg=_row(ple_gate_b),
        alpha=float((2 * depth) ** 0.25),
    )


def _index_keys(ki):
    k_hi = ki.astype(BF16)
    k_lo = (ki - k_hi.astype(F32)).astype(BF16)
    return jnp.concatenate([k_hi, k_hi, k_lo], axis=-1)


def _token_tail(w, x, oa, ob, oc, od, valid_rows, p):
    n = x.shape[0]
    h = _mix(x, oa, ob, oc, od, w["d_pw"], w["d_pwb"], w["w_out"], w["ln1_g"], w["ln1_b"],
             tm=_tile_rows(n, 512), alpha=w["alpha"])
    if valid_rows is not None:
        bsz, t_pad, t = valid_rows
        h = h.reshape(bsz, t_pad, -1)[:, :t].reshape(bsz * t, -1)
    n = h.shape[0]
    n_sel = -(-n // _SELECT_TOKENS) * _SELECT_TOKENS
    h_sel = h if n_sel == n else jnp.pad(h, ((0, n_sel - n), (0, 0)))
    eid_t, gate_t = _peer_select(h_sel, w["wq_t"], w["sk"], tm=_SELECT_TOKENS)
    n_gate = -(-n // _GATE_TOKENS) * _GATE_TOKENS
    wmat = _peer_gates(eid_t[:, :n_gate].T, gate_t[:, :n_gate].T, tm=_tile_rows(n_gate, 128))
    f = _peer_dense(h.astype(BF16), wmat, w["u"], w["v"], tm=_tile_rows(n, 1024), te=8 * N_KEYS)
    return _final(h, f, p, w["ln2_g"], w["ln2_b"], w["wg"], w["bg"], w["ple_w"],
                  tm=_tile_rows(n, 512), alpha=w["alpha"])


def _layer(layer_idx, depth, xp, xs, caches, state_c, state_d, page_table, pp, ps, lw):
    w = _prep_weights(lw, depth)
    bp, tp, d = xp.shape
    bs, ts, _ = xs.shape
    ts_pad = 8
    n_pages = page_table.shape[1]
    page = caches[0].shape[3]
    past = n_pages * page
    lam_init = 0.8 - 0.6 * math.exp(-0.3 * (layer_idx + 1))

    pr = _project(xp.reshape(bp * tp, d), w["w_in"], w["w_in_t"], _tile_rows(bp * tp, 512), feature_major=True)
    tq = _tile_rows(tp, 256)
    tk = _tile_rows(tp, 256)
    oa = _attention_a(pr["qat"], pr["kab"], pr["vat"], w["a_lambda"], w["a_g"], w["gmean"],
                      bsz=bp, tq=tq, tk=tk, past=0, lam_init=lam_init)
    ob = _dsa(pr["qbt"], pr["kbb"], pr["vbt"], pr["qit"], _index_keys(pr["kiwi"][:, :D_IDX]), pr["kiwit"],
              bsz=bp, tq=tq, tk=tk, past=0, topk=min(DSA_TOPK, tp // 4), t_valid=tq)
    r3p = lambda a: a.reshape(bp, tp, a.shape[-1])
    zc = jnp.zeros((bp, _HALO_C, GROUP_W), F32)
    zd = jnp.zeros((bp, _HALO_D, GROUP_W), F32)
    oc, od = _convs(r3p(pr["cb"]), r3p(pr["uc"]), r3p(pr["ud"]), zc, zd, w["c_w"], w["d_w"], w["d_b"],
                    w["d_ln_g"], w["d_ln_b"], tm=_tile_rows(tp, 512))
    yp = _token_tail(w, xp.reshape(bp * tp, d), oa, ob, oc.reshape(bp * tp, -1), od.reshape(bp * tp, -1),
                     None, pp.reshape(bp * tp, -1)).reshape(bp, tp, d)
    new_p = (r3p(pr["ka"]), r3p(pr["va"]), r3p(pr["kb"]), r3p(pr["vb"]), r3p(pr["kiwi"])[:, :, :D_IDX],
             jnp.concatenate([zc, r3p(pr["uc"])], axis=1)[:, -(C_WIDTH - 1):],
             jnp.concatenate([zd, r3p(pr["ud"])], axis=1)[:, -(D_WIDTH - 1):])

    xs_pad = jnp.pad(xs, ((0, 0), (0, ts_pad - ts), (0, 0))).reshape(bs * ts_pad, d)
    sr = _project(xs_pad, w["w_in"], w["w_in_t"], _tile_rows(bs * ts_pad, 512), feature_major=False)
    r3s = lambda a: a.reshape(bs, ts_pad, a.shape[-1])
    news = [r3s(sr["ka"]), r3s(sr["va"]), r3s(sr["kb"]), r3s(sr["vb"]), r3s(sr["kiwi"])[:, :, :D_IDX]]
    oa_s, ob_s = _sample_attention(page_table, caches, layer_idx, news, r3s(sr["qa"]), r3s(sr["qb"]), r3s(sr["qi"]),
                                   r3s(sr["kiwi"]), w["a_lambda"], w["a_g"], w["gmean"],
                                   topk=min(DSA_TOPK, (past + ts) // 4), lam_init=lam_init, t_valid=ts)
    unpad = lambda a: a.reshape(bs * ts_pad, GROUP_W)
    hist_c = jnp.concatenate([jnp.zeros((bs, _HALO_C - (C_WIDTH - 1), GROUP_W), F32), state_c], axis=1)
    hist_d = jnp.concatenate([jnp.zeros((bs, _HALO_D - (D_WIDTH - 1), GROUP_W), F32), state_d], axis=1)
    oc_s, od_s = _convs(r3s(sr["cb"]), r3s(sr["uc"]), r3s(sr["ud"]), hist_c, hist_d, w["c_w"], w["d_w"],
                        w["d_b"], w["d_ln_g"], w["d_ln_b"], tm=ts_pad)
    ys = _token_tail(w, xs_pad, unpad(oa_s), unpad(ob_s), oc_s.reshape(bs * ts_pad, -1),
                     od_s.reshape(bs * ts_pad, -1), (bs, ts_pad, ts), ps.reshape(bs * ts, -1)).reshape(bs, ts, d)
    cut = lambda a: r3s(a)[:, :ts]
    new_s = (cut(sr["ka"]), cut(sr["va"]), cut(sr["kb"]), cut(sr["vb"]), cut(sr["kiwi"])[:, :, :D_IDX],
             jnp.concatenate([state_c, cut(sr["uc"])], axis=1)[:, -(C_WIDTH - 1):],
             jnp.concatenate([state_d, cut(sr["ud"])], axis=1)[:, -(D_WIDTH - 1):])
    return yp, ys, new_p, new_s


def kernel(x_prompt, x_sample, cache_a_k, cache_a_v, cache_b_k, cache_b_v, cache_b_kidx, state_c_conv, state_d_conv, page_table, p_prompt, p_sample, w_in, a_lambda, a_subln_g, c_conv_w, d_conv_w, d_conv_b, d_ln_g, d_ln_b, d_pw_w, d_pw_b, w_out, ln1_g, ln1_b, peer_wq, peer_subkeys, peer_u, peer_v, ln2_g, ln2_b, ple_w, ple_gate_w, ple_gate_b):
    depth = w_in.shape[0]
    n_pool, page = cache_a_k.shape[1], cache_a_k.shape[2]
    feature_major = lambda c: jnp.moveaxis(c, 2, -1).reshape(depth, n_pool, -1, page)
    caches = [feature_major(c) for c in (cache_a_k, cache_a_v, cache_b_k, cache_b_v, cache_b_kidx)]
    xp, xs = x_prompt, x_sample
    news_p, news_s = [], []
    for l in range(depth):
        lw = (w_in[l], a_lambda[l], a_subln_g[l], c_conv_w[l], d_conv_w[l], d_conv_b[l], d_ln_g[l], d_ln_b[l],
              d_pw_w[l], d_pw_b[l], w_out[l], ln1_g[l], ln1_b[l], peer_wq[l], peer_subkeys[l], peer_u[l],
              peer_v[l], ln2_g[l], ln2_b[l], ple_w[l], ple_gate_w[l], ple_gate_b[l])
        xp, xs, new_p, new_s = _layer(l, depth, xp, xs, caches, state_c_conv[l], state_d_conv[l],
                                      page_table, p_prompt[l], p_sample[l], lw)
        news_p.append(new_p)
        news_s.append(new_s)

    bp, tp, _ = x_prompt.shape
    bs, ts, _ = x_sample.shape
    shapes = [(N_HEADS, 2, 32), (N_HEADS, HEAD_W), (N_HEADS, HEAD_W), (N_HEADS, HEAD_W), (D_IDX,)]

    def stack(news, i, lead):
        a = jnp.stack([n[i] for n in news], axis=0)
        return a.reshape((depth,) + lead + shapes[i]) if i < 5 else a

    outs = [xp, xs]
    for i in range(7):
        outs.append(stack(news_p, i, (bp, tp)))
        outs.append(stack(news_s, i, (bs, ts)))
    return tuple(outs)
```

```python
import functools
import math

import jax
import jax.numpy as jnp
from jax import lax
from jax.experimental import pallas as pl
from jax.experimental.pallas import tpu as pltpu

F32 = jnp.float32
BF16 = jnp.bfloat16
I32 = jnp.int32

LN_EPS = 1e-5
N_HEADS = 4
HEAD_W = 64
GROUP_W = 256
D_IDX = 64
DSA_TOPK = 256
C_WIDTH = 3
D_WIDTH = 31
PEER_HEADS = 8
PEER_TOPK = 16
N_KEYS = 128
LANES = 128
NEG_BIG = -1e30
INT_MIN = -2 ** 31
VMEM_LIMIT = 56 * 1024 * 1024

_SEG = dict(qa=0, ka=256, va=512, qb=768, kb=1024, vb=1280, qi=1536, kiwi=1792,
            cb=1920, cc=2176, cx=2432, da=2688, dg=2944)
_N_PACKED = 3200
_SEG_T = dict(qa=0, qb=256, qi=512, kiwi=768, va=896, vb=1152)
_N_PACKED_T = 1408

_NT = (((1,), (1,)), ((), ()))


def _cparams(sem):
    return pltpu.CompilerParams(dimension_semantics=sem, vmem_limit_bytes=VMEM_LIMIT)


def _split_bf16(x):
    hi = x.astype(BF16)
    lo = (x - hi.astype(F32)).astype(BF16)
    return hi, lo


def _layer_norm(x, g, b):
    mu = jnp.mean(x, axis=-1, keepdims=True)
    xc = x - mu
    var = jnp.mean(xc * xc, axis=-1, keepdims=True)
    return xc * lax.rsqrt(var + LN_EPS) * g + b


def _sigmoid(x):
    return 1.0 / (1.0 + jnp.exp(-x))


_PROJ_COMMON = ("ka", "va", "kb", "vb", "kiwi", "cb", "uc", "ud")
_PROJ_FEATURE_MAJOR = ("kab", "kbb", "qat", "qbt", "qit", "kiwit", "vat", "vbt")
_PROJ_ROW_MAJOR = ("qa", "qb", "qi")


def _proj_kernel(x_ref, w_ref, wt_ref, *out_refs, names, scale_a, scale_b):
    o = dict(zip(names, out_refs))
    xb = x_ref[...].astype(BF16)

    def seg(name, width=GROUP_W):
        off = _SEG[name]
        return jnp.dot(xb, w_ref[:, off:off + width], preferred_element_type=F32)

    def seg_t(name, width=GROUP_W):
        off = _SEG_T[name]
        return lax.dot_general(wt_ref[off:off + width, :], xb, _NT, preferred_element_type=F32)

    ka = seg("ka"); kb = seg("kb")
    o["ka"][...] = ka; o["kb"][...] = kb
    o["va"][...] = seg("va"); o["vb"][...] = seg("vb")
    o["kiwi"][...] = seg("kiwi", LANES)
    o["cb"][...] = seg("cb")
    o["uc"][...] = seg("cc") * seg("cx")
    o["ud"][...] = seg("da") * _sigmoid(seg("dg"))
    if "qat" in o:
        o["kab"][...] = ka.astype(BF16); o["kbb"][...] = kb.astype(BF16)
        o["qat"][...] = seg_t("qa") * scale_a
        o["qbt"][...] = seg_t("qb") * scale_b
        o["qit"][...] = seg_t("qi")
        o["kiwit"][...] = seg_t("kiwi", LANES)
        o["vat"][...] = seg_t("va").astype(BF16)
        o["vbt"][...] = seg_t("vb").astype(BF16)
    else:
        o["qa"][...] = seg("qa") * scale_a
        o["qb"][...] = seg("qb") * scale_b
        o["qi"][...] = seg("qi")


def _project(x, w_packed, wt_packed, tm, feature_major):
    n, d = x.shape
    nat = lambda w, dt=F32: (jax.ShapeDtypeStruct((n, w), dt), pl.BlockSpec((tm, w), lambda i: (i, 0)))
    tr = lambda w, dt=F32: (jax.ShapeDtypeStruct((w, n), dt), pl.BlockSpec((w, tm), lambda i: (0, i)))
    outs = [nat(256), nat(256), nat(256), nat(256), nat(LANES), nat(256), nat(256), nat(256)]
    if feature_major:
        names = _PROJ_COMMON + _PROJ_FEATURE_MAJOR
        outs += [nat(256, BF16), nat(256, BF16), tr(256), tr(256), tr(256), tr(LANES), tr(256, BF16), tr(256, BF16)]
    else:
        names = _PROJ_COMMON + _PROJ_ROW_MAJOR
        outs += [nat(256), nat(256), nat(256)]
    kern = functools.partial(_proj_kernel, names=names, scale_a=float(32 ** -0.5), scale_b=float(HEAD_W ** -0.5))
    res = pl.pallas_call(
        kern, grid=(n // tm,),
        in_specs=[pl.BlockSpec((tm, d), lambda i: (i, 0)),
                  pl.BlockSpec((d, _N_PACKED), lambda i: (0, 0)),
                  pl.BlockSpec((_N_PACKED_T, d), lambda i: (0, 0))],
        out_specs=[o[1] for o in outs], out_shape=[o[0] for o in outs],
        compiler_params=_cparams(("parallel",)), name="proj",
    )(x, w_packed, wt_packed)
    return dict(zip(names, res))


def _lane_heads(o, l, base, rows, lane_head):
    out = None
    for h in range(N_HEADS):
        r0 = (base + h) * rows
        part = jnp.where(lane_head == h, o[r0:r0 + rows] / l[r0:r0 + rows], 0.0)
        out = part if out is None else out + part
    return out


def _sample_attn_kernel(pt_ref, *refs, n_pages, page, rows, past, topk, idx_bits, lam_init, t_valid):
    npg = n_pages
    ak, av, bk, bv, kic = (refs[i * npg:(i + 1) * npg] for i in range(5))
    (nak, nav, nbk, nbv, nki, qa_ref, qb_ref, qi_ref, kiwi_ref, lam_ref, g_ref, gm_ref,
     oa_ref, ob_ref, y_ref) = refs[5 * npg:]
    nblk = npg + 1
    s_tot = nblk * page
    lane_head = lax.broadcasted_iota(I32, (1, GROUP_W), 1) // HEAD_W
    col_last = past + lax.broadcasted_iota(I32, (1, page), 1)
    t_row = lax.broadcasted_iota(I32, (rows, 1), 0)
    int_min = jnp.int32(INT_MIN)

    def blocks(page_refs, new_ref, dtype):
        new = new_ref[0]
        pad = jnp.zeros((page - rows, new.shape[-1]), F32)
        return [r[0, 0].astype(dtype) for r in page_refs] + [jnp.concatenate([new, pad], axis=0).T.astype(dtype)]

    def softmax_pv(q_stack, n_groups, k_blocks, v_blocks, masks):
        def masked(s, mk, fill):
            if mk is None:
                return s
            s3 = jnp.where(mk[None], s.reshape(n_groups, rows, page), fill)
            return s3.reshape(n_groups * rows, page)
        s = [masked(jnp.dot(q_stack, kb, preferred_element_type=F32), mk, NEG_BIG)
             for kb, mk in zip(k_blocks, masks)]
        m = s[0].max(axis=1, keepdims=True)
        for sj in s[1:]:
            m = jnp.maximum(m, sj.max(axis=1, keepdims=True))
        l = None
        o = None
        for sj, vb, mk in zip(s, v_blocks, masks):
            p = masked(jnp.exp(sj - m), mk, 0.0)
            lj = p.sum(axis=1, keepdims=True)
            oj = lax.dot_general(p.astype(BF16), vb, _NT, preferred_element_type=F32)
            l = lj if l is None else l + lj
            o = oj if o is None else o + oj
        return o, l

    causal_last = col_last <= (past + t_row)

    qa = qa_ref[0]
    lane_grp = lax.broadcasted_iota(I32, (1, GROUP_W), 1) // 32
    qa_stack = jnp.concatenate(
        [jnp.where(lane_grp == (2 * h + c), qa, 0.0) for c in range(2) for h in range(N_HEADS)],
        axis=0).astype(BF16)
    o, l = softmax_pv(qa_stack, 2 * N_HEADS, blocks(ak, nak, BF16), blocks(av, nav, BF16),
                      [None] * npg + [causal_last])
    o0 = _lane_heads(o, l, 0, rows, lane_head)
    o1 = _lane_heads(o, l, N_HEADS, rows, lane_head)
    lv = lam_ref[...]
    lam = (jnp.exp(jnp.sum(lv[0:1] * lv[1:2], axis=1, keepdims=True))
           - jnp.exp(jnp.sum(lv[2:3] * lv[3:4], axis=1, keepdims=True)) + lam_init)
    od = o0 - lam * o1
    hi, lo = _split_bf16(od * od)
    ms = (jnp.dot(hi, gm_ref[...], preferred_element_type=F32)
          + jnp.dot(lo, gm_ref[...], preferred_element_type=F32))
    oa_ref[0] = od * lax.rsqrt(ms + LN_EPS) * g_ref[...] * (1.0 - lam_init)

    qi = qi_ref[0]
    qi_rows = jnp.concatenate([qi[:, h * D_IDX:(h + 1) * D_IDX] for h in range(N_HEADS)], axis=0)
    q_hi, q_lo = _split_bf16(qi_rows)
    kiwi = kiwi_ref[0]
    w_h = [kiwi[:, D_IDX + h:D_IDX + h + 1] for h in range(N_HEADS)]
    key_blocks = []
    for j, kib in enumerate(blocks(kic, nki, F32)):
        k_hi, k_lo = _split_bf16(kib)
        sc = (jnp.dot(q_hi, k_hi, preferred_element_type=F32)
              + jnp.dot(q_lo, k_hi, preferred_element_type=F32)
              + jnp.dot(q_hi, k_lo, preferred_element_type=F32))
        isc = jnp.maximum(sc[0:rows], 0.0) * w_h[0]
        for h in range(1, N_HEADS):
            isc = isc + jnp.maximum(sc[h * rows:(h + 1) * rows], 0.0) * w_h[h]
        isc = jnp.where(isc == 0.0, 0.0, isc)
        bits = pltpu.bitcast(isc, I32)
        key = bits ^ ((bits >> 31) & jnp.int32(0x7FFFFFFF))
        if j == npg:
            key = jnp.where(causal_last, key, int_min)
        key_blocks.append(key)
    keys = jnp.concatenate(key_blocks, axis=1)
    cols = lax.broadcasted_iota(I32, (1, s_tot), 1)
    kk = jnp.float32(topk)
    count = lambda pred: jnp.sum(jnp.where(pred, 1.0, 0.0), axis=1, keepdims=True)

    def bit_body(r, prefix):
        b1 = lax.shift_left(jnp.int32(1), 31 - 2 * r)
        b0 = lax.shift_left(jnp.int32(1), 30 - 2 * r)
        enough = lambda cand_u: count(keys >= (cand_u ^ int_min)) >= kk
        c10 = prefix | b1
        c11 = c10 | b0
        c01 = prefix | b0
        return jnp.where(enough(c10), jnp.where(enough(c11), c11, c10), jnp.where(enough(c01), c01, prefix))

    thr = lax.fori_loop(0, 16, bit_body, jnp.zeros((rows, 1), I32)) ^ int_min
    r_need = kk - count(keys > thr)
    any_tie = jnp.max(jnp.where(t_row < t_valid, count(keys >= thr) - kk, 0.0)) > 0.0
    y_ref[...] = jnp.full((rows, 1), s_tot, I32)

    @pl.when(any_tie)
    def _():
        def ybody(r, y):
            cand = y | lax.shift_left(jnp.int32(1), idx_bits - 1 - r)
            return jnp.where(count((keys == thr) & (cols < cand)) < r_need, cand, y)
        y_ref[...] = lax.fori_loop(0, idx_bits, ybody, jnp.zeros((rows, 1), I32))

    sel = ((keys > thr) | ((keys == thr) & (cols <= y_ref[...]))) & (cols <= (past + t_row))
    qb = qb_ref[0]
    qb_stack = jnp.concatenate([jnp.where(lane_head == h, qb, 0.0) for h in range(N_HEADS)], axis=0).astype(BF16)
    o, l = softmax_pv(qb_stack, N_HEADS, blocks(bk, nbk, BF16), blocks(bv, nbv, BF16),
                      [sel[:, j * page:(j + 1) * page] for j in range(nblk)])
    ob_ref[0] = _lane_heads(o, l, 0, rows, lane_head)


def _sample_attention(page_table, caches, layer, news, qa, qb, qi, kiwi, a_lambda, g_full, gmean,
                      *, topk, lam_init, t_valid):
    bsz, n_pages = page_table.shape
    page = caches[0].shape[3]
    rows = qa.shape[1]
    past = n_pages * page
    idx_bits = int(math.ceil(math.log2((n_pages + 1) * page))) + 1
    page_spec = lambda c, j: pl.BlockSpec((1, 1, c, page), lambda b, pt: (layer, pt[b, j], 0, 0))
    seq_spec = lambda c: pl.BlockSpec((1, rows, c), lambda b, pt: (b, 0, 0))
    full2 = lambda a: pl.BlockSpec(a.shape, lambda b, pt: (0, 0))
    in_specs, args = [], []
    for cache in caches:
        for j in range(n_pages):
            in_specs.append(page_spec(cache.shape[2], j))
            args.append(cache)
    for a in list(news) + [qa, qb, qi, kiwi]:
        in_specs.append(seq_spec(a.shape[-1]))
        args.append(a)
    for a in (a_lambda, g_full, gmean):
        in_specs.append(full2(a))
        args.append(a)
    kern = functools.partial(_sample_attn_kernel, n_pages=n_pages, page=page, rows=rows, past=past, topk=topk,
                             idx_bits=idx_bits, lam_init=lam_init, t_valid=t_valid)
    return pl.pallas_call(
        kern,
        grid_spec=pltpu.PrefetchScalarGridSpec(
            num_scalar_prefetch=1, grid=(bsz,), in_specs=in_specs,
            out_specs=[seq_spec(GROUP_W), seq_spec(GROUP_W)],
            scratch_shapes=[pltpu.VMEM((rows, 1), I32)]),
        out_shape=[jax.ShapeDtypeStruct((bsz, rows, GROUP_W), F32)] * 2,
        compiler_params=_cparams(("parallel",)), name="sample_attn",
    )(page_table, *args)


def _attn_core(qst, n_groups, k_ref, vt_ref, n_full, n_kv, tq, tk, mask_fn, m_ref, l_ref, acc_ref, zero_masked):
    n_comp = n_groups // N_HEADS
    row_head = lax.broadcasted_iota(I32, (GROUP_W, 1), 0) // HEAD_W
    m_ref[...] = jnp.full(m_ref.shape, NEG_BIG, F32)
    l_ref[...] = jnp.zeros(l_ref.shape, F32)
    acc_ref[...] = jnp.zeros(acc_ref.shape, F32)
    grp = lambda a, g: a[:, g * tq:(g + 1) * tq]

    def step(c, n_blk, masked):
        width = n_blk * tk
        k0 = pl.multiple_of(c * tk, tk)
        kblk = k_ref[pl.ds(k0, width), :]
        vt = vt_ref[:, pl.ds(k0, width)]
        masks = [mask_fn(c + i, k0 + i * tk) for i in range(n_blk)] if masked else None

        def apply_masks(a, fill):
            parts = [jnp.where(masks[i], a[i * tk:(i + 1) * tk], fill) for i in range(n_blk)]
            return parts[0] if n_blk == 1 else jnp.concatenate(parts, axis=0)

        m_old = m_ref[...]
        l_old = l_ref[...]
        m_new, l_new, alpha, pb = [], [], [], []
        for g in range(n_groups):
            s = jnp.dot(kblk, grp(qst, g), preferred_element_type=F32)
            if masked:
                s = apply_masks(s, NEG_BIG)
            m_g = jnp.maximum(grp(m_old, g), jnp.max(s, axis=0, keepdims=True))
            a_g = jnp.exp(grp(m_old, g) - m_g)
            p = jnp.exp(s - m_g)
            if masked and zero_masked:
                p = apply_masks(p, 0.0)
            m_new.append(m_g)
            alpha.append(a_g)
            l_new.append(a_g * grp(l_old, g) + jnp.sum(p, axis=0, keepdims=True))
            pb.append(p.astype(BF16))
        m_ref[...] = jnp.concatenate(m_new, axis=1)
        l_ref[...] = jnp.concatenate(l_new, axis=1)
        for comp in range(n_comp):
            for h in range(N_HEADS):
                g = comp * N_HEADS + h
                rows = slice(h * HEAD_W, (h + 1) * HEAD_W)
                pv = jnp.dot(vt[rows], pb[g], preferred_element_type=F32)
                acc_ref[comp, rows, :] = acc_ref[comp, rows, :] * alpha[g] + pv

    def run(lo, hi, masked):
        def pair_body(i, carry):
            step(lo + 2 * i, 2, masked)
            return carry
        n = hi - lo
        if isinstance(n, int) and n == 0:
            return
        lax.fori_loop(0, n // 2, pair_body, 0)

        @pl.when(n % 2 == 1)
        def _():
            step(hi - 1, 1, masked)

    run(0, n_full, False)
    run(n_full, n_kv, True)


def _normalised(acc, l_row, base, tq):
    rl = 1.0 / l_row
    rf = jnp.concatenate(
        [jnp.broadcast_to(rl[:, (base + h) * tq:(base + h + 1) * tq], (HEAD_W, tq)) for h in range(N_HEADS)],
        axis=0)
    return acc * rf


def _num_kv_blocks(q0, tq, tk, past, s_tot):
    return jnp.minimum((past + q0 + tq + tk - 1) // tk, s_tot // tk)


def _attn_a_kernel(qt_ref, k_ref, vt_ref, lam_ref, g_ref, gm_ref, o_ref, m_ref, l_ref, acc_ref,
                   *, tq, tk, past, s_tot, lam_init):
    q0 = pl.program_id(1) * tq
    qt = qt_ref[...]
    row_grp = lax.broadcasted_iota(I32, (GROUP_W, 1), 0) // 32
    qst = jnp.concatenate(
        [jnp.where(row_grp == (2 * h + c), qt, 0.0) for c in range(2) for h in range(N_HEADS)],
        axis=1).astype(BF16)
    n_kv = _num_kv_blocks(q0, tq, tk, past, s_tot)
    n_full = jnp.minimum((past + q0 + 1) // tk, n_kv)
    row_pos = past + q0 + lax.broadcasted_iota(I32, (1, tq), 1)
    key_iota = lax.broadcasted_iota(I32, (tk, 1), 0)

    def mask_fn(c, k0):
        return (k0 + key_iota) <= row_pos

    _attn_core(qst, 2 * N_HEADS, k_ref, vt_ref, n_full, n_kv, tq, tk, mask_fn,
               m_ref, l_ref, acc_ref, zero_masked=False)
    l_row = l_ref[...]
    o0 = _normalised(acc_ref[0], l_row, 0, tq)
    o1 = _normalised(acc_ref[1], l_row, N_HEADS, tq)
    lv = lam_ref[...]
    lam = (jnp.exp(jnp.sum(lv[0:1] * lv[1:2], axis=1, keepdims=True))
           - jnp.exp(jnp.sum(lv[2:3] * lv[3:4], axis=1, keepdims=True)) + lam_init)
    o = (o0 - lam * o1).T
    hi, lo = _split_bf16(o * o)
    ms = (jnp.dot(hi, gm_ref[...], preferred_element_type=F32)
          + jnp.dot(lo, gm_ref[...], preferred_element_type=F32))
    o_ref[...] = o * lax.rsqrt(ms + LN_EPS) * g_ref[...] * (1.0 - lam_init)


def _attention_a(qt, k, vt, a_lambda, g_full, gmean, *, bsz, tq, tk, past, lam_init):
    t = qt.shape[1] // bsz
    s_tot = k.shape[0] // bsz
    nq = t // tq
    kern = functools.partial(_attn_a_kernel, tq=tq, tk=tk, past=past, s_tot=s_tot, lam_init=lam_init)
    return pl.pallas_call(
        kern, grid=(bsz, nq),
        in_specs=[pl.BlockSpec((GROUP_W, tq), lambda b, i: (0, b * nq + i)),
                  pl.BlockSpec((s_tot, GROUP_W), lambda b, i: (b, 0)),
                  pl.BlockSpec((GROUP_W, s_tot), lambda b, i: (0, b)),
                  pl.BlockSpec(a_lambda.shape, lambda b, i: (0, 0)),
                  pl.BlockSpec((1, GROUP_W), lambda b, i: (0, 0)),
                  pl.BlockSpec((GROUP_W, GROUP_W), lambda b, i: (0, 0))],
        out_specs=pl.BlockSpec((tq, GROUP_W), lambda b, i: (b * nq + i, 0)),
        out_shape=jax.ShapeDtypeStruct((bsz * t, GROUP_W), F32),
        scratch_shapes=[pltpu.VMEM((1, 2 * N_HEADS * tq), F32), pltpu.VMEM((1, 2 * N_HEADS * tq), F32),
                        pltpu.VMEM((2, GROUP_W, tq), F32)],
        compiler_params=_cparams(("parallel", "arbitrary")), name="attn_a",
    )(qt, k, vt, a_lambda, g_full, gmean)


def _dsa_kernel(qt_ref, k_ref, vt_ref, qit_ref, ki_ref, kiwit_ref, o_ref, key_ref, y_ref, m_ref, l_ref, acc_ref,
                *, tq, tk, past, s_tot, topk, idx_bits, t_valid):
    q0 = pl.program_id(1) * tq
    n_kv = _num_kv_blocks(q0, tq, tk, past, s_tot)
    row_pos = past + q0 + lax.broadcasted_iota(I32, (1, tq), 1)
    key_iota = lax.broadcasted_iota(I32, (tk, 1), 0)
    int_min = jnp.int32(INT_MIN)

    qit = qit_ref[...]
    blocks = []
    for h in range(N_HEADS):
        qh = qit[h * D_IDX:(h + 1) * D_IDX, :]
        hi, lo = _split_bf16(qh)
        blocks.append(jnp.concatenate([hi, lo, hi], axis=0))
    qi = jnp.concatenate(blocks, axis=1)
    kiwit = kiwit_ref[...]
    w_h = [kiwit[D_IDX + h:D_IDX + h + 1, :] for h in range(N_HEADS)]

    def score_body(c, carry):
        k0 = pl.multiple_of(c * tk, tk)
        kib = ki_ref[pl.ds(k0, tk), :]
        sc = jnp.dot(kib, qi, preferred_element_type=F32)
        isc = jnp.maximum(sc[:, 0:tq], 0.0) * w_h[0]
        for h in range(1, N_HEADS):
            isc = isc + jnp.maximum(sc[:, h * tq:(h + 1) * tq], 0.0) * w_h[h]
        isc = jnp.where(isc == 0.0, 0.0, isc)
        bits = pltpu.bitcast(isc, I32)
        key = bits ^ ((bits >> 31) & jnp.int32(0x7FFFFFFF))
        key_ref[c] = jnp.where((k0 + key_iota) <= row_pos, key, int_min)
        return carry

    lax.fori_loop(0, n_kv, score_body, 0)

    def count(pred):
        def block_count(c):
            ones = jnp.where(pred(key_ref[c], c * tk), 1.0, 0.0)
            return jnp.sum(ones.reshape(tk // 8, 8, tq), axis=0)

        def body(i, acc):
            return acc + block_count(2 * i) + block_count(2 * i + 1)

        acc = lax.fori_loop(0, n_kv // 2, body, jnp.zeros((8, tq), F32))
        odd = (n_kv & 1).astype(F32)
        acc = acc + odd * block_count(n_kv - 1)
        return jnp.sum(acc, axis=0, keepdims=True)

    kk = jnp.float32(topk)

    def bit_body(r, carry):
        prefix, n_ge, n_gt = carry
        cand_u = prefix | lax.shift_left(jnp.int32(1), 31 - r)
        cand_s = cand_u ^ int_min
        cnt = count(lambda blk, k0: blk >= cand_s)
        ok = cnt >= kk
        return jnp.where(ok, cand_u, prefix), jnp.where(ok, cnt, n_ge), jnp.where(ok, n_gt, cnt)

    n_all = jnp.full((1, tq), 1.0, F32) * (n_kv * tk).astype(F32)
    prefix, n_ge, n_gt = lax.fori_loop(
        0, 32, bit_body, (jnp.zeros((1, tq), I32), n_all, jnp.zeros((1, tq), F32)))
    thr = prefix ^ int_min
    r_need = kk - n_gt
    live = lax.broadcasted_iota(I32, (1, tq), 1) < t_valid
    any_tie = jnp.max(jnp.where(live, n_ge - kk, 0.0)) > 0.0
    y_ref[...] = jnp.full((1, tq), s_tot, I32)

    @pl.when(any_tie)
    def _():
        def ybody(r, y):
            cand = y | lax.shift_left(jnp.int32(1), idx_bits - 1 - r)
            cnt = count(lambda blk, k0: (blk == thr) & ((k0 + key_iota) < cand))
            return jnp.where(cnt < r_need, cand, y)
        y_ref[...] = lax.fori_loop(0, idx_bits, ybody, jnp.zeros((1, tq), I32))

    y_last = y_ref[...]

    def mask_fn(c, k0):
        blk = key_ref[c]
        pos = k0 + key_iota
        sel = (blk > thr) | ((blk == thr) & (pos <= y_last))
        return sel & (pos <= row_pos)

    qt = qt_ref[...]
    row_head = lax.broadcasted_iota(I32, (GROUP_W, 1), 0) // HEAD_W
    qst = jnp.concatenate([jnp.where(row_head == h, qt, 0.0) for h in range(N_HEADS)], axis=1).astype(BF16)
    _attn_core(qst, N_HEADS, k_ref, vt_ref, 0, n_kv, tq, tk, mask_fn, m_ref, l_ref, acc_ref, zero_masked=True)
    o_ref[...] = _normalised(acc_ref[0], l_ref[...], 0, tq).T


def _dsa(qt, k, vt, qit, ki3, kiwit, *, bsz, tq, tk, past, topk, t_valid):
    t = qt.shape[1] // bsz
    s_tot = k.shape[0] // bsz
    nq = t // tq
    idx_bits = int(math.ceil(math.log2(s_tot))) + 1
    kern = functools.partial(_dsa_kernel, tq=tq, tk=tk, past=past, s_tot=s_tot, topk=topk,
                             idx_bits=idx_bits, t_valid=t_valid)
    qspec = lambda w: pl.BlockSpec((w, tq), lambda b, i: (0, b * nq + i))
    return pl.pallas_call(
        kern, grid=(bsz, nq),
        in_specs=[qspec(GROUP_W),
                  pl.BlockSpec((s_tot, GROUP_W), lambda b, i: (b, 0)),
                  pl.BlockSpec((GROUP_W, s_tot), lambda b, i: (0, b)),
                  qspec(GROUP_W),
                  pl.BlockSpec((s_tot, 3 * D_IDX), lambda b, i: (b, 0)),
                  qspec(LANES)],
        out_specs=pl.BlockSpec((tq, GROUP_W), lambda b, i: (b * nq + i, 0)),
        out_shape=jax.ShapeDtypeStruct((bsz * t, GROUP_W), F32),
        scratch_shapes=[pltpu.VMEM((s_tot // tk, tk, tq), I32), pltpu.VMEM((1, tq), I32),
                        pltpu.VMEM((1, N_HEADS * tq), F32), pltpu.VMEM((1, N_HEADS * tq), F32),
                        pltpu.VMEM((1, GROUP_W, tq), F32)],
        compiler_params=_cparams(("parallel", "arbitrary")), name="dsa",
    )(qt, k, vt, qit, ki3, kiwit)


_HALO_C = 8
_HALO_D = 32


def _conv_kernel(*refs, tm, single_tile):
    if single_tile:
        (cb_ref, uc_ref, ud_ref, hc_ref, hd_ref, cw_ref, dw_ref, db_ref, lg_ref, lb_ref,
         oc_ref, od_ref, ec_ref, ed_ref) = refs
        pc_ref = pd_ref = None
    else:
        (cb_ref, uc_ref, ud_ref, pc_ref, pd_ref, hc_ref, hd_ref, cw_ref, dw_ref, db_ref, lg_ref, lb_ref,
         oc_ref, od_ref, ec_ref, ed_ref) = refs
    i = pl.program_id(1)

    @pl.when(i == 0)
    def _():
        ec_ref[0:_HALO_C] = hc_ref[0]
        ed_ref[0:_HALO_D] = hd_ref[0]

    if not single_tile:
        @pl.when(i > 0)
        def _():
            ec_ref[0:_HALO_C] = pc_ref[0]
            ed_ref[0:_HALO_D] = pd_ref[0]

    ec_ref[_HALO_C:_HALO_C + tm] = uc_ref[0]
    ed_ref[_HALO_D:_HALO_D + tm] = ud_ref[0]

    cw = cw_ref[...]
    acc = ec_ref[pl.ds(_HALO_C - (C_WIDTH - 1), tm), :] * cw[0:1]
    for w in range(1, C_WIDTH):
        acc = acc + ec_ref[pl.ds(_HALO_C - (C_WIDTH - 1) + w, tm), :] * cw[w:w + 1]
    oc_ref[0] = cb_ref[0] * acc

    dw = dw_ref[...]
    acc = ed_ref[pl.ds(_HALO_D - (D_WIDTH - 1), tm), :] * dw[0:1]
    for w in range(1, D_WIDTH):
        acc = acc + ed_ref[pl.ds(_HALO_D - (D_WIDTH - 1) + w, tm), :] * dw[w:w + 1]
    z = _layer_norm(acc + db_ref[...], lg_ref[...], lb_ref[...])
    od_ref[0] = z * _sigmoid(z)


def _convs(cb, uc, ud, hist_c, hist_d, c_w, d_w, d_b, ln_g, ln_b, *, tm):
    bsz, t, w = cb.shape
    nt = t // tm
    single = nt == 1
    main = pl.BlockSpec((1, tm, w), lambda b, i: (b, i, 0))
    prev_c = pl.BlockSpec((1, _HALO_C, w), lambda b, i: (b, jnp.maximum(i * (tm // _HALO_C) - 1, 0), 0))
    prev_d = pl.BlockSpec((1, _HALO_D, w), lambda b, i: (b, jnp.maximum(i * (tm // _HALO_D) - 1, 0), 0))
    hist_cs = pl.BlockSpec((1, _HALO_C, w), lambda b, i: (b, 0, 0))
    hist_ds = pl.BlockSpec((1, _HALO_D, w), lambda b, i: (b, 0, 0))
    full2 = lambda a: pl.BlockSpec(a.shape, lambda b, i: (0, 0))
    in_specs = [main, main, main] + ([] if single else [prev_c, prev_d]) + [hist_cs, hist_ds] + \
        [full2(c_w), full2(d_w), full2(d_b), full2(ln_g), full2(ln_b)]
    args = [cb, uc, ud] + ([] if single else [uc, ud]) + [hist_c, hist_d, c_w, d_w, d_b, ln_g, ln_b]
    kern = functools.partial(_conv_kernel, tm=tm, single_tile=single)
    return pl.pallas_call(
        kern, grid=(bsz, nt), in_specs=in_specs,
        out_specs=[main, main],
        out_shape=[jax.ShapeDtypeStruct((bsz, t, w), F32)] * 2,
        scratch_shapes=[pltpu.VMEM((_HALO_C + tm, w), F32), pltpu.VMEM((_HALO_D + tm, w), F32)],
        compiler_params=_cparams(("parallel", "arbitrary")), name="convs",
    )(*args)


def _mix_kernel(x_ref, oa_ref, ob_ref, oc_ref, od_ref, pw_ref, pwb_ref, wo_ref, g_ref, b_ref, h_ref, *, alpha):
    od = jnp.dot(od_ref[...].astype(BF16), pw_ref[...], preferred_element_type=F32) + pwb_ref[...]
    cat = jnp.concatenate([oa_ref[...], ob_ref[...], oc_ref[...], od], axis=1).astype(BF16)
    mix = jnp.dot(cat, wo_ref[...], preferred_element_type=F32)
    h_ref[...] = _layer_norm(alpha * x_ref[...] + mix, g_ref[...], b_ref[...])


def _mix(x, oa, ob, oc, od, pw, pwb, wo, g, b, *, tm, alpha):
    n, d = x.shape
    row = lambda w: pl.BlockSpec((tm, w), lambda i: (i, 0))
    full = lambda a: pl.BlockSpec(a.shape, lambda i: (0, 0))
    return pl.pallas_call(
        functools.partial(_mix_kernel, alpha=alpha), grid=(n // tm,),
        in_specs=[row(d), row(GROUP_W), row(GROUP_W), row(GROUP_W), row(GROUP_W),
                  full(pw), full(pwb), full(wo), full(g), full(b)],
        out_specs=row(d), out_shape=jax.ShapeDtypeStruct((n, d), F32),
        compiler_params=_cparams(("parallel",)), name="mix",
    )(x, oa, ob, oc, od, pw, pwb, wo, g, b)


def _staircase():
    return [(a, b) for a in range(PEER_TOPK) for b in range(PEER_TOPK) if (a + 1) * (b + 1) <= PEER_TOPK]


def _tree_rows(x, op):
    while x.shape[0] > 1:
        half = x.shape[0] // 2
        y = op(x[:half], x[half:2 * half])
        x = y if x.shape[0] % 2 == 0 else jnp.concatenate([y, x[2 * half:]], axis=0)
    return x


def _argmax_rows(x, row):
    v, i = x, row
    while v.shape[0] > 1:
        n = v.shape[0]
        half = n // 2
        pv = v[:2 * half].reshape((half, 2) + v.shape[1:])
        pi = i[:2 * half].reshape((half, 2) + i.shape[1:])
        take_b = pv[:, 1] > pv[:, 0]
        nv = jnp.where(take_b, pv[:, 1], pv[:, 0])
        ni = jnp.where(take_b, pi[:, 1], pi[:, 0])
        if n % 2:
            nv = jnp.concatenate([nv, v[2 * half:]], axis=0)
            ni = jnp.concatenate([ni, i[2 * half:]], axis=0)
        v, i = nv, ni
    return v, i


def _extract_top(x, payload, n_out, n_rows):
    row = lax.broadcasted_iota(I32, x.shape, 0).astype(F32)
    vals, idxs, pays = [], [], []
    for _ in range(n_out):
        m, idx = _argmax_rows(x, row)
        hit = row == idx
        vals.append(m)
        idxs.append(idx)
        if payload is not None:
            pays.append(_tree_rows(jnp.where(hit, payload, -1.0), jnp.maximum))
        x = jnp.where(hit, -jnp.inf, x)
    return vals, idxs, pays


def _peer_select_kernel(h_ref, wq_ref, sk_ref, eid_ref, g_ref, sv_ref, si_ref, *, tm):
    ht = h_ref[...].T.astype(BF16)
    n_groups = 2 * PEER_HEADS
    nt = tm // LANES

    def to_tiles(x):
        return pltpu.einshape("trl->rtl", jnp.stack([x[:, LANES * t:LANES * (t + 1)] for t in range(nt)], axis=0))

    def from_tiles(y):
        z = pltpu.einshape("rtl->trl", y)
        return jnp.concatenate([z[t] for t in range(nt)], axis=1)

    def group_body(g, carry):
        r0 = pl.multiple_of(g * N_KEYS, N_KEYS)
        qg = jnp.dot(wq_ref[pl.ds(r0, N_KEYS), :], ht, preferred_element_type=F32)
        st = jnp.dot(sk_ref[g % 2], qg.astype(BF16), preferred_element_type=F32)
        vals, idxs, _ = _extract_top(to_tiles(st), None, PEER_TOPK, N_KEYS)
        sv_ref[g] = jnp.concatenate(vals, axis=0)
        si_ref[g] = jnp.concatenate(idxs, axis=0)
        return carry

    lax.fori_loop(0, n_groups, group_body, 0)

    pairs = _staircase()

    def head_body(hd, carry):
        sv0 = sv_ref[2 * hd]; sv1 = sv_ref[2 * hd + 1]
        si0 = si_ref[2 * hd]; si1 = si_ref[2 * hd + 1]
        cand = jnp.concatenate([sv0[a:a + 1] + sv1[b:b + 1] for a, b in pairs], axis=0)
        cand_e = jnp.concatenate([si0[a:a + 1] * float(N_KEYS) + si1[b:b + 1] for a, b in pairs], axis=0)
        vals, _, pays = _extract_top(cand, cand_e, PEER_TOPK, len(pairs))
        fv = jnp.concatenate(vals, axis=0)
        e = jnp.exp(fv - fv[0:1])
        gate = e / jnp.sum(e, axis=0, keepdims=True)
        r0 = pl.multiple_of(hd * PEER_TOPK, PEER_TOPK)
        eid_ref[pl.ds(r0, PEER_TOPK), :] = from_tiles(jnp.concatenate(pays, axis=0)).astype(I32)
        g_ref[pl.ds(r0, PEER_TOPK), :] = from_tiles(gate)
        return carry

    lax.fori_loop(0, PEER_HEADS, head_body, 0)


_SELECT_TOKENS = 8 * LANES


def _peer_select(h, wq_t, sk, *, tm):
    n, d = h.shape
    n_slots = PEER_HEADS * PEER_TOPK
    return pl.pallas_call(
        functools.partial(_peer_select_kernel, tm=tm), grid=(n // tm,),
        in_specs=[pl.BlockSpec((tm, d), lambda i: (i, 0)),
                  pl.BlockSpec(wq_t.shape, lambda i: (0, 0)),
                  pl.BlockSpec(sk.shape, lambda i: (0, 0, 0))],
        out_specs=[pl.BlockSpec((n_slots, tm), lambda i: (0, i)),
                   pl.BlockSpec((n_slots, tm), lambda i: (0, i))],
        out_shape=[jax.ShapeDtypeStruct((n_slots, n), I32), jax.ShapeDtypeStruct((n_slots, n), F32)],
        scratch_shapes=[pltpu.VMEM((2 * PEER_HEADS, PEER_TOPK, tm // LANES, LANES), F32),
                        pltpu.VMEM((2 * PEER_HEADS, PEER_TOPK, tm // LANES, LANES), F32)],
        compiler_params=_cparams(("parallel",)), name="peer_select",
    )(h, wq_t, sk)


_GATE_TOKENS = 16


def _peer_gates_kernel(eid_ref, g_ref, w_ref, *, tm):
    sub = lax.broadcasted_iota(I32, (N_KEYS, PEER_HEADS * PEER_TOPK), 0)

    def token_gates(t):
        e = eid_ref[pl.ds(t, 1), :]
        g = g_ref[pl.ds(t, 1), :]
        oh_i = jnp.where((e >> 7) == sub, 1.0, 0.0).astype(BF16)
        g_j = jnp.where((e & (N_KEYS - 1)) == sub, g, 0.0).astype(BF16)
        return lax.dot_general(oh_i, g_j, _NT, preferred_element_type=F32)

    def body(b, carry):
        t0 = pl.multiple_of(b * _GATE_TOKENS, _GATE_TOKENS)
        w = jnp.stack([token_gates(t0 + k) for k in range(_GATE_TOKENS)], axis=0)
        w_ref[:, pl.ds(t0, _GATE_TOKENS), :] = pltpu.einshape("tij->itj", w).astype(BF16)
        return carry

    lax.fori_loop(0, tm // _GATE_TOKENS, body, 0)


def _peer_gates(eid, gate, *, tm):
    n, n_slots = eid.shape
    return pl.pallas_call(
        functools.partial(_peer_gates_kernel, tm=tm), grid=(n // tm,),
        in_specs=[pl.BlockSpec((tm, n_slots), lambda i: (i, 0)),
                  pl.BlockSpec((tm, n_slots), lambda i: (i, 0))],
        out_specs=pl.BlockSpec((N_KEYS, tm, N_KEYS), lambda i: (0, i, 0)),
        out_shape=jax.ShapeDtypeStruct((N_KEYS, n, N_KEYS), BF16),
        compiler_params=_cparams(("parallel",)), name="peer_gates",
    )(eid, gate)


def _gelu(x):
    return 0.5 * x * (1.0 + lax.erf(x * (2.0 ** -0.5)))


def _peer_dense_kernel(h_ref, w_ref, u_ref, v_ref, f_ref):
    @pl.when(pl.program_id(1) == 0)
    def _():
        f_ref[...] = jnp.zeros(f_ref.shape, F32)

    act = _gelu(lax.dot_general(h_ref[...], u_ref[...].astype(BF16), _NT, preferred_element_type=F32))
    y = jnp.concatenate(
        [w_ref[i].astype(F32) * act[:, i * N_KEYS:(i + 1) * N_KEYS] for i in range(w_ref.shape[0])],
        axis=1).astype(BF16)
    f_ref[...] += jnp.dot(y, v_ref[...].astype(BF16), preferred_element_type=F32)


def _peer_dense(hb, wmat, u, v, *, tm, te):
    n, d = hb.shape
    n_exp = u.shape[0]
    return pl.pallas_call(
        _peer_dense_kernel, grid=(n // tm, n_exp // te),
        in_specs=[pl.BlockSpec((tm, d), lambda i, e: (i, 0)),
                  pl.BlockSpec((te // N_KEYS, tm, N_KEYS), lambda i, e: (e, i, 0)),
                  pl.BlockSpec((te, d), lambda i, e: (e, 0)),
                  pl.BlockSpec((te, d), lambda i, e: (e, 0))],
        out_specs=pl.BlockSpec((tm, d), lambda i, e: (i, 0)),
        out_shape=jax.ShapeDtypeStruct((n, d), F32),
        compiler_params=_cparams(("parallel", "arbitrary")), name="peer_dense",
    )(hb, wmat, u, v)


def _final_kernel(h_ref, f_ref, p_ref, g_ref, b_ref, wg_ref, bg_ref, wp_ref, y_ref, *, alpha):
    h2 = _layer_norm(alpha * h_ref[...] + f_ref[...], g_ref[...], b_ref[...])
    gate = _sigmoid(jnp.dot(h2.astype(BF16), wg_ref[...], preferred_element_type=F32) + bg_ref[...])
    emb = jnp.dot(p_ref[...].astype(BF16), wp_ref[...], preferred_element_type=F32)
    y_ref[...] = h2 + gate * emb


def _final(h, f, p, g, b, wg, bg, wp, *, tm, alpha):
    n, d = h.shape
    row = lambda w: pl.BlockSpec((tm, w), lambda i: (i, 0))
    full = lambda a: pl.BlockSpec(a.shape, lambda i: (0, 0))
    return pl.pallas_call(
        functools.partial(_final_kernel, alpha=alpha), grid=(n // tm,),
        in_specs=[row(d), row(d), row(p.shape[1]), full(g), full(b), full(wg), full(bg), full(wp)],
        out_specs=row(d), out_shape=jax.ShapeDtypeStruct((n, d), F32),
        compiler_params=_cparams(("parallel",)), name="final",
    )(h, f, p, g, b, wg, bg, wp)


def _pack_w_in(w_in):
    d = w_in.shape[0]
    cols = [w_in[:, 0:1792], w_in[:, 1792:1860], jnp.zeros((d, 60), w_in.dtype), w_in[:, 1860:3140]]
    nat = jnp.concatenate(cols, axis=1).astype(BF16)
    seg = lambda name, width=GROUP_W: nat[:, _SEG[name]:_SEG[name] + width]
    tr = jnp.concatenate([seg("qa"), seg("qb"), seg("qi"), seg("kiwi", 128), seg("va"), seg("vb")], axis=1).T
    return nat, tr


def _row(a):
    return a.reshape(1, -1)


def _tile_rows(n, pref):
    t = pref
    while n % t:
        t //= 2
    return t


def _prep_weights(lw, depth):
    (w_in, a_lambda, a_subln_g, c_conv_w, d_conv_w, d_conv_b, d_ln_g, d_ln_b, d_pw_w, d_pw_b,
     w_out, ln1_g, ln1_b, peer_wq, peer_subkeys, peer_u, peer_v, ln2_g, ln2_b,
     ple_w, ple_gate_w, ple_gate_b) = lw
    head_of_lane = jnp.arange(GROUP_W) // HEAD_W
    gmean = (head_of_lane[:, None] == head_of_lane[None, :]).astype(BF16) * (1.0 / HEAD_W)
    w_nat, w_tr = _pack_w_in(w_in)
    return dict(
        w_in=w_nat, w_in_t=w_tr, a_lambda=a_lambda, a_g=_row(jnp.tile(a_subln_g, N_HEADS)),
        gmean=gmean.astype(BF16),
        c_w=c_conv_w, d_w=d_conv_w, d_b=_row(d_conv_b), d_ln_g=_row(d_ln_g), d_ln_b=_row(d_ln_b),
        d_pw=d_pw_w.astype(BF16), d_pwb=_row(d_pw_b), w_out=w_out.astype(BF16),
        ln1_g=_row(ln1_g), ln1_b=_row(ln1_b),
        wq_t=peer_wq.T.astype(BF16), sk=peer_subkeys.astype(BF16),
        u=peer_u, v=peer_v,
        ln2_g=_row(ln2_g), ln2_b=_row(ln2_b),
        ple_w=ple_w.astype(BF16), wg=ple_gate_w.astype(BF16), bg=_row(ple_gate_b),
        alpha=float((2 * depth) ** 0.25),
    )


def _index_keys(ki):
    k_hi = ki.astype(BF16)
    k_lo = (ki - k_hi.astype(F32)).astype(BF16)
    return jnp.concatenate([k_hi, k_hi, k_lo], axis=-1)


def _token_tail(w, x, oa, ob, oc, od, valid_rows, p):
    n = x.shape[0]
    h = _mix(x, oa, ob, oc, od, w["d_pw"], w["d_pwb"], w["w_out"], w["ln1_g"], w["ln1_b"],
             tm=_tile_rows(n, 512), alpha=w["alpha"])
    if valid_rows is not None:
        bsz, t_pad, t = valid_rows
        h = h.reshape(bsz, t_pad, -1)[:, :t].reshape(bsz * t, -1)
    n = h.shape[0]
    n_sel = -(-n // _SELECT_TOKENS) * _SELECT_TOKENS
    h_sel = h if n_sel == n else jnp.pad(h, ((0, n_sel - n), (0, 0)))
    eid_t, gate_t = _peer_select(h_sel, w["wq_t"], w["sk"], tm=_SELECT_TOKENS)
    n_gate = -(-n // _GATE_TOKENS) * _GATE_TOKENS
    wmat = _peer_gates(eid_t[:, :n_gate].T, gate_t[:, :n_gate].T, tm=_tile_rows(n_gate, 128))
    f = _peer_dense(h.astype(BF16), wmat, w["u"], w["v"], tm=_tile_rows(n, 1024), te=8 * N_KEYS)
    return _final(h, f, p, w["ln2_g"], w["ln2_b"], w["wg"], w["bg"], w["ple_w"],
                  tm=_tile_rows(n, 512), alpha=w["alpha"])


def _layer(layer_idx, depth, xp, xs, caches, state_c, state_d, page_table, pp, ps, lw):
    w = _prep_weights(lw, depth)
    bp, tp, d = xp.shape
    bs, ts, _ = xs.shape
    ts_pad = 8
    n_pages = page_table.shape[1]
    page = caches[0].shape[3]
    past = n_pages * page
    lam_init = 0.8 - 0.6 * math.exp(-0.3 * (layer_idx + 1))

    pr = _project(xp.reshape(bp * tp, d), w["w_in"], w["w_in_t"], _tile_rows(bp * tp, 512), feature_major=True)
    tq = _tile_rows(tp, 256)
    tk = _tile_rows(tp, 256)
    oa = _attention_a(pr["qat"], pr["kab"], pr["vat"], w["a_lambda"], w["a_g"], w["gmean"],
                      bsz=bp, tq=tq, tk=tk, past=0, lam_init=lam_init)
    ob = _dsa(pr["qbt"], pr["kbb"], pr["vbt"], pr["qit"], _index_keys(pr["kiwi"][:, :D_IDX]), pr["kiwit"],
              bsz=bp, tq=tq, tk=tk, past=0, topk=min(DSA_TOPK, tp // 4), t_valid=tq)
    r3p = lambda a: a.reshape(bp, tp, a.shape[-1])
    zc = jnp.zeros((bp, _HALO_C, GROUP_W), F32)
    zd = jnp.zeros((bp, _HALO_D, GROUP_W), F32)
    oc, od = _convs(r3p(pr["cb"]), r3p(pr["uc"]), r3p(pr["ud"]), zc, zd, w["c_w"], w["d_w"], w["d_b"],
                    w["d_ln_g"], w["d_ln_b"], tm=_tile_rows(tp, 512))
    yp = _token_tail(w, xp.reshape(bp * tp, d), oa, ob, oc.reshape(bp * tp, -1), od.reshape(bp * tp, -1),
                     None, pp.reshape(bp * tp, -1)).reshape(bp, tp, d)
    new_p = (r3p(pr["ka"]), r3p(pr["va"]), r3p(pr["kb"]), r3p(pr["vb"]), r3p(pr["kiwi"])[:, :, :D_IDX],
             jnp.concatenate([zc, r3p(pr["uc"])], axis=1)[:, -(C_WIDTH - 1):],
             jnp.concatenate([zd, r3p(pr["ud"])], axis=1)[:, -(D_WIDTH - 1):])

    xs_pad = jnp.pad(xs, ((0, 0), (0, ts_pad - ts), (0, 0))).reshape(bs * ts_pad, d)
    sr = _project(xs_pad, w["w_in"], w["w_in_t"], _tile_rows(bs * ts_pad, 512), feature_major=False)
    r3s = lambda a: a.reshape(bs, ts_pad, a.shape[-1])
    news = [r3s(sr["ka"]), r3s(sr["va"]), r3s(sr["kb"]), r3s(sr["vb"]), r3s(sr["kiwi"])[:, :, :D_IDX]]
    oa_s, ob_s = _sample_attention(page_table, caches, layer_idx, news, r3s(sr["qa"]), r3s(sr["qb"]), r3s(sr["qi"]),
                                   r3s(sr["kiwi"]), w["a_lambda"], w["a_g"], w["gmean"],
                                   topk=min(DSA_TOPK, (past + ts) // 4), lam_init=lam_init, t_valid=ts)
    unpad = lambda a: a.reshape(bs * ts_pad, GROUP_W)
    hist_c = jnp.concatenate([jnp.zeros((bs, _HALO_C - (C_WIDTH - 1), GROUP_W), F32), state_c], axis=1)
    hist_d = jnp.concatenate([jnp.zeros((bs, _HALO_D - (D_WIDTH - 1), GROUP_W), F32), state_d], axis=1)
    oc_s, od_s = _convs(r3s(sr["cb"]), r3s(sr["uc"]), r3s(sr["ud"]), hist_c, hist_d, w["c_w"], w["d_w"],
                        w["d_b"], w["d_ln_g"], w["d_ln_b"], tm=ts_pad)
    ys = _token_tail(w, xs_pad, unpad(oa_s), unpad(ob_s), oc_s.reshape(bs * ts_pad, -1),
                     od_s.reshape(bs * ts_pad, -1), (bs, ts_pad, ts), ps.reshape(bs * ts, -1)).reshape(bs, ts, d)
    cut = lambda a: r3s(a)[:, :ts]
    new_s = (cut(sr["ka"]), cut(sr["va"]), cut(sr["kb"]), cut(sr["vb"]), cut(sr["kiwi"])[:, :, :D_IDX],
             jnp.concatenate([state_c, cut(sr["uc"])], axis=1)[:, -(C_WIDTH - 1):],
             jnp.concatenate([state_d, cut(sr["ud"])], axis=1)[:, -(D_WIDTH - 1):])
    return yp, ys, new_p, new_s


def kernel(x_prompt, x_sample, cache_a_k, cache_a_v, cache_b_k, cache_b_v, cache_b_kidx, state_c_conv, state_d_conv, page_table, p_prompt, p_sample, w_in, a_lambda, a_subln_g, c_conv_w, d_conv_w, d_conv_b, d_ln_g, d_ln_b, d_pw_w, d_pw_b, w_out, ln1_g, ln1_b, peer_wq, peer_subkeys, peer_u, peer_v, ln2_g, ln2_b, ple_w, ple_gate_w, ple_gate_b):
    depth = w_in.shape[0]
    n_pool, page = cache_a_k.shape[1], cache_a_k.shape[2]
    feature_major = lambda c: jnp.moveaxis(c, 2, -1).reshape(depth, n_pool, -1, page)
    caches = [feature_major(c) for c in (cache_a_k, cache_a_v, cache_b_k, cache_b_v, cache_b_kidx)]
    xp, xs = x_prompt, x_sample
    news_p, news_s = [], []
    for l in range(depth):
        lw = (w_in[l], a_lambda[l], a_subln_g[l], c_conv_w[l], d_conv_w[l], d_conv_b[l], d_ln_g[l], d_ln_b[l],
              d_pw_w[l], d_pw_b[l], w_out[l], ln1_g[l], ln1_b[l], peer_wq[l], peer_subkeys[l], peer_u[l],
              peer_v[l], ln2_g[l], ln2_b[l], ple_w[l], ple_gate_w[l], ple_gate_b[l])
        xp, xs, new_p, new_s = _layer(l, depth, xp, xs, caches, state_c_conv[l], state_d_conv[l],
                                      page_table, p_prompt[l], p_sample[l], lw)
        news_p.append(new_p)
        news_s.append(new_s)

    bp, tp, _ = x_prompt.shape
    bs, ts, _ = x_sample.shape
    shapes = [(N_HEADS, 2, 32), (N_HEADS, HEAD_W), (N_HEADS, HEAD_W), (N_HEADS, HEAD_W), (D_IDX,)]

    def stack(news, i, lead):
        a = jnp.stack([n[i] for n in news], axis=0)
        return a.reshape((depth,) + lead + shapes[i]) if i < 5 else a

    outs = [xp, xs]
    for i in range(7):
        outs.append(stack(news_p, i, (bp, tp)))
        outs.append(stack(news_s, i, (bs, ts)))
    return tuple(outs)
```

```python
import functools
import math

import jax
import jax.numpy as jnp
from jax import lax
from jax.experimental import pallas as pl
from jax.experimental.pallas import tpu as pltpu

F32 = jnp.float32
BF16 = jnp.bfloat16
I32 = jnp.int32

LN_EPS = 1e-5
N_HEADS = 4
HEAD_W = 64
GROUP_W = 256
D_IDX = 64
DSA_TOPK = 256
C_WIDTH = 3
D_WIDTH = 31
PEER_HEADS = 8
PEER_TOPK = 16
N_KEYS = 128
LANES = 128
NEG_BIG = -1e30
INT_MIN = -2 ** 31
VMEM_LIMIT = 56 * 1024 * 1024

_SEG = dict(qa=0, ka=256, va=512, qb=768, kb=1024, vb=1280, qi=1536, kiwi=1792,
            cb=1920, cc=2176, cx=2432, da=2688, dg=2944)
_N_PACKED = 3200
_SEG_T = dict(qa=0, qb=256, qi=512, kiwi=768, va=896, vb=1152)
_N_PACKED_T = 1408

_NT = (((1,), (1,)), ((), ()))


def _cparams(sem):
    return pltpu.CompilerParams(dimension_semantics=sem, vmem_limit_bytes=VMEM_LIMIT)


def _split_bf16(x):
    hi = x.astype(BF16)
    lo = (x - hi.astype(F32)).astype(BF16)
    return hi, lo


def _layer_norm(x, g, b):
    mu = jnp.mean(x, axis=-1, keepdims=True)
    xc = x - mu
    var = jnp.mean(xc * xc, axis=-1, keepdims=True)
    return xc * lax.rsqrt(var + LN_EPS) * g + b


def _sigmoid(x):
    return 1.0 / (1.0 + jnp.exp(-x))


_PROJ_COMMON = ("ka", "va", "kb", "vb", "kiwi", "cb", "uc", "ud")
_PROJ_FEATURE_MAJOR = ("kab", "kbb", "qat", "qbt", "qit", "kiwit", "vat", "vbt")
_PROJ_ROW_MAJOR = ("qa", "qb", "qi")


def _proj_kernel(x_ref, w_ref, wt_ref, *out_refs, names, scale_a, scale_b):
    o = dict(zip(names, out_refs))
    xb = x_ref[...].astype(BF16)

    def seg(name, width=GROUP_W):
        off = _SEG[name]
        return jnp.dot(xb, w_ref[:, off:off + width], preferred_element_type=F32)

    def seg_t(name, width=GROUP_W):
        off = _SEG_T[name]
        return lax.dot_general(wt_ref[off:off + width, :], xb, _NT, preferred_element_type=F32)

    ka = seg("ka"); kb = seg("kb")
    o["ka"][...] = ka; o["kb"][...] = kb
    o["va"][...] = seg("va"); o["vb"][...] = seg("vb")
    o["kiwi"][...] = seg("kiwi", LANES)
    o["cb"][...] = seg("cb")
    o["uc"][...] = seg("cc") * seg("cx")
    o["ud"][...] = seg("da") * _sigmoid(seg("dg"))
    if "qat" in o:
        o["kab"][...] = ka.astype(BF16); o["kbb"][...] = kb.astype(BF16)
        o["qat"][...] = seg_t("qa") * scale_a
        o["qbt"][...] = seg_t("qb") * scale_b
        o["qit"][...] = seg_t("qi")
        o["kiwit"][...] = seg_t("kiwi", LANES)
        o["vat"][...] = seg_t("va").astype(BF16)
        o["vbt"][...] = seg_t("vb").astype(BF16)
    else:
        o["qa"][...] = seg("qa") * scale_a
        o["qb"][...] = seg("qb") * scale_b
        o["qi"][...] = seg("qi")


def _project(x, w_packed, wt_packed, tm, feature_major):
    n, d = x.shape
    nat = lambda w, dt=F32: (jax.ShapeDtypeStruct((n, w), dt), pl.BlockSpec((tm, w), lambda i: (i, 0)))
    tr = lambda w, dt=F32: (jax.ShapeDtypeStruct((w, n), dt), pl.BlockSpec((w, tm), lambda i: (0, i)))
    outs = [nat(256), nat(256), nat(256), nat(256), nat(LANES), nat(256), nat(256), nat(256)]
    if feature_major:
        names = _PROJ_COMMON + _PROJ_FEATURE_MAJOR
        outs += [nat(256, BF16), nat(256, BF16), tr(256), tr(256), tr(256), tr(LANES), tr(256, BF16), tr(256, BF16)]
    else:
        names = _PROJ_COMMON + _PROJ_ROW_MAJOR
        outs += [nat(256), nat(256), nat(256)]
    kern = functools.partial(_proj_kernel, names=names, scale_a=float(32 ** -0.5), scale_b=float(HEAD_W ** -0.5))
    res = pl.pallas_call(
        kern, grid=(n // tm,),
        in_specs=[pl.BlockSpec((tm, d), lambda i: (i, 0)),
                  pl.BlockSpec((d, _N_PACKED), lambda i: (0, 0)),
                  pl.BlockSpec((_N_PACKED_T, d), lambda i: (0, 0))],
        out_specs=[o[1] for o in outs], out_shape=[o[0] for o in outs],
        compiler_params=_cparams(("parallel",)), name="proj",
    )(x, w_packed, wt_packed)
    return dict(zip(names, res))


def _lane_heads(o, l, base, rows, lane_head):
    out = None
    for h in range(N_HEADS):
        r0 = (base + h) * rows
        part = jnp.where(lane_head == h, o[r0:r0 + rows] / l[r0:r0 + rows], 0.0)
        out = part if out is None else out + part
    return out


def _sample_attn_kernel(pt_ref, *refs, n_pages, page, rows, past, topk, idx_bits, lam_init, t_valid):
    npg = n_pages
    ak, av, bk, bv, kic = (refs[i * npg:(i + 1) * npg] for i in range(5))
    (nak, nav, nbk, nbv, nki, qa_ref, qb_ref, qi_ref, kiwi_ref, lam_ref, g_ref, gm_ref,
     oa_ref, ob_ref, y_ref) = refs[5 * npg:]
    nblk = npg + 1
    s_tot = nblk * page
    lane_head = lax.broadcasted_iota(I32, (1, GROUP_W), 1) // HEAD_W
    col_last = past + lax.broadcasted_iota(I32, (1, page), 1)
    t_row = lax.broadcasted_iota(I32, (rows, 1), 0)
    int_min = jnp.int32(INT_MIN)

    def blocks(page_refs, new_ref, dtype):
        new = new_ref[0]
        pad = jnp.zeros((page - rows, new.shape[-1]), F32)
        return [r[0, 0].astype(dtype) for r in page_refs] + [jnp.concatenate([new, pad], axis=0).T.astype(dtype)]

    def softmax_pv(q_stack, n_groups, k_blocks, v_blocks, masks):
        def masked(s, mk, fill):
            if mk is None:
                return s
            s3 = jnp.where(mk[None], s.reshape(n_groups, rows, page), fill)
            return s3.reshape(n_groups * rows, page)
        s = [masked(jnp.dot(q_stack, kb, preferred_element_type=F32), mk, NEG_BIG)
             for kb, mk in zip(k_blocks, masks)]
        m = s[0].max(axis=1, keepdims=True)
        for sj in s[1:]:
            m = jnp.maximum(m, sj.max(axis=1, keepdims=True))
        l = None
        o = None
        for sj, vb, mk in zip(s, v_blocks, masks):
            p = masked(jnp.exp(sj - m), mk, 0.0)
            lj = p.sum(axis=1, keepdims=True)
            oj = lax.dot_general(p.astype(BF16), vb, _NT, preferred_element_type=F32)
            l = lj if l is None else l + lj
            o = oj if o is None else o + oj
        return o, l

    causal_last = col_last <= (past + t_row)

    qa = qa_ref[0]
    lane_grp = lax.broadcasted_iota(I32, (1, GROUP_W), 1) // 32
    qa_stack = jnp.concatenate(
        [jnp.where(lane_grp == (2 * h + c), qa, 0.0) for c in range(2) for h in range(N_HEADS)],
        axis=0).astype(BF16)
    o, l = softmax_pv(qa_stack, 2 * N_HEADS, blocks(ak, nak, BF16), blocks(av, nav, BF16),
                      [None] * npg + [causal_last])
    o0 = _lane_heads(o, l, 0, rows, lane_head)
    o1 = _lane_heads(o, l, N_HEADS, rows, lane_head)
    lv = lam_ref[...]
    lam = (jnp.exp(jnp.sum(lv[0:1] * lv[1:2], axis=1, keepdims=True))
           - jnp.exp(jnp.sum(lv[2:3] * lv[3:4], axis=1, keepdims=True)) + lam_init)
    od = o0 - lam * o1
    hi, lo = _split_bf16(od * od)
    ms = (jnp.dot(hi, gm_ref[...], preferred_element_type=F32)
          + jnp.dot(lo, gm_ref[...], preferred_element_type=F32))
    oa_ref[0] = od * lax.rsqrt(ms + LN_EPS) * g_ref[...] * (1.0 - lam_init)

    qi = qi_ref[0]
    qi_rows = jnp.concatenate([qi[:, h * D_IDX:(h + 1) * D_IDX] for h in range(N_HEADS)], axis=0)
    q_hi, q_lo = _split_bf16(qi_rows)
    kiwi = kiwi_ref[0]
    w_h = [kiwi[:, D_IDX + h:D_IDX + h + 1] for h in range(N_HEADS)]
    key_blocks = []
    for j, kib in enumerate(blocks(kic, nki, F32)):
        k_hi, k_lo = _split_bf16(kib)
        sc = (jnp.dot(q_hi, k_hi, preferred_element_type=F32)
              + jnp.dot(q_lo, k_hi, preferred_element_type=F32)
              + jnp.dot(q_hi, k_lo, preferred_element_type=F32))
        isc = jnp.maximum(sc[0:rows], 0.0) * w_h[0]
        for h in range(1, N_HEADS):
            isc = isc + jnp.maximum(sc[h * rows:(h + 1) * rows], 0.0) * w_h[h]
        isc = jnp.where(isc == 0.0, 0.0, isc)
        bits = pltpu.bitcast(isc, I32)
        key = bits ^ ((bits >> 31) & jnp.int32(0x7FFFFFFF))
        if j == npg:
            key = jnp.where(causal_last, key, int_min)
        key_blocks.append(key)
    keys = jnp.concatenate(key_blocks, axis=1)
    cols = lax.broadcasted_iota(I32, (1, s_tot), 1)
    kk = jnp.float32(topk)
    count = lambda pred: jnp.sum(jnp.where(pred, 1.0, 0.0), axis=1, keepdims=True)

    def bit_body(r, prefix):
        b1 = lax.shift_left(jnp.int32(1), 31 - 2 * r)
        b0 = lax.shift_left(jnp.int32(1), 30 - 2 * r)
        enough = lambda cand_u: count(keys >= (cand_u ^ int_min)) >= kk
        c10 = prefix | b1
        c11 = c10 | b0
        c01 = prefix | b0
        return jnp.where(enough(c10), jnp.where(enough(c11), c11, c10), jnp.where(enough(c01), c01, prefix))

    thr = lax.fori_loop(0, 16, bit_body, jnp.zeros((rows, 1), I32)) ^ int_min
    r_need = kk - count(keys > thr)
    any_tie = jnp.max(jnp.where(t_row < t_valid, count(keys >= thr) - kk, 0.0)) > 0.0
    y_ref[...] = jnp.full((rows, 1), s_tot, I32)

    @pl.when(any_tie)
    def _():
        def ybody(r, y):
            cand = y | lax.shift_left(jnp.int32(1), idx_bits - 1 - r)
            return jnp.where(count((keys == thr) & (cols < cand)) < r_need, cand, y)
        y_ref[...] = lax.fori_loop(0, idx_bits, ybody, jnp.zeros((rows, 1), I32))

    sel = ((keys > thr) | ((keys == thr) & (cols <= y_ref[...]))) & (cols <= (past + t_row))
    qb = qb_ref[0]
    qb_stack = jnp.concatenate([jnp.where(lane_head == h, qb, 0.0) for h in range(N_HEADS)], axis=0).astype(BF16)
    o, l = softmax_pv(qb_stack, N_HEADS, blocks(bk, nbk, BF16), blocks(bv, nbv, BF16),
                      [sel[:, j * page:(j + 1) * page] for j in range(nblk)])
    ob_ref[0] = _lane_heads(o, l, 0, rows, lane_head)


def _sample_attention(page_table, caches, layer, news, qa, qb, qi, kiwi, a_lambda, g_full, gmean,
                      *, topk, lam_init, t_valid):
    bsz, n_pages = page_table.shape
    page = caches[0].shape[3]
    rows = qa.shape[1]
    past = n_pages * page
    idx_bits = int(math.ceil(math.log2((n_pages + 1) * page))) + 1
    page_spec = lambda c, j: pl.BlockSpec((1, 1, c, page), lambda b, pt: (layer, pt[b, j], 0, 0))
    seq_spec = lambda c: pl.BlockSpec((1, rows, c), lambda b, pt: (b, 0, 0))
    full2 = lambda a: pl.BlockSpec(a.shape, lambda b, pt: (0, 0))
    in_specs, args = [], []
    for cache in caches:
        for j in range(n_pages):
            in_specs.append(page_spec(cache.shape[2], j))
            args.append(cache)
    for a in list(news) + [qa, qb, qi, kiwi]:
        in_specs.append(seq_spec(a.shape[-1]))
        args.append(a)
    for a in (a_lambda, g_full, gmean):
        in_specs.append(full2(a))
        args.append(a)
    kern = functools.partial(_sample_attn_kernel, n_pages=n_pages, page=page, rows=rows, past=past, topk=topk,
                             idx_bits=idx_bits, lam_init=lam_init, t_valid=t_valid)
    return pl.pallas_call(
        kern,
        grid_spec=pltpu.PrefetchScalarGridSpec(
            num_scalar_prefetch=1, grid=(bsz,), in_specs=in_specs,
            out_specs=[seq_spec(GROUP_W), seq_spec(GROUP_W)],
            scratch_shapes=[pltpu.VMEM((rows, 1), I32)]),
        out_shape=[jax.ShapeDtypeStruct((bsz, rows, GROUP_W), F32)] * 2,
        compiler_params=_cparams(("parallel",)), name="sample_attn",
    )(page_table, *args)


def _attn_core(qst, n_groups, k_ref, vt_ref, n_full, n_kv, tq, tk, mask_fn, m_ref, l_ref, acc_ref, zero_masked):
    n_comp = n_groups // N_HEADS
    row_head = lax.broadcasted_iota(I32, (GROUP_W, 1), 0) // HEAD_W
    m_ref[...] = jnp.full(m_ref.shape, NEG_BIG, F32)
    l_ref[...] = jnp.zeros(l_ref.shape, F32)
    acc_ref[...] = jnp.zeros(acc_ref.shape, F32)
    grp = lambda a, g: a[:, g * tq:(g + 1) * tq]

    def step(c, n_blk, masked):
        width = n_blk * tk
        k0 = pl.multiple_of(c * tk, tk)
        kblk = k_ref[pl.ds(k0, width), :]
        vt = vt_ref[:, pl.ds(k0, width)]
        masks = [mask_fn(c + i, k0 + i * tk) for i in range(n_blk)] if masked else None

        def apply_masks(a, fill):
            parts = [jnp.where(masks[i], a[i * tk:(i + 1) * tk], fill) for i in range(n_blk)]
            return parts[0] if n_blk == 1 else jnp.concatenate(parts, axis=0)

        m_old = m_ref[...]
        l_old = l_ref[...]
        m_new, l_new, alpha, pb = [], [], [], []
        for g in range(n_groups):
            s = jnp.dot(kblk, grp(qst, g), preferred_element_type=F32)
            if masked:
                s = apply_masks(s, NEG_BIG)
            m_g = jnp.maximum(grp(m_old, g), jnp.max(s, axis=0, keepdims=True))
            a_g = jnp.exp(grp(m_old, g) - m_g)
            p = jnp.exp(s - m_g)
            if masked and zero_masked:
                p = apply_masks(p, 0.0)
            m_new.append(m_g)
            alpha.append(a_g)
            l_new.append(a_g * grp(l_old, g) + jnp.sum(p, axis=0, keepdims=True))
            pb.append(p.astype(BF16))
        m_ref[...] = jnp.concatenate(m_new, axis=1)
        l_ref[...] = jnp.concatenate(l_new, axis=1)
        for comp in range(n_comp):
            for h in range(N_HEADS):
                g = comp * N_HEADS + h
                rows = slice(h * HEAD_W, (h + 1) * HEAD_W)
                pv = jnp.dot(vt[rows], pb[g], preferred_element_type=F32)
                acc_ref[comp, rows, :] = acc_ref[comp, rows, :] * alpha[g] + pv

    def run(lo, hi, masked):
        def pair_body(i, carry):
            step(lo + 2 * i, 2, masked)
            return carry
        n = hi - lo
        if isinstance(n, int) and n == 0:
            return
        lax.fori_loop(0, n // 2, pair_body, 0)

        @pl.when(n % 2 == 1)
        def _():
            step(hi - 1, 1, masked)

    run(0, n_full, False)
    run(n_full, n_kv, True)


def _normalised(acc, l_row, base, tq):
    rl = 1.0 / l_row
    rf = jnp.concatenate(
        [jnp.broadcast_to(rl[:, (base + h) * tq:(base + h + 1) * tq], (HEAD_W, tq)) for h in range(N_HEADS)],
        axis=0)
    return acc * rf


def _num_kv_blocks(q0, tq, tk, past, s_tot):
    return jnp.minimum((past + q0 + tq + tk - 1) // tk, s_tot // tk)


def _attn_a_kernel(qt_ref, k_ref, vt_ref, lam_ref, g_ref, gm_ref, o_ref, m_ref, l_ref, acc_ref,
                   *, tq, tk, past, s_tot, lam_init):
    q0 = pl.program_id(1) * tq
    qt = qt_ref[...]
    row_grp = lax.broadcasted_iota(I32, (GROUP_W, 1), 0) // 32
    qst = jnp.concatenate(
        [jnp.where(row_grp == (2 * h + c), qt, 0.0) for c in range(2) for h in range(N_HEADS)],
        axis=1).astype(BF16)
    n_kv = _num_kv_blocks(q0, tq, tk, past, s_tot)
    n_full = jnp.minimum((past + q0 + 1) // tk, n_kv)
    row_pos = past + q0 + lax.broadcasted_iota(I32, (1, tq), 1)
    key_iota = lax.broadcasted_iota(I32, (tk, 1), 0)

    def mask_fn(c, k0):
        return (k0 + key_iota) <= row_pos

    _attn_core(qst, 2 * N_HEADS, k_ref, vt_ref, n_full, n_kv, tq, tk, mask_fn,
               m_ref, l_ref, acc_ref, zero_masked=False)
    l_row = l_ref[...]
    o0 = _normalised(acc_ref[0], l_row, 0, tq)
    o1 = _normalised(acc_ref[1], l_row, N_HEADS, tq)
    lv = lam_ref[...]
    lam = (jnp.exp(jnp.sum(lv[0:1] * lv[1:2], axis=1, keepdims=True))
           - jnp.exp(jnp.sum(lv[2:3] * lv[3:4], axis=1, keepdims=True)) + lam_init)
    o = (o0 - lam * o1).T
    hi, lo = _split_bf16(o * o)
    ms = (jnp.dot(hi, gm_ref[...], preferred_element_type=F32)
          + jnp.dot(lo, gm_ref[...], preferred_element_type=F32))
    o_ref[...] = o * lax.rsqrt(ms + LN_EPS) * g_ref[...] * (1.0 - lam_init)


def _attention_a(qt, k, vt, a_lambda, g_full, gmean, *, bsz, tq, tk, past, lam_init):
    t = qt.shape[1] // bsz
    s_tot = k.shape[0] // bsz
    nq = t // tq
    kern = functools.partial(_attn_a_kernel, tq=tq, tk=tk, past=past, s_tot=s_tot, lam_init=lam_init)
    return pl.pallas_call(
        kern, grid=(bsz, nq),
        in_specs=[pl.BlockSpec((GROUP_W, tq), lambda b, i: (0, b * nq + i)),
                  pl.BlockSpec((s_tot, GROUP_W), lambda b, i: (b, 0)),
                  pl.BlockSpec((GROUP_W, s_tot), lambda b, i: (0, b)),
                  pl.BlockSpec(a_lambda.shape, lambda b, i: (0, 0)),
                  pl.BlockSpec((1, GROUP_W), lambda b, i: (0, 0)),
                  pl.BlockSpec((GROUP_W, GROUP_W), lambda b, i: (0, 0))],
        out_specs=pl.BlockSpec((tq, GROUP_W), lambda b, i: (b * nq + i, 0)),
        out_shape=jax.ShapeDtypeStruct((bsz * t, GROUP_W), F32),
        scratch_shapes=[pltpu.VMEM((1, 2 * N_HEADS * tq), F32), pltpu.VMEM((1, 2 * N_HEADS * tq), F32),
                        pltpu.VMEM((2, GROUP_W, tq), F32)],
        compiler_params=_cparams(("parallel", "arbitrary")), name="attn_a",
    )(qt, k, vt, a_lambda, g_full, gmean)


def _dsa_kernel(qt_ref, k_ref, vt_ref, qit_ref, ki_ref, kiwit_ref, o_ref, key_ref, y_ref, m_ref, l_ref, acc_ref,
                *, tq, tk, past, s_tot, topk, idx_bits, t_valid):
    q0 = pl.program_id(1) * tq
    n_kv = _num_kv_blocks(q0, tq, tk, past, s_tot)
    row_pos = past + q0 + lax.broadcasted_iota(I32, (1, tq), 1)
    key_iota = lax.broadcasted_iota(I32, (tk, 1), 0)
    int_min = jnp.int32(INT_MIN)

    qit = qit_ref[...]
    blocks = []
    for h in range(N_HEADS):
        qh = qit[h * D_IDX:(h + 1) * D_IDX, :]
        hi, lo = _split_bf16(qh)
        blocks.append(jnp.concatenate([hi, lo, hi], axis=0))
    qi = jnp.concatenate(blocks, axis=1)
    kiwit = kiwit_ref[...]
    w_h = [kiwit[D_IDX + h:D_IDX + h + 1, :] for h in range(N_HEADS)]

    def score_body(c, carry):
        k0 = pl.multiple_of(c * tk, tk)
        kib = ki_ref[pl.ds(k0, tk), :]
        sc = jnp.dot(kib, qi, preferred_element_type=F32)
        isc = jnp.maximum(sc[:, 0:tq], 0.0) * w_h[0]
        for h in range(1, N_HEADS):
            isc = isc + jnp.maximum(sc[:, h * tq:(h + 1) * tq], 0.0) * w_h[h]
        isc = jnp.where(isc == 0.0, 0.0, isc)
        bits = pltpu.bitcast(isc, I32)
        key = bits ^ ((bits >> 31) & jnp.int32(0x7FFFFFFF))
        key_ref[c] = jnp.where((k0 + key_iota) <= row_pos, key, int_min)
        return carry

    lax.fori_loop(0, n_kv, score_body, 0)

    def count(pred):
        def block_count(c):
            ones = jnp.where(pred(key_ref[c], c * tk), 1.0, 0.0)
            return jnp.sum(ones.reshape(tk // 8, 8, tq), axis=0)

        def body(i, acc):
            return acc + block_count(2 * i) + block_count(2 * i + 1)

        acc = lax.fori_loop(0, n_kv // 2, body, jnp.zeros((8, tq), F32))
        odd = (n_kv & 1).astype(F32)
        acc = acc + odd * block_count(n_kv - 1)
        return jnp.sum(acc, axis=0, keepdims=True)

    kk = jnp.float32(topk)

    def bit_body(r, carry):
        prefix, n_ge, n_gt = carry
        cand_u = prefix | lax.shift_left(jnp.int32(1), 31 - r)
        cand_s = cand_u ^ int_min
        cnt = count(lambda blk, k0: blk >= cand_s)
        ok = cnt >= kk
        return jnp.where(ok, cand_u, prefix), jnp.where(ok, cnt, n_ge), jnp.where(ok, n_gt, cnt)

    n_all = jnp.full((1, tq), 1.0, F32) * (n_kv * tk).astype(F32)
    prefix, n_ge, n_gt = lax.fori_loop(
        0, 32, bit_body, (jnp.zeros((1, tq), I32), n_all, jnp.zeros((1, tq), F32)))
    thr = prefix ^ int_min
    r_need = kk - n_gt
    live = lax.broadcasted_iota(I32, (1, tq), 1) < t_valid
    any_tie = jnp.max(jnp.where(live, n_ge - kk, 0.0)) > 0.0
    y_ref[...] = jnp.full((1, tq), s_tot, I32)

    @pl.when(any_tie)
    def _():
        def ybody(r, y):
            cand = y | lax.shift_left(jnp.int32(1), idx_bits - 1 - r)
            cnt = count(lambda blk, k0: (blk == thr) & ((k0 + key_iota) < cand))
            return jnp.where(cnt < r_need, cand, y)
        y_ref[...] = lax.fori_loop(0, idx_bits, ybody, jnp.zeros((1, tq), I32))

    y_last = y_ref[...]

    def mask_fn(c, k0):
        blk = key_ref[c]
        pos = k0 + key_iota
        sel = (blk > thr) | ((blk == thr) & (pos <= y_last))
        return sel & (pos <= row_pos)

    qt = qt_ref[...]
    row_head = lax.broadcasted_iota(I32, (GROUP_W, 1), 0) // HEAD_W
    qst = jnp.concatenate([jnp.where(row_head == h, qt, 0.0) for h in range(N_HEADS)], axis=1).astype(BF16)
    _attn_core(qst, N_HEADS, k_ref, vt_ref, 0, n_kv, tq, tk, mask_fn, m_ref, l_ref, acc_ref, zero_masked=True)
    o_ref[...] = _normalised(acc_ref[0], l_ref[...], 0, tq).T


def _dsa(qt, k, vt, qit, ki3, kiwit, *, bsz, tq, tk, past, topk, t_valid):
    t = qt.shape[1] // bsz
    s_tot = k.shape[0] // bsz
    nq = t // tq
    idx_bits = int(math.ceil(math.log2(s_tot))) + 1
    kern = functools.partial(_dsa_kernel, tq=tq, tk=tk, past=past, s_tot=s_tot, topk=topk,
                             idx_bits=idx_bits, t_valid=t_valid)
    qspec = lambda w: pl.BlockSpec((w, tq), lambda b, i: (0, b * nq + i))
    return pl.pallas_call(
        kern, grid=(bsz, nq),
        in_specs=[qspec(GROUP_W),
                  pl.BlockSpec((s_tot, GROUP_W), lambda b, i: (b, 0)),
                  pl.BlockSpec((GROUP_W, s_tot), lambda b, i: (0, b)),
                  qspec(GROUP_W),
                  pl.BlockSpec((s_tot, 3 * D_IDX), lambda b, i: (b, 0)),
                  qspec(LANES)],
        out_specs=pl.BlockSpec((tq, GROUP_W), lambda b, i: (b * nq + i, 0)),
        out_shape=jax.ShapeDtypeStruct((bsz * t, GROUP_W), F32),
        scratch_shapes=[pltpu.VMEM((s_tot // tk, tk, tq), I32), pltpu.VMEM((1, tq), I32),
                        pltpu.VMEM((1, N_HEADS * tq), F32), pltpu.VMEM((1, N_HEADS * tq), F32),
                        pltpu.VMEM((1, GROUP_W, tq), F32)],
        compiler_params=_cparams(("parallel", "arbitrary")), name="dsa",
    )(qt, k, vt, qit, ki3, kiwit)


_HALO_C = 8
_HALO_D = 32


def _conv_kernel(*refs, tm, single_tile):
    if single_tile:
        (cb_ref, uc_ref, ud_ref, hc_ref, hd_ref, cw_ref, dw_ref, db_ref, lg_ref, lb_ref,
         oc_ref, od_ref, ec_ref, ed_ref) = refs
        pc_ref = pd_ref = None
    else:
        (cb_ref, uc_ref, ud_ref, pc_ref, pd_ref, hc_ref, hd_ref, cw_ref, dw_ref, db_ref, lg_ref, lb_ref,
         oc_ref, od_ref, ec_ref, ed_ref) = refs
    i = pl.program_id(1)

    @pl.when(i == 0)
    def _():
        ec_ref[0:_HALO_C] = hc_ref[0]
        ed_ref[0:_HALO_D] = hd_ref[0]

    if not single_tile:
        @pl.when(i > 0)
        def _():
            ec_ref[0:_HALO_C] = pc_ref[0]
            ed_ref[0:_HALO_D] = pd_ref[0]

    ec_ref[_HALO_C:_HALO_C + tm] = uc_ref[0]
    ed_ref[_HALO_D:_HALO_D + tm] = ud_ref[0]

    cw = cw_ref[...]
    acc = ec_ref[pl.ds(_HALO_C - (C_WIDTH - 1), tm), :] * cw[0:1]
    for w in range(1, C_WIDTH):
        acc = acc + ec_ref[pl.ds(_HALO_C - (C_WIDTH - 1) + w, tm), :] * cw[w:w + 1]
    oc_ref[0] = cb_ref[0] * acc

    dw = dw_ref[...]
    acc = ed_ref[pl.ds(_HALO_D - (D_WIDTH - 1), tm), :] * dw[0:1]
    for w in range(1, D_WIDTH):
        acc = acc + ed_ref[pl.ds(_HALO_D - (D_WIDTH - 1) + w, tm), :] * dw[w:w + 1]
    z = _layer_norm(acc + db_ref[...], lg_ref[...], lb_ref[...])
    od_ref[0] = z * _sigmoid(z)


def _convs(cb, uc, ud, hist_c, hist_d, c_w, d_w, d_b, ln_g, ln_b, *, tm):
    bsz, t, w = cb.shape
    nt = t // tm
    single = nt == 1
    main = pl.BlockSpec((1, tm, w), lambda b, i: (b, i, 0))
    prev_c = pl.BlockSpec((1, _HALO_C, w), lambda b, i: (b, jnp.maximum(i * (tm // _HALO_C) - 1, 0), 0))
    prev_d = pl.BlockSpec((1, _HALO_D, w), lambda b, i: (b, jnp.maximum(i * (tm // _HALO_D) - 1, 0), 0))
    hist_cs = pl.BlockSpec((1, _HALO_C, w), lambda b, i: (b, 0, 0))
    hist_ds = pl.BlockSpec((1, _HALO_D, w), lambda b, i: (b, 0, 0))
    full2 = lambda a: pl.BlockSpec(a.shape, lambda b, i: (0, 0))
    in_specs = [main, main, main] + ([] if single else [prev_c, prev_d]) + [hist_cs, hist_ds] + \
        [full2(c_w), full2(d_w), full2(d_b), full2(ln_g), full2(ln_b)]
    args = [cb, uc, ud] + ([] if single else [uc, ud]) + [hist_c, hist_d, c_w, d_w, d_b, ln_g, ln_b]
    kern = functools.partial(_conv_kernel, tm=tm, single_tile=single)
    return pl.pallas_call(
        kern, grid=(bsz, nt), in_specs=in_specs,
        out_specs=[main, main],
        out_shape=[jax.ShapeDtypeStruct((bsz, t, w), F32)] * 2,
        scratch_shapes=[pltpu.VMEM((_HALO_C + tm, w), F32), pltpu.VMEM((_HALO_D + tm, w), F32)],
        compiler_params=_cparams(("parallel", "arbitrary")), name="convs",
    )(*args)


def _mix_kernel(x_ref, oa_ref, ob_ref, oc_ref, od_ref, pw_ref, pwb_ref, wo_ref, g_ref, b_ref, h_ref, *, alpha):
    od = jnp.dot(od_ref[...].astype(BF16), pw_ref[...], preferred_element_type=F32) + pwb_ref[...]
    cat = jnp.concatenate([oa_ref[...], ob_ref[...], oc_ref[...], od], axis=1).astype(BF16)
    mix = jnp.dot(cat, wo_ref[...], preferred_element_type=F32)
    h_ref[...] = _layer_norm(alpha * x_ref[...] + mix, g_ref[...], b_ref[...])


def _mix(x, oa, ob, oc, od, pw, pwb, wo, g, b, *, tm, alpha):
    n, d = x.shape
    row = lambda w: pl.BlockSpec((tm, w), lambda i: (i, 0))
    full = lambda a: pl.BlockSpec(a.shape, lambda i: (0, 0))
    return pl.pallas_call(
        functools.partial(_mix_kernel, alpha=alpha), grid=(n // tm,),
        in_specs=[row(d), row(GROUP_W), row(GROUP_W), row(GROUP_W), row(GROUP_W),
                  full(pw), full(pwb), full(wo), full(g), full(b)],
        out_specs=row(d), out_shape=jax.ShapeDtypeStruct((n, d), F32),
        compiler_params=_cparams(("parallel",)), name="mix",
    )(x, oa, ob, oc, od, pw, pwb, wo, g, b)


def _staircase():
    return [(a, b) for a in range(PEER_TOPK) for b in range(PEER_TOPK) if (a + 1) * (b + 1) <= PEER_TOPK]


def _tree_rows(x, op):
    while x.shape[0] > 1:
        half = x.shape[0] // 2
        y = op(x[:half], x[half:2 * half])
        x = y if x.shape[0] % 2 == 0 else jnp.concatenate([y, x[2 * half:]], axis=0)
    return x


def _argmax_rows(x, row):
    v, i = x, row
    while v.shape[0] > 1:
        n = v.shape[0]
        half = n // 2
        pv = v[:2 * half].reshape((half, 2) + v.shape[1:])
        pi = i[:2 * half].reshape((half, 2) + i.shape[1:])
        take_b = pv[:, 1] > pv[:, 0]
        nv = jnp.where(take_b, pv[:, 1], pv[:, 0])
        ni = jnp.where(take_b, pi[:, 1], pi[:, 0])
        if n % 2:
            nv = jnp.concatenate([nv, v[2 * half:]], axis=0)
            ni = jnp.concatenate([ni, i[2 * half:]], axis=0)
        v, i = nv, ni
    return v, i


def _extract_top(x, payload, n_out, n_rows):
    row = lax.broadcasted_iota(I32, x.shape, 0).astype(F32)
    vals, idxs, pays = [], [], []
    for _ in range(n_out):
        m, idx = _argmax_rows(x, row)
        hit = row == idx
        vals.append(m)
        idxs.append(idx)
        if payload is not None:
            pays.append(_tree_rows(jnp.where(hit, payload, -1.0), jnp.maximum))
        x = jnp.where(hit, -jnp.inf, x)
    return vals, idxs, pays


def _peer_select_kernel(h_ref, wq_ref, sk_ref, eid_ref, g_ref, sv_ref, si_ref, *, tm):
    ht = h_ref[...].T.astype(BF16)
    n_groups = 2 * PEER_HEADS
    nt = tm // LANES

    def to_tiles(x):
        return pltpu.einshape("trl->rtl", jnp.stack([x[:, LANES * t:LANES * (t + 1)] for t in range(nt)], axis=0))

    def from_tiles(y):
        z = pltpu.einshape("rtl->trl", y)
        return jnp.concatenate([z[t] for t in range(nt)], axis=1)

    def group_body(g, carry):
        r0 = pl.multiple_of(g * N_KEYS, N_KEYS)
        qg = jnp.dot(wq_ref[pl.ds(r0, N_KEYS), :], ht, preferred_element_type=F32)
        st = jnp.dot(sk_ref[g % 2], qg.astype(BF16), preferred_element_type=F32)
        vals, idxs, _ = _extract_top(to_tiles(st), None, PEER_TOPK, N_KEYS)
        sv_ref[g] = jnp.concatenate(vals, axis=0)
        si_ref[g] = jnp.concatenate(idxs, axis=0)
        return carry

    lax.fori_loop(0, n_groups, group_body, 0)

    pairs = _staircase()

    def head_body(hd, carry):
        sv0 = sv_ref[2 * hd]; sv1 = sv_ref[2 * hd + 1]
        si0 = si_ref[2 * hd]; si1 = si_ref[2 * hd + 1]
        cand = jnp.concatenate([sv0[a:a + 1] + sv1[b:b + 1] for a, b in pairs], axis=0)
        cand_e = jnp.concatenate([si0[a:a + 1] * float(N_KEYS) + si1[b:b + 1] for a, b in pairs], axis=0)
        vals, _, pays = _extract_top(cand, cand_e, PEER_TOPK, len(pairs))
        fv = jnp.concatenate(vals, axis=0)
        e = jnp.exp(fv - fv[0:1])
        gate = e / jnp.sum(e, axis=0, keepdims=True)
        r0 = pl.multiple_of(hd * PEER_TOPK, PEER_TOPK)
        eid_ref[pl.ds(r0, PEER_TOPK), :] = from_tiles(jnp.concatenate(pays, axis=0)).astype(I32)
        g_ref[pl.ds(r0, PEER_TOPK), :] = from_tiles(gate)
        return carry

    lax.fori_loop(0, PEER_HEADS, head_body, 0)


_SELECT_TOKENS = 8 * LANES


def _peer_select(h, wq_t, sk, *, tm):
    n, d = h.shape
    n_slots = PEER_HEADS * PEER_TOPK
    return pl.pallas_call(
        functools.partial(_peer_select_kernel, tm=tm), grid=(n // tm,),
        in_specs=[pl.BlockSpec((tm, d), lambda i: (i, 0)),
                  pl.BlockSpec(wq_t.shape, lambda i: (0, 0)),
                  pl.BlockSpec(sk.shape, lambda i: (0, 0, 0))],
        out_specs=[pl.BlockSpec((n_slots, tm), lambda i: (0, i)),
                   pl.BlockSpec((n_slots, tm), lambda i: (0, i))],
        out_shape=[jax.ShapeDtypeStruct((n_slots, n), I32), jax.ShapeDtypeStruct((n_slots, n), F32)],
        scratch_shapes=[pltpu.VMEM((2 * PEER_HEADS, PEER_TOPK, tm // LANES, LANES), F32),
                        pltpu.VMEM((2 * PEER_HEADS, PEER_TOPK, tm // LANES, LANES), F32)],
        compiler_params=_cparams(("parallel",)), name="peer_select",
    )(h, wq_t, sk)


_GATE_TOKENS = 16


def _peer_gates_kernel(eid_ref, g_ref, w_ref, *, tm):
    sub = lax.broadcasted_iota(I32, (N_KEYS, PEER_HEADS * PEER_TOPK), 0)

    def token_gates(t):
        e = eid_ref[pl.ds(t, 1), :]
        g = g_ref[pl.ds(t, 1), :]
        oh_i = jnp.where((e >> 7) == sub, 1.0, 0.0).astype(BF16)
        g_j = jnp.where((e & (N_KEYS - 1)) == sub, g, 0.0).astype(BF16)
        return lax.dot_general(oh_i, g_j, _NT, preferred_element_type=F32)

    def body(b, carry):
        t0 = pl.multiple_of(b * _GATE_TOKENS, _GATE_TOKENS)
        w = jnp.stack([token_gates(t0 + k) for k in range(_GATE_TOKENS)], axis=0)
        w_ref[:, pl.ds(t0, _GATE_TOKENS), :] = pltpu.einshape("tij->itj", w).astype(BF16)
        return carry

    lax.fori_loop(0, tm // _GATE_TOKENS, body, 0)


def _peer_gates(eid, gate, *, tm):
    n, n_slots = eid.shape
    return pl.pallas_call(
        functools.partial(_peer_gates_kernel, tm=tm), grid=(n // tm,),
        in_specs=[pl.BlockSpec((tm, n_slots), lambda i: (i, 0)),
                  pl.BlockSpec((tm, n_slots), lambda i: (i, 0))],
        out_specs=pl.BlockSpec((N_KEYS, tm, N_KEYS), lambda i: (0, i, 0)),
        out_shape=jax.ShapeDtypeStruct((N_KEYS, n, N_KEYS), BF16),
        compiler_params=_cparams(("parallel",)), name="peer_gates",
    )(eid, gate)


def _gelu(x):
    return 0.5 * x * (1.0 + lax.erf(x * (2.0 ** -0.5)))


def _peer_dense_kernel(h_ref, w_ref, u_ref, v_ref, f_ref):
    @pl.when(pl.program_id(1) == 0)
    def _():
        f_ref[...] = jnp.zeros(f_ref.shape, F32)

    act = _gelu(lax.dot_general(h_ref[...], u_ref[0].astype(BF16), _NT, preferred_element_type=F32))
    y = jnp.concatenate(
        [w_ref[i].astype(F32) * act[:, i * N_KEYS:(i + 1) * N_KEYS] for i in range(w_ref.shape[0])],
        axis=1).astype(BF16)
    f_ref[...] += jnp.dot(y, v_ref[0].astype(BF16), preferred_element_type=F32)


def _peer_dense(hb, wmat, u, v, layer, *, tm, te):
    n, d = hb.shape
    n_exp = u.shape[1]
    return pl.pallas_call(
        _peer_dense_kernel, grid=(n // tm, n_exp // te),
        in_specs=[pl.BlockSpec((tm, d), lambda i, e: (i, 0)),
                  pl.BlockSpec((te // N_KEYS, tm, N_KEYS), lambda i, e: (e, i, 0)),
                  pl.BlockSpec((1, te, d), lambda i, e: (layer, e, 0)),
                  pl.BlockSpec((1, te, d), lambda i, e: (layer, e, 0))],
        out_specs=pl.BlockSpec((tm, d), lambda i, e: (i, 0)),
        out_shape=jax.ShapeDtypeStruct((n, d), F32),
        compiler_params=_cparams(("parallel", "arbitrary")), name="peer_dense",
    )(hb, wmat, u, v)


def _final_kernel(h_ref, f_ref, p_ref, g_ref, b_ref, wg_ref, bg_ref, wp_ref, y_ref, *, alpha):
    h2 = _layer_norm(alpha * h_ref[...] + f_ref[...], g_ref[...], b_ref[...])
    gate = _sigmoid(jnp.dot(h2.astype(BF16), wg_ref[...], preferred_element_type=F32) + bg_ref[...])
    emb = jnp.dot(p_ref[...].astype(BF16), wp_ref[...], preferred_element_type=F32)
    y_ref[...] = h2 + gate * emb


def _final(h, f, p, g, b, wg, bg, wp, *, tm, alpha):
    n, d = h.shape
    row = lambda w: pl.BlockSpec((tm, w), lambda i: (i, 0))
    full = lambda a: pl.BlockSpec(a.shape, lambda i: (0, 0))
    return pl.pallas_call(
        functools.partial(_final_kernel, alpha=alpha), grid=(n // tm,),
        in_specs=[row(d), row(d), row(p.shape[1]), full(g), full(b), full(wg), full(bg), full(wp)],
        out_specs=row(d), out_shape=jax.ShapeDtypeStruct((n, d), F32),
        compiler_params=_cparams(("parallel",)), name="final",
    )(h, f, p, g, b, wg, bg, wp)


def _pack_w_in(w_in):
    d = w_in.shape[0]
    cols = [w_in[:, 0:1792], w_in[:, 1792:1860], jnp.zeros((d, 60), w_in.dtype), w_in[:, 1860:3140]]
    nat = jnp.concatenate(cols, axis=1).astype(BF16)
    seg = lambda name, width=GROUP_W: nat[:, _SEG[name]:_SEG[name] + width]
    tr = jnp.concatenate([seg("qa"), seg("qb"), seg("qi"), seg("kiwi", 128), seg("va"), seg("vb")], axis=1).T
    return nat, tr


def _row(a):
    return a.reshape(1, -1)


def _tile_rows(n, pref):
    t = pref
    while n % t:
        t //= 2
    return t


def _prep_weights(lw, depth):
    (w_in, a_lambda, a_subln_g, c_conv_w, d_conv_w, d_conv_b, d_ln_g, d_ln_b, d_pw_w, d_pw_b,
     w_out, ln1_g, ln1_b, peer_wq, peer_subkeys, peer_u, peer_v, ln2_g, ln2_b,
     ple_w, ple_gate_w, ple_gate_b) = lw
    head_of_lane = jnp.arange(GROUP_W) // HEAD_W
    gmean = (head_of_lane[:, None] == head_of_lane[None, :]).astype(BF16) * (1.0 / HEAD_W)
    w_nat, w_tr = _pack_w_in(w_in)
    return dict(
        w_in=w_nat, w_in_t=w_tr, a_lambda=a_lambda, a_g=_row(jnp.tile(a_subln_g, N_HEADS)),
        gmean=gmean.astype(BF16),
        c_w=c_conv_w, d_w=d_conv_w, d_b=_row(d_conv_b), d_ln_g=_row(d_ln_g), d_ln_b=_row(d_ln_b),
        d_pw=d_pw_w.astype(BF16), d_pwb=_row(d_pw_b), w_out=w_out.astype(BF16),
        ln1_g=_row(ln1_g), ln1_b=_row(ln1_b),
        wq_t=peer_wq.T.astype(BF16), sk=peer_subkeys.astype(BF16),
        u=peer_u, v=peer_v,
        ln2_g=_row(ln2_g), ln2_b=_row(ln2_b),
        ple_w=ple_w.astype(BF16), wg=ple_gate_w.astype(BF16), bg=_row(ple_gate_b),
        alpha=float((2 * depth) ** 0.25),
    )


def _index_keys(ki):
    k_hi = ki.astype(BF16)
    k_lo = (ki - k_hi.astype(F32)).astype(BF16)
    return jnp.concatenate([k_hi, k_hi, k_lo], axis=-1)


def _token_tail(w, x, oa, ob, oc, od, valid_rows, p):
    n = x.shape[0]
    h = _mix(x, oa, ob, oc, od, w["d_pw"], w["d_pwb"], w["w_out"], w["ln1_g"], w["ln1_b"],
             tm=_tile_rows(n, 512), alpha=w["alpha"])
    if valid_rows is not None:
        bsz, t_pad, t = valid_rows
        h = h.reshape(bsz, t_pad, -1)[:, :t].reshape(bsz * t, -1)
    n = h.shape[0]
    n_sel = -(-n // _SELECT_TOKENS) * _SELECT_TOKENS
    h_sel = h if n_sel == n else jnp.pad(h, ((0, n_sel - n), (0, 0)))
    eid_t, gate_t = _peer_select(h_sel, w["wq_t"], w["sk"], tm=_SELECT_TOKENS)
    n_gate = -(-n // _GATE_TOKENS) * _GATE_TOKENS
    wmat = _peer_gates(eid_t[:, :n_gate].T, gate_t[:, :n_gate].T, tm=_tile_rows(n_gate, 128))
    f = _peer_dense(h.astype(BF16), wmat, w["u"], w["v"], w["layer"], tm=_tile_rows(n, 1024), te=8 * N_KEYS)
    return _final(h, f, p, w["ln2_g"], w["ln2_b"], w["wg"], w["bg"], w["ple_w"],
                  tm=_tile_rows(n, 512), alpha=w["alpha"])


def _layer(layer_idx, depth, xp, xs, caches, state_c, state_d, page_table, pp, ps, lw):
    w = _prep_weights(lw, depth)
    w["layer"] = layer_idx
    bp, tp, d = xp.shape
    bs, ts, _ = xs.shape
    ts_pad = 8
    n_pages = page_table.shape[1]
    page = caches[0].shape[3]
    past = n_pages * page
    lam_init = 0.8 - 0.6 * math.exp(-0.3 * (layer_idx + 1))

    pr = _project(xp.reshape(bp * tp, d), w["w_in"], w["w_in_t"], _tile_rows(bp * tp, 512), feature_major=True)
    tq = _tile_rows(tp, 256)
    tk = _tile_rows(tp, 256)
    oa = _attention_a(pr["qat"], pr["kab"], pr["vat"], w["a_lambda"], w["a_g"], w["gmean"],
                      bsz=bp, tq=tq, tk=tk, past=0, lam_init=lam_init)
    ob = _dsa(pr["qbt"], pr["kbb"], pr["vbt"], pr["qit"], _index_keys(pr["kiwi"][:, :D_IDX]), pr["kiwit"],
              bsz=bp, tq=tq, tk=tk, past=0, topk=min(DSA_TOPK, tp // 4), t_valid=tq)
    r3p = lambda a: a.reshape(bp, tp, a.shape[-1])
    zc = jnp.zeros((bp, _HALO_C, GROUP_W), F32)
    zd = jnp.zeros((bp, _HALO_D, GROUP_W), F32)
    oc, od = _convs(r3p(pr["cb"]), r3p(pr["uc"]), r3p(pr["ud"]), zc, zd, w["c_w"], w["d_w"], w["d_b"],
                    w["d_ln_g"], w["d_ln_b"], tm=_tile_rows(tp, 512))
    yp = _token_tail(w, xp.reshape(bp * tp, d), oa, ob, oc.reshape(bp * tp, -1), od.reshape(bp * tp, -1),
                     None, pp.reshape(bp * tp, -1)).reshape(bp, tp, d)
    new_p = (r3p(pr["ka"]), r3p(pr["va"]), r3p(pr["kb"]), r3p(pr["vb"]), r3p(pr["kiwi"])[:, :, :D_IDX],
             jnp.concatenate([zc, r3p(pr["uc"])], axis=1)[:, -(C_WIDTH - 1):],
             jnp.concatenate([zd, r3p(pr["ud"])], axis=1)[:, -(D_WIDTH - 1):])

    xs_pad = jnp.pad(xs, ((0, 0), (0, ts_pad - ts), (0, 0))).reshape(bs * ts_pad, d)
    sr = _project(xs_pad, w["w_in"], w["w_in_t"], _tile_rows(bs * ts_pad, 512), feature_major=False)
    r3s = lambda a: a.reshape(bs, ts_pad, a.shape[-1])
    news = [r3s(sr["ka"]), r3s(sr["va"]), r3s(sr["kb"]), r3s(sr["vb"]), r3s(sr["kiwi"])[:, :, :D_IDX]]
    oa_s, ob_s = _sample_attention(page_table, caches, layer_idx, news, r3s(sr["qa"]), r3s(sr["qb"]), r3s(sr["qi"]),
                                   r3s(sr["kiwi"]), w["a_lambda"], w["a_g"], w["gmean"],
                                   topk=min(DSA_TOPK, (past + ts) // 4), lam_init=lam_init, t_valid=ts)
    unpad = lambda a: a.reshape(bs * ts_pad, GROUP_W)
    hist_c = jnp.concatenate([jnp.zeros((bs, _HALO_C - (C_WIDTH - 1), GROUP_W), F32), state_c], axis=1)
    hist_d = jnp.concatenate([jnp.zeros((bs, _HALO_D - (D_WIDTH - 1), GROUP_W), F32), state_d], axis=1)
    oc_s, od_s = _convs(r3s(sr["cb"]), r3s(sr["uc"]), r3s(sr["ud"]), hist_c, hist_d, w["c_w"], w["d_w"],
                        w["d_b"], w["d_ln_g"], w["d_ln_b"], tm=ts_pad)
    ys = _token_tail(w, xs_pad, unpad(oa_s), unpad(ob_s), oc_s.reshape(bs * ts_pad, -1),
                     od_s.reshape(bs * ts_pad, -1), (bs, ts_pad, ts), ps.reshape(bs * ts, -1)).reshape(bs, ts, d)
    cut = lambda a: r3s(a)[:, :ts]
    new_s = (cut(sr["ka"]), cut(sr["va"]), cut(sr["kb"]), cut(sr["vb"]), cut(sr["kiwi"])[:, :, :D_IDX],
             jnp.concatenate([state_c, cut(sr["uc"])], axis=1)[:, -(C_WIDTH - 1):],
             jnp.concatenate([state_d, cut(sr["ud"])], axis=1)[:, -(D_WIDTH - 1):])
    return yp, ys, new_p, new_s


def kernel(x_prompt, x_sample, cache_a_k, cache_a_v, cache_b_k, cache_b_v, cache_b_kidx, state_c_conv, state_d_conv, page_table, p_prompt, p_sample, w_in, a_lambda, a_subln_g, c_conv_w, d_conv_w, d_conv_b, d_ln_g, d_ln_b, d_pw_w, d_pw_b, w_out, ln1_g, ln1_b, peer_wq, peer_subkeys, peer_u, peer_v, ln2_g, ln2_b, ple_w, ple_gate_w, ple_gate_b):
    depth = w_in.shape[0]
    n_pool, page = cache_a_k.shape[1], cache_a_k.shape[2]
    feature_major = lambda c: jnp.moveaxis(c, 2, -1).reshape(depth, n_pool, -1, page)
    caches = [feature_major(c) for c in (cache_a_k, cache_a_v, cache_b_k, cache_b_v, cache_b_kidx)]
    xp, xs = x_prompt, x_sample
    news_p, news_s = [], []
    for l in range(depth):
        lw = (w_in[l], a_lambda[l], a_subln_g[l], c_conv_w[l], d_conv_w[l], d_conv_b[l], d_ln_g[l], d_ln_b[l],
              d_pw_w[l], d_pw_b[l], w_out[l], ln1_g[l], ln1_b[l], peer_wq[l], peer_subkeys[l], peer_u,
              peer_v, ln2_g[l], ln2_b[l], ple_w[l], ple_gate_w[l], ple_gate_b[l])
        xp, xs, new_p, new_s = _layer(l, depth, xp, xs, caches, state_c_conv[l], state_d_conv[l],
                                      page_table, p_prompt[l], p_sample[l], lw)
        news_p.append(new_p)
        news_s.append(new_s)

    bp, tp, _ = x_prompt.shape
    bs, ts, _ = x_sample.shape
    shapes = [(N_HEADS, 2, 32), (N_HEADS, HEAD_W), (N_HEADS, HEAD_W), (N_HEADS, HEAD_W), (D_IDX,)]

    def stack(news, i, lead):
        a = jnp.stack([n[i] for n in news], axis=0)
        return a.reshape((depth,) + lead + shapes[i]) if i < 5 else a

    outs = [xp, xs]
    for i in range(7):
        outs.append(stack(news_p, i, (bp, tp)))
        outs.append(stack(news_s, i, (bs, ts)))
    return tuple(outs)
```

```python
import functools
import math

import jax
import jax.numpy as jnp
from jax import lax
from jax.experimental import pallas as pl
from jax.experimental.pallas import tpu as pltpu

F32 = jnp.float32
BF16 = jnp.bfloat16
I32 = jnp.int32

LN_EPS = 1e-5
N_HEADS = 4
HEAD_W = 64
GROUP_W = 256
D_IDX = 64
DSA_TOPK = 256
C_WIDTH = 3
D_WIDTH = 31
PEER_HEADS = 8
PEER_TOPK = 16
N_KEYS = 128
LANES = 128
NEG_BIG = -1e30
INT_MIN = -2 ** 31
VMEM_LIMIT = 56 * 1024 * 1024

_SEG = dict(qa=0, ka=256, va=512, qb=768, kb=1024, vb=1280, qi=1536, kiwi=1792,
            cb=1920, cc=2176, cx=2432, da=2688, dg=2944)
_N_PACKED = 3200
_SEG_T = dict(qa=0, qb=256, qi=512, kiwi=768, va=896, vb=1152)
_N_PACKED_T = 1408

_NT = (((1,), (1,)), ((), ()))


def _cparams(sem):
    return pltpu.CompilerParams(dimension_semantics=sem, vmem_limit_bytes=VMEM_LIMIT)


def _split_bf16(x):
    hi = x.astype(BF16)
    lo = (x - hi.astype(F32)).astype(BF16)
    return hi, lo


def _layer_norm(x, g, b):
    mu = jnp.mean(x, axis=-1, keepdims=True)
    xc = x - mu
    var = jnp.mean(xc * xc, axis=-1, keepdims=True)
    return xc * lax.rsqrt(var + LN_EPS) * g + b


def _sigmoid(x):
    return 1.0 / (1.0 + jnp.exp(-x))


_PROJ_COMMON = ("ka", "va", "kb", "vb", "kiwi", "cb", "uc", "ud")
_PROJ_FEATURE_MAJOR = ("kab", "kbb", "qat", "qbt", "qit", "kiwit", "vat", "vbt")
_PROJ_ROW_MAJOR = ("qa", "qb", "qi")


def _proj_kernel(x_ref, w_ref, wt_ref, *out_refs, names, scale_a, scale_b):
    o = dict(zip(names, out_refs))
    xb = x_ref[...].astype(BF16)

    def seg(name, width=GROUP_W):
        off = _SEG[name]
        return jnp.dot(xb, w_ref[:, off:off + width], preferred_element_type=F32)

    def seg_t(name, width=GROUP_W):
        off = _SEG_T[name]
        return lax.dot_general(wt_ref[off:off + width, :], xb, _NT, preferred_element_type=F32)

    ka = seg("ka"); kb = seg("kb")
    o["ka"][...] = ka; o["kb"][...] = kb
    o["va"][...] = seg("va"); o["vb"][...] = seg("vb")
    o["kiwi"][...] = seg("kiwi", LANES)
    o["cb"][...] = seg("cb")
    o["uc"][...] = seg("cc") * seg("cx")
    o["ud"][...] = seg("da") * _sigmoid(seg("dg"))
    if "qat" in o:
        o["kab"][...] = ka.astype(BF16); o["kbb"][...] = kb.astype(BF16)
        o["qat"][...] = seg_t("qa") * scale_a
        o["qbt"][...] = seg_t("qb") * scale_b
        o["qit"][...] = seg_t("qi")
        o["kiwit"][...] = seg_t("kiwi", LANES)
        o["vat"][...] = seg_t("va").astype(BF16)
        o["vbt"][...] = seg_t("vb").astype(BF16)
    else:
        o["qa"][...] = seg("qa") * scale_a
        o["qb"][...] = seg("qb") * scale_b
        o["qi"][...] = seg("qi")


def _project(x, w_packed, wt_packed, tm, feature_major):
    n, d = x.shape
    nat = lambda w, dt=F32: (jax.ShapeDtypeStruct((n, w), dt), pl.BlockSpec((tm, w), lambda i: (i, 0)))
    tr = lambda w, dt=F32: (jax.ShapeDtypeStruct((w, n), dt), pl.BlockSpec((w, tm), lambda i: (0, i)))
    outs = [nat(256), nat(256), nat(256), nat(256), nat(LANES), nat(256), nat(256), nat(256)]
    if feature_major:
        names = _PROJ_COMMON + _PROJ_FEATURE_MAJOR
        outs += [nat(256, BF16), nat(256, BF16), tr(256), tr(256), tr(256), tr(LANES), tr(256, BF16), tr(256, BF16)]
    else:
        names = _PROJ_COMMON + _PROJ_ROW_MAJOR
        outs += [nat(256), nat(256), nat(256)]
    kern = functools.partial(_proj_kernel, names=names, scale_a=float(32 ** -0.5), scale_b=float(HEAD_W ** -0.5))
    res = pl.pallas_call(
        kern, grid=(n // tm,),
        in_specs=[pl.BlockSpec((tm, d), lambda i: (i, 0)),
                  pl.BlockSpec((d, _N_PACKED), lambda i: (0, 0)),
                  pl.BlockSpec((_N_PACKED_T, d), lambda i: (0, 0))],
        out_specs=[o[1] for o in outs], out_shape=[o[0] for o in outs],
        compiler_params=_cparams(("parallel",)), name="proj",
    )(x, w_packed, wt_packed)
    return dict(zip(names, res))


def _lane_heads(o, l, base, rows, lane_head):
    out = None
    for h in range(N_HEADS):
        r0 = (base + h) * rows
        part = jnp.where(lane_head == h, o[r0:r0 + rows] / l[r0:r0 + rows], 0.0)
        out = part if out is None else out + part
    return out


def _sample_attn_kernel(pt_ref, *refs, n_pages, page, rows, past, topk, idx_bits, lam_init, t_valid):
    npg = n_pages
    ak, av, bk, bv, kic = (refs[i * npg:(i + 1) * npg] for i in range(5))
    (nak, nav, nbk, nbv, nki, qa_ref, qb_ref, qi_ref, kiwi_ref, lam_ref, g_ref, gm_ref,
     oa_ref, ob_ref, y_ref) = refs[5 * npg:]
    nblk = npg + 1
    s_tot = nblk * page
    lane_head = lax.broadcasted_iota(I32, (1, GROUP_W), 1) // HEAD_W
    col_last = past + lax.broadcasted_iota(I32, (1, page), 1)
    t_row = lax.broadcasted_iota(I32, (rows, 1), 0)
    int_min = jnp.int32(INT_MIN)

    def blocks(page_refs, new_ref, dtype):
        new = new_ref[0]
        pad = jnp.zeros((page - rows, new.shape[-1]), F32)
        return [r[0, 0].astype(dtype) for r in page_refs] + [jnp.concatenate([new, pad], axis=0).T.astype(dtype)]

    def softmax_pv(q_stack, n_groups, k_blocks, v_blocks, masks):
        def masked(s, mk, fill):
            if mk is None:
                return s
            s3 = jnp.where(mk[None], s.reshape(n_groups, rows, page), fill)
            return s3.reshape(n_groups * rows, page)
        s = [masked(jnp.dot(q_stack, kb, preferred_element_type=F32), mk, NEG_BIG)
             for kb, mk in zip(k_blocks, masks)]
        m = s[0].max(axis=1, keepdims=True)
        for sj in s[1:]:
            m = jnp.maximum(m, sj.max(axis=1, keepdims=True))
        l = None
        o = None
        for sj, vb, mk in zip(s, v_blocks, masks):
            p = masked(jnp.exp(sj - m), mk, 0.0)
            lj = p.sum(axis=1, keepdims=True)
            oj = lax.dot_general(p.astype(BF16), vb, _NT, preferred_element_type=F32)
            l = lj if l is None else l + lj
            o = oj if o is None else o + oj
        return o, l

    causal_last = col_last <= (past + t_row)

    qa = qa_ref[0]
    lane_grp = lax.broadcasted_iota(I32, (1, GROUP_W), 1) // 32
    qa_stack = jnp.concatenate(
        [jnp.where(lane_grp == (2 * h + c), qa, 0.0) for c in range(2) for h in range(N_HEADS)],
        axis=0).astype(BF16)
    o, l = softmax_pv(qa_stack, 2 * N_HEADS, blocks(ak, nak, BF16), blocks(av, nav, BF16),
                      [None] * npg + [causal_last])
    o0 = _lane_heads(o, l, 0, rows, lane_head)
    o1 = _lane_heads(o, l, N_HEADS, rows, lane_head)
    lv = lam_ref[...]
    lam = (jnp.exp(jnp.sum(lv[0:1] * lv[1:2], axis=1, keepdims=True))
           - jnp.exp(jnp.sum(lv[2:3] * lv[3:4], axis=1, keepdims=True)) + lam_init)
    od = o0 - lam * o1
    hi, lo = _split_bf16(od * od)
    ms = (jnp.dot(hi, gm_ref[...], preferred_element_type=F32)
          + jnp.dot(lo, gm_ref[...], preferred_element_type=F32))
    oa_ref[0] = od * lax.rsqrt(ms + LN_EPS) * g_ref[...] * (1.0 - lam_init)

    qi = qi_ref[0]
    qi_rows = jnp.concatenate([qi[:, h * D_IDX:(h + 1) * D_IDX] for h in range(N_HEADS)], axis=0)
    q_hi, q_lo = _split_bf16(qi_rows)
    kiwi = kiwi_ref[0]
    w_h = [kiwi[:, D_IDX + h:D_IDX + h + 1] for h in range(N_HEADS)]
    key_blocks = []
    for j, kib in enumerate(blocks(kic, nki, F32)):
        k_hi, k_lo = _split_bf16(kib)
        sc = (jnp.dot(q_hi, k_hi, preferred_element_type=F32)
              + jnp.dot(q_lo, k_hi, preferred_element_type=F32)
              + jnp.dot(q_hi, k_lo, preferred_element_type=F32))
        isc = jnp.maximum(sc[0:rows], 0.0) * w_h[0]
        for h in range(1, N_HEADS):
            isc = isc + jnp.maximum(sc[h * rows:(h + 1) * rows], 0.0) * w_h[h]
        isc = jnp.where(isc == 0.0, 0.0, isc)
        bits = pltpu.bitcast(isc, I32)
        key = bits ^ ((bits >> 31) & jnp.int32(0x7FFFFFFF))
        if j == npg:
            key = jnp.where(causal_last, key, int_min)
        key_blocks.append(key)
    keys = jnp.concatenate(key_blocks, axis=1)
    cols = lax.broadcasted_iota(I32, (1, s_tot), 1)
    kk = jnp.float32(topk)
    count = lambda pred: jnp.sum(jnp.where(pred, 1.0, 0.0), axis=1, keepdims=True)

    def bit_body(r, prefix):
        b1 = lax.shift_left(jnp.int32(1), 31 - 2 * r)
        b0 = lax.shift_left(jnp.int32(1), 30 - 2 * r)
        enough = lambda cand_u: count(keys >= (cand_u ^ int_min)) >= kk
        c10 = prefix | b1
        c11 = c10 | b0
        c01 = prefix | b0
        return jnp.where(enough(c10), jnp.where(enough(c11), c11, c10), jnp.where(enough(c01), c01, prefix))

    thr = lax.fori_loop(0, 16, bit_body, jnp.zeros((rows, 1), I32)) ^ int_min
    r_need = kk - count(keys > thr)
    any_tie = jnp.max(jnp.where(t_row < t_valid, count(keys >= thr) - kk, 0.0)) > 0.0
    y_ref[...] = jnp.full((rows, 1), s_tot, I32)

    @pl.when(any_tie)
    def _():
        def ybody(r, y):
            cand = y | lax.shift_left(jnp.int32(1), idx_bits - 1 - r)
            return jnp.where(count((keys == thr) & (cols < cand)) < r_need, cand, y)
        y_ref[...] = lax.fori_loop(0, idx_bits, ybody, jnp.zeros((rows, 1), I32))

    sel = ((keys > thr) | ((keys == thr) & (cols <= y_ref[...]))) & (cols <= (past + t_row))
    qb = qb_ref[0]
    qb_stack = jnp.concatenate([jnp.where(lane_head == h, qb, 0.0) for h in range(N_HEADS)], axis=0).astype(BF16)
    o, l = softmax_pv(qb_stack, N_HEADS, blocks(bk, nbk, BF16), blocks(bv, nbv, BF16),
                      [sel[:, j * page:(j + 1) * page] for j in range(nblk)])
    ob_ref[0] = _lane_heads(o, l, 0, rows, lane_head)


def _sample_attention(page_table, caches, layer, news, qa, qb, qi, kiwi, a_lambda, g_full, gmean,
                      *, topk, lam_init, t_valid):
    bsz, n_pages = page_table.shape
    page = caches[0].shape[3]
    rows = qa.shape[1]
    past = n_pages * page
    idx_bits = int(math.ceil(math.log2((n_pages + 1) * page))) + 1
    page_spec = lambda c, j: pl.BlockSpec((1, 1, c, page), lambda b, pt: (layer, pt[b, j], 0, 0))
    seq_spec = lambda c: pl.BlockSpec((1, rows, c), lambda b, pt: (b, 0, 0))
    full2 = lambda a: pl.BlockSpec(a.shape, lambda b, pt: (0, 0))
    in_specs, args = [], []
    for cache in caches:
        for j in range(n_pages):
            in_specs.append(page_spec(cache.shape[2], j))
            args.append(cache)
    for a in list(news) + [qa, qb, qi, kiwi]:
        in_specs.append(seq_spec(a.shape[-1]))
        args.append(a)
    for a in (a_lambda, g_full, gmean):
        in_specs.append(full2(a))
        args.append(a)
    kern = functools.partial(_sample_attn_kernel, n_pages=n_pages, page=page, rows=rows, past=past, topk=topk,
                             idx_bits=idx_bits, lam_init=lam_init, t_valid=t_valid)
    return pl.pallas_call(
        kern,
        grid_spec=pltpu.PrefetchScalarGridSpec(
            num_scalar_prefetch=1, grid=(bsz,), in_specs=in_specs,
            out_specs=[seq_spec(GROUP_W), seq_spec(GROUP_W)],
            scratch_shapes=[pltpu.VMEM((rows, 1), I32)]),
        out_shape=[jax.ShapeDtypeStruct((bsz, rows, GROUP_W), F32)] * 2,
        compiler_params=_cparams(("parallel",)), name="sample_attn",
    )(page_table, *args)


def _attn_core(qst, n_groups, k_ref, vt_ref, n_full, n_kv, tq, tk, mask_fn, m_ref, l_ref, acc_ref, zero_masked):
    n_comp = n_groups // N_HEADS
    row_head = lax.broadcasted_iota(I32, (GROUP_W, 1), 0) // HEAD_W
    m_ref[...] = jnp.full(m_ref.shape, NEG_BIG, F32)
    l_ref[...] = jnp.zeros(l_ref.shape, F32)
    acc_ref[...] = jnp.zeros(acc_ref.shape, F32)
    grp = lambda a, g: a[:, g * tq:(g + 1) * tq]

    def step(c, n_blk, masked):
        width = n_blk * tk
        k0 = pl.multiple_of(c * tk, tk)
        kblk = k_ref[pl.ds(k0, width), :]
        vt = vt_ref[:, pl.ds(k0, width)]
        masks = [mask_fn(c + i, k0 + i * tk) for i in range(n_blk)] if masked else None

        def apply_masks(a, fill):
            parts = [jnp.where(masks[i], a[i * tk:(i + 1) * tk], fill) for i in range(n_blk)]
            return parts[0] if n_blk == 1 else jnp.concatenate(parts, axis=0)

        m_old = m_ref[...]
        l_old = l_ref[...]
        m_new, l_new, alpha, pb = [], [], [], []
        for g in range(n_groups):
            s = jnp.dot(kblk, grp(qst, g), preferred_element_type=F32)
            if masked:
                s = apply_masks(s, NEG_BIG)
            m_g = jnp.maximum(grp(m_old, g), jnp.max(s, axis=0, keepdims=True))
            a_g = jnp.exp(grp(m_old, g) - m_g)
            p = jnp.exp(s - m_g)
            if masked and zero_masked:
                p = apply_masks(p, 0.0)
            m_new.append(m_g)
            alpha.append(a_g)
            l_new.append(a_g * grp(l_old, g) + jnp.sum(p, axis=0, keepdims=True))
            pb.append(p.astype(BF16))
        m_ref[...] = jnp.concatenate(m_new, axis=1)
        l_ref[...] = jnp.concatenate(l_new, axis=1)
        for comp in range(n_comp):
            for h in range(N_HEADS):
                g = comp * N_HEADS + h
                rows = slice(h * HEAD_W, (h + 1) * HEAD_W)
                pv = jnp.dot(vt[rows], pb[g], preferred_element_type=F32)
                acc_ref[comp, rows, :] = acc_ref[comp, rows, :] * alpha[g] + pv

    def run(lo, hi, masked):
        def pair_body(i, carry):
            step(lo + 2 * i, 2, masked)
            return carry
        n = hi - lo
        if isinstance(n, int) and n == 0:
            return
        lax.fori_loop(0, n // 2, pair_body, 0)

        @pl.when(n % 2 == 1)
        def _():
            step(hi - 1, 1, masked)

    run(0, n_full, False)
    run(n_full, n_kv, True)


def _normalised(acc, l_row, base, tq):
    rl = 1.0 / l_row
    rf = jnp.concatenate(
        [jnp.broadcast_to(rl[:, (base + h) * tq:(base + h + 1) * tq], (HEAD_W, tq)) for h in range(N_HEADS)],
        axis=0)
    return acc * rf


def _num_kv_blocks(q0, tq, tk, past, s_tot):
    return jnp.minimum((past + q0 + tq + tk - 1) // tk, s_tot // tk)


def _attn_a_kernel(qt_ref, k_ref, vt_ref, lam_ref, g_ref, gm_ref, o_ref, m_ref, l_ref, acc_ref,
                   *, tq, tk, past, s_tot, lam_init):
    q0 = pl.program_id(1) * tq
    qt = qt_ref[...]
    row_grp = lax.broadcasted_iota(I32, (GROUP_W, 1), 0) // 32
    qst = jnp.concatenate(
        [jnp.where(row_grp == (2 * h + c), qt, 0.0) for c in range(2) for h in range(N_HEADS)],
        axis=1).astype(BF16)
    n_kv = _num_kv_blocks(q0, tq, tk, past, s_tot)
    n_full = jnp.minimum((past + q0 + 1) // tk, n_kv)
    row_pos = past + q0 + lax.broadcasted_iota(I32, (1, tq), 1)
    key_iota = lax.broadcasted_iota(I32, (tk, 1), 0)

    def mask_fn(c, k0):
        return (k0 + key_iota) <= row_pos

    _attn_core(qst, 2 * N_HEADS, k_ref, vt_ref, n_full, n_kv, tq, tk, mask_fn,
               m_ref, l_ref, acc_ref, zero_masked=False)
    l_row = l_ref[...]
    o0 = _normalised(acc_ref[0], l_row, 0, tq)
    o1 = _normalised(acc_ref[1], l_row, N_HEADS, tq)
    lv = lam_ref[...]
    lam = (jnp.exp(jnp.sum(lv[0:1] * lv[1:2], axis=1, keepdims=True))
           - jnp.exp(jnp.sum(lv[2:3] * lv[3:4], axis=1, keepdims=True)) + lam_init)
    o = (o0 - lam * o1).T
    hi, lo = _split_bf16(o * o)
    ms = (jnp.dot(hi, gm_ref[...], preferred_element_type=F32)
          + jnp.dot(lo, gm_ref[...], preferred_element_type=F32))
    o_ref[...] = o * lax.rsqrt(ms + LN_EPS) * g_ref[...] * (1.0 - lam_init)


def _attention_a(qt, k, vt, a_lambda, g_full, gmean, *, bsz, tq, tk, past, lam_init):
    t = qt.shape[1] // bsz
    s_tot = k.shape[0] // bsz
    nq = t // tq
    kern = functools.partial(_attn_a_kernel, tq=tq, tk=tk, past=past, s_tot=s_tot, lam_init=lam_init)
    return pl.pallas_call(
        kern, grid=(bsz, nq),
        in_specs=[pl.BlockSpec((GROUP_W, tq), lambda b, i: (0, b * nq + i)),
                  pl.BlockSpec((s_tot, GROUP_W), lambda b, i: (b, 0)),
                  pl.BlockSpec((GROUP_W, s_tot), lambda b, i: (0, b)),
                  pl.BlockSpec(a_lambda.shape, lambda b, i: (0, 0)),
                  pl.BlockSpec((1, GROUP_W), lambda b, i: (0, 0)),
                  pl.BlockSpec((GROUP_W, GROUP_W), lambda b, i: (0, 0))],
        out_specs=pl.BlockSpec((tq, GROUP_W), lambda b, i: (b * nq + i, 0)),
        out_shape=jax.ShapeDtypeStruct((bsz * t, GROUP_W), F32),
        scratch_shapes=[pltpu.VMEM((1, 2 * N_HEADS * tq), F32), pltpu.VMEM((1, 2 * N_HEADS * tq), F32),
                        pltpu.VMEM((2, GROUP_W, tq), F32)],
        compiler_params=_cparams(("parallel", "arbitrary")), name="attn_a",
    )(qt, k, vt, a_lambda, g_full, gmean)


def _dsa_kernel(qt_ref, k_ref, vt_ref, qit_ref, ki_ref, kiwit_ref, o_ref, key_ref, y_ref, m_ref, l_ref, acc_ref,
                *, tq, tk, past, s_tot, topk, idx_bits, t_valid):
    q0 = pl.program_id(1) * tq
    n_kv = _num_kv_blocks(q0, tq, tk, past, s_tot)
    row_pos = past + q0 + lax.broadcasted_iota(I32, (1, tq), 1)
    key_iota = lax.broadcasted_iota(I32, (tk, 1), 0)
    int_min = jnp.int32(INT_MIN)

    qit = qit_ref[...]
    blocks = []
    for h in range(N_HEADS):
        qh = qit[h * D_IDX:(h + 1) * D_IDX, :]
        hi, lo = _split_bf16(qh)
        blocks.append(jnp.concatenate([hi, lo, hi], axis=0))
    qi = jnp.concatenate(blocks, axis=1)
    kiwit = kiwit_ref[...]
    w_h = [kiwit[D_IDX + h:D_IDX + h + 1, :] for h in range(N_HEADS)]

    def score_body(c, carry):
        k0 = pl.multiple_of(c * tk, tk)
        kib = ki_ref[pl.ds(k0, tk), :]
        sc = jnp.dot(kib, qi, preferred_element_type=F32)
        isc = jnp.maximum(sc[:, 0:tq], 0.0) * w_h[0]
        for h in range(1, N_HEADS):
            isc = isc + jnp.maximum(sc[:, h * tq:(h + 1) * tq], 0.0) * w_h[h]
        isc = jnp.where(isc == 0.0, 0.0, isc)
        bits = pltpu.bitcast(isc, I32)
        key = bits ^ ((bits >> 31) & jnp.int32(0x7FFFFFFF))
        key_ref[c] = jnp.where((k0 + key_iota) <= row_pos, key, int_min)
        return carry

    lax.fori_loop(0, n_kv, score_body, 0)

    def count(pred):
        def block_count(c):
            ones = jnp.where(pred(key_ref[c], c * tk), 1.0, 0.0)
            return jnp.sum(ones.reshape(tk // 8, 8, tq), axis=0)

        def body(i, acc):
            return acc + block_count(2 * i) + block_count(2 * i + 1)

        acc = lax.fori_loop(0, n_kv // 2, body, jnp.zeros((8, tq), F32))
        odd = (n_kv & 1).astype(F32)
        acc = acc + odd * block_count(n_kv - 1)
        return jnp.sum(acc, axis=0, keepdims=True)

    kk = jnp.float32(topk)

    def bit_body(r, carry):
        prefix, n_ge, n_gt = carry
        cand_u = prefix | lax.shift_left(jnp.int32(1), 31 - r)
        cand_s = cand_u ^ int_min
        cnt = count(lambda blk, k0: blk >= cand_s)
        ok = cnt >= kk
        return jnp.where(ok, cand_u, prefix), jnp.where(ok, cnt, n_ge), jnp.where(ok, n_gt, cnt)

    n_all = jnp.full((1, tq), 1.0, F32) * (n_kv * tk).astype(F32)
    prefix, n_ge, n_gt = lax.fori_loop(
        0, 32, bit_body, (jnp.zeros((1, tq), I32), n_all, jnp.zeros((1, tq), F32)))
    thr = prefix ^ int_min
    r_need = kk - n_gt
    live = lax.broadcasted_iota(I32, (1, tq), 1) < t_valid
    any_tie = jnp.max(jnp.where(live, n_ge - kk, 0.0)) > 0.0
    y_ref[...] = jnp.full((1, tq), s_tot, I32)

    @pl.when(any_tie)
    def _():
        def ybody(r, y):
            cand = y | lax.shift_left(jnp.int32(1), idx_bits - 1 - r)
            cnt = count(lambda blk, k0: (blk == thr) & ((k0 + key_iota) < cand))
            return jnp.where(cnt < r_need, cand, y)
        y_ref[...] = lax.fori_loop(0, idx_bits, ybody, jnp.zeros((1, tq), I32))

    y_last = y_ref[...]

    def mask_fn(c, k0):
        blk = key_ref[c]
        pos = k0 + key_iota
        sel = (blk > thr) | ((blk == thr) & (pos <= y_last))
        return sel & (pos <= row_pos)

    qt = qt_ref[...]
    row_head = lax.broadcasted_iota(I32, (GROUP_W, 1), 0) // HEAD_W
    qst = jnp.concatenate([jnp.where(row_head == h, qt, 0.0) for h in range(N_HEADS)], axis=1).astype(BF16)
    _attn_core(qst, N_HEADS, k_ref, vt_ref, 0, n_kv, tq, tk, mask_fn, m_ref, l_ref, acc_ref, zero_masked=True)
    o_ref[...] = _normalised(acc_ref[0], l_ref[...], 0, tq).T


def _dsa(qt, k, vt, qit, ki3, kiwit, *, bsz, tq, tk, past, topk, t_valid):
    t = qt.shape[1] // bsz
    s_tot = k.shape[0] // bsz
    nq = t // tq
    idx_bits = int(math.ceil(math.log2(s_tot))) + 1
    kern = functools.partial(_dsa_kernel, tq=tq, tk=tk, past=past, s_tot=s_tot, topk=topk,
                             idx_bits=idx_bits, t_valid=t_valid)
    qspec = lambda w: pl.BlockSpec((w, tq), lambda b, i: (0, b * nq + i))
    return pl.pallas_call(
        kern, grid=(bsz, nq),
        in_specs=[qspec(GROUP_W),
                  pl.BlockSpec((s_tot, GROUP_W), lambda b, i: (b, 0)),
                  pl.BlockSpec((GROUP_W, s_tot), lambda b, i: (0, b)),
                  qspec(GROUP_W),
                  pl.BlockSpec((s_tot, 3 * D_IDX), lambda b, i: (b, 0)),
                  qspec(LANES)],
        out_specs=pl.BlockSpec((tq, GROUP_W), lambda b, i: (b * nq + i, 0)),
        out_shape=jax.ShapeDtypeStruct((bsz * t, GROUP_W), F32),
        scratch_shapes=[pltpu.VMEM((s_tot // tk, tk, tq), I32), pltpu.VMEM((1, tq), I32),
                        pltpu.VMEM((1, N_HEADS * tq), F32), pltpu.VMEM((1, N_HEADS * tq), F32),
                        pltpu.VMEM((1, GROUP_W, tq), F32)],
        compiler_params=_cparams(("parallel", "arbitrary")), name="dsa",
    )(qt, k, vt, qit, ki3, kiwit)


_HALO_C = 8
_HALO_D = 32


def _conv_kernel(*refs, tm, single_tile):
    if single_tile:
        (cb_ref, uc_ref, ud_ref, hc_ref, hd_ref, cw_ref, dw_ref, db_ref, lg_ref, lb_ref,
         oc_ref, od_ref, ec_ref, ed_ref) = refs
        pc_ref = pd_ref = None
    else:
        (cb_ref, uc_ref, ud_ref, pc_ref, pd_ref, hc_ref, hd_ref, cw_ref, dw_ref, db_ref, lg_ref, lb_ref,
         oc_ref, od_ref, ec_ref, ed_ref) = refs
    i = pl.program_id(1)

    @pl.when(i == 0)
    def _():
        ec_ref[0:_HALO_C] = hc_ref[0]
        ed_ref[0:_HALO_D] = hd_ref[0]

    if not single_tile:
        @pl.when(i > 0)
        def _():
            ec_ref[0:_HALO_C] = pc_ref[0]
            ed_ref[0:_HALO_D] = pd_ref[0]

    ec_ref[_HALO_C:_HALO_C + tm] = uc_ref[0]
    ed_ref[_HALO_D:_HALO_D + tm] = ud_ref[0]

    cw = cw_ref[...]
    acc = ec_ref[pl.ds(_HALO_C - (C_WIDTH - 1), tm), :] * cw[0:1]
    for w in range(1, C_WIDTH):
        acc = acc + ec_ref[pl.ds(_HALO_C - (C_WIDTH - 1) + w, tm), :] * cw[w:w + 1]
    oc_ref[0] = cb_ref[0] * acc

    dw = dw_ref[...]
    acc = ed_ref[pl.ds(_HALO_D - (D_WIDTH - 1), tm), :] * dw[0:1]
    for w in range(1, D_WIDTH):
        acc = acc + ed_ref[pl.ds(_HALO_D - (D_WIDTH - 1) + w, tm), :] * dw[w:w + 1]
    z = _layer_norm(acc + db_ref[...], lg_ref[...], lb_ref[...])
    od_ref[0] = z * _sigmoid(z)


def _convs(cb, uc, ud, hist_c, hist_d, c_w, d_w, d_b, ln_g, ln_b, *, tm):
    bsz, t, w = cb.shape
    nt = t // tm
    single = nt == 1
    main = pl.BlockSpec((1, tm, w), lambda b, i: (b, i, 0))
    prev_c = pl.BlockSpec((1, _HALO_C, w), lambda b, i: (b, jnp.maximum(i * (tm // _HALO_C) - 1, 0), 0))
    prev_d = pl.BlockSpec((1, _HALO_D, w), lambda b, i: (b, jnp.maximum(i * (tm // _HALO_D) - 1, 0), 0))
    hist_cs = pl.BlockSpec((1, _HALO_C, w), lambda b, i: (b, 0, 0))
    hist_ds = pl.BlockSpec((1, _HALO_D, w), lambda b, i: (b, 0, 0))
    full2 = lambda a: pl.BlockSpec(a.shape, lambda b, i: (0, 0))
    in_specs = [main, main, main] + ([] if single else [prev_c, prev_d]) + [hist_cs, hist_ds] + \
        [full2(c_w), full2(d_w), full2(d_b), full2(ln_g), full2(ln_b)]
    args = [cb, uc, ud] + ([] if single else [uc, ud]) + [hist_c, hist_d, c_w, d_w, d_b, ln_g, ln_b]
    kern = functools.partial(_conv_kernel, tm=tm, single_tile=single)
    return pl.pallas_call(
        kern, grid=(bsz, nt), in_specs=in_specs,
        out_specs=[main, main],
        out_shape=[jax.ShapeDtypeStruct((bsz, t, w), F32)] * 2,
        scratch_shapes=[pltpu.VMEM((_HALO_C + tm, w), F32), pltpu.VMEM((_HALO_D + tm, w), F32)],
        compiler_params=_cparams(("parallel", "arbitrary")), name="convs",
    )(*args)


def _mix_kernel(x_ref, oa_ref, ob_ref, oc_ref, od_ref, pw_ref, pwb_ref, wo_ref, g_ref, b_ref, h_ref, *, alpha):
    od = jnp.dot(od_ref[...].astype(BF16), pw_ref[...], preferred_element_type=F32) + pwb_ref[...]
    cat = jnp.concatenate([oa_ref[...], ob_ref[...], oc_ref[...], od], axis=1).astype(BF16)
    mix = jnp.dot(cat, wo_ref[...], preferred_element_type=F32)
    h_ref[...] = _layer_norm(alpha * x_ref[...] + mix, g_ref[...], b_ref[...])


def _mix(x, oa, ob, oc, od, pw, pwb, wo, g, b, *, tm, alpha):
    n, d = x.shape
    row = lambda w: pl.BlockSpec((tm, w), lambda i: (i, 0))
    full = lambda a: pl.BlockSpec(a.shape, lambda i: (0, 0))
    return pl.pallas_call(
        functools.partial(_mix_kernel, alpha=alpha), grid=(n // tm,),
        in_specs=[row(d), row(GROUP_W), row(GROUP_W), row(GROUP_W), row(GROUP_W),
                  full(pw), full(pwb), full(wo), full(g), full(b)],
        out_specs=row(d), out_shape=jax.ShapeDtypeStruct((n, d), F32),
        compiler_params=_cparams(("parallel",)), name="mix",
    )(x, oa, ob, oc, od, pw, pwb, wo, g, b)


def _staircase():
    return [(a, b) for a in range(PEER_TOPK) for b in range(PEER_TOPK) if (a + 1) * (b + 1) <= PEER_TOPK]


def _tree_rows(x, op):
    while x.shape[0] > 1:
        half = x.shape[0] // 2
        y = op(x[:half], x[half:2 * half])
        x = y if x.shape[0] % 2 == 0 else jnp.concatenate([y, x[2 * half:]], axis=0)
    return x


def _argmax_rows(x, row):
    v, i = x, row
    while v.shape[0] > 1:
        n = v.shape[0]
        half = n // 2
        pv = v[:2 * half].reshape((half, 2) + v.shape[1:])
        pi = i[:2 * half].reshape((half, 2) + i.shape[1:])
        take_b = pv[:, 1] > pv[:, 0]
        nv = jnp.where(take_b, pv[:, 1], pv[:, 0])
        ni = jnp.where(take_b, pi[:, 1], pi[:, 0])
        if n % 2:
            nv = jnp.concatenate([nv, v[2 * half:]], axis=0)
            ni = jnp.concatenate([ni, i[2 * half:]], axis=0)
        v, i = nv, ni
    return v, i


def _extract_top(x, payload, n_out, n_rows):
    row = lax.broadcasted_iota(I32, x.shape, 0).astype(F32)
    vals, idxs, pays = [], [], []
    for _ in range(n_out):
        m, idx = _argmax_rows(x, row)
        hit = row == idx
        vals.append(m)
        idxs.append(idx)
        if payload is not None:
            pays.append(_tree_rows(jnp.where(hit, payload, -1.0), jnp.maximum))
        x = jnp.where(hit, -jnp.inf, x)
    return vals, idxs, pays


def _peer_select_kernel(h_ref, wq_ref, sk_ref, eid_ref, g_ref, sv_ref, si_ref, *, tm):
    ht = h_ref[...].T.astype(BF16)
    n_groups = 2 * PEER_HEADS
    nt = tm // LANES

    def to_tiles(x):
        return pltpu.einshape("trl->rtl", jnp.stack([x[:, LANES * t:LANES * (t + 1)] for t in range(nt)], axis=0))

    def from_tiles(y):
        z = pltpu.einshape("rtl->trl", y)
        return jnp.concatenate([z[t] for t in range(nt)], axis=1)

    def group_body(g, carry):
        r0 = pl.multiple_of(g * N_KEYS, N_KEYS)
        qg = jnp.dot(wq_ref[pl.ds(r0, N_KEYS), :], ht, preferred_element_type=F32)
        st = jnp.dot(sk_ref[g % 2], qg.astype(BF16), preferred_element_type=F32)
        vals, idxs, _ = _extract_top(to_tiles(st), None, PEER_TOPK, N_KEYS)
        sv_ref[g] = jnp.concatenate(vals, axis=0)
        si_ref[g] = jnp.concatenate(idxs, axis=0)
        return carry

    lax.fori_loop(0, n_groups, group_body, 0)

    pairs = _staircase()

    def head_body(hd, carry):
        sv0 = sv_ref[2 * hd]; sv1 = sv_ref[2 * hd + 1]
        si0 = si_ref[2 * hd]; si1 = si_ref[2 * hd + 1]
        cand = jnp.concatenate([sv0[a:a + 1] + sv1[b:b + 1] for a, b in pairs], axis=0)
        cand_e = jnp.concatenate([si0[a:a + 1] * float(N_KEYS) + si1[b:b + 1] for a, b in pairs], axis=0)
        vals, _, pays = _extract_top(cand, cand_e, PEER_TOPK, len(pairs))
        fv = jnp.concatenate(vals, axis=0)
        e = jnp.exp(fv - fv[0:1])
        gate = e / jnp.sum(e, axis=0, keepdims=True)
        r0 = pl.multiple_of(hd * PEER_TOPK, PEER_TOPK)
        eid_ref[pl.ds(r0, PEER_TOPK), :] = from_tiles(jnp.concatenate(pays, axis=0)).astype(I32)
        g_ref[pl.ds(r0, PEER_TOPK), :] = from_tiles(gate)
        return carry

    lax.fori_loop(0, PEER_HEADS, head_body, 0)


_SELECT_TOKENS = 8 * LANES


def _peer_select(h, wq_t, sk, *, tm):
    n, d = h.shape
    n_slots = PEER_HEADS * PEER_TOPK
    return pl.pallas_call(
        functools.partial(_peer_select_kernel, tm=tm), grid=(n // tm,),
        in_specs=[pl.BlockSpec((tm, d), lambda i: (i, 0)),
                  pl.BlockSpec(wq_t.shape, lambda i: (0, 0)),
                  pl.BlockSpec(sk.shape, lambda i: (0, 0, 0))],
        out_specs=[pl.BlockSpec((n_slots, tm), lambda i: (0, i)),
                   pl.BlockSpec((n_slots, tm), lambda i: (0, i))],
        out_shape=[jax.ShapeDtypeStruct((n_slots, n), I32), jax.ShapeDtypeStruct((n_slots, n), F32)],
        scratch_shapes=[pltpu.VMEM((2 * PEER_HEADS, PEER_TOPK, tm // LANES, LANES), F32),
                        pltpu.VMEM((2 * PEER_HEADS, PEER_TOPK, tm // LANES, LANES), F32)],
        compiler_params=_cparams(("parallel",)), name="peer_select",
    )(h, wq_t, sk)


_GATE_TOKENS = 16


def _peer_gates_kernel(eid_ref, g_ref, w_ref, *, tm):
    sub = lax.broadcasted_iota(I32, (N_KEYS, PEER_HEADS * PEER_TOPK), 0)

    def token_gates(t):
        e = eid_ref[pl.ds(t, 1), :]
        g = g_ref[pl.ds(t, 1), :]
        oh_i = jnp.where((e >> 7) == sub, 1.0, 0.0).astype(BF16)
        g_j = jnp.where((e & (N_KEYS - 1)) == sub, g, 0.0).astype(BF16)
        return lax.dot_general(oh_i, g_j, _NT, preferred_element_type=F32)

    def body(b, carry):
        t0 = pl.multiple_of(b * _GATE_TOKENS, _GATE_TOKENS)
        w = jnp.stack([token_gates(t0 + k) for k in range(_GATE_TOKENS)], axis=0)
        w_ref[:, pl.ds(t0, _GATE_TOKENS), :] = pltpu.einshape("tij->itj", w).astype(BF16)
        return carry

    lax.fori_loop(0, tm // _GATE_TOKENS, body, 0)


def _peer_gates(eid, gate, *, tm):
    n, n_slots = eid.shape
    return pl.pallas_call(
        functools.partial(_peer_gates_kernel, tm=tm), grid=(n // tm,),
        in_specs=[pl.BlockSpec((tm, n_slots), lambda i: (i, 0)),
                  pl.BlockSpec((tm, n_slots), lambda i: (i, 0))],
        out_specs=pl.BlockSpec((N_KEYS, tm, N_KEYS), lambda i: (0, i, 0)),
        out_shape=jax.ShapeDtypeStruct((N_KEYS, n, N_KEYS), BF16),
        compiler_params=_cparams(("parallel",)), name="peer_gates",
    )(eid, gate)


def _gelu(x):
    return 0.5 * x * (1.0 + lax.erf(x * (2.0 ** -0.5)))


def _peer_dense_kernel(h_ref, w_ref, u_ref, v_ref, f_ref):
    @pl.when(pl.program_id(1) == 0)
    def _():
        f_ref[...] = jnp.zeros(f_ref.shape, F32)

    act = _gelu(lax.dot_general(h_ref[...], u_ref[0].astype(BF16), _NT, preferred_element_type=F32))
    y = jnp.concatenate(
        [w_ref[i].astype(F32) * act[:, i * N_KEYS:(i + 1) * N_KEYS] for i in range(w_ref.shape[0])],
        axis=1).astype(BF16)
    f_ref[...] += jnp.dot(y, v_ref[0].astype(BF16), preferred_element_type=F32)


def _peer_dense(hb, wmat, u, v, layer, *, tm, te):
    n, d = hb.shape
    n_exp = u.shape[1]
    return pl.pallas_call(
        _peer_dense_kernel, grid=(n // tm, n_exp // te),
        in_specs=[pl.BlockSpec((tm, d), lambda i, e: (i, 0)),
                  pl.BlockSpec((te // N_KEYS, tm, N_KEYS), lambda i, e: (e, i, 0)),
                  pl.BlockSpec((1, te, d), lambda i, e: (layer, e, 0)),
                  pl.BlockSpec((1, te, d), lambda i, e: (layer, e, 0))],
        out_specs=pl.BlockSpec((tm, d), lambda i, e: (i, 0)),
        out_shape=jax.ShapeDtypeStruct((n, d), F32),
        compiler_params=_cparams(("parallel", "arbitrary")), name="peer_dense",
    )(hb, wmat, u, v)


def _final_kernel(h_ref, f_ref, p_ref, g_ref, b_ref, wg_ref, bg_ref, wp_ref, y_ref, *, alpha):
    h2 = _layer_norm(alpha * h_ref[...] + f_ref[...], g_ref[...], b_ref[...])
    gate = _sigmoid(jnp.dot(h2.astype(BF16), wg_ref[...], preferred_element_type=F32) + bg_ref[...])
    emb = jnp.dot(p_ref[...].astype(BF16), wp_ref[...], preferred_element_type=F32)
    y_ref[...] = h2 + gate * emb


def _final(h, f, p, g, b, wg, bg, wp, *, tm, alpha):
    n, d = h.shape
    row = lambda w: pl.BlockSpec((tm, w), lambda i: (i, 0))
    full = lambda a: pl.BlockSpec(a.shape, lambda i: (0, 0))
    return pl.pallas_call(
        functools.partial(_final_kernel, alpha=alpha), grid=(n // tm,),
        in_specs=[row(d), row(d), row(p.shape[1]), full(g), full(b), full(wg), full(bg), full(wp)],
        out_specs=row(d), out_shape=jax.ShapeDtypeStruct((n, d), F32),
        compiler_params=_cparams(("parallel",)), name="final",
    )(h, f, p, g, b, wg, bg, wp)


def _pack_w_in(w_in):
    d = w_in.shape[0]
    cols = [w_in[:, 0:1792], w_in[:, 1792:1860], jnp.zeros((d, 60), w_in.dtype), w_in[:, 1860:3140]]
    nat = jnp.concatenate(cols, axis=1).astype(BF16)
    seg = lambda name, width=GROUP_W: nat[:, _SEG[name]:_SEG[name] + width]
    tr = jnp.concatenate([seg("qa"), seg("qb"), seg("qi"), seg("kiwi", 128), seg("va"), seg("vb")], axis=1).T
    return nat, tr


def _row(a):
    return a.reshape(1, -1)


def _tile_rows(n, pref):
    t = pref
    while n % t:
        t //= 2
    return t


def _prep_weights(lw, depth):
    (w_in, a_lambda, a_subln_g, c_conv_w, d_conv_w, d_conv_b, d_ln_g, d_ln_b, d_pw_w, d_pw_b,
     w_out, ln1_g, ln1_b, peer_wq, peer_subkeys, peer_u, peer_v, ln2_g, ln2_b,
     ple_w, ple_gate_w, ple_gate_b) = lw
    head_of_lane = jnp.arange(GROUP_W) // HEAD_W
    gmean = (head_of_lane[:, None] == head_of_lane[None, :]).astype(BF16) * (1.0 / HEAD_W)
    w_nat, w_tr = _pack_w_in(w_in)
    return dict(
        w_in=w_nat, w_in_t=w_tr, a_lambda=a_lambda, a_g=_row(jnp.tile(a_subln_g, N_HEADS)),
        gmean=gmean.astype(BF16),
        c_w=c_conv_w, d_w=d_conv_w, d_b=_row(d_conv_b), d_ln_g=_row(d_ln_g), d_ln_b=_row(d_ln_b),
        d_pw=d_pw_w.astype(BF16), d_pwb=_row(d_pw_b), w_out=w_out.astype(BF16),
        ln1_g=_row(ln1_g), ln1_b=_row(ln1_b),
        wq_t=peer_wq.T.astype(BF16), sk=peer_subkeys.astype(BF16),
        u=peer_u, v=peer_v,
        ln2_g=_row(ln2_g), ln2_b=_row(ln2_b),
        ple_w=ple_w.astype(BF16), wg=ple_gate_w.astype(BF16), bg=_row(ple_gate_b),
        alpha=float((2 * depth) ** 0.25),
    )


def _index_keys(ki):
    k_hi = ki.astype(BF16)
    k_lo = (ki - k_hi.astype(F32)).astype(BF16)
    return jnp.concatenate([k_hi, k_hi, k_lo], axis=-1)


def _token_tail(w, x, oa, ob, oc, od, valid_rows, p):
    n = x.shape[0]
    h = _mix(x, oa, ob, oc, od, w["d_pw"], w["d_pwb"], w["w_out"], w["ln1_g"], w["ln1_b"],
             tm=_tile_rows(n, 512), alpha=w["alpha"])
    if valid_rows is not None:
        bsz, t_pad, t = valid_rows
        h = h.reshape(bsz, t_pad, -1)[:, :t].reshape(bsz * t, -1)
    n = h.shape[0]
    n_sel = -(-n // _SELECT_TOKENS) * _SELECT_TOKENS
    h_sel = h if n_sel == n else jnp.pad(h, ((0, n_sel - n), (0, 0)))
    eid_t, gate_t = _peer_select(h_sel, w["wq_t"], w["sk"], tm=_SELECT_TOKENS)
    n_gate = -(-n // _GATE_TOKENS) * _GATE_TOKENS
    wmat = _peer_gates(eid_t[:, :n_gate].T, gate_t[:, :n_gate].T, tm=_tile_rows(n_gate, 128))
    f = _peer_dense(h.astype(BF16), wmat, w["u"], w["v"], w["layer"], tm=_tile_rows(n, 1024), te=8 * N_KEYS)
    return _final(h, f, p, w["ln2_g"], w["ln2_b"], w["wg"], w["bg"], w["ple_w"],
                  tm=_tile_rows(n, 512), alpha=w["alpha"])


def _layer(layer_idx, depth, xp, xs, caches, state_c, state_d, page_table, pp, ps, lw):
    w = _prep_weights(lw, depth)
    w["layer"] = layer_idx
    bp, tp, d = xp.shape
    bs, ts, _ = xs.shape
    ts_pad = 8
    n_pages = page_table.shape[1]
    page = caches[0].shape[3]
    past = n_pages * page
    lam_init = 0.8 - 0.6 * math.exp(-0.3 * (layer_idx + 1))

    pr = _project(xp.reshape(bp * tp, d), w["w_in"], w["w_in_t"], _tile_rows(bp * tp, 512), feature_major=True)
    tq = _tile_rows(tp, 256)
    tk = _tile_rows(tp, 256)
    oa = _attention_a(pr["qat"], pr["kab"], pr["vat"], w["a_lambda"], w["a_g"], w["gmean"],
                      bsz=bp, tq=tq, tk=tk, past=0, lam_init=lam_init)
    tq_b = _tile_rows(tp, 512)
    ob = _dsa(pr["qbt"], pr["kbb"], pr["vbt"], pr["qit"], _index_keys(pr["kiwi"][:, :D_IDX]), pr["kiwit"],
              bsz=bp, tq=tq_b, tk=tk, past=0, topk=min(DSA_TOPK, tp // 4), t_valid=tq_b)
    r3p = lambda a: a.reshape(bp, tp, a.shape[-1])
    zc = jnp.zeros((bp, _HALO_C, GROUP_W), F32)
    zd = jnp.zeros((bp, _HALO_D, GROUP_W), F32)
    oc, od = _convs(r3p(pr["cb"]), r3p(pr["uc"]), r3p(pr["ud"]), zc, zd, w["c_w"], w["d_w"], w["d_b"],
                    w["d_ln_g"], w["d_ln_b"], tm=_tile_rows(tp, 512))
    yp = _token_tail(w, xp.reshape(bp * tp, d), oa, ob, oc.reshape(bp * tp, -1), od.reshape(bp * tp, -1),
                     None, pp.reshape(bp * tp, -1)).reshape(bp, tp, d)
    new_p = (r3p(pr["ka"]), r3p(pr["va"]), r3p(pr["kb"]), r3p(pr["vb"]), r3p(pr["kiwi"])[:, :, :D_IDX],
             jnp.concatenate([zc, r3p(pr["uc"])], axis=1)[:, -(C_WIDTH - 1):],
             jnp.concatenate([zd, r3p(pr["ud"])], axis=1)[:, -(D_WIDTH - 1):])

    xs_pad = jnp.pad(xs, ((0, 0), (0, ts_pad - ts), (0, 0))).reshape(bs * ts_pad, d)
    sr = _project(xs_pad, w["w_in"], w["w_in_t"], _tile_rows(bs * ts_pad, 512), feature_major=False)
    r3s = lambda a: a.reshape(bs, ts_pad, a.shape[-1])
    news = [r3s(sr["ka"]), r3s(sr["va"]), r3s(sr["kb"]), r3s(sr["vb"]), r3s(sr["kiwi"])[:, :, :D_IDX]]
    oa_s, ob_s = _sample_attention(page_table, caches, layer_idx, news, r3s(sr["qa"]), r3s(sr["qb"]), r3s(sr["qi"]),
                                   r3s(sr["kiwi"]), w["a_lambda"], w["a_g"], w["gmean"],
                                   topk=min(DSA_TOPK, (past + ts) // 4), lam_init=lam_init, t_valid=ts)
    unpad = lambda a: a.reshape(bs * ts_pad, GROUP_W)
    hist_c = jnp.concatenate([jnp.zeros((bs, _HALO_C - (C_WIDTH - 1), GROUP_W), F32), state_c], axis=1)
    hist_d = jnp.concatenate([jnp.zeros((bs, _HALO_D - (D_WIDTH - 1), GROUP_W), F32), state_d], axis=1)
    oc_s, od_s = _convs(r3s(sr["cb"]), r3s(sr["uc"]), r3s(sr["ud"]), hist_c, hist_d, w["c_w"], w["d_w"],
                        w["d_b"], w["d_ln_g"], w["d_ln_b"], tm=ts_pad)
    ys = _token_tail(w, xs_pad, unpad(oa_s), unpad(ob_s), oc_s.reshape(bs * ts_pad, -1),
                     od_s.reshape(bs * ts_pad, -1), (bs, ts_pad, ts), ps.reshape(bs * ts, -1)).reshape(bs, ts, d)
    cut = lambda a: r3s(a)[:, :ts]
    new_s = (cut(sr["ka"]), cut(sr["va"]), cut(sr["kb"]), cut(sr["vb"]), cut(sr["kiwi"])[:, :, :D_IDX],
             jnp.concatenate([state_c, cut(sr["uc"])], axis=1)[:, -(C_WIDTH - 1):],
             jnp.concatenate([state_d, cut(sr["ud"])], axis=1)[:, -(D_WIDTH - 1):])
    return yp, ys, new_p, new_s


def kernel(x_prompt, x_sample, cache_a_k, cache_a_v, cache_b_k, cache_b_v, cache_b_kidx, state_c_conv, state_d_conv, page_table, p_prompt, p_sample, w_in, a_lambda, a_subln_g, c_conv_w, d_conv_w, d_conv_b, d_ln_g, d_ln_b, d_pw_w, d_pw_b, w_out, ln1_g, ln1_b, peer_wq, peer_subkeys, peer_u, peer_v, ln2_g, ln2_b, ple_w, ple_gate_w, ple_gate_b):
    depth = w_in.shape[0]
    n_pool, page = cache_a_k.shape[1], cache_a_k.shape[2]
    feature_major = lambda c: jnp.moveaxis(c, 2, -1).reshape(depth, n_pool, -1, page)
    caches = [feature_major(c) for c in (cache_a_k, cache_a_v, cache_b_k, cache_b_v, cache_b_kidx)]
    xp, xs = x_prompt, x_sample
    news_p, news_s = [], []
    for l in range(depth):
        lw = (w_in[l], a_lambda[l], a_subln_g[l], c_conv_w[l], d_conv_w[l], d_conv_b[l], d_ln_g[l], d_ln_b[l],
              d_pw_w[l], d_pw_b[l], w_out[l], ln1_g[l], ln1_b[l], peer_wq[l], peer_subkeys[l], peer_u,
              peer_v, ln2_g[l], ln2_b[l], ple_w[l], ple_gate_w[l], ple_gate_b[l])
        xp, xs, new_p, new_s = _layer(l, depth, xp, xs, caches, state_c_conv[l], state_d_conv[l],
                                      page_table, p_prompt[l], p_sample[l], lw)
        news_p.append(new_p)
        news_s.append(new_s)

    bp, tp, _ = x_prompt.shape
    bs, ts, _ = x_sample.shape
    shapes = [(N_HEADS, 2, 32), (N_HEADS, HEAD_W), (N_HEADS, HEAD_W), (N_HEADS, HEAD_W), (D_IDX,)]

    def stack(news, i, lead):
        a = jnp.stack([n[i] for n in news], axis=0)
        return a.reshape((depth,) + lead + shapes[i]) if i < 5 else a

    outs = [xp, xs]
    for i in range(7):
        outs.append(stack(news_p, i, (bp, tp)))
        outs.append(stack(news_s, i, (bs, ts)))
    return tuple(outs)
```

```python
import functools
import math

import jax
import jax.numpy as jnp
from jax import lax
from jax.experimental import pallas as pl
from jax.experimental.pallas import tpu as pltpu

F32 = jnp.float32
BF16 = jnp.bfloat16
I32 = jnp.int32

LN_EPS = 1e-5
N_HEADS = 4
HEAD_W = 64
GROUP_W = 256
D_IDX = 64
DSA_TOPK = 256
C_WIDTH = 3
D_WIDTH = 31
PEER_HEADS = 8
PEER_TOPK = 16
N_KEYS = 128
LANES = 128
NEG_BIG = -1e30
INT_MIN = -2 ** 31
VMEM_LIMIT = 56 * 1024 * 1024

_SEG = dict(qa=0, ka=256, va=512, qb=768, kb=1024, vb=1280, qi=1536, kiwi=1792,
            cb=1920, cc=2176, cx=2432, da=2688, dg=2944)
_N_PACKED = 3200
_SEG_T = dict(qa=0, qb=256, qi=512, kiwi=768, va=896, vb=1152)
_N_PACKED_T = 1408

_NT = (((1,), (1,)), ((), ()))


def _cparams(sem):
    return pltpu.CompilerParams(dimension_semantics=sem, vmem_limit_bytes=VMEM_LIMIT)


def _split_bf16(x):
    hi = x.astype(BF16)
    lo = (x - hi.astype(F32)).astype(BF16)
    return hi, lo


def _layer_norm(x, g, b):
    mu = jnp.mean(x, axis=-1, keepdims=True)
    xc = x - mu
    var = jnp.mean(xc * xc, axis=-1, keepdims=True)
    return xc * lax.rsqrt(var + LN_EPS) * g + b


def _sigmoid(x):
    return 1.0 / (1.0 + jnp.exp(-x))


_PROJ_COMMON = ("ka", "va", "kb", "vb", "kiwi", "cb", "uc", "ud")
_PROJ_FEATURE_MAJOR = ("kab", "kbb", "qat", "qbt", "qit", "kiwit", "vat", "vbt")
_PROJ_ROW_MAJOR = ("qa", "qb", "qi")


def _proj_kernel(x_ref, w_ref, wt_ref, *out_refs, names, scale_a, scale_b):
    o = dict(zip(names, out_refs))
    xb = x_ref[...].astype(BF16)

    def seg(name, width=GROUP_W):
        off = _SEG[name]
        return jnp.dot(xb, w_ref[:, off:off + width], preferred_element_type=F32)

    def seg_t(name, width=GROUP_W):
        off = _SEG_T[name]
        return lax.dot_general(wt_ref[off:off + width, :], xb, _NT, preferred_element_type=F32)

    ka = seg("ka"); kb = seg("kb")
    o["ka"][...] = ka; o["kb"][...] = kb
    o["va"][...] = seg("va"); o["vb"][...] = seg("vb")
    o["kiwi"][...] = seg("kiwi", LANES)
    o["cb"][...] = seg("cb")
    o["uc"][...] = seg("cc") * seg("cx")
    o["ud"][...] = seg("da") * _sigmoid(seg("dg"))
    if "qat" in o:
        o["kab"][...] = ka.astype(BF16); o["kbb"][...] = kb.astype(BF16)
        o["qat"][...] = seg_t("qa") * scale_a
        o["qbt"][...] = seg_t("qb") * scale_b
        o["qit"][...] = seg_t("qi")
        o["kiwit"][...] = seg_t("kiwi", LANES)
        o["vat"][...] = seg_t("va").astype(BF16)
        o["vbt"][...] = seg_t("vb").astype(BF16)
    else:
        o["qa"][...] = seg("qa") * scale_a
        o["qb"][...] = seg("qb") * scale_b
        o["qi"][...] = seg("qi")


def _project(x, w_packed, wt_packed, tm, feature_major):
    n, d = x.shape
    nat = lambda w, dt=F32: (jax.ShapeDtypeStruct((n, w), dt), pl.BlockSpec((tm, w), lambda i: (i, 0)))
    tr = lambda w, dt=F32: (jax.ShapeDtypeStruct((w, n), dt), pl.BlockSpec((w, tm), lambda i: (0, i)))
    outs = [nat(256), nat(256), nat(256), nat(256), nat(LANES), nat(256), nat(256), nat(256)]
    if feature_major:
        names = _PROJ_COMMON + _PROJ_FEATURE_MAJOR
        outs += [nat(256, BF16), nat(256, BF16), tr(256), tr(256), tr(256), tr(LANES), tr(256, BF16), tr(256, BF16)]
    else:
        names = _PROJ_COMMON + _PROJ_ROW_MAJOR
        outs += [nat(256), nat(256), nat(256)]
    kern = functools.partial(_proj_kernel, names=names, scale_a=float(32 ** -0.5), scale_b=float(HEAD_W ** -0.5))
    res = pl.pallas_call(
        kern, grid=(n // tm,),
        in_specs=[pl.BlockSpec((tm, d), lambda i: (i, 0)),
                  pl.BlockSpec((d, _N_PACKED), lambda i: (0, 0)),
                  pl.BlockSpec((_N_PACKED_T, d), lambda i: (0, 0))],
        out_specs=[o[1] for o in outs], out_shape=[o[0] for o in outs],
        compiler_params=_cparams(("parallel",)), name="proj",
    )(x, w_packed, wt_packed)
    return dict(zip(names, res))


def _lane_heads(o, l, base, rows, lane_head):
    out = None
    for h in range(N_HEADS):
        r0 = (base + h) * rows
        part = jnp.where(lane_head == h, o[r0:r0 + rows] / l[r0:r0 + rows], 0.0)
        out = part if out is None else out + part
    return out


def _sample_attn_kernel(pt_ref, *refs, n_pages, page, rows, past, topk, idx_bits, lam_init, t_valid):
    npg = n_pages
    ak, av, bk, bv, kic = (refs[i * npg:(i + 1) * npg] for i in range(5))
    (nak, nav, nbk, nbv, nki, qa_ref, qb_ref, qi_ref, kiwi_ref, lam_ref, g_ref, gm_ref,
     oa_ref, ob_ref, y_ref) = refs[5 * npg:]
    nblk = npg + 1
    s_tot = nblk * page
    lane_head = lax.broadcasted_iota(I32, (1, GROUP_W), 1) // HEAD_W
    col_last = past + lax.broadcasted_iota(I32, (1, page), 1)
    t_row = lax.broadcasted_iota(I32, (rows, 1), 0)
    int_min = jnp.int32(INT_MIN)

    def blocks(page_refs, new_ref, dtype):
        new = new_ref[0]
        pad = jnp.zeros((page - rows, new.shape[-1]), F32)
        return [r[0, 0].astype(dtype) for r in page_refs] + [jnp.concatenate([new, pad], axis=0).T.astype(dtype)]

    def softmax_pv(q_stack, n_groups, k_blocks, v_blocks, masks):
        def masked(s, mk, fill):
            if mk is None:
                return s
            s3 = jnp.where(mk[None], s.reshape(n_groups, rows, page), fill)
            return s3.reshape(n_groups * rows, page)
        s = [masked(jnp.dot(q_stack, kb, preferred_element_type=F32), mk, NEG_BIG)
             for kb, mk in zip(k_blocks, masks)]
        m = s[0].max(axis=1, keepdims=True)
        for sj in s[1:]:
            m = jnp.maximum(m, sj.max(axis=1, keepdims=True))
        l = None
        o = None
        for sj, vb, mk in zip(s, v_blocks, masks):
            p = masked(jnp.exp(sj - m), mk, 0.0)
            lj = p.sum(axis=1, keepdims=True)
            oj = lax.dot_general(p.astype(BF16), vb, _NT, preferred_element_type=F32)
            l = lj if l is None else l + lj
            o = oj if o is None else o + oj
        return o, l

    causal_last = col_last <= (past + t_row)

    qa = qa_ref[0]
    lane_grp = lax.broadcasted_iota(I32, (1, GROUP_W), 1) // 32
    qa_stack = jnp.concatenate(
        [jnp.where(lane_grp == (2 * h + c), qa, 0.0) for c in range(2) for h in range(N_HEADS)],
        axis=0).astype(BF16)
    o, l = softmax_pv(qa_stack, 2 * N_HEADS, blocks(ak, nak, BF16), blocks(av, nav, BF16),
                      [None] * npg + [causal_last])
    o0 = _lane_heads(o, l, 0, rows, lane_head)
    o1 = _lane_heads(o, l, N_HEADS, rows, lane_head)
    lv = lam_ref[...]
    lam = (jnp.exp(jnp.sum(lv[0:1] * lv[1:2], axis=1, keepdims=True))
           - jnp.exp(jnp.sum(lv[2:3] * lv[3:4], axis=1, keepdims=True)) + lam_init)
    od = o0 - lam * o1
    hi, lo = _split_bf16(od * od)
    ms = (jnp.dot(hi, gm_ref[...], preferred_element_type=F32)
          + jnp.dot(lo, gm_ref[...], preferred_element_type=F32))
    oa_ref[0] = od * lax.rsqrt(ms + LN_EPS) * g_ref[...] * (1.0 - lam_init)

    qi = qi_ref[0]
    qi_rows = jnp.concatenate([qi[:, h * D_IDX:(h + 1) * D_IDX] for h in range(N_HEADS)], axis=0)
    q_hi, q_lo = _split_bf16(qi_rows)
    kiwi = kiwi_ref[0]
    w_h = [kiwi[:, D_IDX + h:D_IDX + h + 1] for h in range(N_HEADS)]
    key_blocks = []
    for j, kib in enumerate(blocks(kic, nki, F32)):
        k_hi, k_lo = _split_bf16(kib)
        sc = (jnp.dot(q_hi, k_hi, preferred_element_type=F32)
              + jnp.dot(q_lo, k_hi, preferred_element_type=F32)
              + jnp.dot(q_hi, k_lo, preferred_element_type=F32))
        isc = jnp.maximum(sc[0:rows], 0.0) * w_h[0]
        for h in range(1, N_HEADS):
            isc = isc + jnp.maximum(sc[h * rows:(h + 1) * rows], 0.0) * w_h[h]
        isc = jnp.where(isc == 0.0, 0.0, isc)
        bits = pltpu.bitcast(isc, I32)
        key = bits ^ ((bits >> 31) & jnp.int32(0x7FFFFFFF))
        if j == npg:
            key = jnp.where(causal_last, key, int_min)
        key_blocks.append(key)
    keys = jnp.concatenate(key_blocks, axis=1)
    cols = lax.broadcasted_iota(I32, (1, s_tot), 1)
    kk = jnp.float32(topk)
    count = lambda pred: jnp.sum(jnp.where(pred, 1.0, 0.0), axis=1, keepdims=True)

    def bit_body(r, prefix):
        b1 = lax.shift_left(jnp.int32(1), 31 - 2 * r)
        b0 = lax.shift_left(jnp.int32(1), 30 - 2 * r)
        enough = lambda cand_u: count(keys >= (cand_u ^ int_min)) >= kk
        c10 = prefix | b1
        c11 = c10 | b0
        c01 = prefix | b0
        return jnp.where(enough(c10), jnp.where(enough(c11), c11, c10), jnp.where(enough(c01), c01, prefix))

    thr = lax.fori_loop(0, 16, bit_body, jnp.zeros((rows, 1), I32)) ^ int_min
    r_need = kk - count(keys > thr)
    any_tie = jnp.max(jnp.where(t_row < t_valid, count(keys >= thr) - kk, 0.0)) > 0.0
    y_ref[...] = jnp.full((rows, 1), s_tot, I32)

    @pl.when(any_tie)
    def _():
        def ybody(r, y):
            cand = y | lax.shift_left(jnp.int32(1), idx_bits - 1 - r)
            return jnp.where(count((keys == thr) & (cols < cand)) < r_need, cand, y)
        y_ref[...] = lax.fori_loop(0, idx_bits, ybody, jnp.zeros((rows, 1), I32))

    sel = ((keys > thr) | ((keys == thr) & (cols <= y_ref[...]))) & (cols <= (past + t_row))
    qb = qb_ref[0]
    qb_stack = jnp.concatenate([jnp.where(lane_head == h, qb, 0.0) for h in range(N_HEADS)], axis=0).astype(BF16)
    o, l = softmax_pv(qb_stack, N_HEADS, blocks(bk, nbk, BF16), blocks(bv, nbv, BF16),
                      [sel[:, j * page:(j + 1) * page] for j in range(nblk)])
    ob_ref[0] = _lane_heads(o, l, 0, rows, lane_head)


def _sample_attention(page_table, caches, layer, news, qa, qb, qi, kiwi, a_lambda, g_full, gmean,
                      *, topk, lam_init, t_valid):
    bsz, n_pages = page_table.shape
    page = caches[0].shape[3]
    rows = qa.shape[1]
    past = n_pages * page
    idx_bits = int(math.ceil(math.log2((n_pages + 1) * page))) + 1
    page_spec = lambda c, j: pl.BlockSpec((1, 1, c, page), lambda b, pt: (layer, pt[b, j], 0, 0))
    seq_spec = lambda c: pl.BlockSpec((1, rows, c), lambda b, pt: (b, 0, 0))
    full2 = lambda a: pl.BlockSpec(a.shape, lambda b, pt: (0, 0))
    in_specs, args = [], []
    for cache in caches:
        for j in range(n_pages):
            in_specs.append(page_spec(cache.shape[2], j))
            args.append(cache)
    for a in list(news) + [qa, qb, qi, kiwi]:
        in_specs.append(seq_spec(a.shape[-1]))
        args.append(a)
    for a in (a_lambda, g_full, gmean):
        in_specs.append(full2(a))
        args.append(a)
    kern = functools.partial(_sample_attn_kernel, n_pages=n_pages, page=page, rows=rows, past=past, topk=topk,
                             idx_bits=idx_bits, lam_init=lam_init, t_valid=t_valid)
    return pl.pallas_call(
        kern,
        grid_spec=pltpu.PrefetchScalarGridSpec(
            num_scalar_prefetch=1, grid=(bsz,), in_specs=in_specs,
            out_specs=[seq_spec(GROUP_W), seq_spec(GROUP_W)],
            scratch_shapes=[pltpu.VMEM((rows, 1), I32)]),
        out_shape=[jax.ShapeDtypeStruct((bsz, rows, GROUP_W), F32)] * 2,
        compiler_params=_cparams(("parallel",)), name="sample_attn",
    )(page_table, *args)


def _attn_core(qst, n_groups, k_ref, vt_ref, n_full, n_kv, tq, tk, mask_fn, m_ref, l_ref, acc_ref, zero_masked):
    n_comp = n_groups // N_HEADS
    row_head = lax.broadcasted_iota(I32, (GROUP_W, 1), 0) // HEAD_W
    m_ref[...] = jnp.full(m_ref.shape, NEG_BIG, F32)
    l_ref[...] = jnp.zeros(l_ref.shape, F32)
    acc_ref[...] = jnp.zeros(acc_ref.shape, F32)
    grp = lambda a, g: a[:, g * tq:(g + 1) * tq]

    def step(c, n_blk, masked):
        width = n_blk * tk
        k0 = pl.multiple_of(c * tk, tk)
        kblk = k_ref[pl.ds(k0, width), :]
        vt = vt_ref[:, pl.ds(k0, width)]
        masks = [mask_fn(c + i, k0 + i * tk) for i in range(n_blk)] if masked else None

        def apply_masks(a, fill):
            parts = [jnp.where(masks[i], a[i * tk:(i + 1) * tk], fill) for i in range(n_blk)]
            return parts[0] if n_blk == 1 else jnp.concatenate(parts, axis=0)

        m_old = m_ref[...]
        l_old = l_ref[...]
        m_new, l_new, alpha, pb = [], [], [], []
        for g in range(n_groups):
            s = jnp.dot(kblk, grp(qst, g), preferred_element_type=F32)
            if masked:
                s = apply_masks(s, NEG_BIG)
            m_g = jnp.maximum(grp(m_old, g), jnp.max(s, axis=0, keepdims=True))
            a_g = jnp.exp(grp(m_old, g) - m_g)
            p = jnp.exp(s - m_g)
            if masked and zero_masked:
                p = apply_masks(p, 0.0)
            m_new.append(m_g)
            alpha.append(a_g)
            l_new.append(a_g * grp(l_old, g) + jnp.sum(p, axis=0, keepdims=True))
            pb.append(p.astype(BF16))
        m_ref[...] = jnp.concatenate(m_new, axis=1)
        l_ref[...] = jnp.concatenate(l_new, axis=1)
        for comp in range(n_comp):
            for h in range(N_HEADS):
                g = comp * N_HEADS + h
                rows = slice(h * HEAD_W, (h + 1) * HEAD_W)
                pv = jnp.dot(vt[rows], pb[g], preferred_element_type=F32)
                acc_ref[comp, rows, :] = acc_ref[comp, rows, :] * alpha[g] + pv

    def run(lo, hi, masked):
        def pair_body(i, carry):
            step(lo + 2 * i, 2, masked)
            return carry
        n = hi - lo
        if isinstance(n, int) and n == 0:
            return
        lax.fori_loop(0, n // 2, pair_body, 0)

        @pl.when(n % 2 == 1)
        def _():
            step(hi - 1, 1, masked)

    run(0, n_full, False)
    run(n_full, n_kv, True)


def _normalised(acc, l_row, base, tq):
    rl = 1.0 / l_row
    rf = jnp.concatenate(
        [jnp.broadcast_to(rl[:, (base + h) * tq:(base + h + 1) * tq], (HEAD_W, tq)) for h in range(N_HEADS)],
        axis=0)
    return acc * rf


def _num_kv_blocks(q0, tq, tk, past, s_tot):
    return jnp.minimum((past + q0 + tq + tk - 1) // tk, s_tot // tk)


def _attn_a_kernel(qt_ref, k_ref, vt_ref, lam_ref, g_ref, gm_ref, o_ref, m_ref, l_ref, acc_ref,
                   *, tq, tk, past, s_tot, lam_init):
    q0 = pl.program_id(1) * tq
    qt = qt_ref[...]
    row_grp = lax.broadcasted_iota(I32, (GROUP_W, 1), 0) // 32
    qst = jnp.concatenate(
        [jnp.where(row_grp == (2 * h + c), qt, 0.0) for c in range(2) for h in range(N_HEADS)],
        axis=1).astype(BF16)
    n_kv = _num_kv_blocks(q0, tq, tk, past, s_tot)
    n_full = jnp.minimum((past + q0 + 1) // tk, n_kv)
    row_pos = past + q0 + lax.broadcasted_iota(I32, (1, tq), 1)
    key_iota = lax.broadcasted_iota(I32, (tk, 1), 0)

    def mask_fn(c, k0):
        return (k0 + key_iota) <= row_pos

    _attn_core(qst, 2 * N_HEADS, k_ref, vt_ref, n_full, n_kv, tq, tk, mask_fn,
               m_ref, l_ref, acc_ref, zero_masked=False)
    l_row = l_ref[...]
    o0 = _normalised(acc_ref[0], l_row, 0, tq)
    o1 = _normalised(acc_ref[1], l_row, N_HEADS, tq)
    lv = lam_ref[...]
    lam = (jnp.exp(jnp.sum(lv[0:1] * lv[1:2], axis=1, keepdims=True))
           - jnp.exp(jnp.sum(lv[2:3] * lv[3:4], axis=1, keepdims=True)) + lam_init)
    o = (o0 - lam * o1).T
    hi, lo = _split_bf16(o * o)
    ms = (jnp.dot(hi, gm_ref[...], preferred_element_type=F32)
          + jnp.dot(lo, gm_ref[...], preferred_element_type=F32))
    o_ref[...] = o * lax.rsqrt(ms + LN_EPS) * g_ref[...] * (1.0 - lam_init)


def _attention_a(qt, k, vt, a_lambda, g_full, gmean, *, bsz, tq, tk, past, lam_init):
    t = qt.shape[1] // bsz
    s_tot = k.shape[0] // bsz
    nq = t // tq
    kern = functools.partial(_attn_a_kernel, tq=tq, tk=tk, past=past, s_tot=s_tot, lam_init=lam_init)
    return pl.pallas_call(
        kern, grid=(bsz, nq),
        in_specs=[pl.BlockSpec((GROUP_W, tq), lambda b, i: (0, b * nq + i)),
                  pl.BlockSpec((s_tot, GROUP_W), lambda b, i: (b, 0)),
                  pl.BlockSpec((GROUP_W, s_tot), lambda b, i: (0, b)),
                  pl.BlockSpec(a_lambda.shape, lambda b, i: (0, 0)),
                  pl.BlockSpec((1, GROUP_W), lambda b, i: (0, 0)),
                  pl.BlockSpec((GROUP_W, GROUP_W), lambda b, i: (0, 0))],
        out_specs=pl.BlockSpec((tq, GROUP_W), lambda b, i: (b * nq + i, 0)),
        out_shape=jax.ShapeDtypeStruct((bsz * t, GROUP_W), F32),
        scratch_shapes=[pltpu.VMEM((1, 2 * N_HEADS * tq), F32), pltpu.VMEM((1, 2 * N_HEADS * tq), F32),
                        pltpu.VMEM((2, GROUP_W, tq), F32)],
        compiler_params=_cparams(("parallel", "arbitrary")), name="attn_a",
    )(qt, k, vt, a_lambda, g_full, gmean)


def _dsa_kernel(qt_ref, k_ref, vt_ref, qit_ref, ki_ref, kiwit_ref, o_ref, key_ref, y_ref, m_ref, l_ref, acc_ref,
                *, tq, tk, past, s_tot, topk, idx_bits, t_valid):
    q0 = pl.program_id(1) * tq
    n_kv = _num_kv_blocks(q0, tq, tk, past, s_tot)
    row_pos = past + q0 + lax.broadcasted_iota(I32, (1, tq), 1)
    key_iota = lax.broadcasted_iota(I32, (tk, 1), 0)
    int_min = jnp.int32(INT_MIN)

    qit = qit_ref[...]
    blocks = []
    for h in range(N_HEADS):
        qh = qit[h * D_IDX:(h + 1) * D_IDX, :]
        hi, lo = _split_bf16(qh)
        blocks.append(jnp.concatenate([hi, lo, hi], axis=0))
    qi = jnp.concatenate(blocks, axis=1)
    kiwit = kiwit_ref[...]
    w_h = [kiwit[D_IDX + h:D_IDX + h + 1, :] for h in range(N_HEADS)]

    def score_body(c, carry):
        k0 = pl.multiple_of(c * tk, tk)
        kib = ki_ref[pl.ds(k0, tk), :]
        sc = jnp.dot(kib, qi, preferred_element_type=F32)
        isc = jnp.maximum(sc[:, 0:tq], 0.0) * w_h[0]
        for h in range(1, N_HEADS):
            isc = isc + jnp.maximum(sc[:, h * tq:(h + 1) * tq], 0.0) * w_h[h]
        isc = jnp.where(isc == 0.0, 0.0, isc)
        bits = pltpu.bitcast(isc, I32)
        key = bits ^ ((bits >> 31) & jnp.int32(0x7FFFFFFF))
        key_ref[c] = jnp.where((k0 + key_iota) <= row_pos, key, int_min)
        return carry

    lax.fori_loop(0, n_kv, score_body, 0)

    def count(pred):
        def block_count(c):
            ones = jnp.where(pred(key_ref[c], c * tk), 1.0, 0.0)
            return jnp.sum(ones.reshape(tk // 8, 8, tq), axis=0)

        def body(i, acc):
            return acc + block_count(2 * i) + block_count(2 * i + 1)

        acc = lax.fori_loop(0, n_kv // 2, body, jnp.zeros((8, tq), F32))
        odd = (n_kv & 1).astype(F32)
        acc = acc + odd * block_count(n_kv - 1)
        return jnp.sum(acc, axis=0, keepdims=True)

    kk = jnp.float32(topk)

    def bit_body(r, carry):
        prefix, n_ge, n_gt = carry
        cand_u = prefix | lax.shift_left(jnp.int32(1), 31 - r)
        cand_s = cand_u ^ int_min
        cnt = count(lambda blk, k0: blk >= cand_s)
        ok = cnt >= kk
        return jnp.where(ok, cand_u, prefix), jnp.where(ok, cnt, n_ge), jnp.where(ok, n_gt, cnt)

    n_all = jnp.full((1, tq), 1.0, F32) * (n_kv * tk).astype(F32)
    prefix, n_ge, n_gt = lax.fori_loop(
        0, 32, bit_body, (jnp.zeros((1, tq), I32), n_all, jnp.zeros((1, tq), F32)))
    thr = prefix ^ int_min
    r_need = kk - n_gt
    live = lax.broadcasted_iota(I32, (1, tq), 1) < t_valid
    any_tie = jnp.max(jnp.where(live, n_ge - kk, 0.0)) > 0.0
    y_ref[...] = jnp.full((1, tq), s_tot, I32)

    @pl.when(any_tie)
    def _():
        def ybody(r, y):
            cand = y | lax.shift_left(jnp.int32(1), idx_bits - 1 - r)
            cnt = count(lambda blk, k0: (blk == thr) & ((k0 + key_iota) < cand))
            return jnp.where(cnt < r_need, cand, y)
        y_ref[...] = lax.fori_loop(0, idx_bits, ybody, jnp.zeros((1, tq), I32))

    y_last = y_ref[...]

    def mask_fn(c, k0):
        blk = key_ref[c]
        pos = k0 + key_iota
        lowered = jnp.where(pos > y_last, blk - 1, blk)
        return (lowered >= thr) & (pos <= row_pos)

    qt = qt_ref[...]
    row_head = lax.broadcasted_iota(I32, (GROUP_W, 1), 0) // HEAD_W
    qst = jnp.concatenate([jnp.where(row_head == h, qt, 0.0) for h in range(N_HEADS)], axis=1).astype(BF16)
    _attn_core(qst, N_HEADS, k_ref, vt_ref, 0, n_kv, tq, tk, mask_fn, m_ref, l_ref, acc_ref, zero_masked=True)
    o_ref[...] = _normalised(acc_ref[0], l_ref[...], 0, tq).T


def _dsa(qt, k, vt, qit, ki3, kiwit, *, bsz, tq, tk, past, topk, t_valid):
    t = qt.shape[1] // bsz
    s_tot = k.shape[0] // bsz
    nq = t // tq
    idx_bits = int(math.ceil(math.log2(s_tot))) + 1
    kern = functools.partial(_dsa_kernel, tq=tq, tk=tk, past=past, s_tot=s_tot, topk=topk,
                             idx_bits=idx_bits, t_valid=t_valid)
    qspec = lambda w: pl.BlockSpec((w, tq), lambda b, i: (0, b * nq + i))
    return pl.pallas_call(
        kern, grid=(bsz, nq),
        in_specs=[qspec(GROUP_W),
                  pl.BlockSpec((s_tot, GROUP_W), lambda b, i: (b, 0)),
                  pl.BlockSpec((GROUP_W, s_tot), lambda b, i: (0, b)),
                  qspec(GROUP_W),
                  pl.BlockSpec((s_tot, 3 * D_IDX), lambda b, i: (b, 0)),
                  qspec(LANES)],
        out_specs=pl.BlockSpec((tq, GROUP_W), lambda b, i: (b * nq + i, 0)),
        out_shape=jax.ShapeDtypeStruct((bsz * t, GROUP_W), F32),
        scratch_shapes=[pltpu.VMEM((s_tot // tk, tk, tq), I32), pltpu.VMEM((1, tq), I32),
                        pltpu.VMEM((1, N_HEADS * tq), F32), pltpu.VMEM((1, N_HEADS * tq), F32),
                        pltpu.VMEM((1, GROUP_W, tq), F32)],
        compiler_params=_cparams(("parallel", "arbitrary")), name="dsa",
    )(qt, k, vt, qit, ki3, kiwit)


_HALO_C = 8
_HALO_D = 32


def _conv_kernel(*refs, tm, single_tile):
    if single_tile:
        (cb_ref, uc_ref, ud_ref, hc_ref, hd_ref, cw_ref, dw_ref, db_ref, lg_ref, lb_ref,
         oc_ref, od_ref, ec_ref, ed_ref) = refs
        pc_ref = pd_ref = None
    else:
        (cb_ref, uc_ref, ud_ref, pc_ref, pd_ref, hc_ref, hd_ref, cw_ref, dw_ref, db_ref, lg_ref, lb_ref,
         oc_ref, od_ref, ec_ref, ed_ref) = refs
    i = pl.program_id(1)

    @pl.when(i == 0)
    def _():
        ec_ref[0:_HALO_C] = hc_ref[0]
        ed_ref[0:_HALO_D] = hd_ref[0]

    if not single_tile:
        @pl.when(i > 0)
        def _():
            ec_ref[0:_HALO_C] = pc_ref[0]
            ed_ref[0:_HALO_D] = pd_ref[0]

    ec_ref[_HALO_C:_HALO_C + tm] = uc_ref[0]
    ed_ref[_HALO_D:_HALO_D + tm] = ud_ref[0]

    cw = cw_ref[...]
    acc = ec_ref[pl.ds(_HALO_C - (C_WIDTH - 1), tm), :] * cw[0:1]
    for w in range(1, C_WIDTH):
        acc = acc + ec_ref[pl.ds(_HALO_C - (C_WIDTH - 1) + w, tm), :] * cw[w:w + 1]
    oc_ref[0] = cb_ref[0] * acc

    dw = dw_ref[...]
    acc = ed_ref[pl.ds(_HALO_D - (D_WIDTH - 1), tm), :] * dw[0:1]
    for w in range(1, D_WIDTH):
        acc = acc + ed_ref[pl.ds(_HALO_D - (D_WIDTH - 1) + w, tm), :] * dw[w:w + 1]
    z = _layer_norm(acc + db_ref[...], lg_ref[...], lb_ref[...])
    od_ref[0] = z * _sigmoid(z)


def _convs(cb, uc, ud, hist_c, hist_d, c_w, d_w, d_b, ln_g, ln_b, *, tm):
    bsz, t, w = cb.shape
    nt = t // tm
    single = nt == 1
    main = pl.BlockSpec((1, tm, w), lambda b, i: (b, i, 0))
    prev_c = pl.BlockSpec((1, _HALO_C, w), lambda b, i: (b, jnp.maximum(i * (tm // _HALO_C) - 1, 0), 0))
    prev_d = pl.BlockSpec((1, _HALO_D, w), lambda b, i: (b, jnp.maximum(i * (tm // _HALO_D) - 1, 0), 0))
    hist_cs = pl.BlockSpec((1, _HALO_C, w), lambda b, i: (b, 0, 0))
    hist_ds = pl.BlockSpec((1, _HALO_D, w), lambda b, i: (b, 0, 0))
    full2 = lambda a: pl.BlockSpec(a.shape, lambda b, i: (0, 0))
    in_specs = [main, main, main] + ([] if single else [prev_c, prev_d]) + [hist_cs, hist_ds] + \
        [full2(c_w), full2(d_w), full2(d_b), full2(ln_g), full2(ln_b)]
    args = [cb, uc, ud] + ([] if single else [uc, ud]) + [hist_c, hist_d, c_w, d_w, d_b, ln_g, ln_b]
    kern = functools.partial(_conv_kernel, tm=tm, single_tile=single)
    return pl.pallas_call(
        kern, grid=(bsz, nt), in_specs=in_specs,
        out_specs=[main, main],
        out_shape=[jax.ShapeDtypeStruct((bsz, t, w), F32)] * 2,
        scratch_shapes=[pltpu.VMEM((_HALO_C + tm, w), F32), pltpu.VMEM((_HALO_D + tm, w), F32)],
        compiler_params=_cparams(("parallel", "arbitrary")), name="convs",
    )(*args)


def _mix_kernel(x_ref, oa_ref, ob_ref, oc_ref, od_ref, pw_ref, pwb_ref, wo_ref, g_ref, b_ref, h_ref, *, alpha):
    od = jnp.dot(od_ref[...].astype(BF16), pw_ref[...], preferred_element_type=F32) + pwb_ref[...]
    cat = jnp.concatenate([oa_ref[...], ob_ref[...], oc_ref[...], od], axis=1).astype(BF16)
    mix = jnp.dot(cat, wo_ref[...], preferred_element_type=F32)
    h_ref[...] = _layer_norm(alpha * x_ref[...] + mix, g_ref[...], b_ref[...])


def _mix(x, oa, ob, oc, od, pw, pwb, wo, g, b, *, tm, alpha):
    n, d = x.shape
    row = lambda w: pl.BlockSpec((tm, w), lambda i: (i, 0))
    full = lambda a: pl.BlockSpec(a.shape, lambda i: (0, 0))
    return pl.pallas_call(
        functools.partial(_mix_kernel, alpha=alpha), grid=(n // tm,),
        in_specs=[row(d), row(GROUP_W), row(GROUP_W), row(GROUP_W), row(GROUP_W),
                  full(pw), full(pwb), full(wo), full(g), full(b)],
        out_specs=row(d), out_shape=jax.ShapeDtypeStruct((n, d), F32),
        compiler_params=_cparams(("parallel",)), name="mix",
    )(x, oa, ob, oc, od, pw, pwb, wo, g, b)


def _staircase():
    return [(a, b) for a in range(PEER_TOPK) for b in range(PEER_TOPK) if (a + 1) * (b + 1) <= PEER_TOPK]


def _tree_rows(x, op):
    while x.shape[0] > 1:
        half = x.shape[0] // 2
        y = op(x[:half], x[half:2 * half])
        x = y if x.shape[0] % 2 == 0 else jnp.concatenate([y, x[2 * half:]], axis=0)
    return x


def _argmax_rows(x, row):
    v, i = x, row
    while v.shape[0] > 1:
        n = v.shape[0]
        half = n // 2
        pv = v[:2 * half].reshape((half, 2) + v.shape[1:])
        pi = i[:2 * half].reshape((half, 2) + i.shape[1:])
        take_b = pv[:, 1] > pv[:, 0]
        nv = jnp.where(take_b, pv[:, 1], pv[:, 0])
        ni = jnp.where(take_b, pi[:, 1], pi[:, 0])
        if n % 2:
            nv = jnp.concatenate([nv, v[2 * half:]], axis=0)
            ni = jnp.concatenate([ni, i[2 * half:]], axis=0)
        v, i = nv, ni
    return v, i


def _extract_top(x, payload, n_out, n_rows):
    row = lax.broadcasted_iota(I32, x.shape, 0).astype(F32)
    vals, idxs, pays = [], [], []
    for _ in range(n_out):
        m, idx = _argmax_rows(x, row)
        hit = row == idx
        vals.append(m)
        idxs.append(idx)
        if payload is not None:
            pays.append(_tree_rows(jnp.where(hit, payload, -1.0), jnp.maximum))
        x = jnp.where(hit, -jnp.inf, x)
    return vals, idxs, pays


def _peer_select_kernel(h_ref, wq_ref, sk_ref, eid_ref, g_ref, sv_ref, si_ref, *, tm):
    ht = h_ref[...].T.astype(BF16)
    n_groups = 2 * PEER_HEADS
    nt = tm // LANES

    def to_tiles(x):
        return pltpu.einshape("trl->rtl", jnp.stack([x[:, LANES * t:LANES * (t + 1)] for t in range(nt)], axis=0))

    def from_tiles(y):
        z = pltpu.einshape("rtl->trl", y)
        return jnp.concatenate([z[t] for t in range(nt)], axis=1)

    def group_body(g, carry):
        r0 = pl.multiple_of(g * N_KEYS, N_KEYS)
        qg = jnp.dot(wq_ref[pl.ds(r0, N_KEYS), :], ht, preferred_element_type=F32)
        st = jnp.dot(sk_ref[g % 2], qg.astype(BF16), preferred_element_type=F32)
        vals, idxs, _ = _extract_top(to_tiles(st), None, PEER_TOPK, N_KEYS)
        sv_ref[g] = jnp.concatenate(vals, axis=0)
        si_ref[g] = jnp.concatenate(idxs, axis=0)
        return carry

    lax.fori_loop(0, n_groups, group_body, 0)

    pairs = _staircase()

    def head_body(hd, carry):
        sv0 = sv_ref[2 * hd]; sv1 = sv_ref[2 * hd + 1]
        si0 = si_ref[2 * hd]; si1 = si_ref[2 * hd + 1]
        cand = jnp.concatenate([sv0[a:a + 1] + sv1[b:b + 1] for a, b in pairs], axis=0)
        cand_e = jnp.concatenate([si0[a:a + 1] * float(N_KEYS) + si1[b:b + 1] for a, b in pairs], axis=0)
        vals, _, pays = _extract_top(cand, cand_e, PEER_TOPK, len(pairs))
        fv = jnp.concatenate(vals, axis=0)
        e = jnp.exp(fv - fv[0:1])
        gate = e / jnp.sum(e, axis=0, keepdims=True)
        r0 = pl.multiple_of(hd * PEER_TOPK, PEER_TOPK)
        eid_ref[pl.ds(r0, PEER_TOPK), :] = from_tiles(jnp.concatenate(pays, axis=0)).astype(I32)
        g_ref[pl.ds(r0, PEER_TOPK), :] = from_tiles(gate)
        return carry

    lax.fori_loop(0, PEER_HEADS, head_body, 0)


_SELECT_TOKENS = 8 * LANES


def _peer_select(h, wq_t, sk, *, tm):
    n, d = h.shape
    n_slots = PEER_HEADS * PEER_TOPK
    return pl.pallas_call(
        functools.partial(_peer_select_kernel, tm=tm), grid=(n // tm,),
        in_specs=[pl.BlockSpec((tm, d), lambda i: (i, 0)),
                  pl.BlockSpec(wq_t.shape, lambda i: (0, 0)),
                  pl.BlockSpec(sk.shape, lambda i: (0, 0, 0))],
        out_specs=[pl.BlockSpec((n_slots, tm), lambda i: (0, i)),
                   pl.BlockSpec((n_slots, tm), lambda i: (0, i))],
        out_shape=[jax.ShapeDtypeStruct((n_slots, n), I32), jax.ShapeDtypeStruct((n_slots, n), F32)],
        scratch_shapes=[pltpu.VMEM((2 * PEER_HEADS, PEER_TOPK, tm // LANES, LANES), F32),
                        pltpu.VMEM((2 * PEER_HEADS, PEER_TOPK, tm // LANES, LANES), F32)],
        compiler_params=_cparams(("parallel",)), name="peer_select",
    )(h, wq_t, sk)


_GATE_TOKENS = 16


def _peer_gates_kernel(eid_ref, g_ref, w_ref, *, tm):
    sub = lax.broadcasted_iota(I32, (N_KEYS, PEER_HEADS * PEER_TOPK), 0)

    def token_gates(t):
        e = eid_ref[pl.ds(t, 1), :]
        g = g_ref[pl.ds(t, 1), :]
        oh_i = jnp.where((e >> 7) == sub, 1.0, 0.0).astype(BF16)
        g_j = jnp.where((e & (N_KEYS - 1)) == sub, g, 0.0).astype(BF16)
        return lax.dot_general(oh_i, g_j, _NT, preferred_element_type=F32)

    def body(b, carry):
        t0 = pl.multiple_of(b * _GATE_TOKENS, _GATE_TOKENS)
        w = jnp.stack([token_gates(t0 + k) for k in range(_GATE_TOKENS)], axis=0)
        w_ref[:, pl.ds(t0, _GATE_TOKENS), :] = pltpu.einshape("tij->itj", w).astype(BF16)
        return carry

    lax.fori_loop(0, tm // _GATE_TOKENS, body, 0)


def _peer_gates(eid, gate, *, tm):
    n, n_slots = eid.shape
    return pl.pallas_call(
        functools.partial(_peer_gates_kernel, tm=tm), grid=(n // tm,),
        in_specs=[pl.BlockSpec((tm, n_slots), lambda i: (i, 0)),
                  pl.BlockSpec((tm, n_slots), lambda i: (i, 0))],
        out_specs=pl.BlockSpec((N_KEYS, tm, N_KEYS), lambda i: (0, i, 0)),
        out_shape=jax.ShapeDtypeStruct((N_KEYS, n, N_KEYS), BF16),
        compiler_params=_cparams(("parallel",)), name="peer_gates",
    )(eid, gate)


def _gelu(x):
    return 0.5 * x * (1.0 + lax.erf(x * (2.0 ** -0.5)))


def _peer_dense_kernel(h_ref, w_ref, u_ref, v_ref, f_ref):
    @pl.when(pl.program_id(1) == 0)
    def _():
        f_ref[...] = jnp.zeros(f_ref.shape, F32)

    act = _gelu(lax.dot_general(h_ref[...], u_ref[0].astype(BF16), _NT, preferred_element_type=F32))
    y = jnp.concatenate(
        [w_ref[i].astype(F32) * act[:, i * N_KEYS:(i + 1) * N_KEYS] for i in range(w_ref.shape[0])],
        axis=1).astype(BF16)
    f_ref[...] += jnp.dot(y, v_ref[0].astype(BF16), preferred_element_type=F32)


def _peer_dense(hb, wmat, u, v, layer, *, tm, te):
    n, d = hb.shape
    n_exp = u.shape[1]
    return pl.pallas_call(
        _peer_dense_kernel, grid=(n // tm, n_exp // te),
        in_specs=[pl.BlockSpec((tm, d), lambda i, e: (i, 0)),
                  pl.BlockSpec((te // N_KEYS, tm, N_KEYS), lambda i, e: (e, i, 0)),
                  pl.BlockSpec((1, te, d), lambda i, e: (layer, e, 0)),
                  pl.BlockSpec((1, te, d), lambda i, e: (layer, e, 0))],
        out_specs=pl.BlockSpec((tm, d), lambda i, e: (i, 0)),
        out_shape=jax.ShapeDtypeStruct((n, d), F32),
        compiler_params=_cparams(("parallel", "arbitrary")), name="peer_dense",
    )(hb, wmat, u, v)


def _final_kernel(h_ref, f_ref, p_ref, g_ref, b_ref, wg_ref, bg_ref, wp_ref, y_ref, *, alpha):
    h2 = _layer_norm(alpha * h_ref[...] + f_ref[...], g_ref[...], b_ref[...])
    gate = _sigmoid(jnp.dot(h2.astype(BF16), wg_ref[...], preferred_element_type=F32) + bg_ref[...])
    emb = jnp.dot(p_ref[...].astype(BF16), wp_ref[...], preferred_element_type=F32)
    y_ref[...] = h2 + gate * emb


def _final(h, f, p, g, b, wg, bg, wp, *, tm, alpha):
    n, d = h.shape
    row = lambda w: pl.BlockSpec((tm, w), lambda i: (i, 0))
    full = lambda a: pl.BlockSpec(a.shape, lambda i: (0, 0))
    return pl.pallas_call(
        functools.partial(_final_kernel, alpha=alpha), grid=(n // tm,),
        in_specs=[row(d), row(d), row(p.shape[1]), full(g), full(b), full(wg), full(bg), full(wp)],
        out_specs=row(d), out_shape=jax.ShapeDtypeStruct((n, d), F32),
        compiler_params=_cparams(("parallel",)), name="final",
    )(h, f, p, g, b, wg, bg, wp)


def _pack_w_in(w_in):
    d = w_in.shape[0]
    cols = [w_in[:, 0:1792], w_in[:, 1792:1860], jnp.zeros((d, 60), w_in.dtype), w_in[:, 1860:3140]]
    nat = jnp.concatenate(cols, axis=1).astype(BF16)
    seg = lambda name, width=GROUP_W: nat[:, _SEG[name]:_SEG[name] + width]
    tr = jnp.concatenate([seg("qa"), seg("qb"), seg("qi"), seg("kiwi", 128), seg("va"), seg("vb")], axis=1).T
    return nat, tr


def _row(a):
    return a.reshape(1, -1)


def _tile_rows(n, pref):
    t = pref
    while n % t:
        t //= 2
    return t


def _prep_weights(lw, depth):
    (w_in, a_lambda, a_subln_g, c_conv_w, d_conv_w, d_conv_b, d_ln_g, d_ln_b, d_pw_w, d_pw_b,
     w_out, ln1_g, ln1_b, peer_wq, peer_subkeys, peer_u, peer_v, ln2_g, ln2_b,
     ple_w, ple_gate_w, ple_gate_b) = lw
    head_of_lane = jnp.arange(GROUP_W) // HEAD_W
    gmean = (head_of_lane[:, None] == head_of_lane[None, :]).astype(BF16) * (1.0 / HEAD_W)
    w_nat, w_tr = _pack_w_in(w_in)
    return dict(
        w_in=w_nat, w_in_t=w_tr, a_lambda=a_lambda, a_g=_row(jnp.tile(a_subln_g, N_HEADS)),
        gmean=gmean.astype(BF16),
        c_w=c_conv_w, d_w=d_conv_w, d_b=_row(d_conv_b), d_ln_g=_row(d_ln_g), d_ln_b=_row(d_ln_b),
        d_pw=d_pw_w.astype(BF16), d_pwb=_row(d_pw_b), w_out=w_out.astype(BF16),
        ln1_g=_row(ln1_g), ln1_b=_row(ln1_b),
        wq_t=peer_wq.T.astype(BF16), sk=peer_subkeys.astype(BF16),
        u=peer_u, v=peer_v,
        ln2_g=_row(ln2_g), ln2_b=_row(ln2_b),
        ple_w=ple_w.astype(BF16), wg=ple_gate_w.astype(BF16), bg=_row(ple_gate_b),
        alpha=float((2 * depth) ** 0.25),
    )


def _index_keys(ki):
    k_hi = ki.astype(BF16)
    k_lo = (ki - k_hi.astype(F32)).astype(BF16)
    return jnp.concatenate([k_hi, k_hi, k_lo], axis=-1)


def _token_tail(w, x, oa, ob, oc, od, valid_rows, p):
    n = x.shape[0]
    h = _mix(x, oa, ob, oc, od, w["d_pw"], w["d_pwb"], w["w_out"], w["ln1_g"], w["ln1_b"],
             tm=_tile_rows(n, 512), alpha=w["alpha"])
    if valid_rows is not None:
        bsz, t_pad, t = valid_rows
        h = h.reshape(bsz, t_pad, -1)[:, :t].reshape(bsz * t, -1)
    n = h.shape[0]
    n_sel = -(-n // _SELECT_TOKENS) * _SELECT_TOKENS
    h_sel = h if n_sel == n else jnp.pad(h, ((0, n_sel - n), (0, 0)))
    eid_t, gate_t = _peer_select(h_sel, w["wq_t"], w["sk"], tm=_SELECT_TOKENS)
    n_gate = -(-n // _GATE_TOKENS) * _GATE_TOKENS
    wmat = _peer_gates(eid_t[:, :n_gate].T, gate_t[:, :n_gate].T, tm=_tile_rows(n_gate, 128))
    f = _peer_dense(h.astype(BF16), wmat, w["u"], w["v"], w["layer"], tm=_tile_rows(n, 1024), te=8 * N_KEYS)
    return _final(h, f, p, w["ln2_g"], w["ln2_b"], w["wg"], w["bg"], w["ple_w"],
                  tm=_tile_rows(n, 512), alpha=w["alpha"])


def _layer(layer_idx, depth, xp, xs, caches, state_c, state_d, page_table, pp, ps, lw):
    w = _prep_weights(lw, depth)
    w["layer"] = layer_idx
    bp, tp, d = xp.shape
    bs, ts, _ = xs.shape
    ts_pad = 8
    n_pages = page_table.shape[1]
    page = caches[0].shape[3]
    past = n_pages * page
    lam_init = 0.8 - 0.6 * math.exp(-0.3 * (layer_idx + 1))

    pr = _project(xp.reshape(bp * tp, d), w["w_in"], w["w_in_t"], _tile_rows(bp * tp, 512), feature_major=True)
    tq = _tile_rows(tp, 256)
    tk = _tile_rows(tp, 256)
    oa = _attention_a(pr["qat"], pr["kab"], pr["vat"], w["a_lambda"], w["a_g"], w["gmean"],
                      bsz=bp, tq=tq, tk=tk, past=0, lam_init=lam_init)
    ob = _dsa(pr["qbt"], pr["kbb"], pr["vbt"], pr["qit"], _index_keys(pr["kiwi"][:, :D_IDX]), pr["kiwit"],
              bsz=bp, tq=tq, tk=tk, past=0, topk=min(DSA_TOPK, tp // 4), t_valid=tq)
    r3p = lambda a: a.reshape(bp, tp, a.shape[-1])
    zc = jnp.zeros((bp, _HALO_C, GROUP_W), F32)
    zd = jnp.zeros((bp, _HALO_D, GROUP_W), F32)
    oc, od = _convs(r3p(pr["cb"]), r3p(pr["uc"]), r3p(pr["ud"]), zc, zd, w["c_w"], w["d_w"], w["d_b"],
                    w["d_ln_g"], w["d_ln_b"], tm=_tile_rows(tp, 512))
    yp = _token_tail(w, xp.reshape(bp * tp, d), oa, ob, oc.reshape(bp * tp, -1), od.reshape(bp * tp, -1),
                     None, pp.reshape(bp * tp, -1)).reshape(bp, tp, d)
    new_p = (r3p(pr["ka"]), r3p(pr["va"]), r3p(pr["kb"]), r3p(pr["vb"]), r3p(pr["kiwi"])[:, :, :D_IDX],
             jnp.concatenate([zc, r3p(pr["uc"])], axis=1)[:, -(C_WIDTH - 1):],
             jnp.concatenate([zd, r3p(pr["ud"])], axis=1)[:, -(D_WIDTH - 1):])

    xs_pad = jnp.pad(xs, ((0, 0), (0, ts_pad - ts), (0, 0))).reshape(bs * ts_pad, d)
    sr = _project(xs_pad, w["w_in"], w["w_in_t"], _tile_rows(bs * ts_pad, 512), feature_major=False)
    r3s = lambda a: a.reshape(bs, ts_pad, a.shape[-1])
    news = [r3s(sr["ka"]), r3s(sr["va"]), r3s(sr["kb"]), r3s(sr["vb"]), r3s(sr["kiwi"])[:, :, :D_IDX]]
    oa_s, ob_s = _sample_attention(page_table, caches, layer_idx, news, r3s(sr["qa"]), r3s(sr["qb"]), r3s(sr["qi"]),
                                   r3s(sr["kiwi"]), w["a_lambda"], w["a_g"], w["gmean"],
                                   topk=min(DSA_TOPK, (past + ts) // 4), lam_init=lam_init, t_valid=ts)
    unpad = lambda a: a.reshape(bs * ts_pad, GROUP_W)
    hist_c = jnp.concatenate([jnp.zeros((bs, _HALO_C - (C_WIDTH - 1), GROUP_W), F32), state_c], axis=1)
    hist_d = jnp.concatenate([jnp.zeros((bs, _HALO_D - (D_WIDTH - 1), GROUP_W), F32), state_d], axis=1)
    oc_s, od_s = _convs(r3s(sr["cb"]), r3s(sr["uc"]), r3s(sr["ud"]), hist_c, hist_d, w["c_w"], w["d_w"],
                        w["d_b"], w["d_ln_g"], w["d_ln_b"], tm=ts_pad)
    ys = _token_tail(w, xs_pad, unpad(oa_s), unpad(ob_s), oc_s.reshape(bs * ts_pad, -1),
                     od_s.reshape(bs * ts_pad, -1), (bs, ts_pad, ts), ps.reshape(bs * ts, -1)).reshape(bs, ts, d)
    cut = lambda a: r3s(a)[:, :ts]
    new_s = (cut(sr["ka"]), cut(sr["va"]), cut(sr["kb"]), cut(sr["vb"]), cut(sr["kiwi"])[:, :, :D_IDX],
             jnp.concatenate([state_c, cut(sr["uc"])], axis=1)[:, -(C_WIDTH - 1):],
             jnp.concatenate([state_d, cut(sr["ud"])], axis=1)[:, -(D_WIDTH - 1):])
    return yp, ys, new_p, new_s


def kernel(x_prompt, x_sample, cache_a_k, cache_a_v, cache_b_k, cache_b_v, cache_b_kidx, state_c_conv, state_d_conv, page_table, p_prompt, p_sample, w_in, a_lambda, a_subln_g, c_conv_w, d_conv_w, d_conv_b, d_ln_g, d_ln_b, d_pw_w, d_pw_b, w_out, ln1_g, ln1_b, peer_wq, peer_subkeys, peer_u, peer_v, ln2_g, ln2_b, ple_w, ple_gate_w, ple_gate_b):
    depth = w_in.shape[0]
    n_pool, page = cache_a_k.shape[1], cache_a_k.shape[2]
    feature_major = lambda c: jnp.moveaxis(c, 2, -1).reshape(depth, n_pool, -1, page)
    caches = [feature_major(c) for c in (cache_a_k, cache_a_v, cache_b_k, cache_b_v, cache_b_kidx)]
    xp, xs = x_prompt, x_sample
    news_p, news_s = [], []
    for l in range(depth):
        lw = (w_in[l], a_lambda[l], a_subln_g[l], c_conv_w[l], d_conv_w[l], d_conv_b[l], d_ln_g[l], d_ln_b[l],
              d_pw_w[l], d_pw_b[l], w_out[l], ln1_g[l], ln1_b[l], peer_wq[l], peer_subkeys[l], peer_u,
              peer_v, ln2_g[l], ln2_b[l], ple_w[l], ple_gate_w[l], ple_gate_b[l])
        xp, xs, new_p, new_s = _layer(l, depth, xp, xs, caches, state_c_conv[l], state_d_conv[l],
                                      page_table, p_prompt[l], p_sample[l], lw)
        news_p.append(new_p)
        news_s.append(new_s)

    bp, tp, _ = x_prompt.shape
    bs, ts, _ = x_sample.shape
    shapes = [(N_HEADS, 2, 32), (N_HEADS, HEAD_W), (N_HEADS, HEAD_W), (N_HEADS, HEAD_W), (D_IDX,)]

    def stack(news, i, lead):
        a = jnp.stack([n[i] for n in news], axis=0)
        return a.reshape((depth,) + lead + shapes[i]) if i < 5 else a

    outs = [xp, xs]
    for i in range(7):
        outs.append(stack(news_p, i, (bp, tp)))
        outs.append(stack(news_s, i, (bs, ts)))
    return tuple(outs)
```

```python
import functools
import math

import jax
import jax.numpy as jnp
from jax import lax
from jax.experimental import pallas as pl
from jax.experimental.pallas import tpu as pltpu

F32 = jnp.float32
BF16 = jnp.bfloat16
I32 = jnp.int32

LN_EPS = 1e-5
N_HEADS = 4
HEAD_W = 64
GROUP_W = 256
D_IDX = 64
DSA_TOPK = 256
C_WIDTH = 3
D_WIDTH = 31
PEER_HEADS = 8
PEER_TOPK = 16
N_KEYS = 128
LANES = 128
NEG_BIG = -1e30
INT_MIN = -2 ** 31
VMEM_LIMIT = 56 * 1024 * 1024

_SEG = dict(qa=0, ka=256, va=512, qb=768, kb=1024, vb=1280, qi=1536, kiwi=1792,
            cb=1920, cc=2176, cx=2432, da=2688, dg=2944)
_N_PACKED = 3200
_SEG_T = dict(qa=0, qb=256, qi=512, kiwi=768, va=896, vb=1152)
_N_PACKED_T = 1408

_NT = (((1,), (1,)), ((), ()))


def _cparams(sem):
    return pltpu.CompilerParams(dimension_semantics=sem, vmem_limit_bytes=VMEM_LIMIT)


def _split_bf16(x):
    hi = x.astype(BF16)
    lo = (x - hi.astype(F32)).astype(BF16)
    return hi, lo


def _layer_norm(x, g, b):
    mu = jnp.mean(x, axis=-1, keepdims=True)
    xc = x - mu
    var = jnp.mean(xc * xc, axis=-1, keepdims=True)
    return xc * lax.rsqrt(var + LN_EPS) * g + b


def _sigmoid(x):
    return 1.0 / (1.0 + jnp.exp(-x))


_PROJ_COMMON = ("ka", "va", "kb", "vb", "kiwi", "cb", "uc", "ud")
_PROJ_FEATURE_MAJOR = ("kab", "kbb", "qat", "qbt", "qit", "kiwit", "vat", "vbt")
_PROJ_ROW_MAJOR = ("qa", "qb", "qi")


def _proj_kernel(x_ref, w_ref, wt_ref, *out_refs, names, scale_a, scale_b):
    o = dict(zip(names, out_refs))
    xb = x_ref[...].astype(BF16)

    def seg(name, width=GROUP_W):
        off = _SEG[name]
        return jnp.dot(xb, w_ref[:, off:off + width], preferred_element_type=F32)

    def seg_t(name, width=GROUP_W):
        off = _SEG_T[name]
        return lax.dot_general(wt_ref[off:off + width, :], xb, _NT, preferred_element_type=F32)

    ka = seg("ka"); kb = seg("kb")
    o["ka"][...] = ka; o["kb"][...] = kb
    o["va"][...] = seg("va"); o["vb"][...] = seg("vb")
    o["kiwi"][...] = seg("kiwi", LANES)
    o["cb"][...] = seg("cb")
    o["uc"][...] = seg("cc") * seg("cx")
    o["ud"][...] = seg("da") * _sigmoid(seg("dg"))
    if "qat" in o:
        o["kab"][...] = ka.astype(BF16); o["kbb"][...] = kb.astype(BF16)
        o["qat"][...] = seg_t("qa") * scale_a
        o["qbt"][...] = seg_t("qb") * scale_b
        o["qit"][...] = seg_t("qi")
        o["kiwit"][...] = seg_t("kiwi", LANES)
        o["vat"][...] = seg_t("va").astype(BF16)
        o["vbt"][...] = seg_t("vb").astype(BF16)
    else:
        o["qa"][...] = seg("qa") * scale_a
        o["qb"][...] = seg("qb") * scale_b
        o["qi"][...] = seg("qi")


def _project(x, w_packed, wt_packed, tm, feature_major):
    n, d = x.shape
    nat = lambda w, dt=F32: (jax.ShapeDtypeStruct((n, w), dt), pl.BlockSpec((tm, w), lambda i: (i, 0)))
    tr = lambda w, dt=F32: (jax.ShapeDtypeStruct((w, n), dt), pl.BlockSpec((w, tm), lambda i: (0, i)))
    outs = [nat(256), nat(256), nat(256), nat(256), nat(LANES), nat(256), nat(256), nat(256)]
    if feature_major:
        names = _PROJ_COMMON + _PROJ_FEATURE_MAJOR
        outs += [nat(256, BF16), nat(256, BF16), tr(256), tr(256), tr(256), tr(LANES), tr(256, BF16), tr(256, BF16)]
    else:
        names = _PROJ_COMMON + _PROJ_ROW_MAJOR
        outs += [nat(256), nat(256), nat(256)]
    kern = functools.partial(_proj_kernel, names=names, scale_a=float(32 ** -0.5), scale_b=float(HEAD_W ** -0.5))
    res = pl.pallas_call(
        kern, grid=(n // tm,),
        in_specs=[pl.BlockSpec((tm, d), lambda i: (i, 0)),
                  pl.BlockSpec((d, _N_PACKED), lambda i: (0, 0)),
                  pl.BlockSpec((_N_PACKED_T, d), lambda i: (0, 0))],
        out_specs=[o[1] for o in outs], out_shape=[o[0] for o in outs],
        compiler_params=_cparams(("parallel",)), name="proj",
    )(x, w_packed, wt_packed)
    return dict(zip(names, res))


def _lane_heads(o, l, base, rows, lane_head):
    out = None
    for h in range(N_HEADS):
        r0 = (base + h) * rows
        part = jnp.where(lane_head == h, o[r0:r0 + rows] / l[r0:r0 + rows], 0.0)
        out = part if out is None else out + part
    return out


def _sample_attn_kernel(pt_ref, *refs, n_pages, page, rows, past, topk, idx_bits, lam_init, t_valid):
    npg = n_pages
    ak, av, bk, bv, kic = (refs[i * npg:(i + 1) * npg] for i in range(5))
    (nak, nav, nbk, nbv, nki, qa_ref, qb_ref, qi_ref, kiwi_ref, lam_ref, g_ref, gm_ref,
     oa_ref, ob_ref, y_ref) = refs[5 * npg:]
    nblk = npg + 1
    s_tot = nblk * page
    lane_head = lax.broadcasted_iota(I32, (1, GROUP_W), 1) // HEAD_W
    col_last = past + lax.broadcasted_iota(I32, (1, page), 1)
    t_row = lax.broadcasted_iota(I32, (rows, 1), 0)
    int_min = jnp.int32(INT_MIN)

    def blocks(page_refs, new_ref, dtype):
        new = new_ref[0]
        pad = jnp.zeros((page - rows, new.shape[-1]), F32)
        return [r[0, 0].astype(dtype) for r in page_refs] + [jnp.concatenate([new, pad], axis=0).T.astype(dtype)]

    def softmax_pv(q_stack, n_groups, k_blocks, v_blocks, masks):
        def masked(s, mk, fill):
            if mk is None:
                return s
            s3 = jnp.where(mk[None], s.reshape(n_groups, rows, page), fill)
            return s3.reshape(n_groups * rows, page)
        s = [masked(jnp.dot(q_stack, kb, preferred_element_type=F32), mk, NEG_BIG)
             for kb, mk in zip(k_blocks, masks)]
        m = s[0].max(axis=1, keepdims=True)
        for sj in s[1:]:
            m = jnp.maximum(m, sj.max(axis=1, keepdims=True))
        l = None
        o = None
        for sj, vb, mk in zip(s, v_blocks, masks):
            p = masked(jnp.exp(sj - m), mk, 0.0)
            lj = p.sum(axis=1, keepdims=True)
            oj = lax.dot_general(p.astype(BF16), vb, _NT, preferred_element_type=F32)
            l = lj if l is None else l + lj
            o = oj if o is None else o + oj
        return o, l

    causal_last = col_last <= (past + t_row)

    qa = qa_ref[0]
    lane_grp = lax.broadcasted_iota(I32, (1, GROUP_W), 1) // 32
    qa_stack = jnp.concatenate(
        [jnp.where(lane_grp == (2 * h + c), qa, 0.0) for c in range(2) for h in range(N_HEADS)],
        axis=0).astype(BF16)
    o, l = softmax_pv(qa_stack, 2 * N_HEADS, blocks(ak, nak, BF16), blocks(av, nav, BF16),
                      [None] * npg + [causal_last])
    o0 = _lane_heads(o, l, 0, rows, lane_head)
    o1 = _lane_heads(o, l, N_HEADS, rows, lane_head)
    lv = lam_ref[...]
    lam = (jnp.exp(jnp.sum(lv[0:1] * lv[1:2], axis=1, keepdims=True))
           - jnp.exp(jnp.sum(lv[2:3] * lv[3:4], axis=1, keepdims=True)) + lam_init)
    od = o0 - lam * o1
    hi, lo = _split_bf16(od * od)
    ms = (jnp.dot(hi, gm_ref[...], preferred_element_type=F32)
          + jnp.dot(lo, gm_ref[...], preferred_element_type=F32))
    oa_ref[0] = od * lax.rsqrt(ms + LN_EPS) * g_ref[...] * (1.0 - lam_init)

    qi = qi_ref[0]
    qi_rows = jnp.concatenate([qi[:, h * D_IDX:(h + 1) * D_IDX] for h in range(N_HEADS)], axis=0)
    q_hi, q_lo = _split_bf16(qi_rows)
    kiwi = kiwi_ref[0]
    w_h = [kiwi[:, D_IDX + h:D_IDX + h + 1] for h in range(N_HEADS)]
    key_blocks = []
    for j, kib in enumerate(blocks(kic, nki, F32)):
        k_hi, k_lo = _split_bf16(kib)
        sc = (jnp.dot(q_hi, k_hi, preferred_element_type=F32)
              + jnp.dot(q_lo, k_hi, preferred_element_type=F32)
              + jnp.dot(q_hi, k_lo, preferred_element_type=F32))
        isc = jnp.maximum(sc[0:rows], 0.0) * w_h[0]
        for h in range(1, N_HEADS):
            isc = isc + jnp.maximum(sc[h * rows:(h + 1) * rows], 0.0) * w_h[h]
        isc = jnp.where(isc == 0.0, 0.0, isc)
        bits = pltpu.bitcast(isc, I32)
        key = bits ^ ((bits >> 31) & jnp.int32(0x7FFFFFFF))
        if j == npg:
            key = jnp.where(causal_last, key, int_min)
        key_blocks.append(key)
    keys = jnp.concatenate(key_blocks, axis=1)
    cols = lax.broadcasted_iota(I32, (1, s_tot), 1)
    kk = jnp.float32(topk)
    count = lambda pred: jnp.sum(jnp.where(pred, 1.0, 0.0), axis=1, keepdims=True)

    def bit_body(r, prefix):
        b1 = lax.shift_left(jnp.int32(1), 31 - 2 * r)
        b0 = lax.shift_left(jnp.int32(1), 30 - 2 * r)
        enough = lambda cand_u: count(keys >= (cand_u ^ int_min)) >= kk
        c10 = prefix | b1
        c11 = c10 | b0
        c01 = prefix | b0
        return jnp.where(enough(c10), jnp.where(enough(c11), c11, c10), jnp.where(enough(c01), c01, prefix))

    thr = lax.fori_loop(0, 16, bit_body, jnp.zeros((rows, 1), I32)) ^ int_min
    r_need = kk - count(keys > thr)
    any_tie = jnp.max(jnp.where(t_row < t_valid, count(keys >= thr) - kk, 0.0)) > 0.0
    y_ref[...] = jnp.full((rows, 1), s_tot, I32)

    @pl.when(any_tie)
    def _():
        def ybody(r, y):
            cand = y | lax.shift_left(jnp.int32(1), idx_bits - 1 - r)
            return jnp.where(count((keys == thr) & (cols < cand)) < r_need, cand, y)
        y_ref[...] = lax.fori_loop(0, idx_bits, ybody, jnp.zeros((rows, 1), I32))

    sel = ((keys > thr) | ((keys == thr) & (cols <= y_ref[...]))) & (cols <= (past + t_row))
    qb = qb_ref[0]
    qb_stack = jnp.concatenate([jnp.where(lane_head == h, qb, 0.0) for h in range(N_HEADS)], axis=0).astype(BF16)
    o, l = softmax_pv(qb_stack, N_HEADS, blocks(bk, nbk, BF16), blocks(bv, nbv, BF16),
                      [sel[:, j * page:(j + 1) * page] for j in range(nblk)])
    ob_ref[0] = _lane_heads(o, l, 0, rows, lane_head)


def _sample_attention(page_table, caches, layer, news, qa, qb, qi, kiwi, a_lambda, g_full, gmean,
                      *, topk, lam_init, t_valid):
    bsz, n_pages = page_table.shape
    page = caches[0].shape[3]
    rows = qa.shape[1]
    past = n_pages * page
    idx_bits = int(math.ceil(math.log2((n_pages + 1) * page))) + 1
    page_spec = lambda c, j: pl.BlockSpec((1, 1, c, page), lambda b, pt: (layer, pt[b, j], 0, 0))
    seq_spec = lambda c: pl.BlockSpec((1, rows, c), lambda b, pt: (b, 0, 0))
    full2 = lambda a: pl.BlockSpec(a.shape, lambda b, pt: (0, 0))
    in_specs, args = [], []
    for cache in caches:
        for j in range(n_pages):
            in_specs.append(page_spec(cache.shape[2], j))
            args.append(cache)
    for a in list(news) + [qa, qb, qi, kiwi]:
        in_specs.append(seq_spec(a.shape[-1]))
        args.append(a)
    for a in (a_lambda, g_full, gmean):
        in_specs.append(full2(a))
        args.append(a)
    kern = functools.partial(_sample_attn_kernel, n_pages=n_pages, page=page, rows=rows, past=past, topk=topk,
                             idx_bits=idx_bits, lam_init=lam_init, t_valid=t_valid)
    return pl.pallas_call(
        kern,
        grid_spec=pltpu.PrefetchScalarGridSpec(
            num_scalar_prefetch=1, grid=(bsz,), in_specs=in_specs,
            out_specs=[seq_spec(GROUP_W), seq_spec(GROUP_W)],
            scratch_shapes=[pltpu.VMEM((rows, 1), I32)]),
        out_shape=[jax.ShapeDtypeStruct((bsz, rows, GROUP_W), F32)] * 2,
        compiler_params=_cparams(("parallel",)), name="sample_attn",
    )(page_table, *args)


def _attn_core(qst, n_groups, k_ref, vt_ref, n_full, n_kv, tq, tk, mask_fn, m_ref, l_ref, acc_ref, zero_masked):
    n_comp = n_groups // N_HEADS
    row_head = lax.broadcasted_iota(I32, (GROUP_W, 1), 0) // HEAD_W
    m_ref[...] = jnp.full(m_ref.shape, 0.5 * NEG_BIG, F32)
    l_ref[...] = jnp.zeros(l_ref.shape, F32)
    acc_ref[...] = jnp.zeros(acc_ref.shape, F32)
    grp = lambda a, g: a[:, g * tq:(g + 1) * tq]

    def step(c, n_blk, masked):
        width = n_blk * tk
        k0 = pl.multiple_of(c * tk, tk)
        kblk = k_ref[pl.ds(k0, width), :]
        vt = vt_ref[:, pl.ds(k0, width)]
        if masked:
            parts = [jnp.where(mask_fn(c + i, k0 + i * tk), 0.0, NEG_BIG) for i in range(n_blk)]
            bias = parts[0] if n_blk == 1 else jnp.concatenate(parts, axis=0)

        m_old = m_ref[...]
        l_old = l_ref[...]
        m_new, l_new, alpha, pb = [], [], [], []
        for g in range(n_groups):
            s = jnp.dot(kblk, grp(qst, g), preferred_element_type=F32)
            if masked:
                s = s + bias
            m_g = jnp.maximum(grp(m_old, g), jnp.max(s, axis=0, keepdims=True))
            a_g = jnp.exp(grp(m_old, g) - m_g)
            p = jnp.exp(s - m_g)
            m_new.append(m_g)
            alpha.append(a_g)
            l_new.append(a_g * grp(l_old, g) + jnp.sum(p, axis=0, keepdims=True))
            pb.append(p.astype(BF16))
        m_ref[...] = jnp.concatenate(m_new, axis=1)
        l_ref[...] = jnp.concatenate(l_new, axis=1)
        for comp in range(n_comp):
            for h in range(N_HEADS):
                g = comp * N_HEADS + h
                rows = slice(h * HEAD_W, (h + 1) * HEAD_W)
                pv = jnp.dot(vt[rows], pb[g], preferred_element_type=F32)
                acc_ref[comp, rows, :] = acc_ref[comp, rows, :] * alpha[g] + pv

    def run(lo, hi, masked):
        def pair_body(i, carry):
            step(lo + 2 * i, 2, masked)
            return carry
        n = hi - lo
        if isinstance(n, int) and n == 0:
            return
        lax.fori_loop(0, n // 2, pair_body, 0)

        @pl.when(n % 2 == 1)
        def _():
            step(hi - 1, 1, masked)

    run(0, n_full, False)
    run(n_full, n_kv, True)


def _normalised(acc, l_row, base, tq):
    rl = 1.0 / l_row
    rf = jnp.concatenate(
        [jnp.broadcast_to(rl[:, (base + h) * tq:(base + h + 1) * tq], (HEAD_W, tq)) for h in range(N_HEADS)],
        axis=0)
    return acc * rf


def _num_kv_blocks(q0, tq, tk, past, s_tot):
    return jnp.minimum((past + q0 + tq + tk - 1) // tk, s_tot // tk)


def _attn_a_kernel(qt_ref, k_ref, vt_ref, lam_ref, g_ref, gm_ref, o_ref, m_ref, l_ref, acc_ref,
                   *, tq, tk, past, s_tot, lam_init):
    q0 = pl.program_id(1) * tq
    qt = qt_ref[...]
    row_grp = lax.broadcasted_iota(I32, (GROUP_W, 1), 0) // 32
    qst = jnp.concatenate(
        [jnp.where(row_grp == (2 * h + c), qt, 0.0) for c in range(2) for h in range(N_HEADS)],
        axis=1).astype(BF16)
    n_kv = _num_kv_blocks(q0, tq, tk, past, s_tot)
    n_full = jnp.minimum((past + q0 + 1) // tk, n_kv)
    row_pos = past + q0 + lax.broadcasted_iota(I32, (1, tq), 1)
    key_iota = lax.broadcasted_iota(I32, (tk, 1), 0)

    def mask_fn(c, k0):
        return (k0 + key_iota) <= row_pos

    _attn_core(qst, 2 * N_HEADS, k_ref, vt_ref, n_full, n_kv, tq, tk, mask_fn,
               m_ref, l_ref, acc_ref, zero_masked=False)
    l_row = l_ref[...]
    o0 = _normalised(acc_ref[0], l_row, 0, tq)
    o1 = _normalised(acc_ref[1], l_row, N_HEADS, tq)
    lv = lam_ref[...]
    lam = (jnp.exp(jnp.sum(lv[0:1] * lv[1:2], axis=1, keepdims=True))
           - jnp.exp(jnp.sum(lv[2:3] * lv[3:4], axis=1, keepdims=True)) + lam_init)
    o = (o0 - lam * o1).T
    hi, lo = _split_bf16(o * o)
    ms = (jnp.dot(hi, gm_ref[...], preferred_element_type=F32)
          + jnp.dot(lo, gm_ref[...], preferred_element_type=F32))
    o_ref[...] = o * lax.rsqrt(ms + LN_EPS) * g_ref[...] * (1.0 - lam_init)


def _attention_a(qt, k, vt, a_lambda, g_full, gmean, *, bsz, tq, tk, past, lam_init):
    t = qt.shape[1] // bsz
    s_tot = k.shape[0] // bsz
    nq = t // tq
    kern = functools.partial(_attn_a_kernel, tq=tq, tk=tk, past=past, s_tot=s_tot, lam_init=lam_init)
    return pl.pallas_call(
        kern, grid=(bsz, nq),
        in_specs=[pl.BlockSpec((GROUP_W, tq), lambda b, i: (0, b * nq + i)),
                  pl.BlockSpec((s_tot, GROUP_W), lambda b, i: (b, 0)),
                  pl.BlockSpec((GROUP_W, s_tot), lambda b, i: (0, b)),
                  pl.BlockSpec(a_lambda.shape, lambda b, i: (0, 0)),
                  pl.BlockSpec((1, GROUP_W), lambda b, i: (0, 0)),
                  pl.BlockSpec((GROUP_W, GROUP_W), lambda b, i: (0, 0))],
        out_specs=pl.BlockSpec((tq, GROUP_W), lambda b, i: (b * nq + i, 0)),
        out_shape=jax.ShapeDtypeStruct((bsz * t, GROUP_W), F32),
        scratch_shapes=[pltpu.VMEM((1, 2 * N_HEADS * tq), F32), pltpu.VMEM((1, 2 * N_HEADS * tq), F32),
                        pltpu.VMEM((2, GROUP_W, tq), F32)],
        compiler_params=_cparams(("parallel", "arbitrary")), name="attn_a",
    )(qt, k, vt, a_lambda, g_full, gmean)


def _dsa_kernel(qt_ref, k_ref, vt_ref, qit_ref, ki_ref, kiwit_ref, o_ref, key_ref, y_ref, m_ref, l_ref, acc_ref,
                *, tq, tk, past, s_tot, topk, idx_bits, t_valid):
    q0 = pl.program_id(1) * tq
    n_kv = _num_kv_blocks(q0, tq, tk, past, s_tot)
    row_pos = past + q0 + lax.broadcasted_iota(I32, (1, tq), 1)
    key_iota = lax.broadcasted_iota(I32, (tk, 1), 0)
    int_min = jnp.int32(INT_MIN)

    qit = qit_ref[...]
    blocks = []
    for h in range(N_HEADS):
        qh = qit[h * D_IDX:(h + 1) * D_IDX, :]
        hi, lo = _split_bf16(qh)
        blocks.append(jnp.concatenate([hi, lo, hi], axis=0))
    qi = jnp.concatenate(blocks, axis=1)
    kiwit = kiwit_ref[...]
    w_h = [kiwit[D_IDX + h:D_IDX + h + 1, :] for h in range(N_HEADS)]

    def score_body(c, carry):
        k0 = pl.multiple_of(c * tk, tk)
        kib = ki_ref[pl.ds(k0, tk), :]
        sc = jnp.dot(kib, qi, preferred_element_type=F32)
        isc = jnp.maximum(sc[:, 0:tq], 0.0) * w_h[0]
        for h in range(1, N_HEADS):
            isc = isc + jnp.maximum(sc[:, h * tq:(h + 1) * tq], 0.0) * w_h[h]
        isc = jnp.where(isc == 0.0, 0.0, isc)
        bits = pltpu.bitcast(isc, I32)
        key = bits ^ ((bits >> 31) & jnp.int32(0x7FFFFFFF))
        key_ref[c] = jnp.where((k0 + key_iota) <= row_pos, key, int_min)
        return carry

    lax.fori_loop(0, n_kv, score_body, 0)

    def count(pred):
        def block_count(c):
            ones = jnp.where(pred(key_ref[c], c * tk), 1.0, 0.0)
            return jnp.sum(ones.reshape(tk // 8, 8, tq), axis=0)

        def body(i, acc):
            return acc + block_count(2 * i) + block_count(2 * i + 1)

        acc = lax.fori_loop(0, n_kv // 2, body, jnp.zeros((8, tq), F32))
        odd = (n_kv & 1).astype(F32)
        acc = acc + odd * block_count(n_kv - 1)
        return jnp.sum(acc, axis=0, keepdims=True)

    kk = jnp.float32(topk)

    def bit_body(r, carry):
        prefix, n_ge, n_gt = carry
        cand_u = prefix | lax.shift_left(jnp.int32(1), 31 - r)
        cand_s = cand_u ^ int_min
        cnt = count(lambda blk, k0: blk >= cand_s)
        ok = cnt >= kk
        return jnp.where(ok, cand_u, prefix), jnp.where(ok, cnt, n_ge), jnp.where(ok, n_gt, cnt)

    n_all = jnp.full((1, tq), 1.0, F32) * (n_kv * tk).astype(F32)
    prefix, n_ge, n_gt = lax.fori_loop(
        0, 32, bit_body, (jnp.zeros((1, tq), I32), n_all, jnp.zeros((1, tq), F32)))
    thr = prefix ^ int_min
    r_need = kk - n_gt
    live = lax.broadcasted_iota(I32, (1, tq), 1) < t_valid
    any_tie = jnp.max(jnp.where(live, n_ge - kk, 0.0)) > 0.0
    y_ref[...] = jnp.full((1, tq), s_tot, I32)

    @pl.when(any_tie)
    def _():
        def ybody(r, y):
            cand = y | lax.shift_left(jnp.int32(1), idx_bits - 1 - r)
            cnt = count(lambda blk, k0: (blk == thr) & ((k0 + key_iota) < cand))
            return jnp.where(cnt < r_need, cand, y)
        y_ref[...] = lax.fori_loop(0, idx_bits, ybody, jnp.zeros((1, tq), I32))

    y_last = y_ref[...]

    def mask_fn(c, k0):
        blk = key_ref[c]
        pos = k0 + key_iota
        lowered = jnp.where(pos > y_last, blk - 1, blk)
        return (lowered >= thr) & (pos <= row_pos)

    qt = qt_ref[...]
    row_head = lax.broadcasted_iota(I32, (GROUP_W, 1), 0) // HEAD_W
    qst = jnp.concatenate([jnp.where(row_head == h, qt, 0.0) for h in range(N_HEADS)], axis=1).astype(BF16)
    _attn_core(qst, N_HEADS, k_ref, vt_ref, 0, n_kv, tq, tk, mask_fn, m_ref, l_ref, acc_ref, zero_masked=True)
    o_ref[...] = _normalised(acc_ref[0], l_ref[...], 0, tq).T


def _dsa(qt, k, vt, qit, ki3, kiwit, *, bsz, tq, tk, past, topk, t_valid):
    t = qt.shape[1] // bsz
    s_tot = k.shape[0] // bsz
    nq = t // tq
    idx_bits = int(math.ceil(math.log2(s_tot))) + 1
    kern = functools.partial(_dsa_kernel, tq=tq, tk=tk, past=past, s_tot=s_tot, topk=topk,
                             idx_bits=idx_bits, t_valid=t_valid)
    qspec = lambda w: pl.BlockSpec((w, tq), lambda b, i: (0, b * nq + i))
    return pl.pallas_call(
        kern, grid=(bsz, nq),
        in_specs=[qspec(GROUP_W),
                  pl.BlockSpec((s_tot, GROUP_W), lambda b, i: (b, 0)),
                  pl.BlockSpec((GROUP_W, s_tot), lambda b, i: (0, b)),
                  qspec(GROUP_W),
                  pl.BlockSpec((s_tot, 3 * D_IDX), lambda b, i: (b, 0)),
                  qspec(LANES)],
        out_specs=pl.BlockSpec((tq, GROUP_W), lambda b, i: (b * nq + i, 0)),
        out_shape=jax.ShapeDtypeStruct((bsz * t, GROUP_W), F32),
        scratch_shapes=[pltpu.VMEM((s_tot // tk, tk, tq), I32), pltpu.VMEM((1, tq), I32),
                        pltpu.VMEM((1, N_HEADS * tq), F32), pltpu.VMEM((1, N_HEADS * tq), F32),
                        pltpu.VMEM((1, GROUP_W, tq), F32)],
        compiler_params=_cparams(("parallel", "arbitrary")), name="dsa",
    )(qt, k, vt, qit, ki3, kiwit)


_HALO_C = 8
_HALO_D = 32


def _conv_kernel(*refs, tm, single_tile):
    if single_tile:
        (cb_ref, uc_ref, ud_ref, hc_ref, hd_ref, cw_ref, dw_ref, db_ref, lg_ref, lb_ref,
         oc_ref, od_ref, ec_ref, ed_ref) = refs
        pc_ref = pd_ref = None
    else:
        (cb_ref, uc_ref, ud_ref, pc_ref, pd_ref, hc_ref, hd_ref, cw_ref, dw_ref, db_ref, lg_ref, lb_ref,
         oc_ref, od_ref, ec_ref, ed_ref) = refs
    i = pl.program_id(1)

    @pl.when(i == 0)
    def _():
        ec_ref[0:_HALO_C] = hc_ref[0]
        ed_ref[0:_HALO_D] = hd_ref[0]

    if not single_tile:
        @pl.when(i > 0)
        def _():
            ec_ref[0:_HALO_C] = pc_ref[0]
            ed_ref[0:_HALO_D] = pd_ref[0]

    ec_ref[_HALO_C:_HALO_C + tm] = uc_ref[0]
    ed_ref[_HALO_D:_HALO_D + tm] = ud_ref[0]

    cw = cw_ref[...]
    acc = ec_ref[pl.ds(_HALO_C - (C_WIDTH - 1), tm), :] * cw[0:1]
    for w in range(1, C_WIDTH):
        acc = acc + ec_ref[pl.ds(_HALO_C - (C_WIDTH - 1) + w, tm), :] * cw[w:w + 1]
    oc_ref[0] = cb_ref[0] * acc

    dw = dw_ref[...]
    acc = ed_ref[pl.ds(_HALO_D - (D_WIDTH - 1), tm), :] * dw[0:1]
    for w in range(1, D_WIDTH):
        acc = acc + ed_ref[pl.ds(_HALO_D - (D_WIDTH - 1) + w, tm), :] * dw[w:w + 1]
    z = _layer_norm(acc + db_ref[...], lg_ref[...], lb_ref[...])
    od_ref[0] = z * _sigmoid(z)


def _convs(cb, uc, ud, hist_c, hist_d, c_w, d_w, d_b, ln_g, ln_b, *, tm):
    bsz, t, w = cb.shape
    nt = t // tm
    single = nt == 1
    main = pl.BlockSpec((1, tm, w), lambda b, i: (b, i, 0))
    prev_c = pl.BlockSpec((1, _HALO_C, w), lambda b, i: (b, jnp.maximum(i * (tm // _HALO_C) - 1, 0), 0))
    prev_d = pl.BlockSpec((1, _HALO_D, w), lambda b, i: (b, jnp.maximum(i * (tm // _HALO_D) - 1, 0), 0))
    hist_cs = pl.BlockSpec((1, _HALO_C, w), lambda b, i: (b, 0, 0))
    hist_ds = pl.BlockSpec((1, _HALO_D, w), lambda b, i: (b, 0, 0))
    full2 = lambda a: pl.BlockSpec(a.shape, lambda b, i: (0, 0))
    in_specs = [main, main, main] + ([] if single else [prev_c, prev_d]) + [hist_cs, hist_ds] + \
        [full2(c_w), full2(d_w), full2(d_b), full2(ln_g), full2(ln_b)]
    args = [cb, uc, ud] + ([] if single else [uc, ud]) + [hist_c, hist_d, c_w, d_w, d_b, ln_g, ln_b]
    kern = functools.partial(_conv_kernel, tm=tm, single_tile=single)
    return pl.pallas_call(
        kern, grid=(bsz, nt), in_specs=in_specs,
        out_specs=[main, main],
        out_shape=[jax.ShapeDtypeStruct((bsz, t, w), F32)] * 2,
        scratch_shapes=[pltpu.VMEM((_HALO_C + tm, w), F32), pltpu.VMEM((_HALO_D + tm, w), F32)],
        compiler_params=_cparams(("parallel", "arbitrary")), name="convs",
    )(*args)


def _mix_kernel(x_ref, oa_ref, ob_ref, oc_ref, od_ref, pw_ref, pwb_ref, wo_ref, g_ref, b_ref, h_ref, *, alpha):
    od = jnp.dot(od_ref[...].astype(BF16), pw_ref[...], preferred_element_type=F32) + pwb_ref[...]
    cat = jnp.concatenate([oa_ref[...], ob_ref[...], oc_ref[...], od], axis=1).astype(BF16)
    mix = jnp.dot(cat, wo_ref[...], preferred_element_type=F32)
    h_ref[...] = _layer_norm(alpha * x_ref[...] + mix, g_ref[...], b_ref[...])


def _mix(x, oa, ob, oc, od, pw, pwb, wo, g, b, *, tm, alpha):
    n, d = x.shape
    row = lambda w: pl.BlockSpec((tm, w), lambda i: (i, 0))
    full = lambda a: pl.BlockSpec(a.shape, lambda i: (0, 0))
    return pl.pallas_call(
        functools.partial(_mix_kernel, alpha=alpha), grid=(n // tm,),
        in_specs=[row(d), row(GROUP_W), row(GROUP_W), row(GROUP_W), row(GROUP_W),
                  full(pw), full(pwb), full(wo), full(g), full(b)],
        out_specs=row(d), out_shape=jax.ShapeDtypeStruct((n, d), F32),
        compiler_params=_cparams(("parallel",)), name="mix",
    )(x, oa, ob, oc, od, pw, pwb, wo, g, b)


def _staircase():
    return [(a, b) for a in range(PEER_TOPK) for b in range(PEER_TOPK) if (a + 1) * (b + 1) <= PEER_TOPK]


def _tree_rows(x, op):
    while x.shape[0] > 1:
        half = x.shape[0] // 2
        y = op(x[:half], x[half:2 * half])
        x = y if x.shape[0] % 2 == 0 else jnp.concatenate([y, x[2 * half:]], axis=0)
    return x


def _argmax_rows(x, row):
    v, i = x, row
    while v.shape[0] > 1:
        n = v.shape[0]
        half = n // 2
        pv = v[:2 * half].reshape((half, 2) + v.shape[1:])
        pi = i[:2 * half].reshape((half, 2) + i.shape[1:])
        take_b = pv[:, 1] > pv[:, 0]
        nv = jnp.where(take_b, pv[:, 1], pv[:, 0])
        ni = jnp.where(take_b, pi[:, 1], pi[:, 0])
        if n % 2:
            nv = jnp.concatenate([nv, v[2 * half:]], axis=0)
            ni = jnp.concatenate([ni, i[2 * half:]], axis=0)
        v, i = nv, ni
    return v, i


def _extract_top(x, payload, n_out, n_rows):
    row = lax.broadcasted_iota(I32, x.shape, 0).astype(F32)
    vals, idxs, pays = [], [], []
    for _ in range(n_out):
        m, idx = _argmax_rows(x, row)
        hit = row == idx
        vals.append(m)
        idxs.append(idx)
        if payload is not None:
            pays.append(_tree_rows(jnp.where(hit, payload, -1.0), jnp.maximum))
        x = jnp.where(hit, -jnp.inf, x)
    return vals, idxs, pays


def _peer_select_kernel(h_ref, wq_ref, sk_ref, eid_ref, g_ref, sv_ref, si_ref, *, tm):
    ht = h_ref[...].T.astype(BF16)
    n_groups = 2 * PEER_HEADS
    nt = tm // LANES

    def to_tiles(x):
        return pltpu.einshape("trl->rtl", jnp.stack([x[:, LANES * t:LANES * (t + 1)] for t in range(nt)], axis=0))

    def from_tiles(y):
        z = pltpu.einshape("rtl->trl", y)
        return jnp.concatenate([z[t] for t in range(nt)], axis=1)

    def group_body(g, carry):
        r0 = pl.multiple_of(g * N_KEYS, N_KEYS)
        qg = jnp.dot(wq_ref[pl.ds(r0, N_KEYS), :], ht, preferred_element_type=F32)
        st = jnp.dot(sk_ref[g % 2], qg.astype(BF16), preferred_element_type=F32)
        vals, idxs, _ = _extract_top(to_tiles(st), None, PEER_TOPK, N_KEYS)
        sv_ref[g] = jnp.concatenate(vals, axis=0)
        si_ref[g] = jnp.concatenate(idxs, axis=0)
        return carry

    lax.fori_loop(0, n_groups, group_body, 0)

    pairs = _staircase()

    def head_body(hd, carry):
        sv0 = sv_ref[2 * hd]; sv1 = sv_ref[2 * hd + 1]
        si0 = si_ref[2 * hd]; si1 = si_ref[2 * hd + 1]
        cand = jnp.concatenate([sv0[a:a + 1] + sv1[b:b + 1] for a, b in pairs], axis=0)
        cand_e = jnp.concatenate([si0[a:a + 1] * float(N_KEYS) + si1[b:b + 1] for a, b in pairs], axis=0)
        vals, _, pays = _extract_top(cand, cand_e, PEER_TOPK, len(pairs))
        fv = jnp.concatenate(vals, axis=0)
        e = jnp.exp(fv - fv[0:1])
        gate = e / jnp.sum(e, axis=0, keepdims=True)
        r0 = pl.multiple_of(hd * PEER_TOPK, PEER_TOPK)
        eid_ref[pl.ds(r0, PEER_TOPK), :] = from_tiles(jnp.concatenate(pays, axis=0)).astype(I32)
        g_ref[pl.ds(r0, PEER_TOPK), :] = from_tiles(gate)
        return carry

    lax.fori_loop(0, PEER_HEADS, head_body, 0)


_SELECT_TOKENS = 8 * LANES


def _peer_select(h, wq_t, sk, *, tm):
    n, d = h.shape
    n_slots = PEER_HEADS * PEER_TOPK
    return pl.pallas_call(
        functools.partial(_peer_select_kernel, tm=tm), grid=(n // tm,),
        in_specs=[pl.BlockSpec((tm, d), lambda i: (i, 0)),
                  pl.BlockSpec(wq_t.shape, lambda i: (0, 0)),
                  pl.BlockSpec(sk.shape, lambda i: (0, 0, 0))],
        out_specs=[pl.BlockSpec((n_slots, tm), lambda i: (0, i)),
                   pl.BlockSpec((n_slots, tm), lambda i: (0, i))],
        out_shape=[jax.ShapeDtypeStruct((n_slots, n), I32), jax.ShapeDtypeStruct((n_slots, n), F32)],
        scratch_shapes=[pltpu.VMEM((2 * PEER_HEADS, PEER_TOPK, tm // LANES, LANES), F32),
                        pltpu.VMEM((2 * PEER_HEADS, PEER_TOPK, tm // LANES, LANES), F32)],
        compiler_params=_cparams(("parallel",)), name="peer_select",
    )(h, wq_t, sk)


_GATE_TOKENS = 16


def _peer_gates_kernel(eid_ref, g_ref, w_ref, *, tm):
    sub = lax.broadcasted_iota(I32, (N_KEYS, PEER_HEADS * PEER_TOPK), 0)

    def token_gates(t):
        e = eid_ref[pl.ds(t, 1), :]
        g = g_ref[pl.ds(t, 1), :]
        oh_i = jnp.where((e >> 7) == sub, 1.0, 0.0).astype(BF16)
        g_j = jnp.where((e & (N_KEYS - 1)) == sub, g, 0.0).astype(BF16)
        return lax.dot_general(oh_i, g_j, _NT, preferred_element_type=F32)

    def body(b, carry):
        t0 = pl.multiple_of(b * _GATE_TOKENS, _GATE_TOKENS)
        w = jnp.stack([token_gates(t0 + k) for k in range(_GATE_TOKENS)], axis=0)
        w_ref[:, pl.ds(t0, _GATE_TOKENS), :] = pltpu.einshape("tij->itj", w).astype(BF16)
        return carry

    lax.fori_loop(0, tm // _GATE_TOKENS, body, 0)


def _peer_gates(eid, gate, *, tm):
    n, n_slots = eid.shape
    return pl.pallas_call(
        functools.partial(_peer_gates_kernel, tm=tm), grid=(n // tm,),
        in_specs=[pl.BlockSpec((tm, n_slots), lambda i: (i, 0)),
                  pl.BlockSpec((tm, n_slots), lambda i: (i, 0))],
        out_specs=pl.BlockSpec((N_KEYS, tm, N_KEYS), lambda i: (0, i, 0)),
        out_shape=jax.ShapeDtypeStruct((N_KEYS, n, N_KEYS), BF16),
        compiler_params=_cparams(("parallel",)), name="peer_gates",
    )(eid, gate)


def _gelu(x):
    return 0.5 * x * (1.0 + lax.erf(x * (2.0 ** -0.5)))


def _peer_dense_kernel(h_ref, w_ref, u_ref, v_ref, f_ref):
    @pl.when(pl.program_id(1) == 0)
    def _():
        f_ref[...] = jnp.zeros(f_ref.shape, F32)

    act = _gelu(lax.dot_general(h_ref[...], u_ref[0].astype(BF16), _NT, preferred_element_type=F32))
    y = jnp.concatenate(
        [w_ref[i].astype(F32) * act[:, i * N_KEYS:(i + 1) * N_KEYS] for i in range(w_ref.shape[0])],
        axis=1).astype(BF16)
    f_ref[...] += jnp.dot(y, v_ref[0].astype(BF16), preferred_element_type=F32)


def _peer_dense(hb, wmat, u, v, layer, *, tm, te):
    n, d = hb.shape
    n_exp = u.shape[1]
    return pl.pallas_call(
        _peer_dense_kernel, grid=(n // tm, n_exp // te),
        in_specs=[pl.BlockSpec((tm, d), lambda i, e: (i, 0)),
                  pl.BlockSpec((te // N_KEYS, tm, N_KEYS), lambda i, e: (e, i, 0)),
                  pl.BlockSpec((1, te, d), lambda i, e: (layer, e, 0)),
                  pl.BlockSpec((1, te, d), lambda i, e: (layer, e, 0))],
        out_specs=pl.BlockSpec((tm, d), lambda i, e: (i, 0)),
        out_shape=jax.ShapeDtypeStruct((n, d), F32),
        compiler_params=_cparams(("parallel", "arbitrary")), name="peer_dense",
    )(hb, wmat, u, v)


def _final_kernel(h_ref, f_ref, p_ref, g_ref, b_ref, wg_ref, bg_ref, wp_ref, y_ref, *, alpha):
    h2 = _layer_norm(alpha * h_ref[...] + f_ref[...], g_ref[...], b_ref[...])
    gate = _sigmoid(jnp.dot(h2.astype(BF16), wg_ref[...], preferred_element_type=F32) + bg_ref[...])
    emb = jnp.dot(p_ref[...].astype(BF16), wp_ref[...], preferred_element_type=F32)
    y_ref[...] = h2 + gate * emb


def _final(h, f, p, g, b, wg, bg, wp, *, tm, alpha):
    n, d = h.shape
    row = lambda w: pl.BlockSpec((tm, w), lambda i: (i, 0))
    full = lambda a: pl.BlockSpec(a.shape, lambda i: (0, 0))
    return pl.pallas_call(
        functools.partial(_final_kernel, alpha=alpha), grid=(n // tm,),
        in_specs=[row(d), row(d), row(p.shape[1]), full(g), full(b), full(wg), full(bg), full(wp)],
        out_specs=row(d), out_shape=jax.ShapeDtypeStruct((n, d), F32),
        compiler_params=_cparams(("parallel",)), name="final",
    )(h, f, p, g, b, wg, bg, wp)


def _pack_w_in(w_in):
    d = w_in.shape[0]
    cols = [w_in[:, 0:1792], w_in[:, 1792:1860], jnp.zeros((d, 60), w_in.dtype), w_in[:, 1860:3140]]
    nat = jnp.concatenate(cols, axis=1).astype(BF16)
    seg = lambda name, width=GROUP_W: nat[:, _SEG[name]:_SEG[name] + width]
    tr = jnp.concatenate([seg("qa"), seg("qb"), seg("qi"), seg("kiwi", 128), seg("va"), seg("vb")], axis=1).T
    return nat, tr


def _row(a):
    return a.reshape(1, -1)


def _tile_rows(n, pref):
    t = pref
    while n % t:
        t //= 2
    return t


def _prep_weights(lw, depth):
    (w_in, a_lambda, a_subln_g, c_conv_w, d_conv_w, d_conv_b, d_ln_g, d_ln_b, d_pw_w, d_pw_b,
     w_out, ln1_g, ln1_b, peer_wq, peer_subkeys, peer_u, peer_v, ln2_g, ln2_b,
     ple_w, ple_gate_w, ple_gate_b) = lw
    head_of_lane = jnp.arange(GROUP_W) // HEAD_W
    gmean = (head_of_lane[:, None] == head_of_lane[None, :]).astype(BF16) * (1.0 / HEAD_W)
    w_nat, w_tr = _pack_w_in(w_in)
    return dict(
        w_in=w_nat, w_in_t=w_tr, a_lambda=a_lambda, a_g=_row(jnp.tile(a_subln_g, N_HEADS)),
        gmean=gmean.astype(BF16),
        c_w=c_conv_w, d_w=d_conv_w, d_b=_row(d_conv_b), d_ln_g=_row(d_ln_g), d_ln_b=_row(d_ln_b),
        d_pw=d_pw_w.astype(BF16), d_pwb=_row(d_pw_b), w_out=w_out.astype(BF16),
        ln1_g=_row(ln1_g), ln1_b=_row(ln1_b),
        wq_t=peer_wq.T.astype(BF16), sk=peer_subkeys.astype(BF16),
        u=peer_u, v=peer_v,
        ln2_g=_row(ln2_g), ln2_b=_row(ln2_b),
        ple_w=ple_w.astype(BF16), wg=ple_gate_w.astype(BF16), bg=_row(ple_gate_b),
        alpha=float((2 * depth) ** 0.25),
    )


def _index_keys(ki):
    k_hi = ki.astype(BF16)
    k_lo = (ki - k_hi.astype(F32)).astype(BF16)
    return jnp.concatenate([k_hi, k_hi, k_lo], axis=-1)


def _token_tail(w, x, oa, ob, oc, od, valid_rows, p):
    n = x.shape[0]
    h = _mix(x, oa, ob, oc, od, w["d_pw"], w["d_pwb"], w["w_out"], w["ln1_g"], w["ln1_b"],
             tm=_tile_rows(n, 512), alpha=w["alpha"])
    if valid_rows is not None:
        bsz, t_pad, t = valid_rows
        h = h.reshape(bsz, t_pad, -1)[:, :t].reshape(bsz * t, -1)
    n = h.shape[0]
    n_sel = -(-n // _SELECT_TOKENS) * _SELECT_TOKENS
    h_sel = h if n_sel == n else jnp.pad(h, ((0, n_sel - n), (0, 0)))
    eid_t, gate_t = _peer_select(h_sel, w["wq_t"], w["sk"], tm=_SELECT_TOKENS)
    n_gate = -(-n // _GATE_TOKENS) * _GATE_TOKENS
    wmat = _peer_gates(eid_t[:, :n_gate].T, gate_t[:, :n_gate].T, tm=_tile_rows(n_gate, 128))
    f = _peer_dense(h.astype(BF16), wmat, w["u"], w["v"], w["layer"], tm=_tile_rows(n, 1024), te=8 * N_KEYS)
    return _final(h, f, p, w["ln2_g"], w["ln2_b"], w["wg"], w["bg"], w["ple_w"],
                  tm=_tile_rows(n, 512), alpha=w["alpha"])


def _layer(layer_idx, depth, xp, xs, caches, state_c, state_d, page_table, pp, ps, lw):
    w = _prep_weights(lw, depth)
    w["layer"] = layer_idx
    bp, tp, d = xp.shape
    bs, ts, _ = xs.shape
    ts_pad = 8
    n_pages = page_table.shape[1]
    page = caches[0].shape[3]
    past = n_pages * page
    lam_init = 0.8 - 0.6 * math.exp(-0.3 * (layer_idx + 1))

    pr = _project(xp.reshape(bp * tp, d), w["w_in"], w["w_in_t"], _tile_rows(bp * tp, 512), feature_major=True)
    tq = _tile_rows(tp, 256)
    tk = _tile_rows(tp, 256)
    oa = _attention_a(pr["qat"], pr["kab"], pr["vat"], w["a_lambda"], w["a_g"], w["gmean"],
                      bsz=bp, tq=tq, tk=tk, past=0, lam_init=lam_init)
    ob = _dsa(pr["qbt"], pr["kbb"], pr["vbt"], pr["qit"], _index_keys(pr["kiwi"][:, :D_IDX]), pr["kiwit"],
              bsz=bp, tq=tq, tk=tk, past=0, topk=min(DSA_TOPK, tp // 4), t_valid=tq)
    r3p = lambda a: a.reshape(bp, tp, a.shape[-1])
    zc = jnp.zeros((bp, _HALO_C, GROUP_W), F32)
    zd = jnp.zeros((bp, _HALO_D, GROUP_W), F32)
    oc, od = _convs(r3p(pr["cb"]), r3p(pr["uc"]), r3p(pr["ud"]), zc, zd, w["c_w"], w["d_w"], w["d_b"],
                    w["d_ln_g"], w["d_ln_b"], tm=_tile_rows(tp, 512))
    yp = _token_tail(w, xp.reshape(bp * tp, d), oa, ob, oc.reshape(bp * tp, -1), od.reshape(bp * tp, -1),
                     None, pp.reshape(bp * tp, -1)).reshape(bp, tp, d)
    new_p = (r3p(pr["ka"]), r3p(pr["va"]), r3p(pr["kb"]), r3p(pr["vb"]), r3p(pr["kiwi"])[:, :, :D_IDX],
             jnp.concatenate([zc, r3p(pr["uc"])], axis=1)[:, -(C_WIDTH - 1):],
             jnp.concatenate([zd, r3p(pr["ud"])], axis=1)[:, -(D_WIDTH - 1):])

    xs_pad = jnp.pad(xs, ((0, 0), (0, ts_pad - ts), (0, 0))).reshape(bs * ts_pad, d)
    sr = _project(xs_pad, w["w_in"], w["w_in_t"], _tile_rows(bs * ts_pad, 512), feature_major=False)
    r3s = lambda a: a.reshape(bs, ts_pad, a.shape[-1])
    news = [r3s(sr["ka"]), r3s(sr["va"]), r3s(sr["kb"]), r3s(sr["vb"]), r3s(sr["kiwi"])[:, :, :D_IDX]]
    oa_s, ob_s = _sample_attention(page_table, caches, layer_idx, news, r3s(sr["qa"]), r3s(sr["qb"]), r3s(sr["qi"]),
                                   r3s(sr["kiwi"]), w["a_lambda"], w["a_g"], w["gmean"],
                                   topk=min(DSA_TOPK, (past + ts) // 4), lam_init=lam_init, t_valid=ts)
    unpad = lambda a: a.reshape(bs * ts_pad, GROUP_W)
    hist_c = jnp.concatenate([jnp.zeros((bs, _HALO_C - (C_WIDTH - 1), GROUP_W), F32), state_c], axis=1)
    hist_d = jnp.concatenate([jnp.zeros((bs, _HALO_D - (D_WIDTH - 1), GROUP_W), F32), state_d], axis=1)
    oc_s, od_s = _convs(r3s(sr["cb"]), r3s(sr["uc"]), r3s(sr["ud"]), hist_c, hist_d, w["c_w"], w["d_w"],
                        w["d_b"], w["d_ln_g"], w["d_ln_b"], tm=ts_pad)
    ys = _token_tail(w, xs_pad, unpad(oa_s), unpad(ob_s), oc_s.reshape(bs * ts_pad, -1),
                     od_s.reshape(bs * ts_pad, -1), (bs, ts_pad, ts), ps.reshape(bs * ts, -1)).reshape(bs, ts, d)
    cut = lambda a: r3s(a)[:, :ts]
    new_s = (cut(sr["ka"]), cut(sr["va"]), cut(sr["kb"]), cut(sr["vb"]), cut(sr["kiwi"])[:, :, :D_IDX],
             jnp.concatenate([state_c, cut(sr["uc"])], axis=1)[:, -(C_WIDTH - 1):],
             jnp.concatenate([state_d, cut(sr["ud"])], axis=1)[:, -(D_WIDTH - 1):])
    return yp, ys, new_p, new_s


def kernel(x_prompt, x_sample, cache_a_k, cache_a_v, cache_b_k, cache_b_v, cache_b_kidx, state_c_conv, state_d_conv, page_table, p_prompt, p_sample, w_in, a_lambda, a_subln_g, c_conv_w, d_conv_w, d_conv_b, d_ln_g, d_ln_b, d_pw_w, d_pw_b, w_out, ln1_g, ln1_b, peer_wq, peer_subkeys, peer_u, peer_v, ln2_g, ln2_b, ple_w, ple_gate_w, ple_gate_b):
    depth = w_in.shape[0]
    n_pool, page = cache_a_k.shape[1], cache_a_k.shape[2]
    feature_major = lambda c: jnp.moveaxis(c, 2, -1).reshape(depth, n_pool, -1, page)
    caches = [feature_major(c) for c in (cache_a_k, cache_a_v, cache_b_k, cache_b_v, cache_b_kidx)]
    xp, xs = x_prompt, x_sample
    news_p, news_s = [], []
    for l in range(depth):
        lw = (w_in[l], a_lambda[l], a_subln_g[l], c_conv_w[l], d_conv_w[l], d_conv_b[l], d_ln_g[l], d_ln_b[l],
              d_pw_w[l], d_pw_b[l], w_out[l], ln1_g[l], ln1_b[l], peer_wq[l], peer_subkeys[l], peer_u,
              peer_v, ln2_g[l], ln2_b[l], ple_w[l], ple_gate_w[l], ple_gate_b[l])
        xp, xs, new_p, new_s = _layer(l, depth, xp, xs, caches, state_c_conv[l], state_d_conv[l],
                                      page_table, p_prompt[l], p_sample[l], lw)
        news_p.append(new_p)
        news_s.append(new_s)

    bp, tp, _ = x_prompt.shape
    bs, ts, _ = x_sample.shape
    shapes = [(N_HEADS, 2, 32), (N_HEADS, HEAD_W), (N_HEADS, HEAD_W), (N_HEADS, HEAD_W), (D_IDX,)]

    def stack(news, i, lead):
        a = jnp.stack([n[i] for n in news], axis=0)
        return a.reshape((depth,) + lead + shapes[i]) if i < 5 else a

    outs = [xp, xs]
    for i in range(7):
        outs.append(stack(news_p, i, (bp, tp)))
        outs.append(stack(news_s, i, (bs, ts)))
    return tuple(outs)
```

```python
import functools
import math

import jax
import jax.numpy as jnp
from jax import lax
from jax.experimental import pallas as pl
from jax.experimental.pallas import tpu as pltpu

F32 = jnp.float32
BF16 = jnp.bfloat16
I32 = jnp.int32

LN_EPS = 1e-5
N_HEADS = 4
HEAD_W = 64
GROUP_W = 256
D_IDX = 64
DSA_TOPK = 256
C_WIDTH = 3
D_WIDTH = 31
PEER_HEADS = 8
PEER_TOPK = 16
N_KEYS = 128
LANES = 128
NEG_BIG = -1e30
INT_MIN = -2 ** 31
VMEM_LIMIT = 56 * 1024 * 1024

_SEG = dict(qa=0, ka=256, va=512, qb=768, kb=1024, vb=1280, qi=1536, kiwi=1792,
            cb=1920, cc=2176, cx=2432, da=2688, dg=2944)
_N_PACKED = 3200
_SEG_T = dict(qa=0, qb=256, qi=512, kiwi=768, va=896, vb=1152)
_N_PACKED_T = 1408

_NT = (((1,), (1,)), ((), ()))


def _cparams(sem):
    return pltpu.CompilerParams(dimension_semantics=sem, vmem_limit_bytes=VMEM_LIMIT)


def _split_bf16(x):
    hi = x.astype(BF16)
    lo = (x - hi.astype(F32)).astype(BF16)
    return hi, lo


def _layer_norm(x, g, b):
    mu = jnp.mean(x, axis=-1, keepdims=True)
    xc = x - mu
    var = jnp.mean(xc * xc, axis=-1, keepdims=True)
    return xc * lax.rsqrt(var + LN_EPS) * g + b


def _sigmoid(x):
    return 1.0 / (1.0 + jnp.exp(-x))


_PROJ_COMMON = ("ka", "va", "kb", "vb", "kiwi", "cb", "uc", "ud")
_PROJ_FEATURE_MAJOR = ("kab", "kbb", "qat", "qbt", "qit", "kiwit", "vat", "vbt")
_PROJ_ROW_MAJOR = ("qa", "qb", "qi")


def _proj_kernel(x_ref, w_ref, wt_ref, *out_refs, names, scale_a, scale_b):
    o = dict(zip(names, out_refs))
    xb = x_ref[...].astype(BF16)

    def seg(name, width=GROUP_W):
        off = _SEG[name]
        return jnp.dot(xb, w_ref[:, off:off + width], preferred_element_type=F32)

    def seg_t(name, width=GROUP_W):
        off = _SEG_T[name]
        return lax.dot_general(wt_ref[off:off + width, :], xb, _NT, preferred_element_type=F32)

    ka = seg("ka"); kb = seg("kb")
    o["ka"][...] = ka; o["kb"][...] = kb
    o["va"][...] = seg("va"); o["vb"][...] = seg("vb")
    o["kiwi"][...] = seg("kiwi", LANES)
    o["cb"][...] = seg("cb")
    o["uc"][...] = seg("cc") * seg("cx")
    o["ud"][...] = seg("da") * _sigmoid(seg("dg"))
    if "qat" in o:
        o["kab"][...] = ka.astype(BF16); o["kbb"][...] = kb.astype(BF16)
        o["qat"][...] = seg_t("qa") * scale_a
        o["qbt"][...] = seg_t("qb") * scale_b
        o["qit"][...] = seg_t("qi")
        o["kiwit"][...] = seg_t("kiwi", LANES)
        o["vat"][...] = seg_t("va").astype(BF16)
        o["vbt"][...] = seg_t("vb").astype(BF16)
    else:
        o["qa"][...] = seg("qa") * scale_a
        o["qb"][...] = seg("qb") * scale_b
        o["qi"][...] = seg("qi")


def _project(x, w_packed, wt_packed, tm, feature_major):
    n, d = x.shape
    nat = lambda w, dt=F32: (jax.ShapeDtypeStruct((n, w), dt), pl.BlockSpec((tm, w), lambda i: (i, 0)))
    tr = lambda w, dt=F32: (jax.ShapeDtypeStruct((w, n), dt), pl.BlockSpec((w, tm), lambda i: (0, i)))
    outs = [nat(256), nat(256), nat(256), nat(256), nat(LANES), nat(256), nat(256), nat(256)]
    if feature_major:
        names = _PROJ_COMMON + _PROJ_FEATURE_MAJOR
        outs += [nat(256, BF16), nat(256, BF16), tr(256), tr(256), tr(256), tr(LANES), tr(256, BF16), tr(256, BF16)]
    else:
        names = _PROJ_COMMON + _PROJ_ROW_MAJOR
        outs += [nat(256), nat(256), nat(256)]
    kern = functools.partial(_proj_kernel, names=names, scale_a=float(32 ** -0.5), scale_b=float(HEAD_W ** -0.5))
    res = pl.pallas_call(
        kern, grid=(n // tm,),
        in_specs=[pl.BlockSpec((tm, d), lambda i: (i, 0)),
                  pl.BlockSpec((d, _N_PACKED), lambda i: (0, 0)),
                  pl.BlockSpec((_N_PACKED_T, d), lambda i: (0, 0))],
        out_specs=[o[1] for o in outs], out_shape=[o[0] for o in outs],
        compiler_params=_cparams(("parallel",)), name="proj",
    )(x, w_packed, wt_packed)
    return dict(zip(names, res))


def _lane_heads(o, l, base, rows, lane_head):
    out = None
    for h in range(N_HEADS):
        r0 = (base + h) * rows
        part = jnp.where(lane_head == h, o[r0:r0 + rows] / l[r0:r0 + rows], 0.0)
        out = part if out is None else out + part
    return out


def _sample_attn_kernel(pt_ref, *refs, n_pages, page, rows, past, topk, idx_bits, lam_init, t_valid):
    npg = n_pages
    ak, av, bk, bv, kic = (refs[i * npg:(i + 1) * npg] for i in range(5))
    (nak, nav, nbk, nbv, nki, qa_ref, qb_ref, qi_ref, kiwi_ref, lam_ref, g_ref, gm_ref,
     oa_ref, ob_ref, y_ref) = refs[5 * npg:]
    nblk = npg + 1
    s_tot = nblk * page
    lane_head = lax.broadcasted_iota(I32, (1, GROUP_W), 1) // HEAD_W
    col_last = past + lax.broadcasted_iota(I32, (1, page), 1)
    t_row = lax.broadcasted_iota(I32, (rows, 1), 0)
    int_min = jnp.int32(INT_MIN)

    def blocks(page_refs, new_ref, dtype):
        new = new_ref[0]
        pad = jnp.zeros((page - rows, new.shape[-1]), F32)
        return [r[0, 0].astype(dtype) for r in page_refs] + [jnp.concatenate([new, pad], axis=0).T.astype(dtype)]

    def softmax_pv(q_stack, n_groups, k_blocks, v_blocks, masks):
        def masked(s, mk):
            if mk is None:
                return s
            bias = jnp.where(mk, 0.0, NEG_BIG)[None]
            return (s.reshape(n_groups, rows, page) + bias).reshape(n_groups * rows, page)
        s = [masked(jnp.dot(q_stack, kb, preferred_element_type=F32), mk)
             for kb, mk in zip(k_blocks, masks)]
        m = s[0].max(axis=1, keepdims=True)
        for sj in s[1:]:
            m = jnp.maximum(m, sj.max(axis=1, keepdims=True))
        l = None
        o = None
        for sj, vb, mk in zip(s, v_blocks, masks):
            p = jnp.exp(sj - m)
            lj = p.sum(axis=1, keepdims=True)
            oj = lax.dot_general(p.astype(BF16), vb, _NT, preferred_element_type=F32)
            l = lj if l is None else l + lj
            o = oj if o is None else o + oj
        return o, l

    causal_last = col_last <= (past + t_row)

    qa = qa_ref[0]
    lane_grp = lax.broadcasted_iota(I32, (1, GROUP_W), 1) // 32
    qa_stack = jnp.concatenate(
        [jnp.where(lane_grp == (2 * h + c), qa, 0.0) for c in range(2) for h in range(N_HEADS)],
        axis=0).astype(BF16)
    o, l = softmax_pv(qa_stack, 2 * N_HEADS, blocks(ak, nak, BF16), blocks(av, nav, BF16),
                      [None] * npg + [causal_last])
    o0 = _lane_heads(o, l, 0, rows, lane_head)
    o1 = _lane_heads(o, l, N_HEADS, rows, lane_head)
    lv = lam_ref[...]
    lam = (jnp.exp(jnp.sum(lv[0:1] * lv[1:2], axis=1, keepdims=True))
           - jnp.exp(jnp.sum(lv[2:3] * lv[3:4], axis=1, keepdims=True)) + lam_init)
    od = o0 - lam * o1
    hi, lo = _split_bf16(od * od)
    ms = (jnp.dot(hi, gm_ref[...], preferred_element_type=F32)
          + jnp.dot(lo, gm_ref[...], preferred_element_type=F32))
    oa_ref[0] = od * lax.rsqrt(ms + LN_EPS) * g_ref[...] * (1.0 - lam_init)

    qi = qi_ref[0]
    qi_rows = jnp.concatenate([qi[:, h * D_IDX:(h + 1) * D_IDX] for h in range(N_HEADS)], axis=0)
    q_hi, q_lo = _split_bf16(qi_rows)
    kiwi = kiwi_ref[0]
    w_h = [kiwi[:, D_IDX + h:D_IDX + h + 1] for h in range(N_HEADS)]
    key_blocks = []
    for j, kib in enumerate(blocks(kic, nki, F32)):
        k_hi, k_lo = _split_bf16(kib)
        sc = (jnp.dot(q_hi, k_hi, preferred_element_type=F32)
              + jnp.dot(q_lo, k_hi, preferred_element_type=F32)
              + jnp.dot(q_hi, k_lo, preferred_element_type=F32))
        isc = jnp.maximum(sc[0:rows], 0.0) * w_h[0]
        for h in range(1, N_HEADS):
            isc = isc + jnp.maximum(sc[h * rows:(h + 1) * rows], 0.0) * w_h[h]
        isc = jnp.where(isc == 0.0, 0.0, isc)
        bits = pltpu.bitcast(isc, I32)
        key = bits ^ ((bits >> 31) & jnp.int32(0x7FFFFFFF))
        if j == npg:
            key = jnp.where(causal_last, key, int_min)
        key_blocks.append(key)
    keys = jnp.concatenate(key_blocks, axis=1)
    cols = lax.broadcasted_iota(I32, (1, s_tot), 1)
    kk = jnp.float32(topk)
    count = lambda pred: jnp.sum(jnp.where(pred, 1.0, 0.0), axis=1, keepdims=True)

    def bit_body(r, prefix):
        b1 = lax.shift_left(jnp.int32(1), 31 - 2 * r)
        b0 = lax.shift_left(jnp.int32(1), 30 - 2 * r)
        enough = lambda cand_u: count(keys >= (cand_u ^ int_min)) >= kk
        c10 = prefix | b1
        c11 = c10 | b0
        c01 = prefix | b0
        return jnp.where(enough(c10), jnp.where(enough(c11), c11, c10), jnp.where(enough(c01), c01, prefix))

    thr = lax.fori_loop(0, 16, bit_body, jnp.zeros((rows, 1), I32)) ^ int_min
    r_need = kk - count(keys > thr)
    any_tie = jnp.max(jnp.where(t_row < t_valid, count(keys >= thr) - kk, 0.0)) > 0.0
    y_ref[...] = jnp.full((rows, 1), s_tot, I32)

    @pl.when(any_tie)
    def _():
        def ybody(r, y):
            cand = y | lax.shift_left(jnp.int32(1), idx_bits - 1 - r)
            return jnp.where(count((keys == thr) & (cols < cand)) < r_need, cand, y)
        y_ref[...] = lax.fori_loop(0, idx_bits, ybody, jnp.zeros((rows, 1), I32))

    sel = ((keys > thr) | ((keys == thr) & (cols <= y_ref[...]))) & (cols <= (past + t_row))
    qb = qb_ref[0]
    qb_stack = jnp.concatenate([jnp.where(lane_head == h, qb, 0.0) for h in range(N_HEADS)], axis=0).astype(BF16)
    o, l = softmax_pv(qb_stack, N_HEADS, blocks(bk, nbk, BF16), blocks(bv, nbv, BF16),
                      [sel[:, j * page:(j + 1) * page] for j in range(nblk)])
    ob_ref[0] = _lane_heads(o, l, 0, rows, lane_head)


def _sample_attention(page_table, caches, layer, news, qa, qb, qi, kiwi, a_lambda, g_full, gmean,
                      *, topk, lam_init, t_valid):
    bsz, n_pages = page_table.shape
    page = caches[0].shape[3]
    rows = qa.shape[1]
    past = n_pages * page
    idx_bits = int(math.ceil(math.log2((n_pages + 1) * page))) + 1
    page_spec = lambda c, j: pl.BlockSpec((1, 1, c, page), lambda b, pt: (layer, pt[b, j], 0, 0))
    seq_spec = lambda c: pl.BlockSpec((1, rows, c), lambda b, pt: (b, 0, 0))
    full2 = lambda a: pl.BlockSpec(a.shape, lambda b, pt: (0, 0))
    in_specs, args = [], []
    for cache in caches:
        for j in range(n_pages):
            in_specs.append(page_spec(cache.shape[2], j))
            args.append(cache)
    for a in list(news) + [qa, qb, qi, kiwi]:
        in_specs.append(seq_spec(a.shape[-1]))
        args.append(a)
    for a in (a_lambda, g_full, gmean):
        in_specs.append(full2(a))
        args.append(a)
    kern = functools.partial(_sample_attn_kernel, n_pages=n_pages, page=page, rows=rows, past=past, topk=topk,
                             idx_bits=idx_bits, lam_init=lam_init, t_valid=t_valid)
    return pl.pallas_call(
        kern,
        grid_spec=pltpu.PrefetchScalarGridSpec(
            num_scalar_prefetch=1, grid=(bsz,), in_specs=in_specs,
            out_specs=[seq_spec(GROUP_W), seq_spec(GROUP_W)],
            scratch_shapes=[pltpu.VMEM((rows, 1), I32)]),
        out_shape=[jax.ShapeDtypeStruct((bsz, rows, GROUP_W), F32)] * 2,
        compiler_params=_cparams(("parallel",)), name="sample_attn",
    )(page_table, *args)


def _attn_core(qst, n_groups, k_ref, vt_ref, n_full, n_kv, tq, tk, mask_fn, m_ref, l_ref, acc_ref, zero_masked):
    n_comp = n_groups // N_HEADS
    row_head = lax.broadcasted_iota(I32, (GROUP_W, 1), 0) // HEAD_W
    m_ref[...] = jnp.full(m_ref.shape, 0.5 * NEG_BIG, F32)
    l_ref[...] = jnp.zeros(l_ref.shape, F32)
    acc_ref[...] = jnp.zeros(acc_ref.shape, F32)
    grp = lambda a, g: a[:, g * tq:(g + 1) * tq]

    def step(c, n_blk, masked):
        width = n_blk * tk
        k0 = pl.multiple_of(c * tk, tk)
        kblk = k_ref[pl.ds(k0, width), :]
        vt = vt_ref[:, pl.ds(k0, width)]
        if masked:
            parts = [jnp.where(mask_fn(c + i, k0 + i * tk), 0.0, NEG_BIG) for i in range(n_blk)]
            bias = parts[0] if n_blk == 1 else jnp.concatenate(parts, axis=0)

        m_old = m_ref[...]
        l_old = l_ref[...]
        m_new, l_new, alpha, pb = [], [], [], []
        for g in range(n_groups):
            s = jnp.dot(kblk, grp(qst, g), preferred_element_type=F32)
            if masked:
                s = s + bias
            m_g = jnp.maximum(grp(m_old, g), jnp.max(s, axis=0, keepdims=True))
            a_g = jnp.exp(grp(m_old, g) - m_g)
            p = jnp.exp(s - m_g)
            m_new.append(m_g)
            alpha.append(a_g)
            l_new.append(a_g * grp(l_old, g) + jnp.sum(p, axis=0, keepdims=True))
            pb.append(p.astype(BF16))
        m_ref[...] = jnp.concatenate(m_new, axis=1)
        l_ref[...] = jnp.concatenate(l_new, axis=1)
        for comp in range(n_comp):
            for h in range(N_HEADS):
                g = comp * N_HEADS + h
                rows = slice(h * HEAD_W, (h + 1) * HEAD_W)
                pv = jnp.dot(vt[rows], pb[g], preferred_element_type=F32)
                acc_ref[comp, rows, :] = acc_ref[comp, rows, :] * alpha[g] + pv

    def run(lo, hi, masked):
        def pair_body(i, carry):
            step(lo + 2 * i, 2, masked)
            return carry
        n = hi - lo
        if isinstance(n, int) and n == 0:
            return
        lax.fori_loop(0, n // 2, pair_body, 0)

        @pl.when(n % 2 == 1)
        def _():
            step(hi - 1, 1, masked)

    run(0, n_full, False)
    run(n_full, n_kv, True)


def _normalised(acc, l_row, base, tq):
    rl = 1.0 / l_row
    rf = jnp.concatenate(
        [jnp.broadcast_to(rl[:, (base + h) * tq:(base + h + 1) * tq], (HEAD_W, tq)) for h in range(N_HEADS)],
        axis=0)
    return acc * rf


def _num_kv_blocks(q0, tq, tk, past, s_tot):
    return jnp.minimum((past + q0 + tq + tk - 1) // tk, s_tot // tk)


def _attn_a_kernel(qt_ref, k_ref, vt_ref, lam_ref, g_ref, gm_ref, o_ref, m_ref, l_ref, acc_ref,
                   *, tq, tk, past, s_tot, lam_init):
    q0 = pl.program_id(1) * tq
    qt = qt_ref[...]
    row_grp = lax.broadcasted_iota(I32, (GROUP_W, 1), 0) // 32
    qst = jnp.concatenate(
        [jnp.where(row_grp == (2 * h + c), qt, 0.0) for c in range(2) for h in range(N_HEADS)],
        axis=1).astype(BF16)
    n_kv = _num_kv_blocks(q0, tq, tk, past, s_tot)
    n_full = jnp.minimum((past + q0 + 1) // tk, n_kv)
    row_pos = past + q0 + lax.broadcasted_iota(I32, (1, tq), 1)
    key_iota = lax.broadcasted_iota(I32, (tk, 1), 0)

    def mask_fn(c, k0):
        return (k0 + key_iota) <= row_pos

    _attn_core(qst, 2 * N_HEADS, k_ref, vt_ref, n_full, n_kv, tq, tk, mask_fn,
               m_ref, l_ref, acc_ref, zero_masked=False)
    l_row = l_ref[...]
    o0 = _normalised(acc_ref[0], l_row, 0, tq)
    o1 = _normalised(acc_ref[1], l_row, N_HEADS, tq)
    lv = lam_ref[...]
    lam = (jnp.exp(jnp.sum(lv[0:1] * lv[1:2], axis=1, keepdims=True))
           - jnp.exp(jnp.sum(lv[2:3] * lv[3:4], axis=1, keepdims=True)) + lam_init)
    o = (o0 - lam * o1).T
    hi, lo = _split_bf16(o * o)
    ms = (jnp.dot(hi, gm_ref[...], preferred_element_type=F32)
          + jnp.dot(lo, gm_ref[...], preferred_element_type=F32))
    o_ref[...] = o * lax.rsqrt(ms + LN_EPS) * g_ref[...] * (1.0 - lam_init)


def _attention_a(qt, k, vt, a_lambda, g_full, gmean, *, bsz, tq, tk, past, lam_init):
    t = qt.shape[1] // bsz
    s_tot = k.shape[0] // bsz
    nq = t // tq
    kern = functools.partial(_attn_a_kernel, tq=tq, tk=tk, past=past, s_tot=s_tot, lam_init=lam_init)
    return pl.pallas_call(
        kern, grid=(bsz, nq),
        in_specs=[pl.BlockSpec((GROUP_W, tq), lambda b, i: (0, b * nq + i)),
                  pl.BlockSpec((s_tot, GROUP_W), lambda b, i: (b, 0)),
                  pl.BlockSpec((GROUP_W, s_tot), lambda b, i: (0, b)),
                  pl.BlockSpec(a_lambda.shape, lambda b, i: (0, 0)),
                  pl.BlockSpec((1, GROUP_W), lambda b, i: (0, 0)),
                  pl.BlockSpec((GROUP_W, GROUP_W), lambda b, i: (0, 0))],
        out_specs=pl.BlockSpec((tq, GROUP_W), lambda b, i: (b * nq + i, 0)),
        out_shape=jax.ShapeDtypeStruct((bsz * t, GROUP_W), F32),
        scratch_shapes=[pltpu.VMEM((1, 2 * N_HEADS * tq), F32), pltpu.VMEM((1, 2 * N_HEADS * tq), F32),
                        pltpu.VMEM((2, GROUP_W, tq), F32)],
        compiler_params=_cparams(("parallel", "arbitrary")), name="attn_a",
    )(qt, k, vt, a_lambda, g_full, gmean)


def _dsa_kernel(qt_ref, k_ref, vt_ref, qit_ref, ki_ref, kiwit_ref, o_ref, key_ref, y_ref, m_ref, l_ref, acc_ref,
                *, tq, tk, past, s_tot, topk, idx_bits, t_valid):
    q0 = pl.program_id(1) * tq
    n_kv = _num_kv_blocks(q0, tq, tk, past, s_tot)
    row_pos = past + q0 + lax.broadcasted_iota(I32, (1, tq), 1)
    key_iota = lax.broadcasted_iota(I32, (tk, 1), 0)
    int_min = jnp.int32(INT_MIN)

    qit = qit_ref[...]
    blocks = []
    for h in range(N_HEADS):
        qh = qit[h * D_IDX:(h + 1) * D_IDX, :]
        hi, lo = _split_bf16(qh)
        blocks.append(jnp.concatenate([hi, lo, hi], axis=0))
    qi = jnp.concatenate(blocks, axis=1)
    kiwit = kiwit_ref[...]
    w_h = [kiwit[D_IDX + h:D_IDX + h + 1, :] for h in range(N_HEADS)]

    def score_body(c, carry):
        k0 = pl.multiple_of(c * tk, tk)
        kib = ki_ref[pl.ds(k0, tk), :]
        sc = jnp.dot(kib, qi, preferred_element_type=F32)
        isc = jnp.maximum(sc[:, 0:tq], 0.0) * w_h[0]
        for h in range(1, N_HEADS):
            isc = isc + jnp.maximum(sc[:, h * tq:(h + 1) * tq], 0.0) * w_h[h]
        isc = jnp.where(isc == 0.0, 0.0, isc)
        bits = pltpu.bitcast(isc, I32)
        key = bits ^ ((bits >> 31) & jnp.int32(0x7FFFFFFF))
        key_ref[c] = jnp.where((k0 + key_iota) <= row_pos, key, int_min)
        return carry

    lax.fori_loop(0, n_kv, score_body, 0)

    def count(pred):
        def block_count(c):
            ones = jnp.where(pred(key_ref[c], c * tk), 1.0, 0.0)
            return jnp.sum(ones.reshape(tk // 8, 8, tq), axis=0)

        def body(i, acc):
            return acc + block_count(2 * i) + block_count(2 * i + 1)

        acc = lax.fori_loop(0, n_kv // 2, body, jnp.zeros((8, tq), F32))
        odd = (n_kv & 1).astype(F32)
        acc = acc + odd * block_count(n_kv - 1)
        return jnp.sum(acc, axis=0, keepdims=True)

    kk = jnp.float32(topk)

    def bit_body(r, carry):
        prefix, n_ge, n_gt = carry
        cand_u = prefix | lax.shift_left(jnp.int32(1), 31 - r)
        cand_s = cand_u ^ int_min
        cnt = count(lambda blk, k0: blk >= cand_s)
        ok = cnt >= kk
        return jnp.where(ok, cand_u, prefix), jnp.where(ok, cnt, n_ge), jnp.where(ok, n_gt, cnt)

    n_all = jnp.full((1, tq), 1.0, F32) * (n_kv * tk).astype(F32)
    prefix, n_ge, n_gt = lax.fori_loop(
        0, 32, bit_body, (jnp.zeros((1, tq), I32), n_all, jnp.zeros((1, tq), F32)))
    thr = prefix ^ int_min
    r_need = kk - n_gt
    live = lax.broadcasted_iota(I32, (1, tq), 1) < t_valid
    any_tie = jnp.max(jnp.where(live, n_ge - kk, 0.0)) > 0.0
    y_ref[...] = jnp.full((1, tq), s_tot, I32)

    @pl.when(any_tie)
    def _():
        def ybody(r, y):
            cand = y | lax.shift_left(jnp.int32(1), idx_bits - 1 - r)
            cnt = count(lambda blk, k0: (blk == thr) & ((k0 + key_iota) < cand))
            return jnp.where(cnt < r_need, cand, y)
        y_ref[...] = lax.fori_loop(0, idx_bits, ybody, jnp.zeros((1, tq), I32))

    y_last = y_ref[...]

    def mask_fn(c, k0):
        blk = key_ref[c]
        pos = k0 + key_iota
        lowered = jnp.where(pos > y_last, blk - 1, blk)
        return (lowered >= thr) & (pos <= row_pos)

    qt = qt_ref[...]
    row_head = lax.broadcasted_iota(I32, (GROUP_W, 1), 0) // HEAD_W
    qst = jnp.concatenate([jnp.where(row_head == h, qt, 0.0) for h in range(N_HEADS)], axis=1).astype(BF16)
    _attn_core(qst, N_HEADS, k_ref, vt_ref, 0, n_kv, tq, tk, mask_fn, m_ref, l_ref, acc_ref, zero_masked=True)
    o_ref[...] = _normalised(acc_ref[0], l_ref[...], 0, tq).T


def _dsa(qt, k, vt, qit, ki3, kiwit, *, bsz, tq, tk, past, topk, t_valid):
    t = qt.shape[1] // bsz
    s_tot = k.shape[0] // bsz
    nq = t // tq
    idx_bits = int(math.ceil(math.log2(s_tot))) + 1
    kern = functools.partial(_dsa_kernel, tq=tq, tk=tk, past=past, s_tot=s_tot, topk=topk,
                             idx_bits=idx_bits, t_valid=t_valid)
    qspec = lambda w: pl.BlockSpec((w, tq), lambda b, i: (0, b * nq + i))
    return pl.pallas_call(
        kern, grid=(bsz, nq),
        in_specs=[qspec(GROUP_W),
                  pl.BlockSpec((s_tot, GROUP_W), lambda b, i: (b, 0)),
                  pl.BlockSpec((GROUP_W, s_tot), lambda b, i: (0, b)),
                  qspec(GROUP_W),
                  pl.BlockSpec((s_tot, 3 * D_IDX), lambda b, i: (b, 0)),
                  qspec(LANES)],
        out_specs=pl.BlockSpec((tq, GROUP_W), lambda b, i: (b * nq + i, 0)),
        out_shape=jax.ShapeDtypeStruct((bsz * t, GROUP_W), F32),
        scratch_shapes=[pltpu.VMEM((s_tot // tk, tk, tq), I32), pltpu.VMEM((1, tq), I32),
                        pltpu.VMEM((1, N_HEADS * tq), F32), pltpu.VMEM((1, N_HEADS * tq), F32),
                        pltpu.VMEM((1, GROUP_W, tq), F32)],
        compiler_params=_cparams(("parallel", "arbitrary")), name="dsa",
    )(qt, k, vt, qit, ki3, kiwit)


_HALO_C = 8
_HALO_D = 32


def _conv_kernel(*refs, tm, single_tile):
    if single_tile:
        (cb_ref, uc_ref, ud_ref, hc_ref, hd_ref, cw_ref, dw_ref, db_ref, lg_ref, lb_ref,
         oc_ref, od_ref, ec_ref, ed_ref) = refs
        pc_ref = pd_ref = None
    else:
        (cb_ref, uc_ref, ud_ref, pc_ref, pd_ref, hc_ref, hd_ref, cw_ref, dw_ref, db_ref, lg_ref, lb_ref,
         oc_ref, od_ref, ec_ref, ed_ref) = refs
    i = pl.program_id(1)

    @pl.when(i == 0)
    def _():
        ec_ref[0:_HALO_C] = hc_ref[0]
        ed_ref[0:_HALO_D] = hd_ref[0]

    if not single_tile:
        @pl.when(i > 0)
        def _():
            ec_ref[0:_HALO_C] = pc_ref[0]
            ed_ref[0:_HALO_D] = pd_ref[0]

    ec_ref[_HALO_C:_HALO_C + tm] = uc_ref[0]
    ed_ref[_HALO_D:_HALO_D + tm] = ud_ref[0]

    cw = cw_ref[...]
    acc = ec_ref[pl.ds(_HALO_C - (C_WIDTH - 1), tm), :] * cw[0:1]
    for w in range(1, C_WIDTH):
        acc = acc + ec_ref[pl.ds(_HALO_C - (C_WIDTH - 1) + w, tm), :] * cw[w:w + 1]
    oc_ref[0] = cb_ref[0] * acc

    dw = dw_ref[...]
    acc = ed_ref[pl.ds(_HALO_D - (D_WIDTH - 1), tm), :] * dw[0:1]
    for w in range(1, D_WIDTH):
        acc = acc + ed_ref[pl.ds(_HALO_D - (D_WIDTH - 1) + w, tm), :] * dw[w:w + 1]
    z = _layer_norm(acc + db_ref[...], lg_ref[...], lb_ref[...])
    od_ref[0] = z * _sigmoid(z)


def _convs(cb, uc, ud, hist_c, hist_d, c_w, d_w, d_b, ln_g, ln_b, *, tm):
    bsz, t, w = cb.shape
    nt = t // tm
    single = nt == 1
    main = pl.BlockSpec((1, tm, w), lambda b, i: (b, i, 0))
    prev_c = pl.BlockSpec((1, _HALO_C, w), lambda b, i: (b, jnp.maximum(i * (tm // _HALO_C) - 1, 0), 0))
    prev_d = pl.BlockSpec((1, _HALO_D, w), lambda b, i: (b, jnp.maximum(i * (tm // _HALO_D) - 1, 0), 0))
    hist_cs = pl.BlockSpec((1, _HALO_C, w), lambda b, i: (b, 0, 0))
    hist_ds = pl.BlockSpec((1, _HALO_D, w), lambda b, i: (b, 0, 0))
    full2 = lambda a: pl.BlockSpec(a.shape, lambda b, i: (0, 0))
    in_specs = [main, main, main] + ([] if single else [prev_c, prev_d]) + [hist_cs, hist_ds] + \
        [full2(c_w), full2(d_w), full2(d_b), full2(ln_g), full2(ln_b)]
    args = [cb, uc, ud] + ([] if single else [uc, ud]) + [hist_c, hist_d, c_w, d_w, d_b, ln_g, ln_b]
    kern = functools.partial(_conv_kernel, tm=tm, single_tile=single)
    return pl.pallas_call(
        kern, grid=(bsz, nt), in_specs=in_specs,
        out_specs=[main, main],
        out_shape=[jax.ShapeDtypeStruct((bsz, t, w), F32)] * 2,
        scratch_shapes=[pltpu.VMEM((_HALO_C + tm, w), F32), pltpu.VMEM((_HALO_D + tm, w), F32)],
        compiler_params=_cparams(("parallel", "arbitrary")), name="convs",
    )(*args)


def _mix_kernel(x_ref, oa_ref, ob_ref, oc_ref, od_ref, pw_ref, pwb_ref, wo_ref, g_ref, b_ref, h_ref, *, alpha):
    od = jnp.dot(od_ref[...].astype(BF16), pw_ref[...], preferred_element_type=F32) + pwb_ref[...]
    cat = jnp.concatenate([oa_ref[...], ob_ref[...], oc_ref[...], od], axis=1).astype(BF16)
    mix = jnp.dot(cat, wo_ref[...], preferred_element_type=F32)
    h_ref[...] = _layer_norm(alpha * x_ref[...] + mix, g_ref[...], b_ref[...])


def _mix(x, oa, ob, oc, od, pw, pwb, wo, g, b, *, tm, alpha):
    n, d = x.shape
    row = lambda w: pl.BlockSpec((tm, w), lambda i: (i, 0))
    full = lambda a: pl.BlockSpec(a.shape, lambda i: (0, 0))
    return pl.pallas_call(
        functools.partial(_mix_kernel, alpha=alpha), grid=(n // tm,),
        in_specs=[row(d), row(GROUP_W), row(GROUP_W), row(GROUP_W), row(GROUP_W),
                  full(pw), full(pwb), full(wo), full(g), full(b)],
        out_specs=row(d), out_shape=jax.ShapeDtypeStruct((n, d), F32),
        compiler_params=_cparams(("parallel",)), name="mix",
    )(x, oa, ob, oc, od, pw, pwb, wo, g, b)


def _staircase():
    return [(a, b) for a in range(PEER_TOPK) for b in range(PEER_TOPK) if (a + 1) * (b + 1) <= PEER_TOPK]


def _tree_rows(x, op):
    while x.shape[0] > 1:
        half = x.shape[0] // 2
        y = op(x[:half], x[half:2 * half])
        x = y if x.shape[0] % 2 == 0 else jnp.concatenate([y, x[2 * half:]], axis=0)
    return x


def _argmax_rows(x, row):
    v, i = x, row
    while v.shape[0] > 1:
        n = v.shape[0]
        half = n // 2
        pv = v[:2 * half].reshape((half, 2) + v.shape[1:])
        pi = i[:2 * half].reshape((half, 2) + i.shape[1:])
        take_b = pv[:, 1] > pv[:, 0]
        nv = jnp.where(take_b, pv[:, 1], pv[:, 0])
        ni = jnp.where(take_b, pi[:, 1], pi[:, 0])
        if n % 2:
            nv = jnp.concatenate([nv, v[2 * half:]], axis=0)
            ni = jnp.concatenate([ni, i[2 * half:]], axis=0)
        v, i = nv, ni
    return v, i


def _extract_top(x, payload, n_out, n_rows):
    row = lax.broadcasted_iota(I32, x.shape, 0).astype(F32)
    vals, idxs, pays = [], [], []
    for _ in range(n_out):
        m, idx = _argmax_rows(x, row)
        hit = row == idx
        vals.append(m)
        idxs.append(idx)
        if payload is not None:
            pays.append(_tree_rows(jnp.where(hit, payload, -1.0), jnp.maximum))
        x = jnp.where(hit, -jnp.inf, x)
    return vals, idxs, pays


def _peer_select_kernel(h_ref, wq_ref, sk_ref, eid_ref, g_ref, sv_ref, si_ref, *, tm):
    ht = h_ref[...].T.astype(BF16)
    n_groups = 2 * PEER_HEADS
    nt = tm // LANES

    def to_tiles(x):
        return pltpu.einshape("trl->rtl", jnp.stack([x[:, LANES * t:LANES * (t + 1)] for t in range(nt)], axis=0))

    def from_tiles(y):
        z = pltpu.einshape("rtl->trl", y)
        return jnp.concatenate([z[t] for t in range(nt)], axis=1)

    def group_body(g, carry):
        r0 = pl.multiple_of(g * N_KEYS, N_KEYS)
        qg = jnp.dot(wq_ref[pl.ds(r0, N_KEYS), :], ht, preferred_element_type=F32)
        st = jnp.dot(sk_ref[g % 2], qg.astype(BF16), preferred_element_type=F32)
        vals, idxs, _ = _extract_top(to_tiles(st), None, PEER_TOPK, N_KEYS)
        sv_ref[g] = jnp.concatenate(vals, axis=0)
        si_ref[g] = jnp.concatenate(idxs, axis=0)
        return carry

    lax.fori_loop(0, n_groups, group_body, 0)

    pairs = _staircase()

    def head_body(hd, carry):
        sv0 = sv_ref[2 * hd]; sv1 = sv_ref[2 * hd + 1]
        si0 = si_ref[2 * hd]; si1 = si_ref[2 * hd + 1]
        cand = jnp.concatenate([sv0[a:a + 1] + sv1[b:b + 1] for a, b in pairs], axis=0)
        cand_e = jnp.concatenate([si0[a:a + 1] * float(N_KEYS) + si1[b:b + 1] for a, b in pairs], axis=0)
        vals, _, pays = _extract_top(cand, cand_e, PEER_TOPK, len(pairs))
        fv = jnp.concatenate(vals, axis=0)
        e = jnp.exp(fv - fv[0:1])
        gate = e / jnp.sum(e, axis=0, keepdims=True)
        r0 = pl.multiple_of(hd * PEER_TOPK, PEER_TOPK)
        eid_ref[pl.ds(r0, PEER_TOPK), :] = from_tiles(jnp.concatenate(pays, axis=0)).astype(I32)
        g_ref[pl.ds(r0, PEER_TOPK), :] = from_tiles(gate)
        return carry

    lax.fori_loop(0, PEER_HEADS, head_body, 0)


_SELECT_TOKENS = 8 * LANES


def _peer_select(h, wq_t, sk, *, tm):
    n, d = h.shape
    n_slots = PEER_HEADS * PEER_TOPK
    return pl.pallas_call(
        functools.partial(_peer_select_kernel, tm=tm), grid=(n // tm,),
        in_specs=[pl.BlockSpec((tm, d), lambda i: (i, 0)),
                  pl.BlockSpec(wq_t.shape, lambda i: (0, 0)),
                  pl.BlockSpec(sk.shape, lambda i: (0, 0, 0))],
        out_specs=[pl.BlockSpec((n_slots, tm), lambda i: (0, i)),
                   pl.BlockSpec((n_slots, tm), lambda i: (0, i))],
        out_shape=[jax.ShapeDtypeStruct((n_slots, n), I32), jax.ShapeDtypeStruct((n_slots, n), F32)],
        scratch_shapes=[pltpu.VMEM((2 * PEER_HEADS, PEER_TOPK, tm // LANES, LANES), F32),
                        pltpu.VMEM((2 * PEER_HEADS, PEER_TOPK, tm // LANES, LANES), F32)],
        compiler_params=_cparams(("parallel",)), name="peer_select",
    )(h, wq_t, sk)


_GATE_TOKENS = 16


def _peer_gates_kernel(eid_ref, g_ref, w_ref, *, tm):
    sub = lax.broadcasted_iota(I32, (N_KEYS, PEER_HEADS * PEER_TOPK), 0)

    def token_gates(t):
        e = eid_ref[pl.ds(t, 1), :]
        g = g_ref[pl.ds(t, 1), :]
        oh_i = jnp.where((e >> 7) == sub, 1.0, 0.0).astype(BF16)
        g_j = jnp.where((e & (N_KEYS - 1)) == sub, g, 0.0).astype(BF16)
        return lax.dot_general(oh_i, g_j, _NT, preferred_element_type=F32)

    def body(b, carry):
        t0 = pl.multiple_of(b * _GATE_TOKENS, _GATE_TOKENS)
        w = jnp.stack([token_gates(t0 + k) for k in range(_GATE_TOKENS)], axis=0)
        w_ref[:, pl.ds(t0, _GATE_TOKENS), :] = pltpu.einshape("tij->itj", w).astype(BF16)
        return carry

    lax.fori_loop(0, tm // _GATE_TOKENS, body, 0)


def _peer_gates(eid, gate, *, tm):
    n, n_slots = eid.shape
    return pl.pallas_call(
        functools.partial(_peer_gates_kernel, tm=tm), grid=(n // tm,),
        in_specs=[pl.BlockSpec((tm, n_slots), lambda i: (i, 0)),
                  pl.BlockSpec((tm, n_slots), lambda i: (i, 0))],
        out_specs=pl.BlockSpec((N_KEYS, tm, N_KEYS), lambda i: (0, i, 0)),
        out_shape=jax.ShapeDtypeStruct((N_KEYS, n, N_KEYS), BF16),
        compiler_params=_cparams(("parallel",)), name="peer_gates",
    )(eid, gate)


def _gelu(x):
    return 0.5 * x * (1.0 + lax.erf(x * (2.0 ** -0.5)))


def _peer_dense_kernel(h_ref, w_ref, u_ref, v_ref, f_ref):
    @pl.when(pl.program_id(1) == 0)
    def _():
        f_ref[...] = jnp.zeros(f_ref.shape, F32)

    act = _gelu(lax.dot_general(h_ref[...], u_ref[0].astype(BF16), _NT, preferred_element_type=F32))
    y = jnp.concatenate(
        [w_ref[i].astype(F32) * act[:, i * N_KEYS:(i + 1) * N_KEYS] for i in range(w_ref.shape[0])],
        axis=1).astype(BF16)
    f_ref[...] += jnp.dot(y, v_ref[0].astype(BF16), preferred_element_type=F32)


def _peer_dense(hb, wmat, u, v, layer, *, tm, te):
    n, d = hb.shape
    n_exp = u.shape[1]
    return pl.pallas_call(
        _peer_dense_kernel, grid=(n // tm, n_exp // te),
        in_specs=[pl.BlockSpec((tm, d), lambda i, e: (i, 0)),
                  pl.BlockSpec((te // N_KEYS, tm, N_KEYS), lambda i, e: (e, i, 0)),
                  pl.BlockSpec((1, te, d), lambda i, e: (layer, e, 0)),
                  pl.BlockSpec((1, te, d), lambda i, e: (layer, e, 0))],
        out_specs=pl.BlockSpec((tm, d), lambda i, e: (i, 0)),
        out_shape=jax.ShapeDtypeStruct((n, d), F32),
        compiler_params=_cparams(("parallel", "arbitrary")), name="peer_dense",
    )(hb, wmat, u, v)


def _final_kernel(h_ref, f_ref, p_ref, g_ref, b_ref, wg_ref, bg_ref, wp_ref, y_ref, *, alpha):
    h2 = _layer_norm(alpha * h_ref[...] + f_ref[...], g_ref[...], b_ref[...])
    gate = _sigmoid(jnp.dot(h2.astype(BF16), wg_ref[...], preferred_element_type=F32) + bg_ref[...])
    emb = jnp.dot(p_ref[...].astype(BF16), wp_ref[...], preferred_element_type=F32)
    y_ref[...] = h2 + gate * emb


def _final(h, f, p, g, b, wg, bg, wp, *, tm, alpha):
    n, d = h.shape
    row = lambda w: pl.BlockSpec((tm, w), lambda i: (i, 0))
    full = lambda a: pl.BlockSpec(a.shape, lambda i: (0, 0))
    return pl.pallas_call(
        functools.partial(_final_kernel, alpha=alpha), grid=(n // tm,),
        in_specs=[row(d), row(d), row(p.shape[1]), full(g), full(b), full(wg), full(bg), full(wp)],
        out_specs=row(d), out_shape=jax.ShapeDtypeStruct((n, d), F32),
        compiler_params=_cparams(("parallel",)), name="final",
    )(h, f, p, g, b, wg, bg, wp)


def _pack_w_in(w_in):
    d = w_in.shape[0]
    cols = [w_in[:, 0:1792], w_in[:, 1792:1860], jnp.zeros((d, 60), w_in.dtype), w_in[:, 1860:3140]]
    nat = jnp.concatenate(cols, axis=1).astype(BF16)
    seg = lambda name, width=GROUP_W: nat[:, _SEG[name]:_SEG[name] + width]
    tr = jnp.concatenate([seg("qa"), seg("qb"), seg("qi"), seg("kiwi", 128), seg("va"), seg("vb")], axis=1).T
    return nat, tr


def _row(a):
    return a.reshape(1, -1)


def _tile_rows(n, pref):
    t = pref
    while n % t:
        t //= 2
    return t


def _prep_weights(lw, depth):
    (w_in, a_lambda, a_subln_g, c_conv_w, d_conv_w, d_conv_b, d_ln_g, d_ln_b, d_pw_w, d_pw_b,
     w_out, ln1_g, ln1_b, peer_wq, peer_subkeys, peer_u, peer_v, ln2_g, ln2_b,
     ple_w, ple_gate_w, ple_gate_b) = lw
    head_of_lane = jnp.arange(GROUP_W) // HEAD_W
    gmean = (head_of_lane[:, None] == head_of_lane[None, :]).astype(BF16) * (1.0 / HEAD_W)
    w_nat, w_tr = _pack_w_in(w_in)
    return dict(
        w_in=w_nat, w_in_t=w_tr, a_lambda=a_lambda, a_g=_row(jnp.tile(a_subln_g, N_HEADS)),
        gmean=gmean.astype(BF16),
        c_w=c_conv_w, d_w=d_conv_w, d_b=_row(d_conv_b), d_ln_g=_row(d_ln_g), d_ln_b=_row(d_ln_b),
        d_pw=d_pw_w.astype(BF16), d_pwb=_row(d_pw_b), w_out=w_out.astype(BF16),
        ln1_g=_row(ln1_g), ln1_b=_row(ln1_b),
        wq_t=peer_wq.T.astype(BF16), sk=peer_subkeys.astype(BF16),
        u=peer_u, v=peer_v,
        ln2_g=_row(ln2_g), ln2_b=_row(ln2_b),
        ple_w=ple_w.astype(BF16), wg=ple_gate_w.astype(BF16), bg=_row(ple_gate_b),
        alpha=float((2 * depth) ** 0.25),
    )


def _index_keys(ki):
    k_hi = ki.astype(BF16)
    k_lo = (ki - k_hi.astype(F32)).astype(BF16)
    return jnp.concatenate([k_hi, k_hi, k_lo], axis=-1)


def _token_tail(w, x, oa, ob, oc, od, valid_rows, p):
    n = x.shape[0]
    h = _mix(x, oa, ob, oc, od, w["d_pw"], w["d_pwb"], w["w_out"], w["ln1_g"], w["ln1_b"],
             tm=_tile_rows(n, 512), alpha=w["alpha"])
    if valid_rows is not None:
        bsz, t_pad, t = valid_rows
        h = h.reshape(bsz, t_pad, -1)[:, :t].reshape(bsz * t, -1)
    n = h.shape[0]
    n_sel = -(-n // _SELECT_TOKENS) * _SELECT_TOKENS
    h_sel = h if n_sel == n else jnp.pad(h, ((0, n_sel - n), (0, 0)))
    eid_t, gate_t = _peer_select(h_sel, w["wq_t"], w["sk"], tm=_SELECT_TOKENS)
    n_gate = -(-n // _GATE_TOKENS) * _GATE_TOKENS
    wmat = _peer_gates(eid_t[:, :n_gate].T, gate_t[:, :n_gate].T, tm=_tile_rows(n_gate, 128))
    f = _peer_dense(h.astype(BF16), wmat, w["u"], w["v"], w["layer"], tm=_tile_rows(n, 1024), te=8 * N_KEYS)
    return _final(h, f, p, w["ln2_g"], w["ln2_b"], w["wg"], w["bg"], w["ple_w"],
                  tm=_tile_rows(n, 512), alpha=w["alpha"])


def _layer(layer_idx, depth, xp, xs, caches, state_c, state_d, page_table, pp, ps, lw):
    w = _prep_weights(lw, depth)
    w["layer"] = layer_idx
    bp, tp, d = xp.shape
    bs, ts, _ = xs.shape
    ts_pad = 8
    n_pages = page_table.shape[1]
    page = caches[0].shape[3]
    past = n_pages * page
    lam_init = 0.8 - 0.6 * math.exp(-0.3 * (layer_idx + 1))

    pr = _project(xp.reshape(bp * tp, d), w["w_in"], w["w_in_t"], _tile_rows(bp * tp, 512), feature_major=True)
    tq = _tile_rows(tp, 256)
    tk = _tile_rows(tp, 256)
    oa = _attention_a(pr["qat"], pr["kab"], pr["vat"], w["a_lambda"], w["a_g"], w["gmean"],
                      bsz=bp, tq=tq, tk=tk, past=0, lam_init=lam_init)
    ob = _dsa(pr["qbt"], pr["kbb"], pr["vbt"], pr["qit"], _index_keys(pr["kiwi"][:, :D_IDX]), pr["kiwit"],
              bsz=bp, tq=tq, tk=tk, past=0, topk=min(DSA_TOPK, tp // 4), t_valid=tq)
    r3p = lambda a: a.reshape(bp, tp, a.shape[-1])
    zc = jnp.zeros((bp, _HALO_C, GROUP_W), F32)
    zd = jnp.zeros((bp, _HALO_D, GROUP_W), F32)
    oc, od = _convs(r3p(pr["cb"]), r3p(pr["uc"]), r3p(pr["ud"]), zc, zd, w["c_w"], w["d_w"], w["d_b"],
                    w["d_ln_g"], w["d_ln_b"], tm=_tile_rows(tp, 512))
    yp = _token_tail(w, xp.reshape(bp * tp, d), oa, ob, oc.reshape(bp * tp, -1), od.reshape(bp * tp, -1),
                     None, pp.reshape(bp * tp, -1)).reshape(bp, tp, d)
    new_p = (r3p(pr["ka"]), r3p(pr["va"]), r3p(pr["kb"]), r3p(pr["vb"]), r3p(pr["kiwi"])[:, :, :D_IDX],
             jnp.concatenate([zc, r3p(pr["uc"])], axis=1)[:, -(C_WIDTH - 1):],
             jnp.concatenate([zd, r3p(pr["ud"])], axis=1)[:, -(D_WIDTH - 1):])

    xs_pad = jnp.pad(xs, ((0, 0), (0, ts_pad - ts), (0, 0))).reshape(bs * ts_pad, d)
    sr = _project(xs_pad, w["w_in"], w["w_in_t"], _tile_rows(bs * ts_pad, 512), feature_major=False)
    r3s = lambda a: a.reshape(bs, ts_pad, a.shape[-1])
    news = [r3s(sr["ka"]), r3s(sr["va"]), r3s(sr["kb"]), r3s(sr["vb"]), r3s(sr["kiwi"])[:, :, :D_IDX]]
    oa_s, ob_s = _sample_attention(page_table, caches, layer_idx, news, r3s(sr["qa"]), r3s(sr["qb"]), r3s(sr["qi"]),
                                   r3s(sr["kiwi"]), w["a_lambda"], w["a_g"], w["gmean"],
                                   topk=min(DSA_TOPK, (past + ts) // 4), lam_init=lam_init, t_valid=ts)
    unpad = lambda a: a.reshape(bs * ts_pad, GROUP_W)
    hist_c = jnp.concatenate([jnp.zeros((bs, _HALO_C - (C_WIDTH - 1), GROUP_W), F32), state_c], axis=1)
    hist_d = jnp.concatenate([jnp.zeros((bs, _HALO_D - (D_WIDTH - 1), GROUP_W), F32), state_d], axis=1)
    oc_s, od_s = _convs(r3s(sr["cb"]), r3s(sr["uc"]), r3s(sr["ud"]), hist_c, hist_d, w["c_w"], w["d_w"],
                        w["d_b"], w["d_ln_g"], w["d_ln_b"], tm=ts_pad)
    ys = _token_tail(w, xs_pad, unpad(oa_s), unpad(ob_s), oc_s.reshape(bs * ts_pad, -1),
                     od_s.reshape(bs * ts_pad, -1), (bs, ts_pad, ts), ps.reshape(bs * ts, -1)).reshape(bs, ts, d)
    cut = lambda a: r3s(a)[:, :ts]
    new_s = (cut(sr["ka"]), cut(sr["va"]), cut(sr["kb"]), cut(sr["vb"]), cut(sr["kiwi"])[:, :, :D_IDX],
             jnp.concatenate([state_c, cut(sr["uc"])], axis=1)[:, -(C_WIDTH - 1):],
             jnp.concatenate([state_d, cut(sr["ud"])], axis=1)[:, -(D_WIDTH - 1):])
    return yp, ys, new_p, new_s


def kernel(x_prompt, x_sample, cache_a_k, cache_a_v, cache_b_k, cache_b_v, cache_b_kidx, state_c_conv, state_d_conv, page_table, p_prompt, p_sample, w_in, a_lambda, a_subln_g, c_conv_w, d_conv_w, d_conv_b, d_ln_g, d_ln_b, d_pw_w, d_pw_b, w_out, ln1_g, ln1_b, peer_wq, peer_subkeys, peer_u, peer_v, ln2_g, ln2_b, ple_w, ple_gate_w, ple_gate_b):
    depth = w_in.shape[0]
    n_pool, page = cache_a_k.shape[1], cache_a_k.shape[2]
    feature_major = lambda c: jnp.moveaxis(c, 2, -1).reshape(depth, n_pool, -1, page)
    caches = [feature_major(c) for c in (cache_a_k, cache_a_v, cache_b_k, cache_b_v, cache_b_kidx)]
    xp, xs = x_prompt, x_sample
    news_p, news_s = [], []
    for l in range(depth):
        lw = (w_in[l], a_lambda[l], a_subln_g[l], c_conv_w[l], d_conv_w[l], d_conv_b[l], d_ln_g[l], d_ln_b[l],
              d_pw_w[l], d_pw_b[l], w_out[l], ln1_g[l], ln1_b[l], peer_wq[l], peer_subkeys[l], peer_u,
              peer_v, ln2_g[l], ln2_b[l], ple_w[l], ple_gate_w[l], ple_gate_b[l])
        xp, xs, new_p, new_s = _layer(l, depth, xp, xs, caches, state_c_conv[l], state_d_conv[l],
                                      page_table, p_prompt[l], p_sample[l], lw)
        news_p.append(new_p)
        news_s.append(new_s)

    bp, tp, _ = x_prompt.shape
    bs, ts, _ = x_sample.shape
    shapes = [(N_HEADS, 2, 32), (N_HEADS, HEAD_W), (N_HEADS, HEAD_W), (N_HEADS, HEAD_W), (D_IDX,)]

    def stack(news, i, lead):
        a = jnp.stack([n[i] for n in news], axis=0)
        return a.reshape((depth,) + lead + shapes[i]) if i < 5 else a

    outs = [xp, xs]
    for i in range(7):
        outs.append(stack(news_p, i, (bp, tp)))
        outs.append(stack(news_s, i, (bs, ts)))
    return tuple(outs)
```
